```python
import math
import jax, jax.numpy as jnp
from jax import lax
import numpy as np

D_MODEL = 2048
BATCH = 16
SEQ = 256
DEPTH = 4
DEC_BATCH = 2
DEC_SEQ = 1024
PAST_LEN = 256

GRID_W = 64
EPS = 1e-6
V_HEAD_DIM = 128
N_HEADS = (D_MODEL // 2) // V_HEAD_DIM
QK_NOPE_DIM = 128
QK_ROPE_DIM = 64
QK_HEAD_DIM = QK_NOPE_DIM + QK_ROPE_DIM
Q_RANK = D_MODEL // 4
KV_RANK = D_MODEL // 8
ROPE_BASE = 10000.0
ATTN_WIDTH = N_HEADS * V_HEAD_DIM
Q_BLOCK = 128
CHUNK = 128
GM_GROUPS = 4
GM_WIDTH = D_MODEL // 4
GM_GROUP_DIM = GM_WIDTH // GM_GROUPS
POOL_WINDOWS = (2, 4, 8, 16)
N_POOL = len(POOL_WINDOWS)
POOL_WIDTH = D_MODEL // 4
POOL_GROUP_DIM = POOL_WIDTH // N_POOL
MIX_WIDTH = ATTN_WIDTH + GM_WIDTH + POOL_WIDTH
IN_WIDTH = Q_RANK + KV_RANK + QK_ROPE_DIM + 2 * GM_WIDTH + POOL_WIDTH
IN_SPLITS = (Q_RANK, Q_RANK + KV_RANK, Q_RANK + KV_RANK + QK_ROPE_DIM,
             Q_RANK + KV_RANK + QK_ROPE_DIM + 2 * GM_WIDTH)
D_FF = 5632
N_EXPERTS = 8
TOP_K = 2
D_FF_EXPERT = 2816
N_DENSE = (DEPTH + 1) // 2
N_MOE = DEPTH // 2

kernel_name = "hybrid_mla_gmlp_pool_dit_step"


def rms_norm(x, g):
    xf = x.astype(jnp.float32)
    y = xf * lax.rsqrt(jnp.mean(xf * xf, axis=-1, keepdims=True) + EPS)
    return (y * g.astype(jnp.float32)).astype(x.dtype)


def modulation(cond, w_ada, b_ada):
    m = jax.nn.silu(cond) @ w_ada + b_ada
    m = m.reshape(m.shape[:-1] + (1, 6 * D_MODEL))
    return jnp.split(m, 6, axis=-1)


def axial_rope_tables(n_tok):
    rows = n_tok // GRID_W
    row = jnp.repeat(jnp.arange(rows), GRID_W).astype(jnp.float32)
    col = jnp.tile(jnp.arange(GRID_W), rows).astype(jnp.float32)
    nf = QK_ROPE_DIM // 4
    inv = ROPE_BASE ** (-jnp.arange(nf, dtype=jnp.float32) / nf)
    ar = row[:, None] * inv[None, :]
    ac = col[:, None] * inv[None, :]
    ang = jnp.concatenate([ar, ar, ac, ac], axis=-1)
    return jnp.cos(ang), jnp.sin(ang)


def rope_tail(x, cos, sin):
    x_nope, x_rope = x[..., :QK_NOPE_DIM], x[..., QK_NOPE_DIM:]
    xf = x_rope.astype(jnp.float32)
    a, b, c2, d = jnp.split(xf, 4, axis=-1)
    rot = jnp.concatenate([-b, a, -d, c2], axis=-1)
    cs, sn = cos[None, :, None, :], sin[None, :, None, :]
    x_rope = (xf * cs + rot * sn).astype(x.dtype)
    return jnp.concatenate([x_nope, x_rope], axis=-1)


def mla_queries(q_lat, p):
    b, s, _ = q_lat.shape
    q = rms_norm(q_lat, p["qlat_g"]) @ p["w_uq"]
    q = q.reshape(b, s, N_HEADS, QK_HEAD_DIM)
    return rms_norm(q, p["q_head_g"])


def mla_keys_values(ckv, krope, p):
    b, s, _ = ckv.shape
    kv = (ckv @ p["w_ukv"]).reshape(b, s, N_HEADS, QK_NOPE_DIM + V_HEAD_DIM)
    k_nope, v = kv[..., :QK_NOPE_DIM], kv[..., QK_NOPE_DIM:]
    k_r = jnp.broadcast_to(krope[:, :, None, :], (b, s, N_HEADS, QK_ROPE_DIM))
    k = rms_norm(jnp.concatenate([k_nope, k_r], axis=-1), p["k_head_g"])
    return k, v


def attend(q, k, v):
    b, sq, h, dh = q.shape
    nb = sq // Q_BLOCK
    scale = 1.0 / math.sqrt(dh)
    qb = q.reshape(b, nb, Q_BLOCK, h, dh).transpose(1, 0, 2, 3, 4)

    def one_block(qblk):
        s = jnp.einsum("bqhd,bkhd->bhqk", qblk, k, preferred_element_type=jnp.float32) * scale
        pr = jax.nn.softmax(s, axis=-1)
        return jnp.einsum("bhqk,bkhd->bqhd", pr.astype(v.dtype), v)

    out = lax.map(one_block, qb)
    return out.transpose(1, 0, 2, 3, 4).reshape(b, sq, h * v.shape[-1])


def chunk_spatial_gate(z_gm, p):
    z = jax.nn.gelu(z_gm)
    u, v = jnp.split(z, 2, axis=-1)
    b, s, _ = v.shape
    v = rms_norm(v.reshape(b, s, GM_GROUPS, GM_GROUP_DIM), p["gm_g"])
    vc = v.reshape(b, s // CHUNK, CHUNK, GM_GROUPS, GM_GROUP_DIM)
    sv = jnp.einsum("gpq,bnqgc->bnpgc", p["gm_w_s"], vc) + p["gm_b_s"].T[None, None, :, :, None]
    return u * sv.reshape(b, s, GM_WIDTH)


def multiscale_pool(y, p):
    b, s, _ = y.shape
    yg = y.reshape(b, s, N_POOL, POOL_GROUP_DIM)
    yf = yg.astype(jnp.float32)
    csum = jnp.concatenate([jnp.zeros((b, 1, N_POOL, POOL_GROUP_DIM), jnp.float32),
                            jnp.cumsum(yf, axis=1)], axis=1)
    t = jnp.arange(s)
    outs = []
    for gi, w in enumerate(POOL_WINDOWS):
        lo = jnp.clip(t - w // 2, 0, s - 1)
        hi = jnp.clip(t + w - 1 - w // 2, 0, s - 1)
        win_sum = csum[:, hi + 1, gi] - csum[:, lo, gi]
        cnt = (hi - lo + 1).astype(jnp.float32)[None, :, None]
        outs.append(win_sum / cnt - yf[:, :, gi])
    pooled = jnp.stack(outs, axis=2).astype(y.dtype)
    mixed = jnp.einsum("bsgc,gcd->bsgd", pooled, p["pool_w"])
    return mixed.reshape(b, s, POOL_WIDTH) * p["pool_scale"]


def token_mixer(h, p, rope=None, ctx_cache=None):
    z = h @ p["w_in"]
    q_lat, kv_lat, k_rope, z_gm, z_pool = jnp.split(z, IN_SPLITS, axis=-1)
    q = mla_queries(q_lat, p)
    ckv = rms_norm(kv_lat, p["kvlat_g"])
    k, v = mla_keys_values(ckv, k_rope, p)
    if rope is not None:
        q = rope_tail(q, *rope)
        k = rope_tail(k, *rope)
    if ctx_cache is not None:
        k_c, v_c = mla_keys_values(ctx_cache[0], ctx_cache[1], p)
        k = jnp.concatenate([k, k_c], axis=1)
        v = jnp.concatenate([v, v_c], axis=1)
    attn = attend(q, k, v)
    gm = chunk_spatial_gate(z_gm, p)
    pool = multiscale_pool(z_pool, p)
    out = jnp.concatenate([attn, gm, pool], axis=-1) @ p["w_o"]
    return out, ckv, k_rope


def swiglu(h, w1, w3, w2):
    return (jax.nn.silu(h @ w1) * (h @ w3)) @ w2


def moe_swiglu(h, w_router, w1, w3, w2):
    logits = (h @ w_router).astype(jnp.float32)
    top_v, top_i = lax.top_k(logits, TOP_K)
    gates = jax.nn.softmax(top_v, axis=-1)
    combine = jnp.einsum("bsk,bske->bse", gates, jax.nn.one_hot(top_i, N_EXPERTS, dtype=jnp.float32))
    combine = combine.astype(h.dtype)
    y = jnp.zeros_like(h)
    for e in range(N_EXPERTS):
        y = y + combine[..., e:e + 1] * swiglu(h, w1[e], w3[e], w2[e])
    return y


def channel_mixer(h, layer, p):
    if layer % 2 == 0:
        return swiglu(h, p["ffn_w1"], p["ffn_w3"], p["ffn_w2"])
    return moe_swiglu(h, p["router_w"], p["moe_w1"], p["moe_w3"], p["moe_w2"])


def trunk_layer(x, cond, layer, p, rope=None, ctx_cache=None):
    sh1, sc1, g1, sh2, sc2, g2 = modulation(cond, p["w_ada"], p["b_ada"])
    h = rms_norm(x, p["norm1_g"]) * (1.0 + sc1) + sh1
    mix, ckv, k_rope = token_mixer(h, p, rope, ctx_cache)
    x = x + g1 * mix
    h2 = rms_norm(x, p["norm2_g"]) * (1.0 + sc2) + sh2
    x = x + g2 * channel_mixer(h2, layer, p)
    return x, ckv, k_rope


def setup_inputs(seed: int = 0) -> dict:
    key = jax.random.key(seed)
    ks = iter(jax.random.split(key, 40))

    def nrm(shape, scale):
        return jax.random.normal(next(ks), shape, jnp.float32) * scale

    def gain(shape):
        return 1.0 + nrm(shape, 0.05)

    D = D_MODEL
    return {
        "x_prompt": nrm((BATCH, SEQ, D), 1.0),
        "x_sample": nrm((DEC_BATCH, DEC_SEQ, D), 1.0),
        "cache_ckv": nrm((DEC_BATCH, DEPTH, PAST_LEN, KV_RANK), 1.0),
        "cache_krope": nrm((DEC_BATCH, DEPTH, PAST_LEN, QK_ROPE_DIM), 1.0),
        "c": nrm((DEC_BATCH, D), 1.0),
        "c_ctx": nrm((D,), 1.0),
        "w_ada": nrm((DEPTH, D, 6 * D), 0.5 * D ** -0.5),
        "b_ada": nrm((DEPTH, 6 * D), 0.02),
        "norm1_g": gain((DEPTH, D)),
        "norm2_g": gain((DEPTH, D)),
        "w_in": nrm((DEPTH, D, IN_WIDTH), D ** -0.5),
        "qlat_norm_g": gain((DEPTH, Q_RANK)),
        "kvlat_norm_g": gain((DEPTH, KV_RANK)),
        "w_uq": nrm((DEPTH, Q_RANK, N_HEADS * QK_HEAD_DIM), Q_RANK ** -0.5),
        "w_ukv": nrm((DEPTH, KV_RANK, N_HEADS * (QK_NOPE_DIM + V_HEAD_DIM)), KV_RANK ** -0.5),
        "q_head_norm_g": gain((DEPTH, QK_HEAD_DIM)),
        "k_head_norm_g": gain((DEPTH, QK_HEAD_DIM)),
        "gm_norm_g": gain((DEPTH, GM_GROUPS, GM_GROUP_DIM)),
        "gm_w_s": nrm((DEPTH, GM_GROUPS, CHUNK, CHUNK), CHUNK ** -0.5),
        "gm_b_s": gain((DEPTH, GM_GROUPS, CHUNK)),
        "pool_w": nrm((DEPTH, N_POOL, POOL_GROUP_DIM, POOL_GROUP_DIM), POOL_GROUP_DIM ** -0.5),
        "pool_scale": gain((DEPTH, POOL_WIDTH)),
        "w_o": nrm((DEPTH, MIX_WIDTH, D), MIX_WIDTH ** -0.5),
        "ffn_w1": nrm((N_DENSE, D, D_FF), D ** -0.5),
        "ffn_w3": nrm((N_DENSE, D, D_FF), D ** -0.5),
        "ffn_w2": nrm((N_DENSE, D_FF, D), D_FF ** -0.5),
        "router_w": nrm((N_MOE, D, N_EXPERTS), D ** -0.5),
        "moe_w1": nrm((N_MOE, N_EXPERTS, D, D_FF_EXPERT), D ** -0.5),
        "moe_w3": nrm((N_MOE, N_EXPERTS, D, D_FF_EXPERT), D ** -0.5),
        "moe_w2": nrm((N_MOE, N_EXPERTS, D_FF_EXPERT, D), D_FF_EXPERT ** -0.5),
    }


def reference(x_prompt, x_sample, cache_ckv, cache_krope, c, c_ctx, w_ada, b_ada,
              norm1_g, norm2_g, w_in, qlat_norm_g, kvlat_norm_g, w_uq, w_ukv,
              q_head_norm_g, k_head_norm_g, gm_norm_g, gm_w_s, gm_b_s, pool_w, pool_scale,
              w_o, ffn_w1, ffn_w3, ffn_w2, router_w, moe_w1, moe_w3, moe_w2):
    n_lat = x_sample.shape[1]
    rope = axial_rope_tables(n_lat)
    x_p, x_s = x_prompt, x_sample
    new_ckv_list, new_krope_list = [], []
    for l in range(DEPTH):
        p = {
            "w_ada": w_ada[l], "b_ada": b_ada[l], "norm1_g": norm1_g[l], "norm2_g": norm2_g[l],
            "w_in": w_in[l], "qlat_g": qlat_norm_g[l], "kvlat_g": kvlat_norm_g[l],
            "w_uq": w_uq[l], "w_ukv": w_ukv[l], "q_head_g": q_head_norm_g[l],
            "k_head_g": k_head_norm_g[l], "gm_g": gm_norm_g[l], "gm_w_s": gm_w_s[l],
            "gm_b_s": gm_b_s[l], "pool_w": pool_w[l], "pool_scale": pool_scale[l], "w_o": w_o[l],
        }
        if l % 2 == 0:
            i = l // 2
            p["ffn_w1"], p["ffn_w3"], p["ffn_w2"] = ffn_w1[i], ffn_w3[i], ffn_w2[i]
        else:
            i = l // 2
            p["router_w"], p["moe_w1"], p["moe_w3"], p["moe_w2"] = router_w[i], moe_w1[i], moe_w3[i], moe_w2[i]
        x_p, ckv_l, krope_l = trunk_layer(x_p, c_ctx, l, p)
        new_ckv_list.append(ckv_l)
        new_krope_list.append(krope_l)
        x_s, _, _ = trunk_layer(x_s, c, l, p, rope=rope,
                                ctx_cache=(cache_ckv[:, l], cache_krope[:, l]))
    new_ckv = jnp.stack(new_ckv_list, axis=1)
    new_krope = jnp.stack(new_krope_list, axis=1)
    return (x_p, x_s, new_ckv, new_krope)
```

```python
import functools
import math

import jax
import jax.numpy as jnp
import numpy as np
from jax import lax
from jax.experimental import pallas as pl
from jax.experimental.pallas import tpu as pltpu

F32 = jnp.float32
BF16 = jnp.bfloat16

D = 2048
BATCH, SEQ = 16, 256
DEC_BATCH, DEC_SEQ = 2, 1024
PAST = 256
DEPTH = 4
GRID_W = 64
EPS = 1e-6
N_HEADS = 8
NOPE, ROPE, VDIM = 128, 64, 128
QK_DIM = NOPE + ROPE
HEAD_PAD = 256
Q_RANK, KV_RANK = 512, 256
ROPE_BASE = 10000.0
ATTN_W = N_HEADS * VDIM
CHUNK, GM_GROUPS, GM_W = 128, 4, 512
POOL_WINDOWS = (2, 4, 8, 16)
POOL_W = 512
POOL_HALO = 8
D_FF, N_EXP, D_FFE = 5632, 8, 2816

C_Q, C_KV, C_KR, C_GM, C_PL = 0, 512, 768, 896, 1920
IN_PAD = 2432

LANE = 128
TM = 256
M_P, M_S = BATCH * SEQ, DEC_BATCH * DEC_SEQ
M_TOK = M_P + M_S
N_TILES = M_TOK // TM
P_TILES = M_P // TM
S_TILES = DEC_SEQ // TM
MOD_ROWS = 16
TE = 512
TF_DENSE, TF_MOE = 512, 256
TN_DOWN = 512
ROWS_MOE = 2 * M_TOK + N_EXP * TE
NT_MOE = ROWS_MOE // TE
NT_DENSE = M_TOK // TE
ADA_TN = 512
VMEM_LIMIT = 56 * 1024 * 1024


def _cparams(sem):
    return pltpu.CompilerParams(dimension_semantics=sem, vmem_limit_bytes=VMEM_LIMIT)


def _const_spec(shape):
    nd = len(shape)
    return pl.BlockSpec(shape, lambda *_: (0,) * nd)


def _mod_row(i):
    return jnp.where(i < P_TILES, 0, 1 + (i - P_TILES) // S_TILES)


def _rms(x, g, n):
    ms = jnp.sum(x * x, axis=-1, keepdims=True) * (1.0 / n)
    return x * lax.rsqrt(ms + EPS) * g


def _rope(r, cos, sin_a, sin_b):
    return r * cos + pltpu.roll(r, LANE - 16, 1) * sin_a + pltpu.roll(r, 16, 1) * sin_b


def _ada_kernel(cond_ref, w_ref, b_ref, o_ref):
    c = cond_ref[...]
    s = c * jax.nn.sigmoid(c)
    s_hi = s.astype(BF16)
    s_lo = (s - s_hi.astype(F32)).astype(BF16)
    w = w_ref[...]
    w_hi = w.astype(BF16)
    w_lo = (w - w_hi.astype(F32)).astype(BF16)
    lhs = jnp.concatenate([s_hi, s_lo], axis=0)
    r1 = jnp.dot(lhs, w_hi, preferred_element_type=F32)
    r2 = jnp.dot(s_hi, w_lo, preferred_element_type=F32)
    o_ref[...] = r1[:MOD_ROWS] + r1[MOD_ROWS:] + r2 + b_ref[...]


def _ada_call(cond, w_ada, b_ada):
    n = 6 * D
    return pl.pallas_call(
        _ada_kernel,
        grid=(DEPTH, n // ADA_TN),
        in_specs=[
            pl.BlockSpec((MOD_ROWS, D), lambda l, j: (0, 0)),
            pl.BlockSpec((None, D, ADA_TN), lambda l, j: (l, 0, j)),
            pl.BlockSpec((None, 1, ADA_TN), lambda l, j: (l, 0, j)),
        ],
        out_specs=pl.BlockSpec((None, MOD_ROWS, ADA_TN), lambda l, j: (l, 0, j)),
        out_shape=jax.ShapeDtypeStruct((DEPTH, MOD_ROWS, n), F32),
        compiler_params=_cparams(("parallel", "parallel")),
        name="ada",
    )(cond, w_ada, b_ada.reshape(DEPTH, 1, n))


def _write_kv(ckv_b, kr, w_ukv_ref, kg_n, kg_r, rope, k_ref, v_ref):
    kv = jnp.dot(ckv_b, w_ukv_ref[...], preferred_element_type=F32)
    ssr = jnp.sum(kr * kr, axis=-1, keepdims=True)
    for h in range(N_HEADS):
        kn = kv[:, h * HEAD_PAD:h * HEAD_PAD + NOPE]
        vv = kv[:, h * HEAD_PAD + NOPE:(h + 1) * HEAD_PAD]
        ms = (jnp.sum(kn * kn, axis=-1, keepdims=True) + ssr) * (1.0 / QK_DIM)
        inv = lax.rsqrt(ms + EPS)
        r = kr * inv * kg_r
        if rope is not None:
            r = _rope(r, *rope)
        k_ref[:, h * HEAD_PAD:h * HEAD_PAD + NOPE] = (kn * inv * kg_n).astype(BF16)
        k_ref[:, h * HEAD_PAD + NOPE:(h + 1) * HEAD_PAD] = r.astype(BF16)
        v_ref[:, h * VDIM:(h + 1) * VDIM] = vv.astype(BF16)


def _cache_kv_kernel(ckv_ref, kr_ref, w_ukv_ref, kgn_ref, kgr_ref, k_ref, v_ref):
    _write_kv(ckv_ref[...].astype(BF16), kr_ref[...], w_ukv_ref, kgn_ref[...], kgr_ref[...],
              None, k_ref, v_ref)


def _cache_kv_call(cache_ckv, cache_kr_p, w_ukv_b, kg_n, kg_r):
    return pl.pallas_call(
        _cache_kv_kernel,
        grid=(DEPTH, DEC_BATCH),
        in_specs=[
            pl.BlockSpec((None, None, PAST, KV_RANK), lambda l, b: (b, l, 0, 0)),
            pl.BlockSpec((None, None, PAST, LANE), lambda l, b: (b, l, 0, 0)),
            pl.BlockSpec((None, KV_RANK, N_HEADS * HEAD_PAD), lambda l, b: (l, 0, 0)),
            pl.BlockSpec((None, 1, LANE), lambda l, b: (l, 0, 0)),
            pl.BlockSpec((None, 1, LANE), lambda l, b: (l, 0, 0)),
        ],
        out_specs=[
            pl.BlockSpec((None, None, PAST, N_HEADS * HEAD_PAD), lambda l, b: (l, b, 0, 0)),
            pl.BlockSpec((None, None, PAST, ATTN_W), lambda l, b: (l, b, 0, 0)),
        ],
        out_shape=[
            jax.ShapeDtypeStruct((DEPTH, DEC_BATCH, PAST, N_HEADS * HEAD_PAD), BF16),
            jax.ShapeDtypeStruct((DEPTH, DEC_BATCH, PAST, ATTN_W), BF16),
        ],
        compiler_params=_cparams(("parallel", "parallel")),
        name="cache_kv",
    )(cache_ckv, cache_kr_p, w_ukv_b, kg_n, kg_r)


def _in_proj_kernel(x_ref, mods_ref, n1g_ref, w_in_ref, qlg_ref, kvlg_ref, w_uq_ref, qg_ref,
                    w_ukv_ref, kgn_ref, kgr_ref, rope_ref, gmg_ref, gmw_ref, gmb_ref,
                    q_ref, k_ref, v_ref, ckv_ref, kr_ref, zg_ref, zp_ref):
    i = pl.program_id(0)
    m = mods_ref[pl.ds(_mod_row(i), 1), :]
    sh1, sc1 = m[:, 0:D], m[:, D:2 * D]
    x = x_ref[...]
    h = _rms(x, n1g_ref[...], D) * (1.0 + sc1) + sh1
    z = jnp.dot(h.astype(BF16), w_in_ref[...], preferred_element_type=F32)
    rope = (rope_ref[0], rope_ref[1], rope_ref[2])

    ql = _rms(z[:, C_Q:C_Q + Q_RANK], qlg_ref[...], Q_RANK)
    qf = jnp.dot(ql.astype(BF16), w_uq_ref[...], preferred_element_type=F32)
    qg = qg_ref[...]
    for hd in range(N_HEADS):
        seg = qf[:, hd * HEAD_PAD:(hd + 1) * HEAD_PAD]
        seg = _rms(seg, qg, QK_DIM)
        q_ref[:, hd * HEAD_PAD:hd * HEAD_PAD + NOPE] = seg[:, :NOPE].astype(BF16)
        q_ref[:, hd * HEAD_PAD + NOPE:(hd + 1) * HEAD_PAD] = _rope(seg[:, NOPE:], *rope).astype(BF16)

    ckv = _rms(z[:, C_KV:C_KV + KV_RANK], kvlg_ref[...], KV_RANK)
    kr = z[:, C_KR:C_KR + LANE]
    ckv_ref[...] = ckv
    kr_ref[...] = kr
    _write_kv(ckv.astype(BF16), kr, w_ukv_ref, kgn_ref[...], kgr_ref[...], rope, k_ref, v_ref)

    zg = jax.nn.gelu(z[:, C_GM:C_GM + 2 * GM_W], approximate=True)
    for g in range(GM_GROUPS):
        u = zg[:, g * LANE:(g + 1) * LANE]
        vg = _rms(zg[:, GM_W + g * LANE:GM_W + (g + 1) * LANE], gmg_ref[g:g + 1, :], LANE)
        vb = vg.astype(BF16)
        for c in range(TM // CHUNK):
            sv = jnp.dot(gmw_ref[g], vb[c * CHUNK:(c + 1) * CHUNK, :], preferred_element_type=F32)
            sv = sv + gmb_ref[g]
            zg_ref[c * CHUNK:(c + 1) * CHUNK, g * LANE:(g + 1) * LANE] = (
                u[c * CHUNK:(c + 1) * CHUNK, :] * sv).astype(BF16)

    zp_ref[...] = z[:, C_PL:C_PL + POOL_W]


def _in_proj_call(x, mods, n1g, w_in_b, qlg, kvlg, w_uq_b, qg, w_ukv_b, kg_n, kg_r, rope_tab,
                  gmg, gmw_b, gmb):
    def rope_idx(i):
        return (0, jnp.where(i < P_TILES, 0, 1 + (i - P_TILES) % S_TILES), 0)

    row = lambda w: pl.BlockSpec((TM, w), lambda i: (i, 0))
    return pl.pallas_call(
        _in_proj_kernel,
        grid=(N_TILES,),
        in_specs=[
            row(D),
            _const_spec((MOD_ROWS, 6 * D)),
            _const_spec((1, D)),
            _const_spec((D, IN_PAD)),
            _const_spec((1, Q_RANK)),
            _const_spec((1, KV_RANK)),
            _const_spec((Q_RANK, N_HEADS * HEAD_PAD)),
            _const_spec((1, HEAD_PAD)),
            _const_spec((KV_RANK, N_HEADS * HEAD_PAD)),
            _const_spec((1, LANE)),
            _const_spec((1, LANE)),
            pl.BlockSpec((3, TM, LANE), rope_idx),
            _const_spec((GM_GROUPS, LANE)),
            _const_spec((GM_GROUPS, CHUNK, CHUNK)),
            _const_spec((GM_GROUPS, CHUNK, LANE)),
        ],
        out_specs=[row(N_HEADS * HEAD_PAD), row(N_HEADS * HEAD_PAD), row(ATTN_W), row(KV_RANK),
                   row(LANE), row(GM_W), row(POOL_W)],
        out_shape=[
            jax.ShapeDtypeStruct((M_TOK, N_HEADS * HEAD_PAD), BF16),
            jax.ShapeDtypeStruct((M_TOK, N_HEADS * HEAD_PAD), BF16),
            jax.ShapeDtypeStruct((M_TOK, ATTN_W), BF16),
            jax.ShapeDtypeStruct((M_TOK, KV_RANK), F32),
            jax.ShapeDtypeStruct((M_TOK, LANE), F32),
            jax.ShapeDtypeStruct((M_TOK, GM_W), BF16),
            jax.ShapeDtypeStruct((M_TOK, POOL_W), F32),
        ],
        compiler_params=_cparams(("parallel",)),
        name="in_proj",
    )(x, mods, n1g, w_in_b, qlg, kvlg, w_uq_b, qg, w_ukv_b, kg_n, kg_r, rope_tab, gmg, gmw_b, gmb)


def _attend_heads(q_ref, kv_pairs, o_ref):
    nt = (((1,), (1,)), ((), ()))
    for h in range(N_HEADS):
        qh = q_ref[:, h * HEAD_PAD:(h + 1) * HEAD_PAD]
        ss = [lax.dot_general(qh, k_ref[:, h * HEAD_PAD:(h + 1) * HEAD_PAD], nt,
                              preferred_element_type=F32) for k_ref, _ in kv_pairs]
        mx = ss[0].max(axis=-1, keepdims=True)
        for s in ss[1:]:
            mx = jnp.maximum(mx, s.max(axis=-1, keepdims=True))
        den = None
        acc = None
        for s, (_, v_ref) in zip(ss, kv_pairs):
            p = jnp.exp(s - mx)
            d = jnp.sum(p, axis=-1, keepdims=True)
            a = jnp.dot(p.astype(BF16), v_ref[:, h * VDIM:(h + 1) * VDIM], preferred_element_type=F32)
            den = d if den is None else den + d
            acc = a if acc is None else acc + a
        o_ref[:, h * VDIM:(h + 1) * VDIM] = (acc / den).astype(BF16)


def _attn_kernel(q_ref, kp_ref, vp_ref, ks_ref, vs_ref, kc_ref, vc_ref, o_ref):
    i = pl.program_id(0)

    @pl.when(i < P_TILES)
    def _():
        _attend_heads(q_ref, [(kp_ref, vp_ref)], o_ref)

    @pl.when(i >= P_TILES)
    def _():
        _attend_heads(q_ref, [(ks_ref, vs_ref), (kc_ref, vc_ref)], o_ref)


def _attn_call(q, k, v, k_c, v_c):
    def seq_b(i):
        return jnp.maximum(i - P_TILES, 0) // S_TILES

    small = lambda w: pl.BlockSpec((TM, w), lambda i: (jnp.minimum(i, P_TILES - 1), 0))
    big = lambda w: pl.BlockSpec((DEC_SEQ, w), lambda i: (M_P // DEC_SEQ + seq_b(i), 0))
    cache = lambda w: pl.BlockSpec((None, PAST, w), lambda i: (seq_b(i), 0, 0))
    return pl.pallas_call(
        _attn_kernel,
        grid=(N_TILES,),
        in_specs=[
            pl.BlockSpec((TM, N_HEADS * HEAD_PAD), lambda i: (i, 0)),
            small(N_HEADS * HEAD_PAD), small(ATTN_W),
            big(N_HEADS * HEAD_PAD), big(ATTN_W),
            cache(N_HEADS * HEAD_PAD), cache(ATTN_W),
        ],
        out_specs=pl.BlockSpec((TM, ATTN_W), lambda i: (i, 0)),
        out_shape=jax.ShapeDtypeStruct((M_TOK, ATTN_W), BF16),
        compiler_params=_cparams(("parallel",)),
        name="attention",
    )(q, k, v, k, v, k_c, v_c)


def _out_proj_kernel(moe, attn_ref, zg_ref, zprev_ref, zcur_ref, znext_ref, x_ref, mods_ref,
                     pw_ref, ps_ref, w_o_ref, n2g_ref, *rest):
    if moe:
        rw_ref, xo_ref, h2b_ref, h2f_ref, rinfo_ref, ext_ref, mix_ref = rest
    else:
        xo_ref, h2b_ref, ext_ref, mix_ref = rest
    i = pl.program_id(0)
    is_ctx = i < P_TILES
    t_in_seq = jnp.where(is_ctx, 0, (i - P_TILES) % S_TILES)
    n_tiles_seq = jnp.where(is_ctx, 1, S_TILES)
    first = t_in_seq == 0
    last = t_in_seq == n_tiles_seq - 1
    seq_len = n_tiles_seq * TM

    zero_halo = jnp.zeros((POOL_HALO, POOL_W), F32)
    ext_ref[0:POOL_HALO, :] = jnp.where(first, zero_halo, zprev_ref[TM - POOL_HALO:, :])
    ext_ref[POOL_HALO:POOL_HALO + TM, :] = zcur_ref[...]
    ext_ref[POOL_HALO + TM:, :] = jnp.where(last, zero_halo, znext_ref[0:POOL_HALO, :])

    mix_ref[:, 0:ATTN_W] = attn_ref[...]
    mix_ref[:, ATTN_W:ATTN_W + GM_W] = zg_ref[...]
    pos = t_in_seq * TM + lax.broadcasted_iota(jnp.int32, (TM, 1), 0)
    for gi, w in enumerate(POOL_WINDOWS):
        cols = slice(gi * LANE, (gi + 1) * LANE)
        win = None
        for dlt in range(-(w // 2), w - w // 2):
            part = ext_ref[POOL_HALO + dlt:POOL_HALO + dlt + TM, cols]
            win = part if win is None else win + part
        lo = jnp.maximum(pos - w // 2, 0)
        hi = jnp.minimum(pos + (w - 1 - w // 2), seq_len - 1)
        cnt = (hi - lo + 1).astype(F32)
        pooled = win / cnt - zcur_ref[:, cols]
        mixed = jnp.dot(pooled.astype(BF16), pw_ref[gi], preferred_element_type=F32)
        mix_ref[:, ATTN_W + GM_W + gi * LANE:ATTN_W + GM_W + (gi + 1) * LANE] = (
            mixed * ps_ref[:, cols]).astype(BF16)

    out = jnp.dot(mix_ref[...], w_o_ref[...], preferred_element_type=F32)
    m = mods_ref[pl.ds(_mod_row(i), 1), :]
    g1, sh2, sc2 = m[:, 2 * D:3 * D], m[:, 3 * D:4 * D], m[:, 4 * D:5 * D]
    xn = x_ref[...] + g1 * out
    xo_ref[...] = xn
    h2 = _rms(xn, n2g_ref[...], D) * (1.0 + sc2) + sh2
    h2b_ref[...] = h2.astype(BF16)

    if moe:
        h2f_ref[...] = h2
        h_hi = h2.astype(BF16)
        h_lo = (h2 - h_hi.astype(F32)).astype(BF16)
        rw = rw_ref[...]
        w_hi = rw.astype(BF16)
        w_lo = (rw - w_hi.astype(F32)).astype(BF16)
        logits = (jnp.dot(h_hi, w_hi, preferred_element_type=F32)
                  + jnp.dot(h_lo, w_hi, preferred_element_type=F32)
                  + jnp.dot(h_hi, w_lo, preferred_element_type=F32))
        lane = lax.broadcasted_iota(jnp.int32, (TM, LANE), 1).astype(F32)
        neg = jnp.float32(-jnp.inf)
        logits = jnp.where(lane < N_EXP, logits, neg)
        v1 = logits.max(axis=-1, keepdims=True)
        i1 = jnp.where(logits == v1, lane, float(LANE)).min(axis=-1, keepdims=True)
        l2 = jnp.where(lane == i1, neg, logits)
        v2 = l2.max(axis=-1, keepdims=True)
        i2 = jnp.where(l2 == v2, lane, float(LANE)).min(axis=-1, keepdims=True)
        e2 = jnp.exp(v2 - v1)
        gate1 = 1.0 / (1.0 + e2)
        gate2 = e2 / (1.0 + e2)
        rinfo_ref[...] = jnp.where(lane == 0, i1, jnp.where(lane == 1, i2, jnp.where(
            lane == 2, gate1, jnp.where(lane == 3, gate2, 0.0))))


def _out_proj_call(moe, attn, zg, zp, x, mods, pw_b, ps, w_o_b, n2g, rw_p=None):
    row = lambda w: pl.BlockSpec((TM, w), lambda i: (i, 0))
    in_specs = [
        row(ATTN_W), row(GM_W),
        pl.BlockSpec((TM, POOL_W), lambda i: (jnp.maximum(i - 1, 0), 0)),
        row(POOL_W),
        pl.BlockSpec((TM, POOL_W), lambda i: (jnp.minimum(i + 1, N_TILES - 1), 0)),
        row(D),
        _const_spec((MOD_ROWS, 6 * D)),
        _const_spec((len(POOL_WINDOWS), LANE, LANE)),
        _const_spec((1, POOL_W)),
        _const_spec((D, D)),
        _const_spec((1, D)),
    ]
    out_specs = [row(D), row(D)]
    out_shape = [jax.ShapeDtypeStruct((M_TOK, D), F32), jax.ShapeDtypeStruct((M_TOK, D), BF16)]
    args = [attn, zg, zp, zp, zp, x, mods, pw_b, ps, w_o_b, n2g]
    if moe:
        in_specs.append(_const_spec((D, LANE)))
        out_specs += [row(D), row(LANE)]
        out_shape += [jax.ShapeDtypeStruct((M_TOK, D), F32), jax.ShapeDtypeStruct((M_TOK, LANE), F32)]
        args.append(rw_p)
    return pl.pallas_call(
        functools.partial(_out_proj_kernel, moe),
        grid=(N_TILES,),
        in_specs=in_specs,
        out_specs=out_specs,
        out_shape=out_shape,
        scratch_shapes=[pltpu.VMEM((TM + 2 * POOL_HALO, POOL_W), F32), pltpu.VMEM((TM, D), BF16)],
        compiler_params=_cparams(("parallel",)),
        name="out_proj_moe" if moe else "out_proj",
    )(*args)


def _up_kernel(te_ref, nact_ref, h_ref, w1_ref, w3_ref, g_ref, w1b_ref, w3b_ref):
    t = pl.program_id(1)
    changed = te_ref[t] != te_ref[jnp.maximum(t - 1, 0)]

    @pl.when((t == 0) | changed)
    def _():
        w1b_ref[...] = w1_ref[...].astype(BF16)
        w3b_ref[...] = w3_ref[...].astype(BF16)

    @pl.when(t < nact_ref[0])
    def _():
        h = h_ref[...]
        a = jnp.dot(h, w1b_ref[...], preferred_element_type=F32)
        b = jnp.dot(h, w3b_ref[...], preferred_element_type=F32)
        g_ref[...] = (a * jax.nn.sigmoid(a) * b).astype(BF16)

    @pl.when(t >= nact_ref[0])
    def _():
        g_ref[...] = jnp.zeros_like(g_ref)


def _up_call(tile_e, n_act, h, w1, w3, tf, name):
    rows, ff = h.shape[0], w1.shape[-1]
    nt = rows // TE
    tile = lambda t, na: jnp.minimum(t, na[0] - 1)
    return pl.pallas_call(
        _up_kernel,
        grid_spec=pltpu.PrefetchScalarGridSpec(
            num_scalar_prefetch=2,
            grid=(ff // tf, nt),
            in_specs=[
                pl.BlockSpec((TE, D), lambda j, t, te, na: (tile(t, na), 0)),
                pl.BlockSpec((None, D, tf), lambda j, t, te, na: (te[t], 0, j)),
                pl.BlockSpec((None, D, tf), lambda j, t, te, na: (te[t], 0, j)),
            ],
            out_specs=pl.BlockSpec((TE, tf), lambda j, t, te, na: (t, j)),
            scratch_shapes=[pltpu.VMEM((D, tf), BF16), pltpu.VMEM((D, tf), BF16)],
        ),
        out_shape=jax.ShapeDtypeStruct((rows, ff), BF16),
        compiler_params=_cparams(("arbitrary", "arbitrary")),
        name=name,
    )(tile_e, n_act, h, w1, w3)


def _down_kernel(dense, te_ref, nact_ref, g_ref, w2_ref, *rest):
    if dense:
        x_ref, mods_ref, o_ref, w2b_ref = rest
    else:
        gate_ref, o_ref, w2b_ref = rest
    t = pl.program_id(1)
    changed = te_ref[t] != te_ref[jnp.maximum(t - 1, 0)]

    @pl.when((t == 0) | changed)
    def _():
        w2b_ref[...] = w2_ref[...].astype(BF16)

    @pl.when(t < nact_ref[0])
    def _():
        y = jnp.dot(g_ref[...], w2b_ref[...], preferred_element_type=F32)
        if dense:
            tiles_per_seq = DEC_SEQ // TE
            ctx_tiles = M_P // TE
            row = jnp.where(t < ctx_tiles, 0, 1 + (t - ctx_tiles) // tiles_per_seq)
            g2 = mods_ref[pl.ds(row, 1), :]
            o_ref[...] = x_ref[...] + g2 * y
        else:
            o_ref[...] = gate_ref[...] * y

    @pl.when(t >= nact_ref[0])
    def _():
        o_ref[...] = jnp.zeros_like(o_ref)


def _down_call(dense, tile_e, n_act, g, w2, extra, name):
    rows, ff = g.shape
    nt = rows // TE
    tn = TN_DOWN
    tile = lambda t, na: jnp.minimum(t, na[0] - 1)
    in_specs = [
        pl.BlockSpec((TE, ff), lambda j, t, te, na: (tile(t, na), 0)),
        pl.BlockSpec((None, ff, tn), lambda j, t, te, na: (te[t], 0, j)),
    ]
    if dense:
        x, mods = extra
        in_specs += [
            pl.BlockSpec((TE, tn), lambda j, t, te, na: (t, j)),
            pl.BlockSpec((MOD_ROWS, tn), lambda j, t, te, na: (0, 5 * (D // tn) + j)),
        ]
    else:
        in_specs.append(pl.BlockSpec((TE, 1), lambda j, t, te, na: (tile(t, na), 0)))
    return pl.pallas_call(
        functools.partial(_down_kernel, dense),
        grid_spec=pltpu.PrefetchScalarGridSpec(
            num_scalar_prefetch=2,
            grid=(D // tn, nt),
            in_specs=in_specs,
            out_specs=pl.BlockSpec((TE, tn), lambda j, t, te, na: (t, j)),
            scratch_shapes=[pltpu.VMEM((ff, tn), BF16)],
        ),
        out_shape=jax.ShapeDtypeStruct((rows, D), F32),
        compiler_params=_cparams(("arbitrary", "arbitrary")),
        name=name,
    )(tile_e, n_act, g, w2, *extra)


def _row_copy(src_hbm, tok, buf, slot, r, sem):
    return pltpu.make_async_copy(src_hbm.at[pl.ds(tok, 1)], buf.at[slot, pl.ds(r, 1)], sem.at[slot])


def _dispatch_kernel(src_ref, nact_ref, h_hbm, o_ref, buf, sem):
    t = pl.program_id(0)
    nact = nact_ref[0]

    def issue(tile, slot):
        def body(r, c):
            _row_copy(h_hbm, src_ref[tile * TE + r], buf, slot, r, sem).start()
            return c
        lax.fori_loop(0, TE, body, 0)

    @pl.when(t == 0)
    def _():
        issue(0, 0)

    @pl.when(t + 1 < nact)
    def _():
        issue(t + 1, (t + 1) % 2)

    @pl.when(t < nact)
    def _():
        slot = t % 2

        def body(r, c):
            _row_copy(h_hbm, 0, buf, slot, r, sem).wait()
            return c
        lax.fori_loop(0, TE, body, 0)
        o_ref[...] = buf[slot].astype(BF16)

    @pl.when(t >= nact)
    def _():
        o_ref[...] = jnp.zeros_like(o_ref)


def _dispatch_call(src, n_act, h2f):
    return pl.pallas_call(
        _dispatch_kernel,
        grid_spec=pltpu.PrefetchScalarGridSpec(
            num_scalar_prefetch=2,
            grid=(NT_MOE,),
            in_specs=[pl.BlockSpec(memory_space=pl.ANY)],
            out_specs=pl.BlockSpec((TE, D), lambda t, s, na: (t, 0)),
            scratch_shapes=[pltpu.VMEM((2, TE, D), F32), pltpu.SemaphoreType.DMA((2,))],
        ),
        out_shape=jax.ShapeDtypeStruct((ROWS_MOE, D), BF16),
        compiler_params=_cparams(("arbitrary",)),
        name="moe_dispatch",
    )(src, n_act, h2f)


def _combine_kernel(pos_ref, yo_hbm, x_ref, mods_ref, o_ref, buf, sem):
    i = pl.program_id(0)
    n = pl.num_programs(0)

    def issue(tile, slot):
        def body(r, c):
            _row_copy(yo_hbm, pos_ref[tile * 2 * TM + r], buf, slot, r, sem).start()
            return c
        lax.fori_loop(0, 2 * TM, body, 0)

    @pl.when(i == 0)
    def _():
        issue(0, 0)

    @pl.when(i + 1 < n)
    def _():
        issue(i + 1, (i + 1) % 2)

    slot = i % 2

    def body(r, c):
        _row_copy(yo_hbm, 0, buf, slot, r, sem).wait()
        return c
    lax.fori_loop(0, 2 * TM, body, 0)
    g2 = mods_ref[pl.ds(_mod_row(i), 1), :][:, 5 * D:6 * D]
    y = buf[slot, 0:TM, :] + buf[slot, TM:2 * TM, :]
    o_ref[...] = x_ref[...] + g2 * y


def _combine_call(pos, yo, x, mods):
    return pl.pallas_call(
        _combine_kernel,
        grid_spec=pltpu.PrefetchScalarGridSpec(
            num_scalar_prefetch=1,
            grid=(N_TILES,),
            in_specs=[
                pl.BlockSpec(memory_space=pl.ANY),
                pl.BlockSpec((TM, D), lambda i, p: (i, 0)),
                pl.BlockSpec((MOD_ROWS, 6 * D), lambda i, p: (0, 0)),
            ],
            out_specs=pl.BlockSpec((TM, D), lambda i, p: (i, 0)),
            scratch_shapes=[pltpu.VMEM((2, 2 * TM, D), F32), pltpu.SemaphoreType.DMA((2,))],
        ),
        out_shape=jax.ShapeDtypeStruct((M_TOK, D), F32),
        compiler_params=_cparams(("arbitrary",)),
        name="moe_combine",
    )(pos, yo, x, mods)


def _route_plan(rinfo):
    idx = rinfo[:, 0:2].astype(jnp.int32)
    gates = rinfo[:, 2:4]
    flat_e = idx.reshape(-1)
    onehot = (flat_e[:, None] == jnp.arange(N_EXP)[None, :]).astype(jnp.int32)
    csum = jnp.cumsum(onehot, axis=0)
    rank = jnp.take_along_axis(csum, flat_e[:, None], axis=1)[:, 0] - 1
    counts = csum[-1]
    padded = ((counts + TE - 1) // TE) * TE
    ends = jnp.cumsum(padded)
    pos = (ends - padded)[flat_e] + rank
    src = jnp.zeros((ROWS_MOE,), jnp.int32).at[pos].set(jnp.arange(2 * M_TOK, dtype=jnp.int32) // 2)
    gate_row = jnp.zeros((ROWS_MOE,), F32).at[pos].set(gates.reshape(-1))
    n_act = (ends[-1] // TE).astype(jnp.int32)
    tile_start = jnp.arange(NT_MOE, dtype=jnp.int32) * TE
    tile_e = jnp.searchsorted(ends, jnp.minimum(tile_start, ends[-1] - 1), side="right")
    tile_e = jnp.minimum(tile_e, N_EXP - 1).astype(jnp.int32)
    pos_tiles = pos.reshape(N_TILES, TM, 2).transpose(0, 2, 1).reshape(-1).astype(jnp.int32)
    return src, gate_row.reshape(ROWS_MOE, 1), tile_e, n_act.reshape(1), pos_tiles


def _rope_tables():
    rows = DEC_SEQ // GRID_W
    row = jnp.repeat(jnp.arange(rows), GRID_W).astype(F32)
    col = jnp.tile(jnp.arange(GRID_W), rows).astype(F32)
    nf = ROPE // 4
    inv = ROPE_BASE ** (-jnp.arange(nf, dtype=F32) / nf)
    ar = row[:, None] * inv[None, :]
    ac = col[:, None] * inv[None, :]
    ang = jnp.concatenate([ar, ar, ac, ac], axis=-1)
    cos, sin = jnp.cos(ang), jnp.sin(ang)
    first = (jnp.arange(ROPE) % 32) < 16
    pad = lambda a, v: jnp.pad(a, ((0, 0), (0, LANE - ROPE)), constant_values=v)
    cos_p = pad(cos, 1.0)
    sin_a = pad(jnp.where(first[None, :], -sin, 0.0), 0.0)
    sin_b = pad(jnp.where(first[None, :], 0.0, sin), 0.0)
    ident = jnp.stack([jnp.ones((TM, LANE), F32), jnp.zeros((TM, LANE), F32), jnp.zeros((TM, LANE), F32)])
    return jnp.concatenate([ident, jnp.stack([cos_p, sin_a, sin_b])], axis=1)


def kernel(x_prompt, x_sample, cache_ckv, cache_krope, c, c_ctx, w_ada, b_ada, norm1_g, norm2_g, w_in,
           qlat_norm_g, kvlat_norm_g, w_uq, w_ukv, q_head_norm_g, k_head_norm_g, gm_norm_g, gm_w_s,
           gm_b_s, pool_w, pool_scale, w_o, ffn_w1, ffn_w3, ffn_w2, router_w, moe_w1, moe_w3, moe_w2):
    w_in_p = jnp.concatenate([
        w_in[:, :, :C_KR], w_in[:, :, C_KR:C_KR + ROPE], jnp.zeros((DEPTH, D, LANE - ROPE), F32),
        w_in[:, :, C_KR + ROPE:]], axis=-1).astype(BF16)
    w_uq_p = jnp.pad(w_uq.reshape(DEPTH, Q_RANK, N_HEADS, QK_DIM),
                     ((0, 0), (0, 0), (0, 0), (0, HEAD_PAD - QK_DIM))
                     ).reshape(DEPTH, Q_RANK, N_HEADS * HEAD_PAD).astype(BF16)
    w_ukv_b = w_ukv.astype(BF16)
    w_o_b = w_o.astype(BF16)
    qg_p = jnp.pad(q_head_norm_g * (1.0 / math.sqrt(QK_DIM)), ((0, 0), (0, HEAD_PAD - QK_DIM)))
    kg_n = k_head_norm_g[:, :NOPE].reshape(DEPTH, 1, LANE)
    kg_r = jnp.pad(k_head_norm_g[:, NOPE:], ((0, 0), (0, LANE - ROPE))).reshape(DEPTH, 1, LANE)
    gmw_b = gm_w_s.astype(BF16)
    gmb = jnp.broadcast_to(gm_b_s[:, :, :, None], (DEPTH, GM_GROUPS, CHUNK, LANE))
    pw_b = pool_w.astype(BF16)
    rw_p = jnp.pad(router_w, ((0, 0), (0, 0), (0, LANE - N_EXP)))
    n_moe = moe_w1.shape[0]
    moe_w1_flat = moe_w1.reshape(n_moe * N_EXP, D, D_FFE)
    moe_w3_flat = moe_w3.reshape(n_moe * N_EXP, D, D_FFE)
    moe_w2_flat = moe_w2.reshape(n_moe * N_EXP, D_FFE, D)
    rope_tab = _rope_tables()
    cache_kr_p = jnp.pad(cache_krope, ((0, 0), (0, 0), (0, 0), (0, LANE - ROPE)))

    cond = jnp.zeros((MOD_ROWS, D), F32).at[0].set(c_ctx).at[1:1 + DEC_BATCH].set(c)
    mods_all = _ada_call(cond, w_ada, b_ada)
    k_c, v_c = _cache_kv_call(cache_ckv, cache_kr_p, w_ukv_b, kg_n, kg_r)

    x = jnp.concatenate([x_prompt.reshape(M_P, D), x_sample.reshape(M_S, D)], axis=0)
    dense_tiles = jnp.zeros((NT_DENSE,), jnp.int32)
    dense_nact = jnp.full((1,), NT_DENSE, jnp.int32)
    ckv_out, kr_out = [], []
    for l in range(DEPTH):
        mods = mods_all[l]
        q, k, v, ckv, kr, zg, zp = _in_proj_call(
            x, mods, norm1_g[l].reshape(1, D), w_in_p[l], qlat_norm_g[l].reshape(1, Q_RANK),
            kvlat_norm_g[l].reshape(1, KV_RANK), w_uq_p[l], qg_p[l].reshape(1, HEAD_PAD), w_ukv_b[l],
            kg_n[l], kg_r[l], rope_tab, gm_norm_g[l], gmw_b[l], gmb[l])
        ckv_out.append(ckv[:M_P].reshape(BATCH, SEQ, KV_RANK))
        kr_out.append(kr[:M_P, :ROPE].reshape(BATCH, SEQ, ROPE))
        attn = _attn_call(q, k, v, k_c[l], v_c[l])
        n2g = norm2_g[l].reshape(1, D)
        ps = pool_scale[l].reshape(1, POOL_W)
        if l % 2 == 0:
            x, h2b = _out_proj_call(False, attn, zg, zp, x, mods, pw_b[l], ps, w_o_b[l], n2g)
            i = l // 2
            g = _up_call(dense_tiles + i, dense_nact, h2b, ffn_w1, ffn_w3, TF_DENSE, "ffn_up")
            x = _down_call(True, dense_tiles + i, dense_nact, g, ffn_w2, (x, mods), "ffn_down")
        else:
            i = l // 2
            x, h2b, h2f, rinfo = _out_proj_call(True, attn, zg, zp, x, mods, pw_b[l], ps, w_o_b[l],
                                                n2g, rw_p[i])
            src, gate_row, tile_e, n_act, pos_tiles = _route_plan(rinfo)
            hs = _dispatch_call(src, n_act, h2f)
            tile_w = tile_e + i * N_EXP
            g = _up_call(tile_w, n_act, hs, moe_w1_flat, moe_w3_flat, TF_MOE, "moe_up")
            yo = _down_call(False, tile_w, n_act, g, moe_w2_flat, (gate_row,), "moe_down")
            x = _combine_call(pos_tiles, yo, x, mods)

    y_prompt = x[:M_P].reshape(BATCH, SEQ, D)
    y_sample = x[M_P:].reshape(DEC_BATCH, DEC_SEQ, D)
    return (y_prompt, y_sample, jnp.stack(ckv_out, axis=1), jnp.stack(kr_out, axis=1))
```

```python
import functools
import math

import jax
import jax.numpy as jnp
from jax import lax
from jax.experimental import pallas as pl
from jax.experimental.pallas import tpu as pltpu

F32 = jnp.float32
BF16 = jnp.bfloat16
U32 = jnp.uint32
I32 = jnp.int32

D = 2048
BATCH, SEQ = 16, 256
DEC_BATCH, DEC_SEQ = 2, 1024
PAST = 256
DEPTH = 4
GRID_W = 64
EPS = 1e-6
N_HEADS = 8
NOPE, ROPE, VDIM = 128, 64, 128
QK_DIM = NOPE + ROPE
HEAD_PAD = 256
QK_W = N_HEADS * HEAD_PAD
Q_RANK, KV_RANK = 512, 256
ROPE_BASE = 10000.0
ATTN_W = N_HEADS * VDIM
CHUNK, GM_GROUPS, GM_W = 128, 4, 512
POOL_WINDOWS = (2, 4, 8, 16)
POOL_W = 512
POOL_HALO = 8
D_FF, N_EXP, D_FFE, TOP_K = 5632, 8, 2816, 2

C_Q, C_KV, C_KR, C_GM, C_PL = 0, 512, 768, 896, 1920
IN_PAD = 2432

LANE = 128
TM = 256
M_P, M_S = BATCH * SEQ, DEC_BATCH * DEC_SEQ
M_TOK = M_P + M_S
N_TILES = M_TOK // TM
P_TILES = M_P // TM
S_TILES = DEC_SEQ // TM
MOD_ROWS = 16
ADA_TN = 512
TE_UP, TF_UP = 1024, 512
TE_DOWN, TN_DOWN = 512, 512
HALF = D // 2
N_SLOTS = TOP_K * M_TOK
RT = 2048
N_ITEMS = N_EXP + N_SLOTS // RT
GC = 256
N_GC = RT // GC
MC = 512
TF_MOE = 256
TN_MOE = 256
VMEM_LIMIT = 56 * 1024 * 1024


def _cparams(sem):
    return pltpu.CompilerParams(dimension_semantics=sem, vmem_limit_bytes=VMEM_LIMIT)


def _const_spec(shape):
    nd = len(shape)
    return pl.BlockSpec(shape, lambda *_: (0,) * nd)


def _layer_spec(l, shape):
    nd = len(shape)
    return pl.BlockSpec((None,) + tuple(shape), lambda *_: (l,) + (0,) * nd)


def _mod_row(i):
    return jnp.where(i < P_TILES, 0, 1 + (i - P_TILES) // S_TILES)


def _rms(x, g, n):
    ms = jnp.sum(x * x, axis=-1, keepdims=True) * (1.0 / n)
    return x * lax.rsqrt(ms + EPS) * g


def _rope(r, cos, sin_a, sin_b):
    return r * cos + pltpu.roll(r, LANE - 16, 1) * sin_a + pltpu.roll(r, 16, 1) * sin_b


def _pack(lo, hi):
    lo_b = lax.bitcast_convert_type(lo.astype(BF16).astype(F32), U32)
    hi_b = lax.bitcast_convert_type(hi.astype(BF16).astype(F32), U32)
    return (lo_b >> 16) | hi_b


def _unpack(w):
    lo = lax.bitcast_convert_type(w << 16, F32)
    hi = lax.bitcast_convert_type(w & jnp.uint32(0xFFFF0000), F32)
    return lo, hi


def _ada_kernel(cond_ref, w_ref, b_ref, o_ref):
    c = cond_ref[...]
    s = c * jax.nn.sigmoid(c)
    s_hi = s.astype(BF16)
    s_lo = (s - s_hi.astype(F32)).astype(BF16)
    w = w_ref[...]
    w_hi = w.astype(BF16)
    w_lo = (w - w_hi.astype(F32)).astype(BF16)
    lhs = jnp.concatenate([s_hi, s_lo], axis=0)
    r1 = jnp.dot(lhs, w_hi, preferred_element_type=F32)
    r2 = jnp.dot(s_hi, w_lo, preferred_element_type=F32)
    o_ref[...] = r1[:MOD_ROWS] + r1[MOD_ROWS:] + r2 + b_ref[...]


def _ada_call(cond, w_ada, b_ada):
    n = 6 * D
    return pl.pallas_call(
        _ada_kernel,
        grid=(DEPTH, n // ADA_TN),
        in_specs=[
            pl.BlockSpec((MOD_ROWS, D), lambda l, j: (0, 0)),
            pl.BlockSpec((None, D, ADA_TN), lambda l, j: (l, 0, j)),
            pl.BlockSpec((None, 1, ADA_TN), lambda l, j: (l, 0, j)),
        ],
        out_specs=pl.BlockSpec((None, MOD_ROWS, ADA_TN), lambda l, j: (l, 0, j)),
        out_shape=jax.ShapeDtypeStruct((DEPTH, MOD_ROWS, n), F32),
        compiler_params=_cparams(("parallel", "parallel")),
        name="ada",
    )(cond, w_ada, b_ada.reshape(DEPTH, 1, n))


def _write_kv(ckv_b, kr, w_ukv_ref, kg_n, kg_r, rope, k_ref, v_ref):
    kv = jnp.dot(ckv_b, w_ukv_ref[...], preferred_element_type=F32)
    ssr = jnp.sum(kr * kr, axis=-1, keepdims=True)
    for h in range(N_HEADS):
        kn = kv[:, h * HEAD_PAD:h * HEAD_PAD + NOPE]
        vv = kv[:, h * HEAD_PAD + NOPE:(h + 1) * HEAD_PAD]
        ms = (jnp.sum(kn * kn, axis=-1, keepdims=True) + ssr) * (1.0 / QK_DIM)
        inv = lax.rsqrt(ms + EPS)
        r = kr * inv * kg_r
        if rope is not None:
            r = _rope(r, *rope)
        k_ref[:, h * HEAD_PAD:h * HEAD_PAD + NOPE] = (kn * inv * kg_n).astype(BF16)
        k_ref[:, h * HEAD_PAD + NOPE:(h + 1) * HEAD_PAD] = r.astype(BF16)
        v_ref[:, h * VDIM:(h + 1) * VDIM] = vv.astype(BF16)


def _cache_kv_kernel(ckv_ref, kr_ref, w_ukv_ref, kgn_ref, kgr_ref, k_ref, v_ref):
    _write_kv(ckv_ref[...].astype(BF16), kr_ref[...], w_ukv_ref, kgn_ref[...], kgr_ref[...],
              None, k_ref, v_ref)


def _cache_kv_call(cache_ckv, cache_kr_p, w_ukv_b, kg_n, kg_r):
    return pl.pallas_call(
        _cache_kv_kernel,
        grid=(DEPTH, DEC_BATCH),
        in_specs=[
            pl.BlockSpec((None, None, PAST, KV_RANK), lambda l, b: (b, l, 0, 0)),
            pl.BlockSpec((None, None, PAST, LANE), lambda l, b: (b, l, 0, 0)),
            pl.BlockSpec((None, KV_RANK, QK_W), lambda l, b: (l, 0, 0)),
            pl.BlockSpec((None, 1, LANE), lambda l, b: (l, 0, 0)),
            pl.BlockSpec((None, 1, LANE), lambda l, b: (l, 0, 0)),
        ],
        out_specs=[
            pl.BlockSpec((None, None, PAST, QK_W), lambda l, b: (l, b, 0, 0)),
            pl.BlockSpec((None, None, PAST, ATTN_W), lambda l, b: (l, b, 0, 0)),
        ],
        out_shape=[
            jax.ShapeDtypeStruct((DEPTH, DEC_BATCH, PAST, QK_W), BF16),
            jax.ShapeDtypeStruct((DEPTH, DEC_BATCH, PAST, ATTN_W), BF16),
        ],
        compiler_params=_cparams(("parallel", "parallel")),
        name="cache_kv",
    )(cache_ckv, cache_kr_p, w_ukv_b, kg_n, kg_r)


def _in_proj_kernel(x_ref, mods_ref, n1g_ref, w_in_ref, qlg_ref, kvlg_ref, w_uq_ref, qg_ref,
                    w_ukv_ref, kgn_ref, kgr_ref, rope_ref, gmg_ref, gmw_ref, gmb_ref,
                    q_ref, k_ref, v_ref, ckv_ref, kr_ref, zg_ref, zp_ref):
    i = pl.program_id(0)
    m = mods_ref[pl.ds(_mod_row(i), 1), :]
    sh1, sc1 = m[:, 0:D], m[:, D:2 * D]
    x = x_ref[...]
    h = _rms(x, n1g_ref[...], D) * (1.0 + sc1) + sh1
    z = jnp.dot(h.astype(BF16), w_in_ref[...], preferred_element_type=F32)
    rope = (rope_ref[0], rope_ref[1], rope_ref[2])

    ql = _rms(z[:, C_Q:C_Q + Q_RANK], qlg_ref[...], Q_RANK)
    qf = jnp.dot(ql.astype(BF16), w_uq_ref[...], preferred_element_type=F32)
    qg = qg_ref[...]
    for hd in range(N_HEADS):
        seg = qf[:, hd * HEAD_PAD:(hd + 1) * HEAD_PAD]
        seg = _rms(seg, qg, QK_DIM)
        q_ref[:, hd * HEAD_PAD:hd * HEAD_PAD + NOPE] = seg[:, :NOPE].astype(BF16)
        q_ref[:, hd * HEAD_PAD + NOPE:(hd + 1) * HEAD_PAD] = _rope(seg[:, NOPE:], *rope).astype(BF16)

    ckv = _rms(z[:, C_KV:C_KV + KV_RANK], kvlg_ref[...], KV_RANK)
    kr = z[:, C_KR:C_KR + LANE]
    ckv_ref[...] = ckv
    kr_ref[...] = kr
    _write_kv(ckv.astype(BF16), kr, w_ukv_ref, kgn_ref[...], kgr_ref[...], rope, k_ref, v_ref)

    zg = jax.nn.gelu(z[:, C_GM:C_GM + 2 * GM_W], approximate=True)
    for g in range(GM_GROUPS):
        u = zg[:, g * LANE:(g + 1) * LANE]
        vg = _rms(zg[:, GM_W + g * LANE:GM_W + (g + 1) * LANE], gmg_ref[g:g + 1, :], LANE)
        vb = vg.astype(BF16)
        for c in range(TM // CHUNK):
            sv = jnp.dot(gmw_ref[g], vb[c * CHUNK:(c + 1) * CHUNK, :], preferred_element_type=F32)
            sv = sv + gmb_ref[g]
            zg_ref[c * CHUNK:(c + 1) * CHUNK, g * LANE:(g + 1) * LANE] = (
                u[c * CHUNK:(c + 1) * CHUNK, :] * sv).astype(BF16)

    zp_ref[...] = z[:, C_PL:C_PL + POOL_W]


def _in_proj_call(l, x, mods_all, n1g, w_in_b, qlg, kvlg, w_uq_b, qg, w_ukv_b, kg_n, kg_r, rope_tab,
                  gmg, gmw_b, gmb):
    def rope_idx(i):
        return (0, jnp.where(i < P_TILES, 0, 1 + (i - P_TILES) % S_TILES), 0)

    row = lambda w: pl.BlockSpec((TM, w), lambda i: (i, 0))
    return pl.pallas_call(
        _in_proj_kernel,
        grid=(N_TILES,),
        in_specs=[
            row(D),
            _layer_spec(l, (MOD_ROWS, 6 * D)),
            _layer_spec(l, (1, D)),
            _layer_spec(l, (D, IN_PAD)),
            _layer_spec(l, (1, Q_RANK)),
            _layer_spec(l, (1, KV_RANK)),
            _layer_spec(l, (Q_RANK, QK_W)),
            _layer_spec(l, (1, HEAD_PAD)),
            _layer_spec(l, (KV_RANK, QK_W)),
            _layer_spec(l, (1, LANE)),
            _layer_spec(l, (1, LANE)),
            pl.BlockSpec((3, TM, LANE), rope_idx),
            _layer_spec(l, (GM_GROUPS, LANE)),
            _layer_spec(l, (GM_GROUPS, CHUNK, CHUNK)),
            _layer_spec(l, (GM_GROUPS, CHUNK, LANE)),
        ],
        out_specs=[row(QK_W), row(QK_W), row(ATTN_W), row(KV_RANK), row(LANE), row(GM_W), row(POOL_W)],
        out_shape=[
            jax.ShapeDtypeStruct((M_TOK, QK_W), BF16),
            jax.ShapeDtypeStruct((M_TOK, QK_W), BF16),
            jax.ShapeDtypeStruct((M_TOK, ATTN_W), BF16),
            jax.ShapeDtypeStruct((M_TOK, KV_RANK), F32),
            jax.ShapeDtypeStruct((M_TOK, LANE), F32),
            jax.ShapeDtypeStruct((M_TOK, GM_W), BF16),
            jax.ShapeDtypeStruct((M_TOK, POOL_W), F32),
        ],
        compiler_params=_cparams(("parallel",)),
        name="in_proj",
    )(x, mods_all, n1g, w_in_b, qlg, kvlg, w_uq_b, qg, w_ukv_b, kg_n, kg_r, rope_tab, gmg, gmw_b, gmb)


def _attend_heads(q_ref, kv_pairs, o_ref):
    nt = (((1,), (1,)), ((), ()))
    for h in range(N_HEADS):
        qh = q_ref[:, h * HEAD_PAD:(h + 1) * HEAD_PAD]
        ss = [lax.dot_general(qh, k_ref[:, h * HEAD_PAD:(h + 1) * HEAD_PAD], nt,
                              preferred_element_type=F32) for k_ref, _ in kv_pairs]
        mx = ss[0].max(axis=-1, keepdims=True)
        for s in ss[1:]:
            mx = jnp.maximum(mx, s.max(axis=-1, keepdims=True))
        den = None
        acc = None
        for s, (_, v_ref) in zip(ss, kv_pairs):
            p = jnp.exp(s - mx)
            d = jnp.sum(p, axis=-1, keepdims=True)
            a = jnp.dot(p.astype(BF16), v_ref[:, h * VDIM:(h + 1) * VDIM], preferred_element_type=F32)
            den = d if den is None else den + d
            acc = a if acc is None else acc + a
        o_ref[:, h * VDIM:(h + 1) * VDIM] = (acc / den).astype(BF16)


def _attn_kernel(q_ref, kp_ref, vp_ref, ks_ref, vs_ref, kc_ref, vc_ref, o_ref):
    i = pl.program_id(0)

    @pl.when(i < P_TILES)
    def _():
        _attend_heads(q_ref, [(kp_ref, vp_ref)], o_ref)

    @pl.when(i >= P_TILES)
    def _():
        _attend_heads(q_ref, [(ks_ref, vs_ref), (kc_ref, vc_ref)], o_ref)


def _attn_call(l, q, k, v, k_c, v_c):
    def seq_b(i):
        return jnp.maximum(i - P_TILES, 0) // S_TILES

    small = lambda w: pl.BlockSpec((TM, w), lambda i: (jnp.minimum(i, P_TILES - 1), 0))
    big = lambda w: pl.BlockSpec((DEC_SEQ, w), lambda i: (M_P // DEC_SEQ + seq_b(i), 0))
    cache = lambda w: pl.BlockSpec((None, None, PAST, w), lambda i: (l, seq_b(i), 0, 0))
    return pl.pallas_call(
        _attn_kernel,
        grid=(N_TILES,),
        in_specs=[
            pl.BlockSpec((TM, QK_W), lambda i: (i, 0)),
            small(QK_W), small(ATTN_W),
            big(QK_W), big(ATTN_W),
            cache(QK_W), cache(ATTN_W),
        ],
        out_specs=pl.BlockSpec((TM, ATTN_W), lambda i: (i, 0)),
        out_shape=jax.ShapeDtypeStruct((M_TOK, ATTN_W), BF16),
        compiler_params=_cparams(("parallel",)),
        name="attention",
    )(q, k, v, k, v, k_c, v_c)


def _out_proj_kernel(moe, attn_ref, zg_ref, zprev_ref, zcur_ref, znext_ref, x_ref, mods_ref,
                     pw_ref, ps_ref, w_o_ref, n2g_ref, *rest):
    if moe:
        rw_ref, xo_ref, hp_ref, rinfo_ref, ext_ref, mix_ref = rest
    else:
        xo_ref, h2b_ref, ext_ref, mix_ref = rest
    i = pl.program_id(0)
    is_ctx = i < P_TILES
    t_in_seq = jnp.where(is_ctx, 0, (i - P_TILES) % S_TILES)
    n_tiles_seq = jnp.where(is_ctx, 1, S_TILES)
    first = t_in_seq == 0
    last = t_in_seq == n_tiles_seq - 1
    seq_len = n_tiles_seq * TM

    zero_halo = jnp.zeros((POOL_HALO, POOL_W), F32)
    ext_ref[0:POOL_HALO, :] = jnp.where(first, zero_halo, zprev_ref[TM - POOL_HALO:, :])
    ext_ref[POOL_HALO:POOL_HALO + TM, :] = zcur_ref[...]
    ext_ref[POOL_HALO + TM:, :] = jnp.where(last, zero_halo, znext_ref[0:POOL_HALO, :])

    mix_ref[:, 0:ATTN_W] = attn_ref[...]
    mix_ref[:, ATTN_W:ATTN_W + GM_W] = zg_ref[...]
    pos = t_in_seq * TM + lax.broadcasted_iota(I32, (TM, 1), 0)
    for gi, w in enumerate(POOL_WINDOWS):
        cols = slice(gi * LANE, (gi + 1) * LANE)
        win = None
        for dlt in range(-(w // 2), w - w // 2):
            part = ext_ref[POOL_HALO + dlt:POOL_HALO + dlt + TM, cols]
            win = part if win is None else win + part
        lo = jnp.maximum(pos - w // 2, 0)
        hi = jnp.minimum(pos + (w - 1 - w // 2), seq_len - 1)
        cnt = (hi - lo + 1).astype(F32)
        pooled = win / cnt - zcur_ref[:, cols]
        mixed = jnp.dot(pooled.astype(BF16), pw_ref[gi], preferred_element_type=F32)
        mix_ref[:, ATTN_W + GM_W + gi * LANE:ATTN_W + GM_W + (gi + 1) * LANE] = (
            mixed * ps_ref[:, cols]).astype(BF16)

    out = jnp.dot(mix_ref[...], w_o_ref[...], preferred_element_type=F32)
    m = mods_ref[pl.ds(_mod_row(i), 1), :]
    g1, sh2, sc2 = m[:, 2 * D:3 * D], m[:, 3 * D:4 * D], m[:, 4 * D:5 * D]
    xn = x_ref[...] + g1 * out
    xo_ref[...] = xn
    h2 = _rms(xn, n2g_ref[...], D) * (1.0 + sc2) + sh2

    if not moe:
        h2b_ref[...] = h2.astype(BF16)
        return

    hp_ref[...] = _pack(h2[:, :HALF], h2[:, HALF:])
    h_hi = h2.astype(BF16)
    h_lo = (h2 - h_hi.astype(F32)).astype(BF16)
    rw = rw_ref[...]
    w_hi = rw.astype(BF16)
    w_lo = (rw - w_hi.astype(F32)).astype(BF16)
    logits = (jnp.dot(h_hi, w_hi, preferred_element_type=F32)
              + jnp.dot(h_lo, w_hi, preferred_element_type=F32)
              + jnp.dot(h_hi, w_lo, preferred_element_type=F32))
    lane = lax.broadcasted_iota(I32, (TM, LANE), 1).astype(F32)
    neg = jnp.float32(-jnp.inf)
    logits = jnp.where(lane < N_EXP, logits, neg)
    v1 = logits.max(axis=-1, keepdims=True)
    i1 = jnp.where(logits == v1, lane, float(LANE)).min(axis=-1, keepdims=True)
    l2 = jnp.where(lane == i1, neg, logits)
    v2 = l2.max(axis=-1, keepdims=True)
    i2 = jnp.where(l2 == v2, lane, float(LANE)).min(axis=-1, keepdims=True)
    e2 = jnp.exp(v2 - v1)
    gate1 = 1.0 / (1.0 + e2)
    gate2 = e2 / (1.0 + e2)
    rinfo_ref[...] = jnp.where(lane == 0, i1, jnp.where(lane == 1, i2, jnp.where(
        lane == 2, gate1, jnp.where(lane == 3, gate2, 0.0))))


def _out_proj_call(l, moe, attn, zg, zp, x, mods_all, pw_b, ps, w_o_b, n2g, rw_p=None):
    row = lambda w: pl.BlockSpec((TM, w), lambda i: (i, 0))
    in_specs = [
        row(ATTN_W), row(GM_W),
        pl.BlockSpec((TM, POOL_W), lambda i: (jnp.maximum(i - 1, 0), 0)),
        row(POOL_W),
        pl.BlockSpec((TM, POOL_W), lambda i: (jnp.minimum(i + 1, N_TILES - 1), 0)),
        row(D),
        _layer_spec(l, (MOD_ROWS, 6 * D)),
        _layer_spec(l, (len(POOL_WINDOWS), LANE, LANE)),
        _layer_spec(l, (1, POOL_W)),
        _layer_spec(l, (D, D)),
        _layer_spec(l, (1, D)),
    ]
    args = [attn, zg, zp, zp, zp, x, mods_all, pw_b, ps, w_o_b, n2g]
    if moe:
        in_specs.append(_layer_spec(l // 2, (D, LANE)))
        args.append(rw_p)
        out_specs = [row(D), row(HALF), row(LANE)]
        out_shape = [jax.ShapeDtypeStruct((M_TOK, D), F32), jax.ShapeDtypeStruct((M_TOK, HALF), U32),
                     jax.ShapeDtypeStruct((M_TOK, LANE), F32)]
    else:
        out_specs = [row(D), row(D)]
        out_shape = [jax.ShapeDtypeStruct((M_TOK, D), F32), jax.ShapeDtypeStruct((M_TOK, D), BF16)]
    return pl.pallas_call(
        functools.partial(_out_proj_kernel, moe),
        grid=(N_TILES,),
        in_specs=in_specs,
        out_specs=out_specs,
        out_shape=out_shape,
        scratch_shapes=[pltpu.VMEM((TM + 2 * POOL_HALO, POOL_W), F32), pltpu.VMEM((TM, D), BF16)],
        compiler_params=_cparams(("parallel",)),
        name="out_proj_moe" if moe else "out_proj",
    )(*args)


def _ffn_up_kernel(h_ref, w1_ref, w3_ref, g_ref, w1b_ref, w3b_ref):
    @pl.when(pl.program_id(1) == 0)
    def _():
        w1b_ref[...] = w1_ref[...].astype(BF16)
        w3b_ref[...] = w3_ref[...].astype(BF16)

    h = h_ref[...]
    a = jnp.dot(h, w1b_ref[...], preferred_element_type=F32)
    b = jnp.dot(h, w3b_ref[...], preferred_element_type=F32)
    g_ref[...] = (a * jax.nn.sigmoid(a) * b).astype(BF16)


def _ffn_up_call(i, h, w1, w3):
    wspec = pl.BlockSpec((None, D, TF_UP), lambda j, t: (i, 0, j))
    return pl.pallas_call(
        _ffn_up_kernel,
        grid=(D_FF // TF_UP, M_TOK // TE_UP),
        in_specs=[pl.BlockSpec((TE_UP, D), lambda j, t: (t, 0)), wspec, wspec],
        out_specs=pl.BlockSpec((TE_UP, TF_UP), lambda j, t: (t, j)),
        out_shape=jax.ShapeDtypeStruct((M_TOK, D_FF), BF16),
        scratch_shapes=[pltpu.VMEM((D, TF_UP), BF16), pltpu.VMEM((D, TF_UP), BF16)],
        compiler_params=_cparams(("arbitrary", "arbitrary")),
        name="ffn_up",
    )(h, w1, w3)


def _ffn_down_kernel(g_ref, w2_ref, x_ref, mods_ref, o_ref, w2b_ref):
    t = pl.program_id(1)

    @pl.when(t == 0)
    def _():
        w2b_ref[...] = w2_ref[...].astype(BF16)

    y = jnp.dot(g_ref[...], w2b_ref[...], preferred_element_type=F32)
    ctx_tiles = M_P // TE_DOWN
    row = jnp.where(t < ctx_tiles, 0, 1 + (t - ctx_tiles) // (DEC_SEQ // TE_DOWN))
    o_ref[...] = x_ref[...] + mods_ref[pl.ds(row, 1), :] * y


def _ffn_down_call(l, g, w2, x, mods_all):
    i = l // 2
    return pl.pallas_call(
        _ffn_down_kernel,
        grid=(D // TN_DOWN, M_TOK // TE_DOWN),
        in_specs=[
            pl.BlockSpec((TE_DOWN, D_FF), lambda j, t: (t, 0)),
            pl.BlockSpec((None, D_FF, TN_DOWN), lambda j, t: (i, 0, j)),
            pl.BlockSpec((TE_DOWN, TN_DOWN), lambda j, t: (t, j)),
            pl.BlockSpec((None, MOD_ROWS, TN_DOWN), lambda j, t: (l, 0, 5 * (D // TN_DOWN) + j)),
        ],
        out_specs=pl.BlockSpec((TE_DOWN, TN_DOWN), lambda j, t: (t, j)),
        out_shape=jax.ShapeDtypeStruct((M_TOK, D), F32),
        scratch_shapes=[pltpu.VMEM((D_FF, TN_DOWN), BF16)],
        compiler_params=_cparams(("arbitrary", "arbitrary")),
        name="ffn_down",
    )(g, w2, x, mods_all)


def _src_kernel(cpos_ref, src_ref):
    def body(i, c):
        src_ref[cpos_ref[i]] = i // TOP_K
        return c
    lax.fori_loop(0, N_SLOTS, body, 0, unroll=8)


def _src_call(cpos):
    return pl.pallas_call(
        _src_kernel,
        in_specs=[pl.BlockSpec(memory_space=pltpu.SMEM)],
        out_specs=pl.BlockSpec(memory_space=pltpu.SMEM),
        out_shape=jax.ShapeDtypeStruct((N_SLOTS,), I32),
        name="moe_src",
    )(cpos)


def _gather_copy(hp_hbm, tok, stage, slot, r, sem):
    return pltpu.make_async_copy(hp_hbm.at[pl.ds(tok, 1)], stage.at[slot, pl.ds(r, 1)], sem.at[slot])


def _moe_up_kernel(src_ref, ie_ref, ics_ref, icnt_ref, nit_ref, hp_hbm, w1_ref, w3_ref, g_ref,
                   buf, stage, w1b_ref, w3b_ref, sem):
    k = pl.program_id(0)
    j = pl.program_id(1)
    nit = nit_ref[0]

    def chunk_rows(item, c):
        return jnp.clip(icnt_ref[item] - c * GC, 0, GC)

    def issue_chunk(item, c, slot):
        base = ics_ref[item] + c * GC

        def body(r, carry):
            _gather_copy(hp_hbm, src_ref[base + r], stage, slot, r, sem).start()
            return carry
        lax.fori_loop(0, chunk_rows(item, c), body, 0)

    def finish_chunk(item, c, slot, b):
        def body(r, carry):
            _gather_copy(hp_hbm, 0, stage, slot, r, sem).wait()
            return carry
        lax.fori_loop(0, chunk_rows(item, c), body, 0)
        lo, hi = _unpack(stage[slot])
        rows = pl.ds(pl.multiple_of(c * GC, GC), GC)
        buf[b, rows, 0:HALF] = lo.astype(BF16)
        buf[b, rows, HALF:D] = hi.astype(BF16)

    @pl.when((k == 0) & (j == 0))
    def _():
        buf[...] = jnp.zeros_like(buf)
        stage[...] = jnp.zeros_like(stage)
        issue_chunk(0, 0, 0)

        def body(c, carry):
            @pl.when(c + 1 < N_GC)
            def _():
                issue_chunk(0, c + 1, (c + 1) % 2)
            finish_chunk(0, c, c % 2, 0)
            return carry
        lax.fori_loop(0, N_GC, body, 0)

    nxt = k + 1

    @pl.when((nxt < nit) & (j >= 1) & (j <= N_GC))
    def _():
        finish_chunk(nxt, j - 1, (j - 1) % 2, nxt % 2)

    @pl.when((nxt < nit) & (j < N_GC))
    def _():
        issue_chunk(nxt, j, j % 2)

    g_ref[...] = jnp.zeros_like(g_ref)

    @pl.when(k < nit)
    def _():
        w1b_ref[...] = w1_ref[...].astype(BF16)
        w3b_ref[...] = w3_ref[...].astype(BF16)
        b = k % 2

        def body(c, carry):
            rows = pl.ds(pl.multiple_of(c * MC, MC), MC)
            h = buf[b, rows, :]
            a = jnp.dot(h, w1b_ref[...], preferred_element_type=F32)
            bb = jnp.dot(h, w3b_ref[...], preferred_element_type=F32)
            g_ref[rows, :] = (a * jax.nn.sigmoid(a) * bb).astype(BF16)
            return carry
        lax.fori_loop(0, (icnt_ref[k] + MC - 1) // MC, body, 0)


def _moe_up_call(layer_e0, src, item_e, item_cs, item_cnt, n_items, hp, w1, w3):
    n_f = D_FFE // TF_MOE
    wspec = pl.BlockSpec((None, D, TF_MOE), lambda k, j, s, ie, ics, ic, ni: (
        layer_e0 + ie[k], 0, jnp.where(k < ni[0], j, n_f - 1)))
    return pl.pallas_call(
        _moe_up_kernel,
        grid_spec=pltpu.PrefetchScalarGridSpec(
            num_scalar_prefetch=5,
            grid=(N_ITEMS, D_FFE // TF_MOE),
            in_specs=[pl.BlockSpec(memory_space=pl.ANY), wspec, wspec],
            out_specs=pl.BlockSpec((RT, TF_MOE), lambda k, j, s, ie, ics, ic, ni: (k, j)),
            scratch_shapes=[
                pltpu.VMEM((2, RT, D), BF16),
                pltpu.VMEM((2, GC, HALF), U32),
                pltpu.VMEM((D, TF_MOE), BF16),
                pltpu.VMEM((D, TF_MOE), BF16),
                pltpu.SemaphoreType.DMA((2,)),
            ],
        ),
        out_shape=jax.ShapeDtypeStruct((N_ITEMS * RT, D_FFE), BF16),
        compiler_params=_cparams(("arbitrary", "arbitrary")),
        name="moe_up",
    )(src, item_e, item_cs, item_cnt, n_items, hp, w1, w3)


def _moe_down_kernel(ie_ref, icnt_ref, nit_ref, g_ref, w2a_ref, w2b_ref, o_ref, wa_ref, wb_ref):
    k = pl.program_id(0)
    o_ref[...] = jnp.zeros_like(o_ref)

    @pl.when(k < nit_ref[0])
    def _():
        wa_ref[...] = w2a_ref[...].astype(BF16)
        wb_ref[...] = w2b_ref[...].astype(BF16)

        def body(c, carry):
            rows = pl.ds(pl.multiple_of(c * MC, MC), MC)
            g = g_ref[rows, :]
            lo = jnp.dot(g, wa_ref[...], preferred_element_type=F32)
            hi = jnp.dot(g, wb_ref[...], preferred_element_type=F32)
            o_ref[rows, :] = _pack(lo, hi)
            return carry
        lax.fori_loop(0, (icnt_ref[k] + MC - 1) // MC, body, 0)


def _moe_down_call(layer_e0, item_e, item_cnt, n_items, g, w2):
    live = lambda k, ni: jnp.minimum(k, ni[0] - 1)
    n_half = HALF // TN_MOE
    col = lambda k, n, ni: jnp.where(k < ni[0], n, n_half - 1)
    return pl.pallas_call(
        _moe_down_kernel,
        grid_spec=pltpu.PrefetchScalarGridSpec(
            num_scalar_prefetch=3,
            grid=(N_ITEMS, n_half),
            in_specs=[
                pl.BlockSpec((RT, D_FFE), lambda k, n, ie, ic, ni: (live(k, ni), 0)),
                pl.BlockSpec((None, D_FFE, TN_MOE),
                             lambda k, n, ie, ic, ni: (layer_e0 + ie[k], 0, col(k, n, ni))),
                pl.BlockSpec((None, D_FFE, TN_MOE),
                             lambda k, n, ie, ic, ni: (layer_e0 + ie[k], 0, n_half + col(k, n, ni))),
            ],
            out_specs=pl.BlockSpec((RT, TN_MOE), lambda k, n, ie, ic, ni: (k, n)),
            scratch_shapes=[pltpu.VMEM((D_FFE, TN_MOE), BF16), pltpu.VMEM((D_FFE, TN_MOE), BF16)],
        ),
        out_shape=jax.ShapeDtypeStruct((N_ITEMS * RT, HALF), U32),
        compiler_params=_cparams(("arbitrary", "arbitrary")),
        name="moe_down",
    )(item_e, item_cnt, n_items, g, w2, w2)


def _combine_copy(yo_hbm, row, buf, slot, r, sem):
    return pltpu.make_async_copy(yo_hbm.at[pl.ds(row, 1)], buf.at[slot, pl.ds(r, 1)], sem.at[slot])


def _combine_kernel(pos_ref, yo_hbm, x_ref, rinfo_ref, mods_ref, o_ref, buf, sem):
    i = pl.program_id(0)
    n = pl.num_programs(0)
    rows = TOP_K * TM

    def issue(tile, slot):
        def body(r, c):
            _combine_copy(yo_hbm, pos_ref[tile * rows + r], buf, slot, r, sem).start()
            return c
        lax.fori_loop(0, rows, body, 0, unroll=8)

    @pl.when(i == 0)
    def _():
        issue(0, 0)

    @pl.when(i + 1 < n)
    def _():
        issue(i + 1, (i + 1) % 2)

    slot = i % 2

    def body(r, c):
        _combine_copy(yo_hbm, 0, buf, slot, r, sem).wait()
        return c
    lax.fori_loop(0, rows, body, 0, unroll=8)
    rinfo = rinfo_ref[...]
    lo0, hi0 = _unpack(buf[slot, 0:TM, :])
    lo1, hi1 = _unpack(buf[slot, TM:2 * TM, :])
    gt0, gt1 = rinfo[:, 2:3], rinfo[:, 3:4]
    g2 = mods_ref[pl.ds(_mod_row(i), 1), :]
    o_ref[:, 0:HALF] = x_ref[:, 0:HALF] + g2[:, 0:HALF] * (gt0 * lo0 + gt1 * lo1)
    o_ref[:, HALF:D] = x_ref[:, HALF:D] + g2[:, HALF:D] * (gt0 * hi0 + gt1 * hi1)


def _combine_call(l, pos, yo, x, rinfo, mods_all):
    return pl.pallas_call(
        _combine_kernel,
        grid_spec=pltpu.PrefetchScalarGridSpec(
            num_scalar_prefetch=1,
            grid=(N_TILES,),
            in_specs=[
                pl.BlockSpec(memory_space=pl.ANY),
                pl.BlockSpec((TM, D), lambda i, p: (i, 0)),
                pl.BlockSpec((TM, LANE), lambda i, p: (i, 0)),
                pl.BlockSpec((None, MOD_ROWS, D), lambda i, p: (l, 0, 5)),
            ],
            out_specs=pl.BlockSpec((TM, D), lambda i, p: (i, 0)),
            scratch_shapes=[pltpu.VMEM((2, TOP_K * TM, HALF), U32), pltpu.SemaphoreType.DMA((2,))],
        ),
        out_shape=jax.ShapeDtypeStruct((M_TOK, D), F32),
        compiler_params=_cparams(("arbitrary",)),
        name="moe_combine",
    )(pos, yo, x, rinfo, mods_all)


def _route_plan(rinfo):
    flat_e = rinfo[:, 0:TOP_K].astype(I32).reshape(-1)
    onehot = (flat_e[:, None] == jnp.arange(N_EXP, dtype=I32)[None, :]).astype(I32)
    csum = jnp.cumsum(onehot, axis=0)
    pick = lambda table: jnp.sum(onehot * table[None, :], axis=1)
    rank = jnp.sum(onehot * csum, axis=1) - 1
    counts = csum[-1]
    cstart = jnp.cumsum(counts) - counts
    n_it = (counts + RT - 1) // RT
    it_end = jnp.cumsum(n_it)
    it_base = it_end - n_it
    n_items = it_end[-1]
    cpos = pick(cstart) + rank
    ppos = (pick(it_base) + rank // RT) * RT + rank % RT
    q = jnp.arange(N_ITEMS, dtype=I32)
    qc = jnp.minimum(q, n_items - 1)
    qe = jnp.minimum(jnp.sum((it_end[None, :] <= qc[:, None]).astype(I32), axis=1), N_EXP - 1)
    qhot = (qe[:, None] == jnp.arange(N_EXP, dtype=I32)[None, :]).astype(I32)
    qpick = lambda table: jnp.sum(qhot * table[None, :], axis=1)
    s = qc - qpick(it_base)
    item_cnt = jnp.where(q < n_items, jnp.clip(qpick(counts) - s * RT, 0, RT), 0)
    item_cs = qpick(cstart) + s * RT
    pos_tiles = ppos.reshape(N_TILES, TM, TOP_K).transpose(0, 2, 1).reshape(-1)
    return (cpos.astype(I32), qe.astype(I32), item_cs.astype(I32), item_cnt.astype(I32),
            n_items.reshape(1).astype(I32), pos_tiles.astype(I32))


def _rope_tables():
    rows = DEC_SEQ // GRID_W
    row = jnp.repeat(jnp.arange(rows), GRID_W).astype(F32)
    col = jnp.tile(jnp.arange(GRID_W), rows).astype(F32)
    nf = ROPE // 4
    inv = ROPE_BASE ** (-jnp.arange(nf, dtype=F32) / nf)
    ar = row[:, None] * inv[None, :]
    ac = col[:, None] * inv[None, :]
    ang = jnp.concatenate([ar, ar, ac, ac], axis=-1)
    cos, sin = jnp.cos(ang), jnp.sin(ang)
    first = (jnp.arange(ROPE) % 32) < 16
    pad = lambda a, v: jnp.pad(a, ((0, 0), (0, LANE - ROPE)), constant_values=v)
    cos_p = pad(cos, 1.0)
    sin_a = pad(jnp.where(first[None, :], -sin, 0.0), 0.0)
    sin_b = pad(jnp.where(first[None, :], 0.0, sin), 0.0)
    ident = jnp.stack([jnp.ones((TM, LANE), F32), jnp.zeros((TM, LANE), F32), jnp.zeros((TM, LANE), F32)])
    return jnp.concatenate([ident, jnp.stack([cos_p, sin_a, sin_b])], axis=1)


def kernel(x_prompt, x_sample, cache_ckv, cache_krope, c, c_ctx, w_ada, b_ada, norm1_g, norm2_g, w_in,
           qlat_norm_g, kvlat_norm_g, w_uq, w_ukv, q_head_norm_g, k_head_norm_g, gm_norm_g, gm_w_s,
           gm_b_s, pool_w, pool_scale, w_o, ffn_w1, ffn_w3, ffn_w2, router_w, moe_w1, moe_w3, moe_w2):
    w_in_p = jnp.concatenate([
        w_in[:, :, :C_KR], w_in[:, :, C_KR:C_KR + ROPE], jnp.zeros((DEPTH, D, LANE - ROPE), F32),
        w_in[:, :, C_KR + ROPE:]], axis=-1).astype(BF16)
    w_uq_p = jnp.pad(w_uq.reshape(DEPTH, Q_RANK, N_HEADS, QK_DIM),
                     ((0, 0), (0, 0), (0, 0), (0, HEAD_PAD - QK_DIM))
                     ).reshape(DEPTH, Q_RANK, QK_W).astype(BF16)
    w_ukv_b = w_ukv.astype(BF16)
    w_o_b = w_o.astype(BF16)
    qg_p = jnp.pad(q_head_norm_g * (1.0 / math.sqrt(QK_DIM)),
                   ((0, 0), (0, HEAD_PAD - QK_DIM))).reshape(DEPTH, 1, HEAD_PAD)
    kg_n = k_head_norm_g[:, :NOPE].reshape(DEPTH, 1, LANE)
    kg_r = jnp.pad(k_head_norm_g[:, NOPE:], ((0, 0), (0, LANE - ROPE))).reshape(DEPTH, 1, LANE)
    gmw_b = gm_w_s.astype(BF16)
    gmb = jnp.broadcast_to(gm_b_s[:, :, :, None], (DEPTH, GM_GROUPS, CHUNK, LANE))
    pw_b = pool_w.astype(BF16)
    ps = pool_scale.reshape(DEPTH, 1, POOL_W)
    n1g = norm1_g.reshape(DEPTH, 1, D)
    n2g = norm2_g.reshape(DEPTH, 1, D)
    qlg = qlat_norm_g.reshape(DEPTH, 1, Q_RANK)
    kvlg = kvlat_norm_g.reshape(DEPTH, 1, KV_RANK)
    rw_p = jnp.pad(router_w, ((0, 0), (0, 0), (0, LANE - N_EXP)))
    n_moe = moe_w1.shape[0]
    moe_w1_flat = moe_w1.reshape(n_moe * N_EXP, D, D_FFE)
    moe_w3_flat = moe_w3.reshape(n_moe * N_EXP, D, D_FFE)
    moe_w2_flat = moe_w2.reshape(n_moe * N_EXP, D_FFE, D)
    rope_tab = _rope_tables()
    cache_kr_p = jnp.pad(cache_krope, ((0, 0), (0, 0), (0, 0), (0, LANE - ROPE)))

    cond = jnp.zeros((MOD_ROWS, D), F32).at[0].set(c_ctx).at[1:1 + DEC_BATCH].set(c)
    mods_all = _ada_call(cond, w_ada, b_ada)
    k_c, v_c = _cache_kv_call(cache_ckv, cache_kr_p, w_ukv_b, kg_n, kg_r)

    x = jnp.concatenate([x_prompt.reshape(M_P, D), x_sample.reshape(M_S, D)], axis=0)
    ckv_out, kr_out = [], []
    for l in range(DEPTH):
        q, k, v, ckv, kr, zg, zp = _in_proj_call(
            l, x, mods_all, n1g, w_in_p, qlg, kvlg, w_uq_p, qg_p, w_ukv_b, kg_n, kg_r, rope_tab,
            gm_norm_g, gmw_b, gmb)
        ckv_out.append(ckv[:M_P].reshape(BATCH, SEQ, KV_RANK))
        kr_out.append(kr[:M_P, :ROPE].reshape(BATCH, SEQ, ROPE))
        attn = _attn_call(l, q, k, v, k_c, v_c)
        i = l // 2
        if l % 2 == 0:
            x, h2b = _out_proj_call(l, False, attn, zg, zp, x, mods_all, pw_b, ps, w_o_b, n2g)
            g = _ffn_up_call(i, h2b, ffn_w1, ffn_w3)
            x = _ffn_down_call(l, g, ffn_w2, x, mods_all)
        else:
            x, hp, rinfo = _out_proj_call(l, True, attn, zg, zp, x, mods_all, pw_b, ps, w_o_b, n2g, rw_p)
            cpos, item_e, item_cs, item_cnt, n_items, pos_tiles = _route_plan(rinfo)
            src = _src_call(cpos)
            g = _moe_up_call(i * N_EXP, src, item_e, item_cs, item_cnt, n_items, hp,
                             moe_w1_flat, moe_w3_flat)
            yo = _moe_down_call(i * N_EXP, item_e, item_cnt, n_items, g, moe_w2_flat)
            x = _combine_call(l, pos_tiles, yo, x, rinfo, mods_all)

    y_prompt = x[:M_P].reshape(BATCH, SEQ, D)
    y_sample = x[M_P:].reshape(DEC_BATCH, DEC_SEQ, D)
    return (y_prompt, y_sample, jnp.stack(ckv_out, axis=1), jnp.stack(kr_out, axis=1))
```

```python
import functools
import math

import jax
import jax.numpy as jnp
from jax import lax
from jax.experimental import pallas as pl
from jax.experimental.pallas import tpu as pltpu

F32 = jnp.float32
BF16 = jnp.bfloat16
U32 = jnp.uint32
I32 = jnp.int32

D = 2048
BATCH, SEQ = 16, 256
DEC_BATCH, DEC_SEQ = 2, 1024
PAST = 256
DEPTH = 4
GRID_W = 64
EPS = 1e-6
N_HEADS = 8
NOPE, ROPE, VDIM = 128, 64, 128
QK_DIM = NOPE + ROPE
HEAD_PAD = 256
QK_W = N_HEADS * HEAD_PAD
Q_RANK, KV_RANK = 512, 256
ROPE_BASE = 10000.0
ATTN_W = N_HEADS * VDIM
CHUNK, GM_GROUPS, GM_W = 128, 4, 512
POOL_WINDOWS = (2, 4, 8, 16)
POOL_W = 512
POOL_HALO = 8
D_FF, N_EXP, D_FFE, TOP_K = 5632, 8, 2816, 2
TOP_K_SHIFT = 1

C_Q, C_KV, C_KR, C_GM, C_PL = 0, 512, 768, 896, 1920
IN_PAD = 2432

LANE = 128
TM = 256
TMI = 512
M_P, M_S = BATCH * SEQ, DEC_BATCH * DEC_SEQ
M_TOK = M_P + M_S
N_TILES = M_TOK // TM
P_TILES = M_P // TM
S_TILES = DEC_SEQ // TM
MOD_ROWS = 16
ADA_TN = 512
TE_UP, TF_UP = 1024, 512
TE_DOWN, TN_DOWN = 512, 512
HALF = D // 2
N_SLOTS = TOP_K * M_TOK
RT = 2048
N_ITEMS = N_EXP + N_SLOTS // RT
GC = 256
N_GC = RT // GC
MC = 512
TF_MOE = 256
TN_MOE = 256
VMEM_LIMIT = 56 * 1024 * 1024


def _cparams(sem):
    return pltpu.CompilerParams(dimension_semantics=sem, vmem_limit_bytes=VMEM_LIMIT)


def _const_spec(shape):
    nd = len(shape)
    return pl.BlockSpec(shape, lambda *_: (0,) * nd)


def _layer_spec(l, shape):
    nd = len(shape)
    return pl.BlockSpec((None,) + tuple(shape), lambda *_: (l,) + (0,) * nd)


def _mod_row(i):
    return jnp.where(i < P_TILES, 0, 1 + (i - P_TILES) // S_TILES)


def _rms(x, g, n):
    ms = jnp.sum(x * x, axis=-1, keepdims=True) * (1.0 / n)
    return x * lax.rsqrt(ms + EPS) * g


def _rope(r, cos, sin_a, sin_b):
    return r * cos + pltpu.roll(r, LANE - 16, 1) * sin_a + pltpu.roll(r, 16, 1) * sin_b


def _pack(lo, hi):
    lo_b = lax.bitcast_convert_type(lo.astype(BF16).astype(F32), U32)
    hi_b = lax.bitcast_convert_type(hi.astype(BF16).astype(F32), U32)
    return (lo_b >> 16) | hi_b


def _unpack(w):
    lo = lax.bitcast_convert_type(w << 16, F32)
    hi = lax.bitcast_convert_type(w & jnp.uint32(0xFFFF0000), F32)
    return lo, hi


def _ada_kernel(cond_ref, w_ref, b_ref, o_ref):
    c = cond_ref[...]
    s = c * jax.nn.sigmoid(c)
    s_hi = s.astype(BF16)
    s_lo = (s - s_hi.astype(F32)).astype(BF16)
    w = w_ref[...]
    w_hi = w.astype(BF16)
    w_lo = (w - w_hi.astype(F32)).astype(BF16)
    lhs = jnp.concatenate([s_hi, s_lo], axis=0)
    r1 = jnp.dot(lhs, w_hi, preferred_element_type=F32)
    r2 = jnp.dot(s_hi, w_lo, preferred_element_type=F32)
    o_ref[...] = r1[:MOD_ROWS] + r1[MOD_ROWS:] + r2 + b_ref[...]


def _ada_call(cond, w_ada, b_ada):
    n = 6 * D
    return pl.pallas_call(
        _ada_kernel,
        grid=(DEPTH, n // ADA_TN),
        in_specs=[
            pl.BlockSpec((MOD_ROWS, D), lambda l, j: (0, 0)),
            pl.BlockSpec((None, D, ADA_TN), lambda l, j: (l, 0, j)),
            pl.BlockSpec((None, 1, ADA_TN), lambda l, j: (l, 0, j)),
        ],
        out_specs=pl.BlockSpec((None, MOD_ROWS, ADA_TN), lambda l, j: (l, 0, j)),
        out_shape=jax.ShapeDtypeStruct((DEPTH, MOD_ROWS, n), F32),
        compiler_params=_cparams(("parallel", "parallel")),
        name="ada",
    )(cond, w_ada, b_ada.reshape(DEPTH, 1, n))


def _write_kv(ckv_b, kr, w_ukv_ref, kg_n, kg_r, rope, k_ref, v_ref, rows=slice(None)):
    kv = jnp.dot(ckv_b, w_ukv_ref[...], preferred_element_type=F32)
    ssr = jnp.sum(kr * kr, axis=-1, keepdims=True)
    for h in range(N_HEADS):
        kn = kv[:, h * HEAD_PAD:h * HEAD_PAD + NOPE]
        vv = kv[:, h * HEAD_PAD + NOPE:(h + 1) * HEAD_PAD]
        ms = (jnp.sum(kn * kn, axis=-1, keepdims=True) + ssr) * (1.0 / QK_DIM)
        inv = lax.rsqrt(ms + EPS)
        r = kr * inv * kg_r
        if rope is not None:
            r = _rope(r, *rope)
        k_ref[rows, h * HEAD_PAD:h * HEAD_PAD + NOPE] = (kn * inv * kg_n).astype(BF16)
        k_ref[rows, h * HEAD_PAD + NOPE:(h + 1) * HEAD_PAD] = r.astype(BF16)
        v_ref[rows, h * VDIM:(h + 1) * VDIM] = vv.astype(BF16)


def _cache_kv_kernel(ckv_ref, kr_ref, w_ukv_ref, kgn_ref, kgr_ref, k_ref, v_ref):
    _write_kv(ckv_ref[...].astype(BF16), kr_ref[...], w_ukv_ref, kgn_ref[...], kgr_ref[...],
              None, k_ref, v_ref)


def _cache_kv_call(cache_ckv, cache_kr_p, w_ukv_b, kg_n, kg_r):
    return pl.pallas_call(
        _cache_kv_kernel,
        grid=(DEPTH, DEC_BATCH),
        in_specs=[
            pl.BlockSpec((None, None, PAST, KV_RANK), lambda l, b: (b, l, 0, 0)),
            pl.BlockSpec((None, None, PAST, LANE), lambda l, b: (b, l, 0, 0)),
            pl.BlockSpec((None, KV_RANK, QK_W), lambda l, b: (l, 0, 0)),
            pl.BlockSpec((None, 1, LANE), lambda l, b: (l, 0, 0)),
            pl.BlockSpec((None, 1, LANE), lambda l, b: (l, 0, 0)),
        ],
        out_specs=[
            pl.BlockSpec((None, None, PAST, QK_W), lambda l, b: (l, b, 0, 0)),
            pl.BlockSpec((None, None, PAST, ATTN_W), lambda l, b: (l, b, 0, 0)),
        ],
        out_shape=[
            jax.ShapeDtypeStruct((DEPTH, DEC_BATCH, PAST, QK_W), BF16),
            jax.ShapeDtypeStruct((DEPTH, DEC_BATCH, PAST, ATTN_W), BF16),
        ],
        compiler_params=_cparams(("parallel", "parallel")),
        name="cache_kv",
    )(cache_ckv, cache_kr_p, w_ukv_b, kg_n, kg_r)


def _in_proj_kernel(x_ref, mods_ref, n1g_ref, w_in_ref, qlg_ref, kvlg_ref, w_uq_ref, qg_ref,
                    w_ukv_ref, kgn_ref, kgr_ref, rope_ref, gmg_ref, gmw_ref, gmb_ref,
                    q_ref, k_ref, v_ref, ckv_ref, kr_ref, zg_ref, zp_ref):
    i = pl.program_id(0)
    row = jnp.where(i < M_P // TMI, 0, 1 + (i - M_P // TMI) // (DEC_SEQ // TMI))
    m = mods_ref[pl.ds(row, 1), :]
    sh1, sc1 = m[:, 0:D], m[:, D:2 * D]
    for sub in range(TMI // TM):
        rs = slice(sub * TM, (sub + 1) * TM)
        x = x_ref[rs, :]
        h = _rms(x, n1g_ref[...], D) * (1.0 + sc1) + sh1
        z = jnp.dot(h.astype(BF16), w_in_ref[...], preferred_element_type=F32)
        rope = (rope_ref[0, rs, :], rope_ref[1, rs, :], rope_ref[2, rs, :])

        ql = _rms(z[:, C_Q:C_Q + Q_RANK], qlg_ref[...], Q_RANK)
        qf = jnp.dot(ql.astype(BF16), w_uq_ref[...], preferred_element_type=F32)
        qg = qg_ref[...]
        for hd in range(N_HEADS):
            seg = qf[:, hd * HEAD_PAD:(hd + 1) * HEAD_PAD]
            seg = _rms(seg, qg, QK_DIM)
            q_ref[rs, hd * HEAD_PAD:hd * HEAD_PAD + NOPE] = seg[:, :NOPE].astype(BF16)
            q_ref[rs, hd * HEAD_PAD + NOPE:(hd + 1) * HEAD_PAD] = _rope(seg[:, NOPE:], *rope).astype(BF16)

        ckv = _rms(z[:, C_KV:C_KV + KV_RANK], kvlg_ref[...], KV_RANK)
        kr = z[:, C_KR:C_KR + LANE]
        ckv_ref[rs, :] = ckv
        kr_ref[rs, :] = kr
        _write_kv(ckv.astype(BF16), kr, w_ukv_ref, kgn_ref[...], kgr_ref[...], rope, k_ref, v_ref, rs)

        zg = jax.nn.gelu(z[:, C_GM:C_GM + 2 * GM_W], approximate=True)
        for g in range(GM_GROUPS):
            u = zg[:, g * LANE:(g + 1) * LANE]
            vg = _rms(zg[:, GM_W + g * LANE:GM_W + (g + 1) * LANE], gmg_ref[g:g + 1, :], LANE)
            vb = vg.astype(BF16)
            for c in range(TM // CHUNK):
                sv = jnp.dot(gmw_ref[g], vb[c * CHUNK:(c + 1) * CHUNK, :], preferred_element_type=F32)
                sv = sv + gmb_ref[g]
                r0 = sub * TM + c * CHUNK
                zg_ref[r0:r0 + CHUNK, g * LANE:(g + 1) * LANE] = (
                    u[c * CHUNK:(c + 1) * CHUNK, :] * sv).astype(BF16)

        zp_ref[rs, :] = z[:, C_PL:C_PL + POOL_W]


def _in_proj_call(l, x, mods_all, n1g, w_in_b, qlg, kvlg, w_uq_b, qg, w_ukv_b, kg_n, kg_r, rope_tab,
                  gmg, gmw_b, gmb):
    def rope_idx(i):
        return (0, jnp.where(i < M_P // TMI, 0, 1 + (i - M_P // TMI) % (DEC_SEQ // TMI)), 0)

    row = lambda w: pl.BlockSpec((TMI, w), lambda i: (i, 0))
    return pl.pallas_call(
        _in_proj_kernel,
        grid=(M_TOK // TMI,),
        in_specs=[
            row(D),
            _layer_spec(l, (MOD_ROWS, 6 * D)),
            _layer_spec(l, (1, D)),
            _layer_spec(l, (D, IN_PAD)),
            _layer_spec(l, (1, Q_RANK)),
            _layer_spec(l, (1, KV_RANK)),
            _layer_spec(l, (Q_RANK, QK_W)),
            _layer_spec(l, (1, HEAD_PAD)),
            _layer_spec(l, (KV_RANK, QK_W)),
            _layer_spec(l, (1, LANE)),
            _layer_spec(l, (1, LANE)),
            pl.BlockSpec((3, TMI, LANE), rope_idx),
            _layer_spec(l, (GM_GROUPS, LANE)),
            _layer_spec(l, (GM_GROUPS, CHUNK, CHUNK)),
            _layer_spec(l, (GM_GROUPS, CHUNK, LANE)),
        ],
        out_specs=[row(QK_W), row(QK_W), row(ATTN_W), row(KV_RANK), row(LANE), row(GM_W), row(POOL_W)],
        out_shape=[
            jax.ShapeDtypeStruct((M_TOK, QK_W), BF16),
            jax.ShapeDtypeStruct((M_TOK, QK_W), BF16),
            jax.ShapeDtypeStruct((M_TOK, ATTN_W), BF16),
            jax.ShapeDtypeStruct((M_TOK, KV_RANK), F32),
            jax.ShapeDtypeStruct((M_TOK, LANE), F32),
            jax.ShapeDtypeStruct((M_TOK, GM_W), BF16),
            jax.ShapeDtypeStruct((M_TOK, POOL_W), F32),
        ],
        compiler_params=_cparams(("parallel",)),
        name="in_proj",
    )(x, mods_all, n1g, w_in_b, qlg, kvlg, w_uq_b, qg, w_ukv_b, kg_n, kg_r, rope_tab, gmg, gmw_b, gmb)


def _attend_heads(q_ref, kv_pairs, o_ref):
    nt = (((1,), (1,)), ((), ()))
    for h in range(N_HEADS):
        qh = q_ref[:, h * HEAD_PAD:(h + 1) * HEAD_PAD]
        ss = [lax.dot_general(qh, k_ref[:, h * HEAD_PAD:(h + 1) * HEAD_PAD], nt,
                              preferred_element_type=F32) for k_ref, _ in kv_pairs]
        mx = ss[0].max(axis=-1, keepdims=True)
        for s in ss[1:]:
            mx = jnp.maximum(mx, s.max(axis=-1, keepdims=True))
        den = None
        acc = None
        for s, (_, v_ref) in zip(ss, kv_pairs):
            p = jnp.exp(s - mx)
            d = jnp.sum(p, axis=-1, keepdims=True)
            a = jnp.dot(p.astype(BF16), v_ref[:, h * VDIM:(h + 1) * VDIM], preferred_element_type=F32)
            den = d if den is None else den + d
            acc = a if acc is None else acc + a
        o_ref[:, h * VDIM:(h + 1) * VDIM] = (acc / den).astype(BF16)


def _attn_kernel(q_ref, kp_ref, vp_ref, ks_ref, vs_ref, kc_ref, vc_ref, o_ref):
    i = pl.program_id(0)

    @pl.when(i < P_TILES)
    def _():
        _attend_heads(q_ref, [(kp_ref, vp_ref)], o_ref)

    @pl.when(i >= P_TILES)
    def _():
        _attend_heads(q_ref, [(ks_ref, vs_ref), (kc_ref, vc_ref)], o_ref)


def _attn_call(l, q, k, v, k_c, v_c):
    def seq_b(i):
        return jnp.maximum(i - P_TILES, 0) // S_TILES

    small = lambda w: pl.BlockSpec((TM, w), lambda i: (jnp.minimum(i, P_TILES - 1), 0))
    big = lambda w: pl.BlockSpec((DEC_SEQ, w), lambda i: (M_P // DEC_SEQ + seq_b(i), 0))
    cache = lambda w: pl.BlockSpec((None, None, PAST, w), lambda i: (l, seq_b(i), 0, 0))
    return pl.pallas_call(
        _attn_kernel,
        grid=(N_TILES,),
        in_specs=[
            pl.BlockSpec((TM, QK_W), lambda i: (i, 0)),
            small(QK_W), small(ATTN_W),
            big(QK_W), big(ATTN_W),
            cache(QK_W), cache(ATTN_W),
        ],
        out_specs=pl.BlockSpec((TM, ATTN_W), lambda i: (i, 0)),
        out_shape=jax.ShapeDtypeStruct((M_TOK, ATTN_W), BF16),
        compiler_params=_cparams(("parallel",)),
        name="attention",
    )(q, k, v, k, v, k_c, v_c)


def _out_proj_kernel(moe, attn_ref, zg_ref, zprev_ref, zcur_ref, znext_ref, x_ref, mods_ref,
                     pw_ref, ps_ref, w_o_ref, n2g_ref, *rest):
    if moe:
        rw_ref, xo_ref, hp_ref, rinfo_ref, ext_ref, mix_ref = rest
    else:
        xo_ref, h2b_ref, ext_ref, mix_ref = rest
    i = pl.program_id(0)
    is_ctx = i < P_TILES
    t_in_seq = jnp.where(is_ctx, 0, (i - P_TILES) % S_TILES)
    n_tiles_seq = jnp.where(is_ctx, 1, S_TILES)
    first = t_in_seq == 0
    last = t_in_seq == n_tiles_seq - 1
    seq_len = n_tiles_seq * TM

    zero_halo = jnp.zeros((POOL_HALO, POOL_W), F32)
    ext_ref[0:POOL_HALO, :] = jnp.where(first, zero_halo, zprev_ref[TM - POOL_HALO:, :])
    ext_ref[POOL_HALO:POOL_HALO + TM, :] = zcur_ref[...]
    ext_ref[POOL_HALO + TM:, :] = jnp.where(last, zero_halo, znext_ref[0:POOL_HALO, :])

    mix_ref[:, 0:ATTN_W] = attn_ref[...]
    mix_ref[:, ATTN_W:ATTN_W + GM_W] = zg_ref[...]
    pos = t_in_seq * TM + lax.broadcasted_iota(I32, (TM, 1), 0)
    for gi, w in enumerate(POOL_WINDOWS):
        cols = slice(gi * LANE, (gi + 1) * LANE)
        win = None
        for dlt in range(-(w // 2), w - w // 2):
            part = ext_ref[POOL_HALO + dlt:POOL_HALO + dlt + TM, cols]
            win = part if win is None else win + part
        lo = jnp.maximum(pos - w // 2, 0)
        hi = jnp.minimum(pos + (w - 1 - w // 2), seq_len - 1)
        cnt = (hi - lo + 1).astype(F32)
        pooled = win / cnt - zcur_ref[:, cols]
        mixed = jnp.dot(pooled.astype(BF16), pw_ref[gi], preferred_element_type=F32)
        mix_ref[:, ATTN_W + GM_W + gi * LANE:ATTN_W + GM_W + (gi + 1) * LANE] = (
            mixed * ps_ref[:, cols]).astype(BF16)

    out = jnp.dot(mix_ref[...], w_o_ref[...], preferred_element_type=F32)
    m = mods_ref[pl.ds(_mod_row(i), 1), :]
    g1, sh2, sc2 = m[:, 2 * D:3 * D], m[:, 3 * D:4 * D], m[:, 4 * D:5 * D]
    xn = x_ref[...] + g1 * out
    xo_ref[...] = xn
    h2 = _rms(xn, n2g_ref[...], D) * (1.0 + sc2) + sh2

    if not moe:
        h2b_ref[...] = h2.astype(BF16)
        return

    hp_ref[...] = _pack(h2[:, :HALF], h2[:, HALF:])
    h_hi = h2.astype(BF16)
    h_lo = (h2 - h_hi.astype(F32)).astype(BF16)
    rw = rw_ref[...]
    w_hi = rw.astype(BF16)
    w_lo = (rw - w_hi.astype(F32)).astype(BF16)
    logits = (jnp.dot(h_hi, w_hi, preferred_element_type=F32)
              + jnp.dot(h_lo, w_hi, preferred_element_type=F32)
              + jnp.dot(h_hi, w_lo, preferred_element_type=F32))
    lane = lax.broadcasted_iota(I32, (TM, LANE), 1).astype(F32)
    neg = jnp.float32(-jnp.inf)
    logits = jnp.where(lane < N_EXP, logits, neg)
    v1 = logits.max(axis=-1, keepdims=True)
    i1 = jnp.where(logits == v1, lane, float(LANE)).min(axis=-1, keepdims=True)
    l2 = jnp.where(lane == i1, neg, logits)
    v2 = l2.max(axis=-1, keepdims=True)
    i2 = jnp.where(l2 == v2, lane, float(LANE)).min(axis=-1, keepdims=True)
    e2 = jnp.exp(v2 - v1)
    gate1 = 1.0 / (1.0 + e2)
    gate2 = e2 / (1.0 + e2)
    rinfo_ref[...] = jnp.where(lane == 0, i1, jnp.where(lane == 1, i2, jnp.where(
        lane == 2, gate1, jnp.where(lane == 3, gate2, 0.0))))


def _out_proj_call(l, moe, attn, zg, zp, x, mods_all, pw_b, ps, w_o_b, n2g, rw_p=None):
    row = lambda w: pl.BlockSpec((TM, w), lambda i: (i, 0))
    in_specs = [
        row(ATTN_W), row(GM_W),
        pl.BlockSpec((TM, POOL_W), lambda i: (jnp.maximum(i - 1, 0), 0)),
        row(POOL_W),
        pl.BlockSpec((TM, POOL_W), lambda i: (jnp.minimum(i + 1, N_TILES - 1), 0)),
        row(D),
        _layer_spec(l, (MOD_ROWS, 6 * D)),
        _layer_spec(l, (len(POOL_WINDOWS), LANE, LANE)),
        _layer_spec(l, (1, POOL_W)),
        _layer_spec(l, (D, D)),
        _layer_spec(l, (1, D)),
    ]
    args = [attn, zg, zp, zp, zp, x, mods_all, pw_b, ps, w_o_b, n2g]
    if moe:
        in_specs.append(_layer_spec(l // 2, (D, LANE)))
        args.append(rw_p)
        out_specs = [row(D), row(HALF), row(LANE)]
        out_shape = [jax.ShapeDtypeStruct((M_TOK, D), F32), jax.ShapeDtypeStruct((M_TOK, HALF), U32),
                     jax.ShapeDtypeStruct((M_TOK, LANE), F32)]
    else:
        out_specs = [row(D), row(D)]
        out_shape = [jax.ShapeDtypeStruct((M_TOK, D), F32), jax.ShapeDtypeStruct((M_TOK, D), BF16)]
    return pl.pallas_call(
        functools.partial(_out_proj_kernel, moe),
        grid=(N_TILES,),
        in_specs=in_specs,
        out_specs=out_specs,
        out_shape=out_shape,
        scratch_shapes=[pltpu.VMEM((TM + 2 * POOL_HALO, POOL_W), F32), pltpu.VMEM((TM, D), BF16)],
        compiler_params=_cparams(("parallel",)),
        name="out_proj_moe" if moe else "out_proj",
    )(*args)


def _ffn_up_kernel(h_ref, w1_ref, w3_ref, g_ref, w1b_ref, w3b_ref):
    @pl.when(pl.program_id(1) == 0)
    def _():
        w1b_ref[...] = w1_ref[...].astype(BF16)
        w3b_ref[...] = w3_ref[...].astype(BF16)

    h = h_ref[...]
    a = jnp.dot(h, w1b_ref[...], preferred_element_type=F32)
    b = jnp.dot(h, w3b_ref[...], preferred_element_type=F32)
    g_ref[...] = (a * jax.nn.sigmoid(a) * b).astype(BF16)


def _ffn_up_call(i, h, w1, w3):
    wspec = pl.BlockSpec((None, D, TF_UP), lambda j, t: (i, 0, j))
    return pl.pallas_call(
        _ffn_up_kernel,
        grid=(D_FF // TF_UP, M_TOK // TE_UP),
        in_specs=[pl.BlockSpec((TE_UP, D), lambda j, t: (t, 0)), wspec, wspec],
        out_specs=pl.BlockSpec((TE_UP, TF_UP), lambda j, t: (t, j)),
        out_shape=jax.ShapeDtypeStruct((M_TOK, D_FF), BF16),
        scratch_shapes=[pltpu.VMEM((D, TF_UP), BF16), pltpu.VMEM((D, TF_UP), BF16)],
        compiler_params=_cparams(("arbitrary", "arbitrary")),
        name="ffn_up",
    )(h, w1, w3)


def _ffn_down_kernel(g_ref, w2_ref, x_ref, mods_ref, o_ref, w2b_ref):
    t = pl.program_id(1)

    @pl.when(t == 0)
    def _():
        w2b_ref[...] = w2_ref[...].astype(BF16)

    y = jnp.dot(g_ref[...], w2b_ref[...], preferred_element_type=F32)
    ctx_tiles = M_P // TE_DOWN
    row = jnp.where(t < ctx_tiles, 0, 1 + (t - ctx_tiles) // (DEC_SEQ // TE_DOWN))
    o_ref[...] = x_ref[...] + mods_ref[pl.ds(row, 1), :] * y


def _ffn_down_call(l, g, w2, x, mods_all):
    i = l // 2
    return pl.pallas_call(
        _ffn_down_kernel,
        grid=(D // TN_DOWN, M_TOK // TE_DOWN),
        in_specs=[
            pl.BlockSpec((TE_DOWN, D_FF), lambda j, t: (t, 0)),
            pl.BlockSpec((None, D_FF, TN_DOWN), lambda j, t: (i, 0, j)),
            pl.BlockSpec((TE_DOWN, TN_DOWN), lambda j, t: (t, j)),
            pl.BlockSpec((None, MOD_ROWS, TN_DOWN), lambda j, t: (l, 0, 5 * (D // TN_DOWN) + j)),
        ],
        out_specs=pl.BlockSpec((TE_DOWN, TN_DOWN), lambda j, t: (t, j)),
        out_shape=jax.ShapeDtypeStruct((M_TOK, D), F32),
        scratch_shapes=[pltpu.VMEM((D_FF, TN_DOWN), BF16)],
        compiler_params=_cparams(("arbitrary", "arbitrary")),
        name="ffn_down",
    )(g, w2, x, mods_all)


def _src_kernel(cpos_ref, src_ref):
    def body(i, c):
        src_ref[cpos_ref[i]] = lax.shift_right_logical(i, TOP_K_SHIFT)
        return c
    lax.fori_loop(0, N_SLOTS, body, 0, unroll=8)


def _src_call(cpos):
    return pl.pallas_call(
        _src_kernel,
        in_specs=[pl.BlockSpec(memory_space=pltpu.SMEM)],
        out_specs=pl.BlockSpec(memory_space=pltpu.SMEM),
        out_shape=jax.ShapeDtypeStruct((N_SLOTS,), I32),
        name="moe_src",
    )(cpos)


def _gather_copy(hp_hbm, tok, stage, slot, r, sem):
    return pltpu.make_async_copy(hp_hbm.at[pl.ds(tok, 1)], stage.at[slot, pl.ds(r, 1)], sem.at[slot])


def _moe_up_kernel(src_ref, ie_ref, ics_ref, icnt_ref, nit_ref, hp_hbm, w1_ref, w3_ref, g_ref,
                   buf, stage, w1b_ref, w3b_ref, sem):
    k = pl.program_id(0)
    j = pl.program_id(1)
    nit = nit_ref[0]

    def chunk_rows(item, c):
        return jnp.clip(icnt_ref[item] - c * GC, 0, GC)

    def issue_chunk(item, c, slot):
        n = chunk_rows(item, c)
        base = ics_ref[item] + c * GC

        @pl.when(n > 0)
        def _():
            def body(r, carry):
                tok = src_ref[base + jnp.minimum(r, n - 1)]
                _gather_copy(hp_hbm, tok, stage, slot, r, sem).start()
                return carry
            lax.fori_loop(0, GC, body, 0, unroll=8)

    def finish_chunk(item, c, slot, b):
        @pl.when(chunk_rows(item, c) > 0)
        def _():
            def body(r, carry):
                _gather_copy(hp_hbm, 0, stage, slot, r, sem).wait()
                return carry
            lax.fori_loop(0, GC, body, 0, unroll=8)
            lo, hi = _unpack(stage[slot])
            rows = pl.ds(pl.multiple_of(c * GC, GC), GC)
            buf[b, rows, 0:HALF] = lo.astype(BF16)
            buf[b, rows, HALF:D] = hi.astype(BF16)

    @pl.when((k == 0) & (j == 0))
    def _():
        buf[...] = jnp.zeros_like(buf)
        stage[...] = jnp.zeros_like(stage)
        issue_chunk(0, 0, 0)

        def body(c, carry):
            @pl.when(c + 1 < N_GC)
            def _():
                issue_chunk(0, c + 1, (c + 1) & 1)
            finish_chunk(0, c, c & 1, 0)
            return carry
        lax.fori_loop(0, N_GC, body, 0)

    nxt = k + 1

    @pl.when((nxt < nit) & (j >= 1) & (j <= N_GC))
    def _():
        finish_chunk(nxt, j - 1, (j - 1) & 1, nxt & 1)

    @pl.when((nxt < nit) & (j < N_GC))
    def _():
        issue_chunk(nxt, j, j & 1)

    g_ref[...] = jnp.zeros_like(g_ref)

    @pl.when(k < nit)
    def _():
        w1b_ref[...] = w1_ref[...].astype(BF16)
        w3b_ref[...] = w3_ref[...].astype(BF16)
        b = k & 1

        def body(c, carry):
            rows = pl.ds(pl.multiple_of(c * MC, MC), MC)
            h = buf[b, rows, :]
            a = jnp.dot(h, w1b_ref[...], preferred_element_type=F32)
            bb = jnp.dot(h, w3b_ref[...], preferred_element_type=F32)
            g_ref[rows, :] = (a * jax.nn.sigmoid(a) * bb).astype(BF16)
            return carry
        lax.fori_loop(0, (icnt_ref[k] + MC - 1) // MC, body, 0)


def _moe_up_call(layer_e0, src, item_e, item_cs, item_cnt, n_items, hp, w1, w3):
    n_f = D_FFE // TF_MOE
    wspec = pl.BlockSpec((None, D, TF_MOE), lambda k, j, s, ie, ics, ic, ni: (
        layer_e0 + ie[k], 0, jnp.where(k < ni[0], j, n_f - 1)))
    return pl.pallas_call(
        _moe_up_kernel,
        grid_spec=pltpu.PrefetchScalarGridSpec(
            num_scalar_prefetch=5,
            grid=(N_ITEMS, D_FFE // TF_MOE),
            in_specs=[pl.BlockSpec(memory_space=pl.ANY), wspec, wspec],
            out_specs=pl.BlockSpec((RT, TF_MOE), lambda k, j, s, ie, ics, ic, ni: (k, j)),
            scratch_shapes=[
                pltpu.VMEM((2, RT, D), BF16),
                pltpu.VMEM((2, GC, HALF), U32),
                pltpu.VMEM((D, TF_MOE), BF16),
                pltpu.VMEM((D, TF_MOE), BF16),
                pltpu.SemaphoreType.DMA((2,)),
            ],
        ),
        out_shape=jax.ShapeDtypeStruct((N_ITEMS * RT, D_FFE), BF16),
        compiler_params=_cparams(("arbitrary", "arbitrary")),
        name="moe_up",
    )(src, item_e, item_cs, item_cnt, n_items, hp, w1, w3)


def _moe_down_kernel(ie_ref, icnt_ref, nit_ref, g_ref, w2a_ref, w2b_ref, o_ref, wa_ref, wb_ref):
    k = pl.program_id(0)
    o_ref[...] = jnp.zeros_like(o_ref)

    @pl.when(k < nit_ref[0])
    def _():
        wa_ref[...] = w2a_ref[...].astype(BF16)
        wb_ref[...] = w2b_ref[...].astype(BF16)

        def body(c, carry):
            rows = pl.ds(pl.multiple_of(c * MC, MC), MC)
            g = g_ref[rows, :]
            lo = jnp.dot(g, wa_ref[...], preferred_element_type=F32)
            hi = jnp.dot(g, wb_ref[...], preferred_element_type=F32)
            o_ref[rows, :] = _pack(lo, hi)
            return carry
        lax.fori_loop(0, (icnt_ref[k] + MC - 1) // MC, body, 0)


def _moe_down_call(layer_e0, item_e, item_cnt, n_items, g, w2):
    live = lambda k, ni: jnp.minimum(k, ni[0] - 1)
    n_half = HALF // TN_MOE
    col = lambda k, n, ni: jnp.where(k < ni[0], n, n_half - 1)
    return pl.pallas_call(
        _moe_down_kernel,
        grid_spec=pltpu.PrefetchScalarGridSpec(
            num_scalar_prefetch=3,
            grid=(N_ITEMS, n_half),
            in_specs=[
                pl.BlockSpec((RT, D_FFE), lambda k, n, ie, ic, ni: (live(k, ni), 0)),
                pl.BlockSpec((None, D_FFE, TN_MOE),
                             lambda k, n, ie, ic, ni: (layer_e0 + ie[k], 0, col(k, n, ni))),
                pl.BlockSpec((None, D_FFE, TN_MOE),
                             lambda k, n, ie, ic, ni: (layer_e0 + ie[k], 0, n_half + col(k, n, ni))),
            ],
            out_specs=pl.BlockSpec((RT, TN_MOE), lambda k, n, ie, ic, ni: (k, n)),
            scratch_shapes=[pltpu.VMEM((D_FFE, TN_MOE), BF16), pltpu.VMEM((D_FFE, TN_MOE), BF16)],
        ),
        out_shape=jax.ShapeDtypeStruct((N_ITEMS * RT, HALF), U32),
        compiler_params=_cparams(("arbitrary", "arbitrary")),
        name="moe_down",
    )(item_e, item_cnt, n_items, g, w2, w2)


def _combine_copy(yo_hbm, row, buf, slot, r, sem):
    return pltpu.make_async_copy(yo_hbm.at[pl.ds(row, 1)], buf.at[slot, pl.ds(r, 1)], sem.at[slot])


def _combine_kernel(pos_ref, yo_hbm, x_ref, rinfo_ref, mods_ref, o_ref, buf, sem):
    i = pl.program_id(0)
    n = pl.num_programs(0)
    rows = TOP_K * TM

    def issue(tile, slot):
        def body(r, c):
            _combine_copy(yo_hbm, pos_ref[tile * rows + r], buf, slot, r, sem).start()
            return c
        lax.fori_loop(0, rows, body, 0, unroll=8)

    @pl.when(i == 0)
    def _():
        issue(0, 0)

    @pl.when(i + 1 < n)
    def _():
        issue(i + 1, (i + 1) & 1)

    slot = i & 1

    def body(r, c):
        _combine_copy(yo_hbm, 0, buf, slot, r, sem).wait()
        return c
    lax.fori_loop(0, rows, body, 0, unroll=8)
    rinfo = rinfo_ref[...]
    lo0, hi0 = _unpack(buf[slot, 0:TM, :])
    lo1, hi1 = _unpack(buf[slot, TM:2 * TM, :])
    gt0, gt1 = rinfo[:, 2:3], rinfo[:, 3:4]
    g2 = mods_ref[pl.ds(_mod_row(i), 1), :]
    o_ref[:, 0:HALF] = x_ref[:, 0:HALF] + g2[:, 0:HALF] * (gt0 * lo0 + gt1 * lo1)
    o_ref[:, HALF:D] = x_ref[:, HALF:D] + g2[:, HALF:D] * (gt0 * hi0 + gt1 * hi1)


def _combine_call(l, pos, yo, x, rinfo, mods_all):
    return pl.pallas_call(
        _combine_kernel,
        grid_spec=pltpu.PrefetchScalarGridSpec(
            num_scalar_prefetch=1,
            grid=(N_TILES,),
            in_specs=[
                pl.BlockSpec(memory_space=pl.ANY),
                pl.BlockSpec((TM, D), lambda i, p: (i, 0)),
                pl.BlockSpec((TM, LANE), lambda i, p: (i, 0)),
                pl.BlockSpec((None, MOD_ROWS, D), lambda i, p: (l, 0, 5)),
            ],
            out_specs=pl.BlockSpec((TM, D), lambda i, p: (i, 0)),
            scratch_shapes=[pltpu.VMEM((2, TOP_K * TM, HALF), U32), pltpu.SemaphoreType.DMA((2,))],
        ),
        out_shape=jax.ShapeDtypeStruct((M_TOK, D), F32),
        compiler_params=_cparams(("arbitrary",)),
        name="moe_combine",
    )(pos, yo, x, rinfo, mods_all)


def _route_plan(rinfo):
    flat_e = rinfo[:, 0:TOP_K].astype(I32).reshape(-1)
    onehot = (flat_e[:, None] == jnp.arange(N_EXP, dtype=I32)[None, :]).astype(I32)
    csum = jnp.cumsum(onehot, axis=0)
    pick = lambda table: jnp.sum(onehot * table[None, :], axis=1)
    rank = jnp.sum(onehot * csum, axis=1) - 1
    counts = csum[-1]
    cstart = jnp.cumsum(counts) - counts
    n_it = (counts + RT - 1) // RT
    it_end = jnp.cumsum(n_it)
    it_base = it_end - n_it
    n_items = it_end[-1]
    cpos = pick(cstart) + rank
    ppos = (pick(it_base) + rank // RT) * RT + rank % RT
    q = jnp.arange(N_ITEMS, dtype=I32)
    qc = jnp.minimum(q, n_items - 1)
    qe = jnp.minimum(jnp.sum((it_end[None, :] <= qc[:, None]).astype(I32), axis=1), N_EXP - 1)
    qhot = (qe[:, None] == jnp.arange(N_EXP, dtype=I32)[None, :]).astype(I32)
    qpick = lambda table: jnp.sum(qhot * table[None, :], axis=1)
    s = qc - qpick(it_base)
    item_cnt = jnp.where(q < n_items, jnp.clip(qpick(counts) - s * RT, 0, RT), 0)
    item_cs = qpick(cstart) + s * RT
    pos_tiles = ppos.reshape(N_TILES, TM, TOP_K).transpose(0, 2, 1).reshape(-1)
    return (cpos.astype(I32), qe.astype(I32), item_cs.astype(I32), item_cnt.astype(I32),
            n_items.reshape(1).astype(I32), pos_tiles.astype(I32))


def _rope_tables():
    rows = DEC_SEQ // GRID_W
    row = jnp.repeat(jnp.arange(rows), GRID_W).astype(F32)
    col = jnp.tile(jnp.arange(GRID_W), rows).astype(F32)
    nf = ROPE // 4
    inv = ROPE_BASE ** (-jnp.arange(nf, dtype=F32) / nf)
    ar = row[:, None] * inv[None, :]
    ac = col[:, None] * inv[None, :]
    ang = jnp.concatenate([ar, ar, ac, ac], axis=-1)
    cos, sin = jnp.cos(ang), jnp.sin(ang)
    first = (jnp.arange(ROPE) % 32) < 16
    pad = lambda a, v: jnp.pad(a, ((0, 0), (0, LANE - ROPE)), constant_values=v)
    cos_p = pad(cos, 1.0)
    sin_a = pad(jnp.where(first[None, :], -sin, 0.0), 0.0)
    sin_b = pad(jnp.where(first[None, :], 0.0, sin), 0.0)
    ident = jnp.stack([jnp.ones((TMI, LANE), F32), jnp.zeros((TMI, LANE), F32), jnp.zeros((TMI, LANE), F32)])
    return jnp.concatenate([ident, jnp.stack([cos_p, sin_a, sin_b])], axis=1)


def kernel(x_prompt, x_sample, cache_ckv, cache_krope, c, c_ctx, w_ada, b_ada, norm1_g, norm2_g, w_in,
           qlat_norm_g, kvlat_norm_g, w_uq, w_ukv, q_head_norm_g, k_head_norm_g, gm_norm_g, gm_w_s,
           gm_b_s, pool_w, pool_scale, w_o, ffn_w1, ffn_w3, ffn_w2, router_w, moe_w1, moe_w3, moe_w2):
    pad_w = IN_PAD - w_in.shape[-1]
    col = jnp.arange(IN_PAD)[None, None, :]
    w_in_p = jnp.where(
        col < C_KR + ROPE, jnp.pad(w_in, ((0, 0), (0, 0), (0, pad_w))),
        jnp.where(col < C_KR + LANE, 0.0, jnp.pad(w_in, ((0, 0), (0, 0), (pad_w, 0))))).astype(BF16)
    w_uq_p = jnp.pad(w_uq.reshape(DEPTH, Q_RANK, N_HEADS, QK_DIM),
                     ((0, 0), (0, 0), (0, 0), (0, HEAD_PAD - QK_DIM))
                     ).reshape(DEPTH, Q_RANK, QK_W).astype(BF16)
    w_ukv_b = w_ukv.astype(BF16)
    w_o_b = w_o.astype(BF16)
    qg_p = jnp.pad(q_head_norm_g * (1.0 / math.sqrt(QK_DIM)),
                   ((0, 0), (0, HEAD_PAD - QK_DIM))).reshape(DEPTH, 1, HEAD_PAD)
    kg_n = k_head_norm_g[:, :NOPE].reshape(DEPTH, 1, LANE)
    kg_r = jnp.pad(k_head_norm_g[:, NOPE:], ((0, 0), (0, LANE - ROPE))).reshape(DEPTH, 1, LANE)
    gmw_b = gm_w_s.astype(BF16)
    gmb = jnp.broadcast_to(gm_b_s[:, :, :, None], (DEPTH, GM_GROUPS, CHUNK, LANE))
    pw_b = pool_w.astype(BF16)
    ps = pool_scale.reshape(DEPTH, 1, POOL_W)
    n1g = norm1_g.reshape(DEPTH, 1, D)
    n2g = norm2_g.reshape(DEPTH, 1, D)
    qlg = qlat_norm_g.reshape(DEPTH, 1, Q_RANK)
    kvlg = kvlat_norm_g.reshape(DEPTH, 1, KV_RANK)
    rw_p = jnp.pad(router_w, ((0, 0), (0, 0), (0, LANE - N_EXP)))
    n_moe = moe_w1.shape[0]
    moe_w1_flat = moe_w1.reshape(n_moe * N_EXP, D, D_FFE)
    moe_w3_flat = moe_w3.reshape(n_moe * N_EXP, D, D_FFE)
    moe_w2_flat = moe_w2.reshape(n_moe * N_EXP, D_FFE, D)
    rope_tab = _rope_tables()
    cache_kr_p = jnp.pad(cache_krope, ((0, 0), (0, 0), (0, 0), (0, LANE - ROPE)))

    cond = jnp.zeros((MOD_ROWS, D), F32).at[0].set(c_ctx).at[1:1 + DEC_BATCH].set(c)
    mods_all = _ada_call(cond, w_ada, b_ada)
    k_c, v_c = _cache_kv_call(cache_ckv, cache_kr_p, w_ukv_b, kg_n, kg_r)

    x = jnp.concatenate([x_prompt.reshape(M_P, D), x_sample.reshape(M_S, D)], axis=0)
    ckv_out, kr_out = [], []
    for l in range(DEPTH):
        q, k, v, ckv, kr, zg, zp = _in_proj_call(
            l, x, mods_all, n1g, w_in_p, qlg, kvlg, w_uq_p, qg_p, w_ukv_b, kg_n, kg_r, rope_tab,
            gm_norm_g, gmw_b, gmb)
        ckv_out.append(ckv[:M_P].reshape(BATCH, SEQ, KV_RANK))
        kr_out.append(kr[:M_P, :ROPE].reshape(BATCH, SEQ, ROPE))
        attn = _attn_call(l, q, k, v, k_c, v_c)
        i = l // 2
        if l % 2 == 0:
            x, h2b = _out_proj_call(l, False, attn, zg, zp, x, mods_all, pw_b, ps, w_o_b, n2g)
            g = _ffn_up_call(i, h2b, ffn_w1, ffn_w3)
            x = _ffn_down_call(l, g, ffn_w2, x, mods_all)
        else:
            x, hp, rinfo = _out_proj_call(l, True, attn, zg, zp, x, mods_all, pw_b, ps, w_o_b, n2g, rw_p)
            cpos, item_e, item_cs, item_cnt, n_items, pos_tiles = _route_plan(rinfo)
            src = _src_call(cpos)
            g = _moe_up_call(i * N_EXP, src, item_e, item_cs, item_cnt, n_items, hp,
                             moe_w1_flat, moe_w3_flat)
            yo = _moe_down_call(i * N_EXP, item_e, item_cnt, n_items, g, moe_w2_flat)
            x = _combine_call(l, pos_tiles, yo, x, rinfo, mods_all)

    y_prompt = x[:M_P].reshape(BATCH, SEQ, D)
    y_sample = x[M_P:].reshape(DEC_BATCH, DEC_SEQ, D)
    return (y_prompt, y_sample, jnp.stack(ckv_out, axis=1), jnp.stack(kr_out, axis=1))
```

```python
import functools
import math

import jax
import jax.numpy as jnp
from jax import lax
from jax.experimental import pallas as pl
from jax.experimental.pallas import tpu as pltpu

F32 = jnp.float32
BF16 = jnp.bfloat16
U32 = jnp.uint32
I32 = jnp.int32

D = 2048
BATCH, SEQ = 16, 256
DEC_BATCH, DEC_SEQ = 2, 1024
PAST = 256
DEPTH = 4
GRID_W = 64
EPS = 1e-6
N_HEADS = 8
NOPE, ROPE, VDIM = 128, 64, 128
QK_DIM = NOPE + ROPE
HEAD_PAD = 256
QK_W = N_HEADS * HEAD_PAD
Q_RANK, KV_RANK = 512, 256
ROPE_BASE = 10000.0
ATTN_W = N_HEADS * VDIM
CHUNK, GM_GROUPS, GM_W = 128, 4, 512
POOL_WINDOWS = (2, 4, 8, 16)
POOL_W = 512
POOL_HALO = 8
D_FF, N_EXP, D_FFE, TOP_K = 5632, 8, 2816, 2
TOP_K_SHIFT = 1

C_Q, C_KV, C_KR, C_GM, C_PL = 0, 512, 768, 896, 1920
IN_PAD = 2432

LANE = 128
SUBLANES = 8
TM = 256
TMI = 512
M_P, M_S = BATCH * SEQ, DEC_BATCH * DEC_SEQ
M_TOK = M_P + M_S
N_TILES = M_TOK // TM
P_TILES = M_P // TM
S_TILES = DEC_SEQ // TM
MOD_ROWS = 16
ADA_TN = 1024
TE_UP, TF_UP = 1024, 512
TE_DOWN, TN_DOWN = 512, 512
HALF = D // 2
N_SLOTS = TOP_K * M_TOK
RT = 2048
N_ITEMS = N_EXP + N_SLOTS // RT
GC = 256
N_GC = RT // GC
MC = 512
REQ_CHUNKS = 2
REQ_GROUPS = GC // SUBLANES // REQ_CHUNKS
TF_MOE = 256
TN_MOE = 256
VMEM_LIMIT = 56 * 1024 * 1024


def _cparams(sem):
    return pltpu.CompilerParams(dimension_semantics=sem, vmem_limit_bytes=VMEM_LIMIT)


def _const_spec(shape):
    nd = len(shape)
    return pl.BlockSpec(shape, lambda *_: (0,) * nd)


def _layer_spec(l, shape):
    nd = len(shape)
    return pl.BlockSpec((None,) + tuple(shape), lambda *_: (l,) + (0,) * nd)


def _mod_row(i):
    return jnp.where(i < P_TILES, 0, 1 + (i - P_TILES) // S_TILES)


def _rms(x, g, n):
    ms = jnp.sum(x * x, axis=-1, keepdims=True) * (1.0 / n)
    return x * lax.rsqrt(ms + EPS) * g


def _rope(r, cos, sin_a, sin_b):
    return r * cos + pltpu.roll(r, LANE - 16, 1) * sin_a + pltpu.roll(r, 16, 1) * sin_b


def _pack(lo, hi):
    lo_b = lax.bitcast_convert_type(lo.astype(BF16).astype(F32), U32)
    hi_b = lax.bitcast_convert_type(hi.astype(BF16).astype(F32), U32)
    return (lo_b >> 16) | hi_b


def _unpack(w):
    lo = lax.bitcast_convert_type(w << 16, F32)
    hi = lax.bitcast_convert_type(w & jnp.uint32(0xFFFF0000), F32)
    return lo, hi


def _ada_kernel(cond_ref, w_ref, b_ref, o_ref):
    c = cond_ref[...]
    s = c * jax.nn.sigmoid(c)
    s_hi = s.astype(BF16)
    s_lo = (s - s_hi.astype(F32)).astype(BF16)
    w = w_ref[...]
    w_hi = w.astype(BF16)
    w_lo = (w - w_hi.astype(F32)).astype(BF16)
    lhs = jnp.concatenate([s_hi, s_lo], axis=0)
    r1 = jnp.dot(lhs, w_hi, preferred_element_type=F32)
    r2 = jnp.dot(s_hi, w_lo, preferred_element_type=F32)
    o_ref[...] = r1[:MOD_ROWS] + r1[MOD_ROWS:] + r2 + b_ref[...]


def _ada_call(cond, w_ada, b_ada):
    n = 6 * D
    return pl.pallas_call(
        _ada_kernel,
        grid=(DEPTH, n // ADA_TN),
        in_specs=[
            pl.BlockSpec((MOD_ROWS, D), lambda l, j: (0, 0)),
            pl.BlockSpec((None, D, ADA_TN), lambda l, j: (l, 0, j)),
            pl.BlockSpec((None, 1, ADA_TN), lambda l, j: (l, 0, j)),
        ],
        out_specs=pl.BlockSpec((None, MOD_ROWS, ADA_TN), lambda l, j: (l, 0, j)),
        out_shape=jax.ShapeDtypeStruct((DEPTH, MOD_ROWS, n), F32),
        compiler_params=_cparams(("parallel", "parallel")),
        name="ada",
    )(cond, w_ada, b_ada.reshape(DEPTH, 1, n))


def _write_kv(ckv_b, kr, w_ukv_ref, kg_n, kg_r, rope, k_ref, v_ref, rows=slice(None)):
    kv = jnp.dot(ckv_b, w_ukv_ref[...], preferred_element_type=F32)
    ssr = jnp.sum(kr * kr, axis=-1, keepdims=True)
    for h in range(N_HEADS):
        kn = kv[:, h * HEAD_PAD:h * HEAD_PAD + NOPE]
        vv = kv[:, h * HEAD_PAD + NOPE:(h + 1) * HEAD_PAD]
        ms = (jnp.sum(kn * kn, axis=-1, keepdims=True) + ssr) * (1.0 / QK_DIM)
        inv = lax.rsqrt(ms + EPS)
        r = kr * inv * kg_r
        if rope is not None:
            r = _rope(r, *rope)
        k_ref[rows, h * HEAD_PAD:h * HEAD_PAD + NOPE] = (kn * inv * kg_n).astype(BF16)
        k_ref[rows, h * HEAD_PAD + NOPE:(h + 1) * HEAD_PAD] = r.astype(BF16)
        v_ref[rows, h * VDIM:(h + 1) * VDIM] = vv.astype(BF16)


def _cache_kv_kernel(ckv_ref, kr_ref, w_ukv_ref, kgn_ref, kgr_ref, k_ref, v_ref):
    _write_kv(ckv_ref[...].astype(BF16), kr_ref[...], w_ukv_ref, kgn_ref[...], kgr_ref[...],
              None, k_ref, v_ref)


def _cache_kv_call(cache_ckv, cache_kr_p, w_ukv_b, kg_n, kg_r):
    return pl.pallas_call(
        _cache_kv_kernel,
        grid=(DEPTH, DEC_BATCH),
        in_specs=[
            pl.BlockSpec((None, None, PAST, KV_RANK), lambda l, b: (b, l, 0, 0)),
            pl.BlockSpec((None, None, PAST, LANE), lambda l, b: (b, l, 0, 0)),
            pl.BlockSpec((None, KV_RANK, QK_W), lambda l, b: (l, 0, 0)),
            pl.BlockSpec((None, 1, LANE), lambda l, b: (l, 0, 0)),
            pl.BlockSpec((None, 1, LANE), lambda l, b: (l, 0, 0)),
        ],
        out_specs=[
            pl.BlockSpec((None, None, PAST, QK_W), lambda l, b: (l, b, 0, 0)),
            pl.BlockSpec((None, None, PAST, ATTN_W), lambda l, b: (l, b, 0, 0)),
        ],
        out_shape=[
            jax.ShapeDtypeStruct((DEPTH, DEC_BATCH, PAST, QK_W), BF16),
            jax.ShapeDtypeStruct((DEPTH, DEC_BATCH, PAST, ATTN_W), BF16),
        ],
        compiler_params=_cparams(("parallel", "parallel")),
        name="cache_kv",
    )(cache_ckv, cache_kr_p, w_ukv_b, kg_n, kg_r)


def _in_proj_kernel(x_ref, mods_ref, n1g_ref, w_in_ref, qlg_ref, kvlg_ref, w_uq_ref, qg_ref,
                    w_ukv_ref, kgn_ref, kgr_ref, rope_ref, gmg_ref, gmw_ref, gmb_ref,
                    q_ref, k_ref, v_ref, ckv_ref, kr_ref, zg_ref, zp_ref):
    i = pl.program_id(0)
    row = jnp.where(i < M_P // TMI, 0, 1 + (i - M_P // TMI) // (DEC_SEQ // TMI))
    m = mods_ref[pl.ds(row, 1), :]
    sh1, sc1 = m[:, 0:D], m[:, D:2 * D]
    for sub in range(TMI // TM):
        rs = slice(sub * TM, (sub + 1) * TM)
        x = x_ref[rs, :]
        h = _rms(x, n1g_ref[...], D) * (1.0 + sc1) + sh1
        z = jnp.dot(h.astype(BF16), w_in_ref[...], preferred_element_type=F32)
        rope = (rope_ref[0, rs, :], rope_ref[1, rs, :], rope_ref[2, rs, :])

        ql = _rms(z[:, C_Q:C_Q + Q_RANK], qlg_ref[...], Q_RANK)
        qf = jnp.dot(ql.astype(BF16), w_uq_ref[...], preferred_element_type=F32)
        qg = qg_ref[...]
        for hd in range(N_HEADS):
            seg = qf[:, hd * HEAD_PAD:(hd + 1) * HEAD_PAD]
            seg = _rms(seg, qg, QK_DIM)
            q_ref[rs, hd * HEAD_PAD:hd * HEAD_PAD + NOPE] = seg[:, :NOPE].astype(BF16)
            q_ref[rs, hd * HEAD_PAD + NOPE:(hd + 1) * HEAD_PAD] = _rope(seg[:, NOPE:], *rope).astype(BF16)

        ckv = _rms(z[:, C_KV:C_KV + KV_RANK], kvlg_ref[...], KV_RANK)
        kr = z[:, C_KR:C_KR + LANE]
        ckv_ref[rs, :] = ckv
        kr_ref[rs, :] = kr
        _write_kv(ckv.astype(BF16), kr, w_ukv_ref, kgn_ref[...], kgr_ref[...], rope, k_ref, v_ref, rs)

        zg = jax.nn.gelu(z[:, C_GM:C_GM + 2 * GM_W], approximate=True)
        for g in range(GM_GROUPS):
            u = zg[:, g * LANE:(g + 1) * LANE]
            vg = _rms(zg[:, GM_W + g * LANE:GM_W + (g + 1) * LANE], gmg_ref[g:g + 1, :], LANE)
            vb = vg.astype(BF16)
            for c in range(TM // CHUNK):
                sv = jnp.dot(gmw_ref[g], vb[c * CHUNK:(c + 1) * CHUNK, :], preferred_element_type=F32)
                sv = sv + gmb_ref[g]
                r0 = sub * TM + c * CHUNK
                zg_ref[r0:r0 + CHUNK, g * LANE:(g + 1) * LANE] = (
                    u[c * CHUNK:(c + 1) * CHUNK, :] * sv).astype(BF16)

        zp_ref[rs, :] = z[:, C_PL:C_PL + POOL_W]


def _in_proj_call(l, x, mods_all, n1g, w_in_b, qlg, kvlg, w_uq_b, qg, w_ukv_b, kg_n, kg_r, rope_tab,
                  gmg, gmw_b, gmb):
    def rope_idx(i):
        return (0, jnp.where(i < M_P // TMI, 0, 1 + (i - M_P // TMI) % (DEC_SEQ // TMI)), 0)

    row = lambda w: pl.BlockSpec((TMI, w), lambda i: (i, 0))
    return pl.pallas_call(
        _in_proj_kernel,
        grid=(M_TOK // TMI,),
        in_specs=[
            row(D),
            _layer_spec(l, (MOD_ROWS, 6 * D)),
            _layer_spec(l, (1, D)),
            _layer_spec(l, (D, IN_PAD)),
            _layer_spec(l, (1, Q_RANK)),
            _layer_spec(l, (1, KV_RANK)),
            _layer_spec(l, (Q_RANK, QK_W)),
            _layer_spec(l, (1, HEAD_PAD)),
            _layer_spec(l, (KV_RANK, QK_W)),
            _layer_spec(l, (1, LANE)),
            _layer_spec(l, (1, LANE)),
            pl.BlockSpec((3, TMI, LANE), rope_idx),
            _layer_spec(l, (GM_GROUPS, LANE)),
            _layer_spec(l, (GM_GROUPS, CHUNK, CHUNK)),
            _layer_spec(l, (GM_GROUPS, CHUNK, LANE)),
        ],
        out_specs=[row(QK_W), row(QK_W), row(ATTN_W), row(KV_RANK), row(LANE), row(GM_W), row(POOL_W)],
        out_shape=[
            jax.ShapeDtypeStruct((M_TOK, QK_W), BF16),
            jax.ShapeDtypeStruct((M_TOK, QK_W), BF16),
            jax.ShapeDtypeStruct((M_TOK, ATTN_W), BF16),
            jax.ShapeDtypeStruct((M_TOK, KV_RANK), F32),
            jax.ShapeDtypeStruct((M_TOK, LANE), F32),
            jax.ShapeDtypeStruct((M_TOK, GM_W), BF16),
            jax.ShapeDtypeStruct((M_TOK, POOL_W), F32),
        ],
        compiler_params=_cparams(("parallel",)),
        name="in_proj",
    )(x, mods_all, n1g, w_in_b, qlg, kvlg, w_uq_b, qg, w_ukv_b, kg_n, kg_r, rope_tab, gmg, gmw_b, gmb)


def _attend_heads(q_ref, kv_pairs, o_ref):
    nt = (((1,), (1,)), ((), ()))
    for h in range(N_HEADS):
        qh = q_ref[:, h * HEAD_PAD:(h + 1) * HEAD_PAD]
        ss = [lax.dot_general(qh, k_ref[:, h * HEAD_PAD:(h + 1) * HEAD_PAD], nt,
                              preferred_element_type=F32) for k_ref, _ in kv_pairs]
        mx = ss[0].max(axis=-1, keepdims=True)
        for s in ss[1:]:
            mx = jnp.maximum(mx, s.max(axis=-1, keepdims=True))
        den = None
        acc = None
        for s, (_, v_ref) in zip(ss, kv_pairs):
            p = jnp.exp(s - mx)
            d = jnp.sum(p, axis=-1, keepdims=True)
            a = jnp.dot(p.astype(BF16), v_ref[:, h * VDIM:(h + 1) * VDIM], preferred_element_type=F32)
            den = d if den is None else den + d
            acc = a if acc is None else acc + a
        o_ref[:, h * VDIM:(h + 1) * VDIM] = (acc / den).astype(BF16)


def _attn_kernel(q_ref, kp_ref, vp_ref, ks_ref, vs_ref, kc_ref, vc_ref, o_ref):
    i = pl.program_id(0)

    @pl.when(i < P_TILES)
    def _():
        _attend_heads(q_ref, [(kp_ref, vp_ref)], o_ref)

    @pl.when(i >= P_TILES)
    def _():
        _attend_heads(q_ref, [(ks_ref, vs_ref), (kc_ref, vc_ref)], o_ref)


def _attn_call(l, q, k, v, k_c, v_c):
    def seq_b(i):
        return jnp.maximum(i - P_TILES, 0) // S_TILES

    small = lambda w: pl.BlockSpec((TM, w), lambda i: (jnp.minimum(i, P_TILES - 1), 0))
    big = lambda w: pl.BlockSpec((DEC_SEQ, w), lambda i: (M_P // DEC_SEQ + seq_b(i), 0))
    cache = lambda w: pl.BlockSpec((None, None, PAST, w), lambda i: (l, seq_b(i), 0, 0))
    return pl.pallas_call(
        _attn_kernel,
        grid=(N_TILES,),
        in_specs=[
            pl.BlockSpec((TM, QK_W), lambda i: (i, 0)),
            small(QK_W), small(ATTN_W),
            big(QK_W), big(ATTN_W),
            cache(QK_W), cache(ATTN_W),
        ],
        out_specs=pl.BlockSpec((TM, ATTN_W), lambda i: (i, 0)),
        out_shape=jax.ShapeDtypeStruct((M_TOK, ATTN_W), BF16),
        compiler_params=_cparams(("parallel",)),
        name="attention",
    )(q, k, v, k, v, k_c, v_c)


def _out_proj_kernel(moe, attn_ref, zg_ref, zprev_ref, zcur_ref, znext_ref, x_ref, mods_ref,
                     pw_ref, ps_ref, w_o_ref, n2g_ref, *rest):
    if moe:
        rw_ref, xo_ref, hp_ref, rinfo_ref, ext_ref, mix_ref = rest
    else:
        xo_ref, h2b_ref, ext_ref, mix_ref = rest
    i = pl.program_id(0)
    ctx_steps = M_P // TMI
    is_ctx = i < ctx_steps
    seq_len = jnp.where(is_ctx, SEQ, DEC_SEQ)
    m = mods_ref[pl.ds(jnp.where(is_ctx, 0, 1 + (i - ctx_steps) // (DEC_SEQ // TMI)), 1), :]
    g1, sh2, sc2 = m[:, 2 * D:3 * D], m[:, 3 * D:4 * D], m[:, 4 * D:5 * D]

    ext_ref[0:POOL_HALO, :] = zprev_ref[TMI - POOL_HALO:, :]
    ext_ref[POOL_HALO:POOL_HALO + TMI, :] = zcur_ref[...]
    ext_ref[POOL_HALO + TMI:, :] = znext_ref[0:POOL_HALO, :]
    mix_ref[:, 0:ATTN_W] = attn_ref[...]
    mix_ref[:, ATTN_W:ATTN_W + GM_W] = zg_ref[...]

    for sub in range(TMI // TM):
        rs = slice(sub * TM, (sub + 1) * TM)
        pos0 = jnp.where(is_ctx, 0, ((i - ctx_steps) % (DEC_SEQ // TMI)) * TMI + sub * TM)
        pos = pos0 + lax.broadcasted_iota(I32, (TM, 1), 0)
        inside = {dlt: (pos + dlt >= 0) & (pos + dlt < seq_len)
                  for dlt in range(-POOL_HALO, POOL_HALO) if dlt != 0}
        for gi, w in enumerate(POOL_WINDOWS):
            cols = slice(gi * LANE, (gi + 1) * LANE)
            win = None
            for dlt in range(-(w // 2), w - w // 2):
                r0 = POOL_HALO + sub * TM + dlt
                part = ext_ref[r0:r0 + TM, cols]
                if dlt != 0:
                    part = jnp.where(inside[dlt], part, 0.0)
                win = part if win is None else win + part
            lo = jnp.maximum(pos - w // 2, 0)
            hi = jnp.minimum(pos + (w - 1 - w // 2), seq_len - 1)
            cnt = (hi - lo + 1).astype(F32)
            pooled = win / cnt - zcur_ref[rs, cols]
            mixed = jnp.dot(pooled.astype(BF16), pw_ref[gi], preferred_element_type=F32)
            mix_ref[rs, ATTN_W + GM_W + gi * LANE:ATTN_W + GM_W + (gi + 1) * LANE] = (
                mixed * ps_ref[:, cols]).astype(BF16)

        out = jnp.dot(mix_ref[rs, :], w_o_ref[...], preferred_element_type=F32)
        xn = x_ref[rs, :] + g1 * out
        xo_ref[rs, :] = xn
        h2 = _rms(xn, n2g_ref[...], D) * (1.0 + sc2) + sh2

        if not moe:
            h2b_ref[rs, :] = h2.astype(BF16)
            continue

        hp_ref[rs, :] = _pack(h2[:, :HALF], h2[:, HALF:])
        h_hi = h2.astype(BF16)
        h_lo = (h2 - h_hi.astype(F32)).astype(BF16)
        rw = rw_ref[...]
        w_hi = rw.astype(BF16)
        w_lo = (rw - w_hi.astype(F32)).astype(BF16)
        logits = (jnp.dot(h_hi, w_hi, preferred_element_type=F32)
                  + jnp.dot(h_lo, w_hi, preferred_element_type=F32)
                  + jnp.dot(h_hi, w_lo, preferred_element_type=F32))
        lane = lax.broadcasted_iota(I32, (TM, LANE), 1).astype(F32)
        neg = jnp.float32(-jnp.inf)
        logits = jnp.where(lane < N_EXP, logits, neg)
        v1 = logits.max(axis=-1, keepdims=True)
        i1 = jnp.where(logits == v1, lane, float(LANE)).min(axis=-1, keepdims=True)
        l2 = jnp.where(lane == i1, neg, logits)
        v2 = l2.max(axis=-1, keepdims=True)
        i2 = jnp.where(l2 == v2, lane, float(LANE)).min(axis=-1, keepdims=True)
        e2 = jnp.exp(v2 - v1)
        gate1 = 1.0 / (1.0 + e2)
        gate2 = e2 / (1.0 + e2)
        rinfo_ref[rs, :] = jnp.where(lane == 0, i1, jnp.where(lane == 1, i2, jnp.where(
            lane == 2, gate1, jnp.where(lane == 3, gate2, 0.0))))


def _out_proj_call(l, moe, attn, zg, zp, x, mods_all, pw_b, ps, w_o_b, n2g, rw_p=None):
    n_steps = M_TOK // TMI
    row = lambda w: pl.BlockSpec((TMI, w), lambda i: (i, 0))
    in_specs = [
        row(ATTN_W), row(GM_W),
        pl.BlockSpec((TMI, POOL_W), lambda i: (jnp.maximum(i - 1, 0), 0)),
        row(POOL_W),
        pl.BlockSpec((TMI, POOL_W), lambda i: (jnp.minimum(i + 1, n_steps - 1), 0)),
        row(D),
        _layer_spec(l, (MOD_ROWS, 6 * D)),
        _layer_spec(l, (len(POOL_WINDOWS), LANE, LANE)),
        _layer_spec(l, (1, POOL_W)),
        _layer_spec(l, (D, D)),
        _layer_spec(l, (1, D)),
    ]
    args = [attn, zg, zp, zp, zp, x, mods_all, pw_b, ps, w_o_b, n2g]
    if moe:
        in_specs.append(_layer_spec(l // 2, (D, LANE)))
        args.append(rw_p)
        out_specs = [row(D), row(HALF), row(LANE)]
        out_shape = [jax.ShapeDtypeStruct((M_TOK, D), F32), jax.ShapeDtypeStruct((M_TOK, HALF), U32),
                     jax.ShapeDtypeStruct((M_TOK, LANE), F32)]
    else:
        out_specs = [row(D), row(D)]
        out_shape = [jax.ShapeDtypeStruct((M_TOK, D), F32), jax.ShapeDtypeStruct((M_TOK, D), BF16)]
    return pl.pallas_call(
        functools.partial(_out_proj_kernel, moe),
        grid=(n_steps,),
        in_specs=in_specs,
        out_specs=out_specs,
        out_shape=out_shape,
        scratch_shapes=[pltpu.VMEM((TMI + 2 * POOL_HALO, POOL_W), F32), pltpu.VMEM((TMI, D), BF16)],
        compiler_params=_cparams(("parallel",)),
        name="out_proj_moe" if moe else "out_proj",
    )(*args)


def _ffn_up_kernel(h_ref, w1_ref, w3_ref, g_ref, w1b_ref, w3b_ref):
    @pl.when(pl.program_id(1) == 0)
    def _():
        w1b_ref[...] = w1_ref[...].astype(BF16)
        w3b_ref[...] = w3_ref[...].astype(BF16)

    h = h_ref[...]
    a = jnp.dot(h, w1b_ref[...], preferred_element_type=F32)
    b = jnp.dot(h, w3b_ref[...], preferred_element_type=F32)
    g_ref[...] = (a * jax.nn.sigmoid(a) * b).astype(BF16)


def _ffn_up_call(i, h, w1, w3):
    wspec = pl.BlockSpec((None, D, TF_UP), lambda j, t: (i, 0, j))
    return pl.pallas_call(
        _ffn_up_kernel,
        grid=(D_FF // TF_UP, M_TOK // TE_UP),
        in_specs=[pl.BlockSpec((TE_UP, D), lambda j, t: (t, 0)), wspec, wspec],
        out_specs=pl.BlockSpec((TE_UP, TF_UP), lambda j, t: (t, j)),
        out_shape=jax.ShapeDtypeStruct((M_TOK, D_FF), BF16),
        scratch_shapes=[pltpu.VMEM((D, TF_UP), BF16), pltpu.VMEM((D, TF_UP), BF16)],
        compiler_params=_cparams(("arbitrary", "arbitrary")),
        name="ffn_up",
    )(h, w1, w3)


def _ffn_down_kernel(g_ref, w2_ref, x_ref, mods_ref, o_ref, w2b_ref):
    t = pl.program_id(1)

    @pl.when(t == 0)
    def _():
        w2b_ref[...] = w2_ref[...].astype(BF16)

    y = jnp.dot(g_ref[...], w2b_ref[...], preferred_element_type=F32)
    ctx_tiles = M_P // TE_DOWN
    row = jnp.where(t < ctx_tiles, 0, 1 + (t - ctx_tiles) // (DEC_SEQ // TE_DOWN))
    o_ref[...] = x_ref[...] + mods_ref[pl.ds(row, 1), :] * y


def _ffn_down_call(l, g, w2, x, mods_all):
    i = l // 2
    return pl.pallas_call(
        _ffn_down_kernel,
        grid=(D // TN_DOWN, M_TOK // TE_DOWN),
        in_specs=[
            pl.BlockSpec((TE_DOWN, D_FF), lambda j, t: (t, 0)),
            pl.BlockSpec((None, D_FF, TN_DOWN), lambda j, t: (i, 0, j)),
            pl.BlockSpec((TE_DOWN, TN_DOWN), lambda j, t: (t, j)),
            pl.BlockSpec((None, MOD_ROWS, TN_DOWN), lambda j, t: (l, 0, 5 * (D // TN_DOWN) + j)),
        ],
        out_specs=pl.BlockSpec((TE_DOWN, TN_DOWN), lambda j, t: (t, j)),
        out_shape=jax.ShapeDtypeStruct((M_TOK, D), F32),
        scratch_shapes=[pltpu.VMEM((D_FF, TN_DOWN), BF16)],
        compiler_params=_cparams(("arbitrary", "arbitrary")),
        name="ffn_down",
    )(g, w2, x, mods_all)


def _src_kernel(cpos_ref, src_ref):
    def body(i, c):
        src_ref[cpos_ref[i]] = lax.shift_right_logical(i, TOP_K_SHIFT)
        return c
    lax.fori_loop(0, N_SLOTS, body, 0, unroll=8)


def _src_call(cpos):
    return pl.pallas_call(
        _src_kernel,
        in_specs=[pl.BlockSpec(memory_space=pltpu.SMEM)],
        out_specs=pl.BlockSpec(memory_space=pltpu.SMEM),
        out_shape=jax.ShapeDtypeStruct((N_SLOTS,), I32),
        name="moe_src",
    )(cpos)


def _gather_copy(hp_hbm, tok, stage, slot, q, u, sem):
    return pltpu.make_async_copy(hp_hbm.at[pl.ds(tok, 1)], stage.at[slot, q, pl.ds(u, 1)], sem.at[slot])


def _moe_up_kernel(src_ref, ie_ref, ics_ref, icnt_ref, nit_ref, hp_hbm, w1_ref, w3_ref, g_ref,
                   buf, stage, w1b_ref, w3b_ref, sem):
    k = pl.program_id(0)
    j = pl.program_id(1)
    nit = nit_ref[0]

    def chunk_rows(item, c):
        return jnp.clip(icnt_ref[item] - c * GC, 0, GC)

    def request_rows(base, n, slot, q):
        for u in range(SUBLANES):
            tok = src_ref[base + jnp.minimum(q * SUBLANES + u, n - 1)]
            _gather_copy(hp_hbm, tok, stage, slot, q, u, sem).start()

    def issue_chunk(item, c, slot):
        n = chunk_rows(item, c)
        base = ics_ref[item] + c * GC

        @pl.when(n > 0)
        def _():
            def body(q, carry):
                request_rows(base, n, slot, q)
                return carry
            lax.fori_loop(0, GC // SUBLANES, body, 0)

    def finish_chunk(item, c, slot, b):
        @pl.when(chunk_rows(item, c) > 0)
        def _():
            def body(r, carry):
                _gather_copy(hp_hbm, 0, stage, slot, 0, 0, sem).wait()
                return carry
            lax.fori_loop(0, GC, body, 0, unroll=8)
            lo, hi = _unpack(stage[slot].reshape(GC, HALF))
            rows = pl.ds(pl.multiple_of(c * GC, GC), GC)
            buf[b, rows, 0:HALF] = lo.astype(BF16)
            buf[b, rows, HALF:D] = hi.astype(BF16)

    @pl.when((k == 0) & (j == 0))
    def _():
        buf[...] = jnp.zeros_like(buf)
        stage[...] = jnp.zeros_like(stage)
        issue_chunk(0, 0, 0)

        def body(c, carry):
            @pl.when(c + 1 < N_GC)
            def _():
                issue_chunk(0, c + 1, (c + 1) & 1)
            finish_chunk(0, c, c & 1, 0)
            return carry
        lax.fori_loop(0, N_GC, body, 0)

    nxt = k + 1

    @pl.when((nxt < nit) & (j >= 1) & (j <= N_GC))
    def _():
        finish_chunk(nxt, j - 1, (j - 1) & 1, nxt & 1)

    nxt_c = jnp.minimum(nxt, N_ITEMS - 1)
    n_req = jnp.where((nxt < nit) & (j < N_GC), chunk_rows(nxt_c, j), 0)
    req_base = ics_ref[nxt_c] + j * GC
    req_slot = j & 1

    g_ref[...] = jnp.zeros_like(g_ref)
    live = k < nit
    n_mc = (icnt_ref[k] + MC - 1) // MC
    b = k & 1

    @pl.when(live)
    def _():
        w1b_ref[...] = w1_ref[...].astype(BF16)
        w3b_ref[...] = w3_ref[...].astype(BF16)

    def mm_chunk(c):
        rows = pl.ds(pl.multiple_of(c * MC, MC), MC)
        h = buf[b, rows, :]
        a = jnp.dot(h, w1b_ref[...], preferred_element_type=F32)
        bb = jnp.dot(h, w3b_ref[...], preferred_element_type=F32)
        g_ref[rows, :] = (a * jax.nn.sigmoid(a) * bb).astype(BF16)

    def plain_chunks(lo, hi):
        def body(c, carry):
            mm_chunk(c)
            return carry
        lax.fori_loop(lo, hi, body, 0)

    @pl.when(live & (n_req > 0))
    def _():
        n_carry = jnp.minimum(n_mc, REQ_CHUNKS)

        def body(c, carry):
            for qq in range(REQ_GROUPS):
                request_rows(req_base, n_req, req_slot, c * REQ_GROUPS + qq)
            mm_chunk(c)
            return carry
        lax.fori_loop(0, n_carry, body, 0)
        plain_chunks(n_carry, n_mc)

        def tail(q, carry):
            request_rows(req_base, n_req, req_slot, q)
            return carry
        lax.fori_loop(n_carry * REQ_GROUPS, GC // SUBLANES, tail, 0)

    @pl.when(live & (n_req == 0))
    def _():
        plain_chunks(0, n_mc)


def _moe_up_call(layer_e0, src, item_e, item_cs, item_cnt, n_items, hp, w1, w3):
    n_f = D_FFE // TF_MOE
    wspec = pl.BlockSpec((None, D, TF_MOE), lambda k, j, s, ie, ics, ic, ni: (
        layer_e0 + ie[k], 0, jnp.where(k < ni[0], j, n_f - 1)))
    return pl.pallas_call(
        _moe_up_kernel,
        grid_spec=pltpu.PrefetchScalarGridSpec(
            num_scalar_prefetch=5,
            grid=(N_ITEMS, D_FFE // TF_MOE),
            in_specs=[pl.BlockSpec(memory_space=pl.ANY), wspec, wspec],
            out_specs=pl.BlockSpec((RT, TF_MOE), lambda k, j, s, ie, ics, ic, ni: (k, j)),
            scratch_shapes=[
                pltpu.VMEM((2, RT, D), BF16),
                pltpu.VMEM((2, GC // SUBLANES, SUBLANES, HALF), U32),
                pltpu.VMEM((D, TF_MOE), BF16),
                pltpu.VMEM((D, TF_MOE), BF16),
                pltpu.SemaphoreType.DMA((2,)),
            ],
        ),
        out_shape=jax.ShapeDtypeStruct((N_ITEMS * RT, D_FFE), BF16),
        compiler_params=_cparams(("arbitrary", "arbitrary")),
        name="moe_up",
    )(src, item_e, item_cs, item_cnt, n_items, hp, w1, w3)


def _moe_down_kernel(ie_ref, icnt_ref, nit_ref, g_ref, w2a_ref, w2b_ref, o_ref, wa_ref, wb_ref):
    k = pl.program_id(0)
    o_ref[...] = jnp.zeros_like(o_ref)

    @pl.when(k < nit_ref[0])
    def _():
        wa_ref[...] = w2a_ref[...].astype(BF16)
        wb_ref[...] = w2b_ref[...].astype(BF16)

        def body(c, carry):
            rows = pl.ds(pl.multiple_of(c * MC, MC), MC)
            g = g_ref[rows, :]
            lo = jnp.dot(g, wa_ref[...], preferred_element_type=F32)
            hi = jnp.dot(g, wb_ref[...], preferred_element_type=F32)
            o_ref[rows, :] = _pack(lo, hi)
            return carry
        lax.fori_loop(0, (icnt_ref[k] + MC - 1) // MC, body, 0)


def _moe_down_call(layer_e0, item_e, item_cnt, n_items, g, w2):
    live = lambda k, ni: jnp.minimum(k, ni[0] - 1)
    n_half = HALF // TN_MOE
    col = lambda k, n, ni: jnp.where(k < ni[0], n, n_half - 1)
    return pl.pallas_call(
        _moe_down_kernel,
        grid_spec=pltpu.PrefetchScalarGridSpec(
            num_scalar_prefetch=3,
            grid=(N_ITEMS, n_half),
            in_specs=[
                pl.BlockSpec((RT, D_FFE), lambda k, n, ie, ic, ni: (live(k, ni), 0)),
                pl.BlockSpec((None, D_FFE, TN_MOE),
                             lambda k, n, ie, ic, ni: (layer_e0 + ie[k], 0, col(k, n, ni))),
                pl.BlockSpec((None, D_FFE, TN_MOE),
                             lambda k, n, ie, ic, ni: (layer_e0 + ie[k], 0, n_half + col(k, n, ni))),
            ],
            out_specs=pl.BlockSpec((RT, TN_MOE), lambda k, n, ie, ic, ni: (k, n)),
            scratch_shapes=[pltpu.VMEM((D_FFE, TN_MOE), BF16), pltpu.VMEM((D_FFE, TN_MOE), BF16)],
        ),
        out_shape=jax.ShapeDtypeStruct((N_ITEMS * RT, HALF), U32),
        compiler_params=_cparams(("arbitrary", "arbitrary")),
        name="moe_down",
    )(item_e, item_cnt, n_items, g, w2, w2)


def _combine_copy(yo_hbm, row, buf, slot, r, sem):
    return pltpu.make_async_copy(yo_hbm.at[pl.ds(row, 1)], buf.at[slot, pl.ds(r, 1)], sem.at[slot])


def _combine_kernel(split, pos_ref, yo_hbm, x_ref, rinfo_ref, mods_ref, *rest):
    if split:
        yp_ref, ys_ref, buf, sem = rest
    else:
        o_ref, buf, sem = rest
    i = pl.program_id(0)
    n = pl.num_programs(0)
    rows = TOP_K * TM

    def issue(tile, slot):
        def body(r, c):
            _combine_copy(yo_hbm, pos_ref[tile * rows + r], buf, slot, r, sem).start()
            return c
        lax.fori_loop(0, rows, body, 0, unroll=8)

    @pl.when(i == 0)
    def _():
        issue(0, 0)

    @pl.when(i + 1 < n)
    def _():
        issue(i + 1, (i + 1) & 1)

    slot = i & 1

    def body(r, c):
        _combine_copy(yo_hbm, 0, buf, slot, r, sem).wait()
        return c
    lax.fori_loop(0, rows, body, 0, unroll=8)
    rinfo = rinfo_ref[...]
    lo0, hi0 = _unpack(buf[slot, 0:TM, :])
    lo1, hi1 = _unpack(buf[slot, TM:2 * TM, :])
    gt0, gt1 = rinfo[:, 2:3], rinfo[:, 3:4]
    g2 = mods_ref[pl.ds(_mod_row(i), 1), :]
    y_lo = x_ref[:, 0:HALF] + g2[:, 0:HALF] * (gt0 * lo0 + gt1 * lo1)
    y_hi = x_ref[:, HALF:D] + g2[:, HALF:D] * (gt0 * hi0 + gt1 * hi1)

    def write(ref):
        ref[:, 0:HALF] = y_lo
        ref[:, HALF:D] = y_hi

    if not split:
        write(o_ref)
        return

    @pl.when(i < P_TILES)
    def _():
        write(yp_ref)

    @pl.when(i >= P_TILES)
    def _():
        write(ys_ref)


def _combine_call(l, split, pos, yo, x, rinfo, mods_all):
    if split:
        out_specs = [pl.BlockSpec((TM, D), lambda i, p: (jnp.minimum(i, P_TILES - 1), 0)),
                     pl.BlockSpec((TM, D), lambda i, p: (jnp.maximum(i - P_TILES, 0), 0))]
        out_shape = [jax.ShapeDtypeStruct((M_P, D), F32), jax.ShapeDtypeStruct((M_S, D), F32)]
    else:
        out_specs = pl.BlockSpec((TM, D), lambda i, p: (i, 0))
        out_shape = jax.ShapeDtypeStruct((M_TOK, D), F32)
    return pl.pallas_call(
        functools.partial(_combine_kernel, split),
        grid_spec=pltpu.PrefetchScalarGridSpec(
            num_scalar_prefetch=1,
            grid=(N_TILES,),
            in_specs=[
                pl.BlockSpec(memory_space=pl.ANY),
                pl.BlockSpec((TM, D), lambda i, p: (i, 0)),
                pl.BlockSpec((TM, LANE), lambda i, p: (i, 0)),
                pl.BlockSpec((None, MOD_ROWS, D), lambda i, p: (l, 0, 5)),
            ],
            out_specs=out_specs,
            scratch_shapes=[pltpu.VMEM((2, TOP_K * TM, HALF), U32), pltpu.SemaphoreType.DMA((2,))],
        ),
        out_shape=out_shape,
        compiler_params=_cparams(("arbitrary",)),
        name="moe_combine",
    )(pos, yo, x, rinfo, mods_all)


def _route_plan(rinfo):
    flat_e = rinfo[:, 0:TOP_K].astype(I32).reshape(-1)
    onehot = (flat_e[:, None] == jnp.arange(N_EXP, dtype=I32)[None, :]).astype(I32)
    csum = jnp.cumsum(onehot, axis=0)
    pick = lambda table: jnp.sum(onehot * table[None, :], axis=1)
    rank = jnp.sum(onehot * csum, axis=1) - 1
    counts = csum[-1]
    cstart = jnp.cumsum(counts) - counts
    n_it = (counts + RT - 1) // RT
    it_end = jnp.cumsum(n_it)
    it_base = it_end - n_it
    n_items = it_end[-1]
    cpos = pick(cstart) + rank
    ppos = (pick(it_base) + rank // RT) * RT + rank % RT
    q = jnp.arange(N_ITEMS, dtype=I32)
    qc = jnp.minimum(q, n_items - 1)
    qe = jnp.minimum(jnp.sum((it_end[None, :] <= qc[:, None]).astype(I32), axis=1), N_EXP - 1)
    qhot = (qe[:, None] == jnp.arange(N_EXP, dtype=I32)[None, :]).astype(I32)
    qpick = lambda table: jnp.sum(qhot * table[None, :], axis=1)
    s = qc - qpick(it_base)
    item_cnt = jnp.where(q < n_items, jnp.clip(qpick(counts) - s * RT, 0, RT), 0)
    item_cs = qpick(cstart) + s * RT
    pos_tiles = ppos.reshape(N_TILES, TM, TOP_K).transpose(0, 2, 1).reshape(-1)
    return (cpos.astype(I32), qe.astype(I32), item_cs.astype(I32), item_cnt.astype(I32),
            n_items.reshape(1).astype(I32), pos_tiles.astype(I32))


def _rope_tables():
    rows = DEC_SEQ // GRID_W
    row = jnp.repeat(jnp.arange(rows), GRID_W).astype(F32)
    col = jnp.tile(jnp.arange(GRID_W), rows).astype(F32)
    nf = ROPE // 4
    inv = ROPE_BASE ** (-jnp.arange(nf, dtype=F32) / nf)
    ar = row[:, None] * inv[None, :]
    ac = col[:, None] * inv[None, :]
    ang = jnp.concatenate([ar, ar, ac, ac], axis=-1)
    cos, sin = jnp.cos(ang), jnp.sin(ang)
    first = (jnp.arange(ROPE) % 32) < 16
    pad = lambda a, v: jnp.pad(a, ((0, 0), (0, LANE - ROPE)), constant_values=v)
    cos_p = pad(cos, 1.0)
    sin_a = pad(jnp.where(first[None, :], -sin, 0.0), 0.0)
    sin_b = pad(jnp.where(first[None, :], 0.0, sin), 0.0)
    ident = jnp.stack([jnp.ones((TMI, LANE), F32), jnp.zeros((TMI, LANE), F32), jnp.zeros((TMI, LANE), F32)])
    return jnp.concatenate([ident, jnp.stack([cos_p, sin_a, sin_b])], axis=1)


def kernel(x_prompt, x_sample, cache_ckv, cache_krope, c, c_ctx, w_ada, b_ada, norm1_g, norm2_g, w_in,
           qlat_norm_g, kvlat_norm_g, w_uq, w_ukv, q_head_norm_g, k_head_norm_g, gm_norm_g, gm_w_s,
           gm_b_s, pool_w, pool_scale, w_o, ffn_w1, ffn_w3, ffn_w2, router_w, moe_w1, moe_w3, moe_w2):
    pad_w = IN_PAD - w_in.shape[-1]
    col = jnp.arange(IN_PAD)[None, None, :]
    w_in_p = jnp.where(
        col < C_KR + ROPE, jnp.pad(w_in, ((0, 0), (0, 0), (0, pad_w))),
        jnp.where(col < C_KR + LANE, 0.0, jnp.pad(w_in, ((0, 0), (0, 0), (pad_w, 0))))).astype(BF16)
    w_uq_p = jnp.pad(w_uq.reshape(DEPTH, Q_RANK, N_HEADS, QK_DIM),
                     ((0, 0), (0, 0), (0, 0), (0, HEAD_PAD - QK_DIM))
                     ).reshape(DEPTH, Q_RANK, QK_W).astype(BF16)
    w_ukv_b = w_ukv.astype(BF16)
    w_o_b = w_o.astype(BF16)
    qg_p = jnp.pad(q_head_norm_g * (1.0 / math.sqrt(QK_DIM)),
                   ((0, 0), (0, HEAD_PAD - QK_DIM))).reshape(DEPTH, 1, HEAD_PAD)
    kg_n = k_head_norm_g[:, :NOPE].reshape(DEPTH, 1, LANE)
    kg_r = jnp.pad(k_head_norm_g[:, NOPE:], ((0, 0), (0, LANE - ROPE))).reshape(DEPTH, 1, LANE)
    gmw_b = gm_w_s.astype(BF16)
    gmb = jnp.broadcast_to(gm_b_s[:, :, :, None], (DEPTH, GM_GROUPS, CHUNK, LANE))
    pw_b = pool_w.astype(BF16)
    ps = pool_scale.reshape(DEPTH, 1, POOL_W)
    n1g = norm1_g.reshape(DEPTH, 1, D)
    n2g = norm2_g.reshape(DEPTH, 1, D)
    qlg = qlat_norm_g.reshape(DEPTH, 1, Q_RANK)
    kvlg = kvlat_norm_g.reshape(DEPTH, 1, KV_RANK)
    rw_p = jnp.pad(router_w, ((0, 0), (0, 0), (0, LANE - N_EXP)))
    n_moe = moe_w1.shape[0]
    moe_w1_flat = moe_w1.reshape(n_moe * N_EXP, D, D_FFE)
    moe_w3_flat = moe_w3.reshape(n_moe * N_EXP, D, D_FFE)
    moe_w2_flat = moe_w2.reshape(n_moe * N_EXP, D_FFE, D)
    rope_tab = _rope_tables()
    cache_kr_p = jnp.pad(cache_krope, ((0, 0), (0, 0), (0, 0), (0, LANE - ROPE)))

    cond = jnp.zeros((MOD_ROWS, D), F32).at[0].set(c_ctx).at[1:1 + DEC_BATCH].set(c)
    mods_all = _ada_call(cond, w_ada, b_ada)
    k_c, v_c = _cache_kv_call(cache_ckv, cache_kr_p, w_ukv_b, kg_n, kg_r)

    x = jnp.concatenate([x_prompt.reshape(M_P, D), x_sample.reshape(M_S, D)], axis=0)
    ckv_out, kr_out = [], []
    for l in range(DEPTH):
        q, k, v, ckv, kr, zg, zp = _in_proj_call(
            l, x, mods_all, n1g, w_in_p, qlg, kvlg, w_uq_p, qg_p, w_ukv_b, kg_n, kg_r, rope_tab,
            gm_norm_g, gmw_b, gmb)
        ckv_out.append(ckv[:M_P].reshape(BATCH, SEQ, KV_RANK))
        kr_out.append(kr[:M_P, :ROPE].reshape(BATCH, SEQ, ROPE))
        attn = _attn_call(l, q, k, v, k_c, v_c)
        i = l // 2
        if l % 2 == 0:
            x, h2b = _out_proj_call(l, False, attn, zg, zp, x, mods_all, pw_b, ps, w_o_b, n2g)
            g = _ffn_up_call(i, h2b, ffn_w1, ffn_w3)
            x = _ffn_down_call(l, g, ffn_w2, x, mods_all)
        else:
            x, hp, rinfo = _out_proj_call(l, True, attn, zg, zp, x, mods_all, pw_b, ps, w_o_b, n2g, rw_p)
            cpos, item_e, item_cs, item_cnt, n_items, pos_tiles = _route_plan(rinfo)
            src = _src_call(cpos)
            g = _moe_up_call(i * N_EXP, src, item_e, item_cs, item_cnt, n_items, hp,
                             moe_w1_flat, moe_w3_flat)
            yo = _moe_down_call(i * N_EXP, item_e, item_cnt, n_items, g, moe_w2_flat)
            x = _combine_call(l, l == DEPTH - 1, pos_tiles, yo, x, rinfo, mods_all)

    if DEPTH % 2 == 0:
        y_p, y_s = x
    else:
        y_p, y_s = x[:M_P], x[M_P:]
    return (y_p.reshape(BATCH, SEQ, D), y_s.reshape(DEC_BATCH, DEC_SEQ, D),
            jnp.stack(ckv_out, axis=1), jnp.stack(kr_out, axis=1))
```

```python
import functools
import math

import jax
import jax.numpy as jnp
from jax import lax
from jax.experimental import pallas as pl
from jax.experimental.pallas import tpu as pltpu

F32 = jnp.float32
BF16 = jnp.bfloat16
U32 = jnp.uint32
I32 = jnp.int32

D = 2048
BATCH, SEQ = 16, 256
DEC_BATCH, DEC_SEQ = 2, 1024
PAST = 256
DEPTH = 4
GRID_W = 64
EPS = 1e-6
N_HEADS = 8
NOPE, ROPE, VDIM = 128, 64, 128
QK_DIM = NOPE + ROPE
HEAD_PAD = 256
QK_W = N_HEADS * HEAD_PAD
Q_RANK, KV_RANK = 512, 256
ROPE_BASE = 10000.0
ATTN_W = N_HEADS * VDIM
CHUNK, GM_GROUPS, GM_W = 128, 4, 512
POOL_WINDOWS = (2, 4, 8, 16)
POOL_W = 512
POOL_HALO = 8
D_FF, N_EXP, D_FFE, TOP_K = 5632, 8, 2816, 2
TOP_K_SHIFT = 1

C_Q, C_KV, C_KR, C_GM, C_PL = 0, 512, 768, 896, 1920
IN_PAD = 2432

LANE = 128
SUBLANES = 8
TM = 256
TMI = 512
M_P, M_S = BATCH * SEQ, DEC_BATCH * DEC_SEQ
M_TOK = M_P + M_S
N_TILES = M_TOK // TM
P_TILES = M_P // TM
S_TILES = DEC_SEQ // TM
MOD_ROWS = 16
ADA_TN = 1024
TE_UP, TF_UP = 1024, 512
TE_DOWN, TN_DOWN = 512, 512
HALF = D // 2
N_SLOTS = TOP_K * M_TOK
RT = 2048
N_ITEMS = N_EXP + N_SLOTS // RT
GC = 256
N_GC = RT // GC
MC = 512
REQ_CHUNKS = 2
REQ_GROUPS = GC // SUBLANES // REQ_CHUNKS
TF_MOE = 256
TN_MOE = 256
VMEM_LIMIT = 56 * 1024 * 1024


def _cparams(sem):
    return pltpu.CompilerParams(dimension_semantics=sem, vmem_limit_bytes=VMEM_LIMIT)


def _const_spec(shape):
    nd = len(shape)
    return pl.BlockSpec(shape, lambda *_: (0,) * nd)


def _layer_spec(l, shape):
    nd = len(shape)
    return pl.BlockSpec((None,) + tuple(shape), lambda *_: (l,) + (0,) * nd)


def _mod_row(i):
    return jnp.where(i < P_TILES, 0, 1 + (i - P_TILES) // S_TILES)


def _rms(x, g, n):
    ms = jnp.sum(x * x, axis=-1, keepdims=True) * (1.0 / n)
    return x * lax.rsqrt(ms + EPS) * g


def _rope(r, cos, sin_a, sin_b):
    return r * cos + pltpu.roll(r, LANE - 16, 1) * sin_a + pltpu.roll(r, 16, 1) * sin_b


def _pack(lo, hi):
    lo_b = lax.bitcast_convert_type(lo.astype(BF16).astype(F32), U32)
    hi_b = lax.bitcast_convert_type(hi.astype(BF16).astype(F32), U32)
    return (lo_b >> 16) | hi_b


def _unpack(w):
    lo = lax.bitcast_convert_type(w << 16, F32)
    hi = lax.bitcast_convert_type(w & jnp.uint32(0xFFFF0000), F32)
    return lo, hi


def _ada_kernel(cond_ref, w_ref, b_ref, o_ref):
    c = cond_ref[...]
    s = c * jax.nn.sigmoid(c)
    s_hi = s.astype(BF16)
    s_lo = (s - s_hi.astype(F32)).astype(BF16)
    w = w_ref[...]
    w_hi = w.astype(BF16)
    w_lo = (w - w_hi.astype(F32)).astype(BF16)
    lhs = jnp.concatenate([s_hi, s_lo], axis=0)
    r1 = jnp.dot(lhs, w_hi, preferred_element_type=F32)
    r2 = jnp.dot(s_hi, w_lo, preferred_element_type=F32)
    o_ref[...] = r1[:MOD_ROWS] + r1[MOD_ROWS:] + r2 + b_ref[...]


def _ada_call(cond, w_ada, b_ada):
    n = 6 * D
    return pl.pallas_call(
        _ada_kernel,
        grid=(DEPTH, n // ADA_TN),
        in_specs=[
            pl.BlockSpec((MOD_ROWS, D), lambda l, j: (0, 0)),
            pl.BlockSpec((None, D, ADA_TN), lambda l, j: (l, 0, j)),
            pl.BlockSpec((None, 1, ADA_TN), lambda l, j: (l, 0, j)),
        ],
        out_specs=pl.BlockSpec((None, MOD_ROWS, ADA_TN), lambda l, j: (l, 0, j)),
        out_shape=jax.ShapeDtypeStruct((DEPTH, MOD_ROWS, n), F32),
        compiler_params=_cparams(("parallel", "parallel")),
        name="ada",
    )(cond, w_ada, b_ada.reshape(DEPTH, 1, n))


def _write_kv(ckv_b, kr, w_ukv_ref, kg_n, kg_r, rope, k_ref, v_ref, rows=slice(None)):
    kv = jnp.dot(ckv_b, w_ukv_ref[...], preferred_element_type=F32)
    ssr = jnp.sum(kr * kr, axis=-1, keepdims=True)
    shared = kr * kg_r
    if rope is not None:
        shared = _rope(shared, *rope)
    for h in range(N_HEADS):
        kn = kv[:, h * HEAD_PAD:h * HEAD_PAD + NOPE]
        vv = kv[:, h * HEAD_PAD + NOPE:(h + 1) * HEAD_PAD]
        ms = (jnp.sum(kn * kn, axis=-1, keepdims=True) + ssr) * (1.0 / QK_DIM)
        inv = lax.rsqrt(ms + EPS)
        k_ref[rows, h * HEAD_PAD:h * HEAD_PAD + NOPE] = (kn * inv * kg_n).astype(BF16)
        k_ref[rows, h * HEAD_PAD + NOPE:(h + 1) * HEAD_PAD] = (shared * inv).astype(BF16)
        v_ref[rows, h * VDIM:(h + 1) * VDIM] = vv.astype(BF16)


def _cache_kv_kernel(ckv_ref, kr_ref, w_ukv_ref, kgn_ref, kgr_ref, k_ref, v_ref):
    _write_kv(ckv_ref[...].astype(BF16), kr_ref[...], w_ukv_ref, kgn_ref[...], kgr_ref[...],
              None, k_ref, v_ref)


def _cache_kv_call(cache_ckv, cache_kr_p, w_ukv_b, kg_n, kg_r):
    return pl.pallas_call(
        _cache_kv_kernel,
        grid=(DEPTH, DEC_BATCH),
        in_specs=[
            pl.BlockSpec((None, None, PAST, KV_RANK), lambda l, b: (b, l, 0, 0)),
            pl.BlockSpec((None, None, PAST, LANE), lambda l, b: (b, l, 0, 0)),
            pl.BlockSpec((None, KV_RANK, QK_W), lambda l, b: (l, 0, 0)),
            pl.BlockSpec((None, 1, LANE), lambda l, b: (l, 0, 0)),
            pl.BlockSpec((None, 1, LANE), lambda l, b: (l, 0, 0)),
        ],
        out_specs=[
            pl.BlockSpec((None, None, PAST, QK_W), lambda l, b: (l, b, 0, 0)),
            pl.BlockSpec((None, None, PAST, ATTN_W), lambda l, b: (l, b, 0, 0)),
        ],
        out_shape=[
            jax.ShapeDtypeStruct((DEPTH, DEC_BATCH, PAST, QK_W), BF16),
            jax.ShapeDtypeStruct((DEPTH, DEC_BATCH, PAST, ATTN_W), BF16),
        ],
        compiler_params=_cparams(("parallel", "parallel")),
        name="cache_kv",
    )(cache_ckv, cache_kr_p, w_ukv_b, kg_n, kg_r)


def _in_proj_kernel(x_ref, mods_ref, n1g_ref, w_in_ref, qlg_ref, kvlg_ref, w_uq_ref, qg_ref,
                    w_ukv_ref, kgn_ref, kgr_ref, rope_ref, gmg_ref, gmw_ref, gmb_ref,
                    q_ref, k_ref, v_ref, ckv_ref, kr_ref, zg_ref, zp_ref):
    i = pl.program_id(0)
    row = jnp.where(i < M_P // TMI, 0, 1 + (i - M_P // TMI) // (DEC_SEQ // TMI))
    m = mods_ref[pl.ds(row, 1), :]
    sh1, sc1 = m[:, 0:D], m[:, D:2 * D]
    for sub in range(TMI // TM):
        rs = slice(sub * TM, (sub + 1) * TM)
        x = x_ref[rs, :]
        h = _rms(x, n1g_ref[...], D) * (1.0 + sc1) + sh1
        z = jnp.dot(h.astype(BF16), w_in_ref[...], preferred_element_type=F32)
        rope = (rope_ref[0, rs, :], rope_ref[1, rs, :], rope_ref[2, rs, :])

        ql = _rms(z[:, C_Q:C_Q + Q_RANK], qlg_ref[...], Q_RANK)
        qf = jnp.dot(ql.astype(BF16), w_uq_ref[...], preferred_element_type=F32)
        qg = qg_ref[...]
        for hd in range(N_HEADS):
            seg = qf[:, hd * HEAD_PAD:(hd + 1) * HEAD_PAD]
            seg = _rms(seg, qg, QK_DIM)
            q_ref[rs, hd * HEAD_PAD:hd * HEAD_PAD + NOPE] = seg[:, :NOPE].astype(BF16)
            q_ref[rs, hd * HEAD_PAD + NOPE:(hd + 1) * HEAD_PAD] = _rope(seg[:, NOPE:], *rope).astype(BF16)

        ckv = _rms(z[:, C_KV:C_KV + KV_RANK], kvlg_ref[...], KV_RANK)
        kr = z[:, C_KR:C_KR + LANE]
        ckv_ref[rs, :] = ckv
        kr_ref[rs, :] = kr
        _write_kv(ckv.astype(BF16), kr, w_ukv_ref, kgn_ref[...], kgr_ref[...], rope, k_ref, v_ref, rs)

        zg = jax.nn.gelu(z[:, C_GM:C_GM + 2 * GM_W], approximate=True)
        for g in range(GM_GROUPS):
            u = zg[:, g * LANE:(g + 1) * LANE]
            vg = _rms(zg[:, GM_W + g * LANE:GM_W + (g + 1) * LANE], gmg_ref[g:g + 1, :], LANE)
            vb = vg.astype(BF16)
            for c in range(TM // CHUNK):
                sv = jnp.dot(gmw_ref[g], vb[c * CHUNK:(c + 1) * CHUNK, :], preferred_element_type=F32)
                sv = sv + gmb_ref[g]
                r0 = sub * TM + c * CHUNK
                zg_ref[r0:r0 + CHUNK, g * LANE:(g + 1) * LANE] = (
                    u[c * CHUNK:(c + 1) * CHUNK, :] * sv).astype(BF16)

        zp_ref[rs, :] = z[:, C_PL:C_PL + POOL_W]


def _in_proj_call(l, x, mods_all, n1g, w_in_b, qlg, kvlg, w_uq_b, qg, w_ukv_b, kg_n, kg_r, rope_tab,
                  gmg, gmw_b, gmb):
    def rope_idx(i):
        return (0, jnp.where(i < M_P // TMI, 0, 1 + (i - M_P // TMI) % (DEC_SEQ // TMI)), 0)

    row = lambda w: pl.BlockSpec((TMI, w), lambda i: (i, 0))
    return pl.pallas_call(
        _in_proj_kernel,
        grid=(M_TOK // TMI,),
        in_specs=[
            row(D),
            _layer_spec(l, (MOD_ROWS, 6 * D)),
            _layer_spec(l, (1, D)),
            _layer_spec(l, (D, IN_PAD)),
            _layer_spec(l, (1, Q_RANK)),
            _layer_spec(l, (1, KV_RANK)),
            _layer_spec(l, (Q_RANK, QK_W)),
            _layer_spec(l, (1, HEAD_PAD)),
            _layer_spec(l, (KV_RANK, QK_W)),
            _layer_spec(l, (1, LANE)),
            _layer_spec(l, (1, LANE)),
            pl.BlockSpec((3, TMI, LANE), rope_idx),
            _layer_spec(l, (GM_GROUPS, LANE)),
            _layer_spec(l, (GM_GROUPS, CHUNK, CHUNK)),
            _layer_spec(l, (GM_GROUPS, CHUNK, LANE)),
        ],
        out_specs=[row(QK_W), row(QK_W), row(ATTN_W), row(KV_RANK), row(LANE), row(GM_W), row(POOL_W)],
        out_shape=[
            jax.ShapeDtypeStruct((M_TOK, QK_W), BF16),
            jax.ShapeDtypeStruct((M_TOK, QK_W), BF16),
            jax.ShapeDtypeStruct((M_TOK, ATTN_W), BF16),
            jax.ShapeDtypeStruct((M_TOK, KV_RANK), F32),
            jax.ShapeDtypeStruct((M_TOK, LANE), F32),
            jax.ShapeDtypeStruct((M_TOK, GM_W), BF16),
            jax.ShapeDtypeStruct((M_TOK, POOL_W), F32),
        ],
        compiler_params=_cparams(("parallel",)),
        name="in_proj",
    )(x, mods_all, n1g, w_in_b, qlg, kvlg, w_uq_b, qg, w_ukv_b, kg_n, kg_r, rope_tab, gmg, gmw_b, gmb)


def _attend_heads(q_ref, kv_pairs, o_ref):
    nt = (((1,), (1,)), ((), ()))
    for h in range(N_HEADS):
        qh = q_ref[:, h * HEAD_PAD:(h + 1) * HEAD_PAD]
        ss = [lax.dot_general(qh, k_ref[:, h * HEAD_PAD:(h + 1) * HEAD_PAD], nt,
                              preferred_element_type=F32) for k_ref, _ in kv_pairs]
        mx = ss[0].max(axis=-1, keepdims=True)
        for s in ss[1:]:
            mx = jnp.maximum(mx, s.max(axis=-1, keepdims=True))
        den = None
        acc = None
        for s, (_, v_ref) in zip(ss, kv_pairs):
            p = jnp.exp(s - mx)
            d = jnp.sum(p, axis=-1, keepdims=True)
            a = jnp.dot(p.astype(BF16), v_ref[:, h * VDIM:(h + 1) * VDIM], preferred_element_type=F32)
            den = d if den is None else den + d
            acc = a if acc is None else acc + a
        o_ref[:, h * VDIM:(h + 1) * VDIM] = (acc / den).astype(BF16)


def _attn_kernel(q_ref, kp_ref, vp_ref, ks_ref, vs_ref, kc_ref, vc_ref, o_ref):
    i = pl.program_id(0)

    @pl.when(i < P_TILES)
    def _():
        _attend_heads(q_ref, [(kp_ref, vp_ref)], o_ref)

    @pl.when(i >= P_TILES)
    def _():
        _attend_heads(q_ref, [(ks_ref, vs_ref), (kc_ref, vc_ref)], o_ref)


def _attn_call(l, q, k, v, k_c, v_c):
    def seq_b(i):
        return jnp.maximum(i - P_TILES, 0) // S_TILES

    small = lambda w: pl.BlockSpec((TM, w), lambda i: (jnp.minimum(i, P_TILES - 1), 0))
    big = lambda w: pl.BlockSpec((DEC_SEQ, w), lambda i: (M_P // DEC_SEQ + seq_b(i), 0))
    cache = lambda w: pl.BlockSpec((None, None, PAST, w), lambda i: (l, seq_b(i), 0, 0))
    return pl.pallas_call(
        _attn_kernel,
        grid=(N_TILES,),
        in_specs=[
            pl.BlockSpec((TM, QK_W), lambda i: (i, 0)),
            small(QK_W), small(ATTN_W),
            big(QK_W), big(ATTN_W),
            cache(QK_W), cache(ATTN_W),
        ],
        out_specs=pl.BlockSpec((TM, ATTN_W), lambda i: (i, 0)),
        out_shape=jax.ShapeDtypeStruct((M_TOK, ATTN_W), BF16),
        compiler_params=_cparams(("parallel",)),
        name="attention",
    )(q, k, v, k, v, k_c, v_c)


def _out_proj_kernel(moe, attn_ref, zg_ref, zprev_ref, zcur_ref, znext_ref, x_ref, mods_ref,
                     pw_ref, ps_ref, w_o_ref, n2g_ref, *rest):
    if moe:
        rw_ref, xo_ref, hp_ref, rinfo_ref, ext_ref, mix_ref = rest
    else:
        xo_ref, h2b_ref, ext_ref, mix_ref = rest
    i = pl.program_id(0)
    ctx_steps = M_P // TMI
    is_ctx = i < ctx_steps
    seq_len = jnp.where(is_ctx, SEQ, DEC_SEQ)
    m = mods_ref[pl.ds(jnp.where(is_ctx, 0, 1 + (i - ctx_steps) // (DEC_SEQ // TMI)), 1), :]
    g1, sh2, sc2 = m[:, 2 * D:3 * D], m[:, 3 * D:4 * D], m[:, 4 * D:5 * D]

    ext_ref[0:POOL_HALO, :] = zprev_ref[TMI - POOL_HALO:, :]
    ext_ref[POOL_HALO:POOL_HALO + TMI, :] = zcur_ref[...]
    ext_ref[POOL_HALO + TMI:, :] = znext_ref[0:POOL_HALO, :]
    mix_ref[:, 0:ATTN_W] = attn_ref[...]
    mix_ref[:, ATTN_W:ATTN_W + GM_W] = zg_ref[...]

    for sub in range(TMI // TM):
        rs = slice(sub * TM, (sub + 1) * TM)
        pos0 = jnp.where(is_ctx, 0, ((i - ctx_steps) % (DEC_SEQ // TMI)) * TMI + sub * TM)
        pos = pos0 + lax.broadcasted_iota(I32, (TM, 1), 0)
        inside = {dlt: (pos + dlt >= 0) & (pos + dlt < seq_len)
                  for dlt in range(-POOL_HALO, POOL_HALO) if dlt != 0}
        for gi, w in enumerate(POOL_WINDOWS):
            cols = slice(gi * LANE, (gi + 1) * LANE)
            win = None
            for dlt in range(-(w // 2), w - w // 2):
                r0 = POOL_HALO + sub * TM + dlt
                part = ext_ref[r0:r0 + TM, cols]
                if dlt != 0:
                    part = jnp.where(inside[dlt], part, 0.0)
                win = part if win is None else win + part
            lo = jnp.maximum(pos - w // 2, 0)
            hi = jnp.minimum(pos + (w - 1 - w // 2), seq_len - 1)
            cnt = (hi - lo + 1).astype(F32)
            pooled = win / cnt - zcur_ref[rs, cols]
            mixed = jnp.dot(pooled.astype(BF16), pw_ref[gi], preferred_element_type=F32)
            mix_ref[rs, ATTN_W + GM_W + gi * LANE:ATTN_W + GM_W + (gi + 1) * LANE] = (
                mixed * ps_ref[:, cols]).astype(BF16)

        out = jnp.dot(mix_ref[rs, :], w_o_ref[...], preferred_element_type=F32)
        xn = x_ref[rs, :] + g1 * out
        xo_ref[rs, :] = xn
        h2 = _rms(xn, n2g_ref[...], D) * (1.0 + sc2) + sh2

        if not moe:
            h2b_ref[rs, :] = h2.astype(BF16)
            continue

        hp_ref[rs, :] = _pack(h2[:, :HALF], h2[:, HALF:])
        h_hi = h2.astype(BF16)
        h_lo = (h2 - h_hi.astype(F32)).astype(BF16)
        rw = rw_ref[...]
        w_hi = rw.astype(BF16)
        w_lo = (rw - w_hi.astype(F32)).astype(BF16)
        logits = (jnp.dot(h_hi, w_hi, preferred_element_type=F32)
                  + jnp.dot(h_lo, w_hi, preferred_element_type=F32)
                  + jnp.dot(h_hi, w_lo, preferred_element_type=F32))
        lane = lax.broadcasted_iota(I32, (TM, LANE), 1).astype(F32)
        neg = jnp.float32(-jnp.inf)
        logits = jnp.where(lane < N_EXP, logits, neg)
        v1 = logits.max(axis=-1, keepdims=True)
        i1 = jnp.where(logits == v1, lane, float(LANE)).min(axis=-1, keepdims=True)
        l2 = jnp.where(lane == i1, neg, logits)
        v2 = l2.max(axis=-1, keepdims=True)
        i2 = jnp.where(l2 == v2, lane, float(LANE)).min(axis=-1, keepdims=True)
        e2 = jnp.exp(v2 - v1)
        gate1 = 1.0 / (1.0 + e2)
        gate2 = e2 / (1.0 + e2)
        rinfo_ref[rs, :] = jnp.where(lane == 0, i1, jnp.where(lane == 1, i2, jnp.where(
            lane == 2, gate1, jnp.where(lane == 3, gate2, 0.0))))


def _out_proj_call(l, moe, attn, zg, zp, x, mods_all, pw_b, ps, w_o_b, n2g, rw_p=None):
    n_steps = M_TOK // TMI
    row = lambda w: pl.BlockSpec((TMI, w), lambda i: (i, 0))
    in_specs = [
        row(ATTN_W), row(GM_W),
        pl.BlockSpec((TMI, POOL_W), lambda i: (jnp.maximum(i - 1, 0), 0)),
        row(POOL_W),
        pl.BlockSpec((TMI, POOL_W), lambda i: (jnp.minimum(i + 1, n_steps - 1), 0)),
        row(D),
        _layer_spec(l, (MOD_ROWS, 6 * D)),
        _layer_spec(l, (len(POOL_WINDOWS), LANE, LANE)),
        _layer_spec(l, (1, POOL_W)),
        _layer_spec(l, (D, D)),
        _layer_spec(l, (1, D)),
    ]
    args = [attn, zg, zp, zp, zp, x, mods_all, pw_b, ps, w_o_b, n2g]
    if moe:
        in_specs.append(_layer_spec(l // 2, (D, LANE)))
        args.append(rw_p)
        out_specs = [row(D), row(HALF), row(LANE)]
        out_shape = [jax.ShapeDtypeStruct((M_TOK, D), F32), jax.ShapeDtypeStruct((M_TOK, HALF), U32),
                     jax.ShapeDtypeStruct((M_TOK, LANE), F32)]
    else:
        out_specs = [row(D), row(D)]
        out_shape = [jax.ShapeDtypeStruct((M_TOK, D), F32), jax.ShapeDtypeStruct((M_TOK, D), BF16)]
    return pl.pallas_call(
        functools.partial(_out_proj_kernel, moe),
        grid=(n_steps,),
        in_specs=in_specs,
        out_specs=out_specs,
        out_shape=out_shape,
        scratch_shapes=[pltpu.VMEM((TMI + 2 * POOL_HALO, POOL_W), F32), pltpu.VMEM((TMI, D), BF16)],
        compiler_params=_cparams(("parallel",)),
        name="out_proj_moe" if moe else "out_proj",
    )(*args)


def _ffn_up_kernel(h_ref, w1_ref, w3_ref, g_ref):
    h = h_ref[...]
    a = jnp.dot(h, w1_ref[...].astype(BF16), preferred_element_type=F32)
    b = jnp.dot(h, w3_ref[...].astype(BF16), preferred_element_type=F32)
    g_ref[...] = (a * jax.nn.sigmoid(a) * b).astype(BF16)


def _ffn_up_call(i, h, w1, w3):
    wspec = pl.BlockSpec((None, D, TF_UP), lambda j, t: (i, 0, j))
    return pl.pallas_call(
        _ffn_up_kernel,
        grid=(D_FF // TF_UP, M_TOK // TE_UP),
        in_specs=[pl.BlockSpec((TE_UP, D), lambda j, t: (t, 0)), wspec, wspec],
        out_specs=pl.BlockSpec((TE_UP, TF_UP), lambda j, t: (t, j)),
        out_shape=jax.ShapeDtypeStruct((M_TOK, D_FF), BF16),
        compiler_params=_cparams(("parallel", "parallel")),
        name="ffn_up",
    )(h, w1, w3)


def _ffn_down_kernel(g_ref, w2_ref, x_ref, mods_ref, o_ref):
    t = pl.program_id(1)
    y = jnp.dot(g_ref[...], w2_ref[...].astype(BF16), preferred_element_type=F32)
    ctx_tiles = M_P // TE_DOWN
    row = jnp.where(t < ctx_tiles, 0, 1 + (t - ctx_tiles) // (DEC_SEQ // TE_DOWN))
    o_ref[...] = x_ref[...] + mods_ref[pl.ds(row, 1), :] * y


def _ffn_down_call(l, g, w2, x, mods_all):
    i = l // 2
    return pl.pallas_call(
        _ffn_down_kernel,
        grid=(D // TN_DOWN, M_TOK // TE_DOWN),
        in_specs=[
            pl.BlockSpec((TE_DOWN, D_FF), lambda j, t: (t, 0)),
            pl.BlockSpec((None, D_FF, TN_DOWN), lambda j, t: (i, 0, j)),
            pl.BlockSpec((TE_DOWN, TN_DOWN), lambda j, t: (t, j)),
            pl.BlockSpec((None, MOD_ROWS, TN_DOWN), lambda j, t: (l, 0, 5 * (D // TN_DOWN) + j)),
        ],
        out_specs=pl.BlockSpec((TE_DOWN, TN_DOWN), lambda j, t: (t, j)),
        out_shape=jax.ShapeDtypeStruct((M_TOK, D), F32),
        compiler_params=_cparams(("parallel", "parallel")),
        name="ffn_down",
    )(g, w2, x, mods_all)


def _src_kernel(cpos_ref, src_ref):
    def body(i, c):
        src_ref[cpos_ref[i]] = lax.shift_right_logical(i, TOP_K_SHIFT)
        return c
    lax.fori_loop(0, N_SLOTS, body, 0, unroll=8)


def _src_call(cpos):
    return pl.pallas_call(
        _src_kernel,
        in_specs=[pl.BlockSpec(memory_space=pltpu.SMEM)],
        out_specs=pl.BlockSpec(memory_space=pltpu.SMEM),
        out_shape=jax.ShapeDtypeStruct((N_SLOTS,), I32),
        name="moe_src",
    )(cpos)


def _gather_copy(hp_hbm, tok, stage, slot, q, u, sem):
    return pltpu.make_async_copy(hp_hbm.at[pl.ds(tok, 1)], stage.at[slot, q, pl.ds(u, 1)], sem.at[slot])


def _moe_up_kernel(src_ref, ie_ref, ics_ref, icnt_ref, nit_ref, hp_hbm, w1_ref, w3_ref, g_ref,
                   buf, stage, sem):
    k = pl.program_id(0)
    j = pl.program_id(1)
    nit = nit_ref[0]

    def chunk_rows(item, c):
        return jnp.clip(icnt_ref[item] - c * GC, 0, GC)

    def request_rows(base, n, slot, q):
        for u in range(SUBLANES):
            tok = src_ref[base + jnp.minimum(q * SUBLANES + u, n - 1)]
            _gather_copy(hp_hbm, tok, stage, slot, q, u, sem).start()

    def issue_chunk(item, c, slot):
        n = chunk_rows(item, c)
        base = ics_ref[item] + c * GC

        @pl.when(n > 0)
        def _():
            def body(q, carry):
                request_rows(base, n, slot, q)
                return carry
            lax.fori_loop(0, GC // SUBLANES, body, 0)

    def finish_chunk(item, c, slot, b):
        @pl.when(chunk_rows(item, c) > 0)
        def _():
            def body(r, carry):
                _gather_copy(hp_hbm, 0, stage, slot, 0, 0, sem).wait()
                return carry
            lax.fori_loop(0, GC, body, 0, unroll=8)
            lo, hi = _unpack(stage[slot].reshape(GC, HALF))
            rows = pl.ds(pl.multiple_of(c * GC, GC), GC)
            buf[b, rows, 0:HALF] = lo.astype(BF16)
            buf[b, rows, HALF:D] = hi.astype(BF16)

    @pl.when((k == 0) & (j == 0))
    def _():
        buf[...] = jnp.zeros_like(buf)
        stage[...] = jnp.zeros_like(stage)
        issue_chunk(0, 0, 0)

        def body(c, carry):
            @pl.when(c + 1 < N_GC)
            def _():
                issue_chunk(0, c + 1, (c + 1) & 1)
            finish_chunk(0, c, c & 1, 0)
            return carry
        lax.fori_loop(0, N_GC, body, 0)

    nxt = k + 1

    @pl.when((nxt < nit) & (j >= 1) & (j <= N_GC))
    def _():
        finish_chunk(nxt, j - 1, (j - 1) & 1, nxt & 1)

    nxt_c = jnp.minimum(nxt, N_ITEMS - 1)
    n_req = jnp.where((nxt < nit) & (j < N_GC), chunk_rows(nxt_c, j), 0)
    req_base = ics_ref[nxt_c] + j * GC
    req_slot = j & 1

    g_ref[...] = jnp.zeros_like(g_ref)
    live = k < nit
    n_mc = (icnt_ref[k] + MC - 1) // MC
    b = k & 1

    def mm_chunk(c):
        rows = pl.ds(pl.multiple_of(c * MC, MC), MC)
        h = buf[b, rows, :]
        a = jnp.dot(h, w1_ref[...].astype(BF16), preferred_element_type=F32)
        bb = jnp.dot(h, w3_ref[...].astype(BF16), preferred_element_type=F32)
        g_ref[rows, :] = (a * jax.nn.sigmoid(a) * bb).astype(BF16)

    def plain_chunks(lo, hi):
        def body(c, carry):
            mm_chunk(c)
            return carry
        lax.fori_loop(lo, hi, body, 0)

    @pl.when(live & (n_req > 0))
    def _():
        n_carry = jnp.minimum(n_mc, REQ_CHUNKS)

        def body(c, carry):
            for qq in range(REQ_GROUPS):
                request_rows(req_base, n_req, req_slot, c * REQ_GROUPS + qq)
            mm_chunk(c)
            return carry
        lax.fori_loop(0, n_carry, body, 0)
        plain_chunks(n_carry, n_mc)

        def tail(q, carry):
            request_rows(req_base, n_req, req_slot, q)
            return carry
        lax.fori_loop(n_carry * REQ_GROUPS, GC // SUBLANES, tail, 0)

    @pl.when(live & (n_req == 0))
    def _():
        plain_chunks(0, n_mc)


def _moe_up_call(layer_e0, src, item_e, item_cs, item_cnt, n_items, hp, w1, w3):
    n_f = D_FFE // TF_MOE
    wspec = pl.BlockSpec((None, D, TF_MOE), lambda k, j, s, ie, ics, ic, ni: (
        layer_e0 + ie[k], 0, jnp.where(k < ni[0], j, n_f - 1)))
    return pl.pallas_call(
        _moe_up_kernel,
        grid_spec=pltpu.PrefetchScalarGridSpec(
            num_scalar_prefetch=5,
            grid=(N_ITEMS, D_FFE // TF_MOE),
            in_specs=[pl.BlockSpec(memory_space=pl.ANY), wspec, wspec],
            out_specs=pl.BlockSpec((RT, TF_MOE), lambda k, j, s, ie, ics, ic, ni: (k, j)),
            scratch_shapes=[
                pltpu.VMEM((2, RT, D), BF16),
                pltpu.VMEM((2, GC // SUBLANES, SUBLANES, HALF), U32),
                pltpu.SemaphoreType.DMA((2,)),
            ],
        ),
        out_shape=jax.ShapeDtypeStruct((N_ITEMS * RT, D_FFE), BF16),
        compiler_params=_cparams(("arbitrary", "arbitrary")),
        name="moe_up",
    )(src, item_e, item_cs, item_cnt, n_items, hp, w1, w3)


def _moe_down_kernel(ie_ref, icnt_ref, nit_ref, g_ref, w2a_ref, w2b_ref, o_ref):
    k = pl.program_id(0)
    o_ref[...] = jnp.zeros_like(o_ref)

    @pl.when(k < nit_ref[0])
    def _():
        def body(c, carry):
            rows = pl.ds(pl.multiple_of(c * MC, MC), MC)
            g = g_ref[rows, :]
            lo = jnp.dot(g, w2a_ref[...].astype(BF16), preferred_element_type=F32)
            hi = jnp.dot(g, w2b_ref[...].astype(BF16), preferred_element_type=F32)
            o_ref[rows, :] = _pack(lo, hi)
            return carry
        lax.fori_loop(0, (icnt_ref[k] + MC - 1) // MC, body, 0)


def _moe_down_call(layer_e0, item_e, item_cnt, n_items, g, w2):
    live = lambda k, ni: jnp.minimum(k, ni[0] - 1)
    n_half = HALF // TN_MOE
    col = lambda k, n, ni: jnp.where(k < ni[0], n, n_half - 1)
    return pl.pallas_call(
        _moe_down_kernel,
        grid_spec=pltpu.PrefetchScalarGridSpec(
            num_scalar_prefetch=3,
            grid=(N_ITEMS, n_half),
            in_specs=[
                pl.BlockSpec((RT, D_FFE), lambda k, n, ie, ic, ni: (live(k, ni), 0)),
                pl.BlockSpec((None, D_FFE, TN_MOE),
                             lambda k, n, ie, ic, ni: (layer_e0 + ie[k], 0, col(k, n, ni))),
                pl.BlockSpec((None, D_FFE, TN_MOE),
                             lambda k, n, ie, ic, ni: (layer_e0 + ie[k], 0, n_half + col(k, n, ni))),
            ],
            out_specs=pl.BlockSpec((RT, TN_MOE), lambda k, n, ie, ic, ni: (k, n)),
        ),
        out_shape=jax.ShapeDtypeStruct((N_ITEMS * RT, HALF), U32),
        compiler_params=_cparams(("arbitrary", "arbitrary")),
        name="moe_down",
    )(item_e, item_cnt, n_items, g, w2, w2)


def _combine_copy(yo_hbm, row, buf, slot, r, sem):
    return pltpu.make_async_copy(yo_hbm.at[pl.ds(row, 1)], buf.at[slot, pl.ds(r, 1)], sem.at[slot])


def _combine_kernel(split, pos_ref, yo_hbm, x_ref, rinfo_ref, mods_ref, *rest):
    if split:
        yp_ref, ys_ref, buf, sem = rest
    else:
        o_ref, buf, sem = rest
    i = pl.program_id(0)
    n = pl.num_programs(0)
    rows = TOP_K * TM

    def issue(tile, slot):
        def body(r, c):
            _combine_copy(yo_hbm, pos_ref[tile * rows + r], buf, slot, r, sem).start()
            return c
        lax.fori_loop(0, rows, body, 0, unroll=8)

    @pl.when(i == 0)
    def _():
        issue(0, 0)

    @pl.when(i + 1 < n)
    def _():
        issue(i + 1, (i + 1) & 1)

    slot = i & 1

    def body(r, c):
        _combine_copy(yo_hbm, 0, buf, slot, r, sem).wait()
        return c
    lax.fori_loop(0, rows, body, 0, unroll=8)
    rinfo = rinfo_ref[...]
    lo0, hi0 = _unpack(buf[slot, 0:TM, :])
    lo1, hi1 = _unpack(buf[slot, TM:2 * TM, :])
    gt0, gt1 = rinfo[:, 2:3], rinfo[:, 3:4]
    g2 = mods_ref[pl.ds(_mod_row(i), 1), :]
    y_lo = x_ref[:, 0:HALF] + g2[:, 0:HALF] * (gt0 * lo0 + gt1 * lo1)
    y_hi = x_ref[:, HALF:D] + g2[:, HALF:D] * (gt0 * hi0 + gt1 * hi1)

    def write(ref):
        ref[:, 0:HALF] = y_lo
        ref[:, HALF:D] = y_hi

    if not split:
        write(o_ref)
        return

    @pl.when(i < P_TILES)
    def _():
        write(yp_ref)

    @pl.when(i >= P_TILES)
    def _():
        write(ys_ref)


def _combine_call(l, split, pos, yo, x, rinfo, mods_all):
    if split:
        out_specs = [pl.BlockSpec((TM, D), lambda i, p: (jnp.minimum(i, P_TILES - 1), 0)),
                     pl.BlockSpec((TM, D), lambda i, p: (jnp.maximum(i - P_TILES, 0), 0))]
        out_shape = [jax.ShapeDtypeStruct((M_P, D), F32), jax.ShapeDtypeStruct((M_S, D), F32)]
    else:
        out_specs = pl.BlockSpec((TM, D), lambda i, p: (i, 0))
        out_shape = jax.ShapeDtypeStruct((M_TOK, D), F32)
    return pl.pallas_call(
        functools.partial(_combine_kernel, split),
        grid_spec=pltpu.PrefetchScalarGridSpec(
            num_scalar_prefetch=1,
            grid=(N_TILES,),
            in_specs=[
                pl.BlockSpec(memory_space=pl.ANY),
                pl.BlockSpec((TM, D), lambda i, p: (i, 0)),
                pl.BlockSpec((TM, LANE), lambda i, p: (i, 0)),
                pl.BlockSpec((None, MOD_ROWS, D), lambda i, p: (l, 0, 5)),
            ],
            out_specs=out_specs,
            scratch_shapes=[pltpu.VMEM((2, TOP_K * TM, HALF), U32), pltpu.SemaphoreType.DMA((2,))],
        ),
        out_shape=out_shape,
        compiler_params=_cparams(("arbitrary",)),
        name="moe_combine",
    )(pos, yo, x, rinfo, mods_all)


def _route_plan(rinfo):
    flat_e = rinfo[:, 0:TOP_K].astype(I32).reshape(-1)
    onehot = (flat_e[:, None] == jnp.arange(N_EXP, dtype=I32)[None, :]).astype(I32)
    csum = jnp.cumsum(onehot, axis=0)
    pick = lambda table: jnp.sum(onehot * table[None, :], axis=1)
    rank = jnp.sum(onehot * csum, axis=1) - 1
    counts = csum[-1]
    cstart = jnp.cumsum(counts) - counts
    n_it = (counts + RT - 1) // RT
    it_end = jnp.cumsum(n_it)
    it_base = it_end - n_it
    n_items = it_end[-1]
    cpos = pick(cstart) + rank
    ppos = (pick(it_base) + rank // RT) * RT + rank % RT
    q = jnp.arange(N_ITEMS, dtype=I32)
    qc = jnp.minimum(q, n_items - 1)
    qe = jnp.minimum(jnp.sum((it_end[None, :] <= qc[:, None]).astype(I32), axis=1), N_EXP - 1)
    qhot = (qe[:, None] == jnp.arange(N_EXP, dtype=I32)[None, :]).astype(I32)
    qpick = lambda table: jnp.sum(qhot * table[None, :], axis=1)
    s = qc - qpick(it_base)
    item_cnt = jnp.where(q < n_items, jnp.clip(qpick(counts) - s * RT, 0, RT), 0)
    item_cs = qpick(cstart) + s * RT
    pos_tiles = ppos.reshape(N_TILES, TM, TOP_K).transpose(0, 2, 1).reshape(-1)
    return (cpos.astype(I32), qe.astype(I32), item_cs.astype(I32), item_cnt.astype(I32),
            n_items.reshape(1).astype(I32), pos_tiles.astype(I32))


def _rope_tables():
    rows = DEC_SEQ // GRID_W
    row = jnp.repeat(jnp.arange(rows), GRID_W).astype(F32)
    col = jnp.tile(jnp.arange(GRID_W), rows).astype(F32)
    nf = ROPE // 4
    inv = ROPE_BASE ** (-jnp.arange(nf, dtype=F32) / nf)
    ar = row[:, None] * inv[None, :]
    ac = col[:, None] * inv[None, :]
    ang = jnp.concatenate([ar, ar, ac, ac], axis=-1)
    cos, sin = jnp.cos(ang), jnp.sin(ang)
    first = (jnp.arange(ROPE) % 32) < 16
    pad = lambda a, v: jnp.pad(a, ((0, 0), (0, LANE - ROPE)), constant_values=v)
    cos_p = pad(cos, 1.0)
    sin_a = pad(jnp.where(first[None, :], -sin, 0.0), 0.0)
    sin_b = pad(jnp.where(first[None, :], 0.0, sin), 0.0)
    ident = jnp.stack([jnp.ones((TMI, LANE), F32), jnp.zeros((TMI, LANE), F32), jnp.zeros((TMI, LANE), F32)])
    return jnp.concatenate([ident, jnp.stack([cos_p, sin_a, sin_b])], axis=1)


def kernel(x_prompt, x_sample, cache_ckv, cache_krope, c, c_ctx, w_ada, b_ada, norm1_g, norm2_g, w_in,
           qlat_norm_g, kvlat_norm_g, w_uq, w_ukv, q_head_norm_g, k_head_norm_g, gm_norm_g, gm_w_s,
           gm_b_s, pool_w, pool_scale, w_o, ffn_w1, ffn_w3, ffn_w2, router_w, moe_w1, moe_w3, moe_w2):
    pad_w = IN_PAD - w_in.shape[-1]
    col = jnp.arange(IN_PAD)[None, None, :]
    w_in_p = jnp.where(
        col < C_KR + ROPE, jnp.pad(w_in, ((0, 0), (0, 0), (0, pad_w))),
        jnp.where(col < C_KR + LANE, 0.0, jnp.pad(w_in, ((0, 0), (0, 0), (pad_w, 0))))).astype(BF16)
    w_uq_p = jnp.pad(w_uq.reshape(DEPTH, Q_RANK, N_HEADS, QK_DIM),
                     ((0, 0), (0, 0), (0, 0), (0, HEAD_PAD - QK_DIM))
                     ).reshape(DEPTH, Q_RANK, QK_W).astype(BF16)
    w_ukv_b = w_ukv.astype(BF16)
    w_o_b = w_o.astype(BF16)
    qg_p = jnp.pad(q_head_norm_g * (1.0 / math.sqrt(QK_DIM)),
                   ((0, 0), (0, HEAD_PAD - QK_DIM))).reshape(DEPTH, 1, HEAD_PAD)
    kg_n = k_head_norm_g[:, :NOPE].reshape(DEPTH, 1, LANE)
    kg_r = jnp.pad(k_head_norm_g[:, NOPE:], ((0, 0), (0, LANE - ROPE))).reshape(DEPTH, 1, LANE)
    gmw_b = gm_w_s.astype(BF16)
    gmb = jnp.broadcast_to(gm_b_s[:, :, :, None], (DEPTH, GM_GROUPS, CHUNK, LANE))
    pw_b = pool_w.astype(BF16)
    ps = pool_scale.reshape(DEPTH, 1, POOL_W)
    n1g = norm1_g.reshape(DEPTH, 1, D)
    n2g = norm2_g.reshape(DEPTH, 1, D)
    qlg = qlat_norm_g.reshape(DEPTH, 1, Q_RANK)
    kvlg = kvlat_norm_g.reshape(DEPTH, 1, KV_RANK)
    rw_p = jnp.pad(router_w, ((0, 0), (0, 0), (0, LANE - N_EXP)))
    n_moe = moe_w1.shape[0]
    moe_w1_flat = moe_w1.reshape(n_moe * N_EXP, D, D_FFE)
    moe_w3_flat = moe_w3.reshape(n_moe * N_EXP, D, D_FFE)
    moe_w2_flat = moe_w2.reshape(n_moe * N_EXP, D_FFE, D)
    rope_tab = _rope_tables()
    cache_kr_p = jnp.pad(cache_krope, ((0, 0), (0, 0), (0, 0), (0, LANE - ROPE)))

    cond = jnp.zeros((MOD_ROWS, D), F32).at[0].set(c_ctx).at[1:1 + DEC_BATCH].set(c)
    mods_all = _ada_call(cond, w_ada, b_ada)
    k_c, v_c = _cache_kv_call(cache_ckv, cache_kr_p, w_ukv_b, kg_n, kg_r)

    x = jnp.concatenate([x_prompt.reshape(M_P, D), x_sample.reshape(M_S, D)], axis=0)
    ckv_out, kr_out = [], []
    for l in range(DEPTH):
        q, k, v, ckv, kr, zg, zp = _in_proj_call(
            l, x, mods_all, n1g, w_in_p, qlg, kvlg, w_uq_p, qg_p, w_ukv_b, kg_n, kg_r, rope_tab,
            gm_norm_g, gmw_b, gmb)
        ckv_out.append(ckv[:M_P].reshape(BATCH, SEQ, KV_RANK))
        kr_out.append(kr[:M_P, :ROPE].reshape(BATCH, SEQ, ROPE))
        attn = _attn_call(l, q, k, v, k_c, v_c)
        i = l // 2
        if l % 2 == 0:
            x, h2b = _out_proj_call(l, False, attn, zg, zp, x, mods_all, pw_b, ps, w_o_b, n2g)
            g = _ffn_up_call(i, h2b, ffn_w1, ffn_w3)
            x = _ffn_down_call(l, g, ffn_w2, x, mods_all)
        else:
            x, hp, rinfo = _out_proj_call(l, True, attn, zg, zp, x, mods_all, pw_b, ps, w_o_b, n2g, rw_p)
            cpos, item_e, item_cs, item_cnt, n_items, pos_tiles = _route_plan(rinfo)
            src = _src_call(cpos)
            g = _moe_up_call(i * N_EXP, src, item_e, item_cs, item_cnt, n_items, hp,
                             moe_w1_flat, moe_w3_flat)
            yo = _moe_down_call(i * N_EXP, item_e, item_cnt, n_items, g, moe_w2_flat)
            x = _combine_call(l, l == DEPTH - 1, pos_tiles, yo, x, rinfo, mods_all)

    if DEPTH % 2 == 0:
        y_p, y_s = x
    else:
        y_p, y_s = x[:M_P], x[M_P:]
    return (y_p.reshape(BATCH, SEQ, D), y_s.reshape(DEC_BATCH, DEC_SEQ, D),
            jnp.stack(ckv_out, axis=1), jnp.stack(kr_out, axis=1))
```

```python
import functools
import math

import jax
import jax.numpy as jnp
from jax import lax
from jax.experimental import pallas as pl
from jax.experimental.pallas import tpu as pltpu

F32 = jnp.float32
BF16 = jnp.bfloat16
U32 = jnp.uint32
I32 = jnp.int32

D = 2048
BATCH, SEQ = 16, 256
DEC_BATCH, DEC_SEQ = 2, 1024
PAST = 256
DEPTH = 4
GRID_W = 64
EPS = 1e-6
N_HEADS = 8
NOPE, ROPE, VDIM = 128, 64, 128
QK_DIM = NOPE + ROPE
HEAD_PAD = 256
QK_W = N_HEADS * HEAD_PAD
Q_RANK, KV_RANK = 512, 256
ROPE_BASE = 10000.0
ATTN_W = N_HEADS * VDIM
CHUNK, GM_GROUPS, GM_W = 128, 4, 512
POOL_WINDOWS = (2, 4, 8, 16)
POOL_W = 512
POOL_HALO = 8
D_FF, N_EXP, D_FFE, TOP_K = 5632, 8, 2816, 2
TOP_K_SHIFT = 1

C_Q, C_KV, C_KR, C_GM, C_PL = 0, 512, 768, 896, 1920
IN_PAD = 2432

LANE = 128
SUBLANES = 8
TM = 256
TMI = 512
TMO = 256
M_P, M_S = BATCH * SEQ, DEC_BATCH * DEC_SEQ
M_TOK = M_P + M_S
N_TILES = M_TOK // TM
P_TILES = M_P // TM
S_TILES = DEC_SEQ // TM
MOD_ROWS = 16
ADA_TN = 1024
TE_UP, TF_UP = 1024, 512
TE_DOWN, TN_DOWN = 512, 512
HALF = D // 2
N_SLOTS = TOP_K * M_TOK
RT = 2048
N_ITEMS = N_EXP + N_SLOTS // RT
GC = 256
N_GC = RT // GC
MC = 512
TF_MOE = 256
TN_MOE = 256
VMEM_LIMIT = 56 * 1024 * 1024


def _cparams(sem):
    return pltpu.CompilerParams(dimension_semantics=sem, vmem_limit_bytes=VMEM_LIMIT)


def _const_spec(shape):
    nd = len(shape)
    return pl.BlockSpec(shape, lambda *_: (0,) * nd)


def _layer_spec(l, shape):
    nd = len(shape)
    return pl.BlockSpec((None,) + tuple(shape), lambda *_: (l,) + (0,) * nd)


def _mod_row(i):
    return jnp.where(i < P_TILES, 0, 1 + (i - P_TILES) // S_TILES)


def _rms(x, g, n):
    ms = jnp.sum(x * x, axis=-1, keepdims=True) * (1.0 / n)
    return x * lax.rsqrt(ms + EPS) * g


def _rope(r, cos, sin_a, sin_b):
    return r * cos + pltpu.roll(r, LANE - 16, 1) * sin_a + pltpu.roll(r, 16, 1) * sin_b


def _pack(lo, hi):
    lo_b = lax.bitcast_convert_type(lo.astype(BF16).astype(F32), U32)
    hi_b = lax.bitcast_convert_type(hi.astype(BF16).astype(F32), U32)
    return (lo_b >> 16) | hi_b


def _unpack(w):
    lo = lax.bitcast_convert_type(w << 16, F32)
    hi = lax.bitcast_convert_type(w & jnp.uint32(0xFFFF0000), F32)
    return lo, hi


def _ada_kernel(cond_ref, w_ref, b_ref, o_ref):
    c = cond_ref[...]
    s = c * jax.nn.sigmoid(c)
    s_hi = s.astype(BF16)
    s_lo = (s - s_hi.astype(F32)).astype(BF16)
    w = w_ref[...]
    w_hi = w.astype(BF16)
    w_lo = (w - w_hi.astype(F32)).astype(BF16)
    lhs = jnp.concatenate([s_hi, s_lo], axis=0)
    r1 = jnp.dot(lhs, w_hi, preferred_element_type=F32)
    r2 = jnp.dot(s_hi, w_lo, preferred_element_type=F32)
    o_ref[...] = r1[:MOD_ROWS] + r1[MOD_ROWS:] + r2 + b_ref[...]


def _ada_call(cond, w_ada, b_ada):
    n = 6 * D
    return pl.pallas_call(
        _ada_kernel,
        grid=(DEPTH, n // ADA_TN),
        in_specs=[
            pl.BlockSpec((MOD_ROWS, D), lambda l, j: (0, 0)),
            pl.BlockSpec((None, D, ADA_TN), lambda l, j: (l, 0, j)),
            pl.BlockSpec((None, 1, ADA_TN), lambda l, j: (l, 0, j)),
        ],
        out_specs=pl.BlockSpec((None, MOD_ROWS, ADA_TN), lambda l, j: (l, 0, j)),
        out_shape=jax.ShapeDtypeStruct((DEPTH, MOD_ROWS, n), F32),
        compiler_params=_cparams(("parallel", "parallel")),
        name="ada",
    )(cond, w_ada, b_ada.reshape(DEPTH, 1, n))


def _write_kv(ckv_b, kr, w_ukv_ref, kg_n, kg_r, rope, k_ref, v_ref, rows=slice(None)):
    kv = jnp.dot(ckv_b, w_ukv_ref[...], preferred_element_type=F32)
    ssr = jnp.sum(kr * kr, axis=-1, keepdims=True)
    shared = kr * kg_r
    if rope is not None:
        shared = _rope(shared, *rope)
    for h in range(N_HEADS):
        kn = kv[:, h * HEAD_PAD:h * HEAD_PAD + NOPE]
        vv = kv[:, h * HEAD_PAD + NOPE:(h + 1) * HEAD_PAD]
        ms = (jnp.sum(kn * kn, axis=-1, keepdims=True) + ssr) * (1.0 / QK_DIM)
        inv = lax.rsqrt(ms + EPS)
        k_ref[rows, h * HEAD_PAD:h * HEAD_PAD + NOPE] = (kn * inv * kg_n).astype(BF16)
        k_ref[rows, h * HEAD_PAD + NOPE:(h + 1) * HEAD_PAD] = (shared * inv).astype(BF16)
        v_ref[rows, h * VDIM:(h + 1) * VDIM] = vv.astype(BF16)


def _cache_kv_kernel(ckv_ref, kr_ref, w_ukv_ref, kgn_ref, kgr_ref, k_ref, v_ref):
    _write_kv(ckv_ref[...].astype(BF16), kr_ref[...], w_ukv_ref, kgn_ref[...], kgr_ref[...],
              None, k_ref, v_ref)


def _cache_kv_call(cache_ckv, cache_kr_p, w_ukv_b, kg_n, kg_r):
    return pl.pallas_call(
        _cache_kv_kernel,
        grid=(DEPTH, DEC_BATCH),
        in_specs=[
            pl.BlockSpec((None, None, PAST, KV_RANK), lambda l, b: (b, l, 0, 0)),
            pl.BlockSpec((None, None, PAST, LANE), lambda l, b: (b, l, 0, 0)),
            pl.BlockSpec((None, KV_RANK, QK_W), lambda l, b: (l, 0, 0)),
            pl.BlockSpec((None, 1, LANE), lambda l, b: (l, 0, 0)),
            pl.BlockSpec((None, 1, LANE), lambda l, b: (l, 0, 0)),
        ],
        out_specs=[
            pl.BlockSpec((None, None, PAST, QK_W), lambda l, b: (l, b, 0, 0)),
            pl.BlockSpec((None, None, PAST, ATTN_W), lambda l, b: (l, b, 0, 0)),
        ],
        out_shape=[
            jax.ShapeDtypeStruct((DEPTH, DEC_BATCH, PAST, QK_W), BF16),
            jax.ShapeDtypeStruct((DEPTH, DEC_BATCH, PAST, ATTN_W), BF16),
        ],
        compiler_params=_cparams(("parallel", "parallel")),
        name="cache_kv",
    )(cache_ckv, cache_kr_p, w_ukv_b, kg_n, kg_r)


def _in_proj_kernel(x_ref, mods_ref, n1g_ref, w_in_ref, qlg_ref, kvlg_ref, w_uq_ref, qg_ref,
                    w_ukv_ref, kgn_ref, kgr_ref, rope_ref, gmg_ref, gmw_ref, gmb_ref,
                    q_ref, k_ref, v_ref, ckv_ref, kr_ref, zg_ref, zp_ref):
    i = pl.program_id(0)
    row = jnp.where(i < M_P // TMI, 0, 1 + (i - M_P // TMI) // (DEC_SEQ // TMI))
    m = mods_ref[pl.ds(row, 1), :]
    sh1, sc1 = m[:, 0:D], m[:, D:2 * D]
    for sub in range(TMI // TM):
        rs = slice(sub * TM, (sub + 1) * TM)
        x = x_ref[rs, :]
        h = _rms(x, n1g_ref[...], D) * (1.0 + sc1) + sh1
        z = jnp.dot(h.astype(BF16), w_in_ref[...], preferred_element_type=F32)
        rope = (rope_ref[0, rs, :], rope_ref[1, rs, :], rope_ref[2, rs, :])

        ql = _rms(z[:, C_Q:C_Q + Q_RANK], qlg_ref[...], Q_RANK)
        qf = jnp.dot(ql.astype(BF16), w_uq_ref[...], preferred_element_type=F32)
        qg = qg_ref[...]
        for hd in range(N_HEADS):
            seg = qf[:, hd * HEAD_PAD:(hd + 1) * HEAD_PAD]
            seg = _rms(seg, qg, QK_DIM)
            q_ref[rs, hd * HEAD_PAD:hd * HEAD_PAD + NOPE] = seg[:, :NOPE].astype(BF16)
            q_ref[rs, hd * HEAD_PAD + NOPE:(hd + 1) * HEAD_PAD] = _rope(seg[:, NOPE:], *rope).astype(BF16)

        ckv = _rms(z[:, C_KV:C_KV + KV_RANK], kvlg_ref[...], KV_RANK)
        kr = z[:, C_KR:C_KR + LANE]
        ckv_ref[rs, :] = ckv
        kr_ref[rs, :] = kr
        _write_kv(ckv.astype(BF16), kr, w_ukv_ref, kgn_ref[...], kgr_ref[...], rope, k_ref, v_ref, rs)

        zg = jax.nn.gelu(z[:, C_GM:C_GM + 2 * GM_W], approximate=True)
        for g in range(GM_GROUPS):
            u = zg[:, g * LANE:(g + 1) * LANE]
            vg = _rms(zg[:, GM_W + g * LANE:GM_W + (g + 1) * LANE], gmg_ref[g:g + 1, :], LANE)
            vb = vg.astype(BF16)
            for c in range(TM // CHUNK):
                sv = jnp.dot(gmw_ref[g], vb[c * CHUNK:(c + 1) * CHUNK, :], preferred_element_type=F32)
                sv = sv + gmb_ref[g]
                r0 = sub * TM + c * CHUNK
                zg_ref[r0:r0 + CHUNK, g * LANE:(g + 1) * LANE] = (
                    u[c * CHUNK:(c + 1) * CHUNK, :] * sv).astype(BF16)

        zp_ref[rs, :] = z[:, C_PL:C_PL + POOL_W]


def _in_proj_call(l, x, mods_all, n1g, w_in_b, qlg, kvlg, w_uq_b, qg, w_ukv_b, kg_n, kg_r, rope_tab,
                  gmg, gmw_b, gmb):
    def rope_idx(i):
        return (0, jnp.where(i < M_P // TMI, 0, 1 + (i - M_P // TMI) % (DEC_SEQ // TMI)), 0)

    row = lambda w: pl.BlockSpec((TMI, w), lambda i: (i, 0))
    return pl.pallas_call(
        _in_proj_kernel,
        grid=(M_TOK // TMI,),
        in_specs=[
            row(D),
            _layer_spec(l, (MOD_ROWS, 6 * D)),
            _layer_spec(l, (1, D)),
            _layer_spec(l, (D, IN_PAD)),
            _layer_spec(l, (1, Q_RANK)),
            _layer_spec(l, (1, KV_RANK)),
            _layer_spec(l, (Q_RANK, QK_W)),
            _layer_spec(l, (1, HEAD_PAD)),
            _layer_spec(l, (KV_RANK, QK_W)),
            _layer_spec(l, (1, LANE)),
            _layer_spec(l, (1, LANE)),
            pl.BlockSpec((3, TMI, LANE), rope_idx),
            _layer_spec(l, (GM_GROUPS, LANE)),
            _layer_spec(l, (GM_GROUPS, CHUNK, CHUNK)),
            _layer_spec(l, (GM_GROUPS, CHUNK, LANE)),
        ],
        out_specs=[row(QK_W), row(QK_W), row(ATTN_W), row(KV_RANK), row(LANE), row(GM_W), row(POOL_W)],
        out_shape=[
            jax.ShapeDtypeStruct((M_TOK, QK_W), BF16),
            jax.ShapeDtypeStruct((M_TOK, QK_W), BF16),
            jax.ShapeDtypeStruct((M_TOK, ATTN_W), BF16),
            jax.ShapeDtypeStruct((M_TOK, KV_RANK), F32),
            jax.ShapeDtypeStruct((M_TOK, LANE), F32),
            jax.ShapeDtypeStruct((M_TOK, GM_W), BF16),
            jax.ShapeDtypeStruct((M_TOK, POOL_W), F32),
        ],
        compiler_params=_cparams(("parallel",)),
        name="in_proj",
    )(x, mods_all, n1g, w_in_b, qlg, kvlg, w_uq_b, qg, w_ukv_b, kg_n, kg_r, rope_tab, gmg, gmw_b, gmb)


def _attend_heads(q_ref, kv_pairs, o_ref):
    nt = (((1,), (1,)), ((), ()))
    for h in range(N_HEADS):
        qh = q_ref[:, h * HEAD_PAD:(h + 1) * HEAD_PAD]
        ss = [lax.dot_general(qh, k_ref[:, h * HEAD_PAD:(h + 1) * HEAD_PAD], nt,
                              preferred_element_type=F32) for k_ref, _ in kv_pairs]
        mx = ss[0].max(axis=-1, keepdims=True)
        for s in ss[1:]:
            mx = jnp.maximum(mx, s.max(axis=-1, keepdims=True))
        den = None
        acc = None
        for s, (_, v_ref) in zip(ss, kv_pairs):
            p = jnp.exp(s - mx)
            d = jnp.sum(p, axis=-1, keepdims=True)
            a = jnp.dot(p.astype(BF16), v_ref[:, h * VDIM:(h + 1) * VDIM], preferred_element_type=F32)
            den = d if den is None else den + d
            acc = a if acc is None else acc + a
        o_ref[:, h * VDIM:(h + 1) * VDIM] = (acc / den).astype(BF16)


def _attn_kernel(q_ref, kp_ref, vp_ref, ks_ref, vs_ref, kc_ref, vc_ref, o_ref):
    i = pl.program_id(0)

    @pl.when(i < P_TILES)
    def _():
        _attend_heads(q_ref, [(kp_ref, vp_ref)], o_ref)

    @pl.when(i >= P_TILES)
    def _():
        _attend_heads(q_ref, [(ks_ref, vs_ref), (kc_ref, vc_ref)], o_ref)


def _attn_call(l, q, k, v, k_c, v_c):
    def seq_b(i):
        return jnp.maximum(i - P_TILES, 0) // S_TILES

    small = lambda w: pl.BlockSpec((TM, w), lambda i: (jnp.minimum(i, P_TILES - 1), 0))
    big = lambda w: pl.BlockSpec((DEC_SEQ, w), lambda i: (M_P // DEC_SEQ + seq_b(i), 0))
    cache = lambda w: pl.BlockSpec((None, None, PAST, w), lambda i: (l, seq_b(i), 0, 0))
    return pl.pallas_call(
        _attn_kernel,
        grid=(N_TILES,),
        in_specs=[
            pl.BlockSpec((TM, QK_W), lambda i: (i, 0)),
            small(QK_W), small(ATTN_W),
            big(QK_W), big(ATTN_W),
            cache(QK_W), cache(ATTN_W),
        ],
        out_specs=pl.BlockSpec((TM, ATTN_W), lambda i: (i, 0)),
        out_shape=jax.ShapeDtypeStruct((M_TOK, ATTN_W), BF16),
        compiler_params=_cparams(("parallel",)),
        name="attention",
    )(q, k, v, k, v, k_c, v_c)


def _out_proj_kernel(moe, attn_ref, zg_ref, zprev_ref, zcur_ref, znext_ref, x_ref, mods_ref,
                     pw_ref, ps_ref, w_o_ref, n2g_ref, *rest):
    if moe:
        rw_ref, xo_ref, hp_ref, rinfo_ref, ext_ref, mix_ref = rest
    else:
        xo_ref, h2b_ref, ext_ref, mix_ref = rest
    i = pl.program_id(0)
    ctx_steps = M_P // TMO
    is_ctx = i < ctx_steps
    seq_len = jnp.where(is_ctx, SEQ, DEC_SEQ)
    m = mods_ref[pl.ds(jnp.where(is_ctx, 0, 1 + (i - ctx_steps) // (DEC_SEQ // TMO)), 1), :]
    g1, sh2, sc2 = m[:, 2 * D:3 * D], m[:, 3 * D:4 * D], m[:, 4 * D:5 * D]

    ext_ref[0:POOL_HALO, :] = zprev_ref[TMO - POOL_HALO:, :]
    ext_ref[POOL_HALO:POOL_HALO + TMO, :] = zcur_ref[...]
    ext_ref[POOL_HALO + TMO:, :] = znext_ref[0:POOL_HALO, :]
    mix_ref[:, 0:ATTN_W] = attn_ref[...]
    mix_ref[:, ATTN_W:ATTN_W + GM_W] = zg_ref[...]

    for sub in range(TMO // TM):
        rs = slice(sub * TM, (sub + 1) * TM)
        pos0 = jnp.where(is_ctx, 0, ((i - ctx_steps) % (DEC_SEQ // TMO)) * TMO + sub * TM)
        pos = pos0 + lax.broadcasted_iota(I32, (TM, 1), 0)
        inside = {dlt: (pos + dlt >= 0) & (pos + dlt < seq_len)
                  for dlt in range(-POOL_HALO, POOL_HALO) if dlt != 0}
        for gi, w in enumerate(POOL_WINDOWS):
            cols = slice(gi * LANE, (gi + 1) * LANE)
            win = None
            for dlt in range(-(w // 2), w - w // 2):
                r0 = POOL_HALO + sub * TM + dlt
                part = ext_ref[r0:r0 + TM, cols]
                if dlt != 0:
                    part = jnp.where(inside[dlt], part, 0.0)
                win = part if win is None else win + part
            lo = jnp.maximum(pos - w // 2, 0)
            hi = jnp.minimum(pos + (w - 1 - w // 2), seq_len - 1)
            cnt = (hi - lo + 1).astype(F32)
            pooled = win / cnt - zcur_ref[rs, cols]
            mixed = jnp.dot(pooled.astype(BF16), pw_ref[gi], preferred_element_type=F32)
            mix_ref[rs, ATTN_W + GM_W + gi * LANE:ATTN_W + GM_W + (gi + 1) * LANE] = (
                mixed * ps_ref[:, cols]).astype(BF16)

        out = jnp.dot(mix_ref[rs, :], w_o_ref[...], preferred_element_type=F32)
        xn = x_ref[rs, :] + g1 * out
        xo_ref[rs, :] = xn
        h2 = _rms(xn, n2g_ref[...], D) * (1.0 + sc2) + sh2

        if not moe:
            h2b_ref[rs, :] = h2.astype(BF16)
            continue

        hp_ref[rs, :] = _pack(h2[:, :HALF], h2[:, HALF:])
        h_hi = h2.astype(BF16)
        h_lo = (h2 - h_hi.astype(F32)).astype(BF16)
        rw = rw_ref[...]
        w_hi = rw.astype(BF16)
        w_lo = (rw - w_hi.astype(F32)).astype(BF16)
        logits = (jnp.dot(h_hi, w_hi, preferred_element_type=F32)
                  + jnp.dot(h_lo, w_hi, preferred_element_type=F32)
                  + jnp.dot(h_hi, w_lo, preferred_element_type=F32))
        lane = lax.broadcasted_iota(I32, (TM, LANE), 1).astype(F32)
        neg = jnp.float32(-jnp.inf)
        logits = jnp.where(lane < N_EXP, logits, neg)
        v1 = logits.max(axis=-1, keepdims=True)
        i1 = jnp.where(logits == v1, lane, float(LANE)).min(axis=-1, keepdims=True)
        l2 = jnp.where(lane == i1, neg, logits)
        v2 = l2.max(axis=-1, keepdims=True)
        i2 = jnp.where(l2 == v2, lane, float(LANE)).min(axis=-1, keepdims=True)
        e2 = jnp.exp(v2 - v1)
        gate1 = 1.0 / (1.0 + e2)
        gate2 = e2 / (1.0 + e2)
        rinfo_ref[rs, :] = jnp.where(lane == 0, i1, jnp.where(lane == 1, i2, jnp.where(
            lane == 2, gate1, jnp.where(lane == 3, gate2, 0.0))))


def _out_proj_call(l, moe, attn, zg, zp, x, mods_all, pw_b, ps, w_o_b, n2g, rw_p=None):
    n_steps = M_TOK // TMO
    row = lambda w: pl.BlockSpec((TMO, w), lambda i: (i, 0))
    in_specs = [
        row(ATTN_W), row(GM_W),
        pl.BlockSpec((TMO, POOL_W), lambda i: (jnp.maximum(i - 1, 0), 0)),
        row(POOL_W),
        pl.BlockSpec((TMO, POOL_W), lambda i: (jnp.minimum(i + 1, n_steps - 1), 0)),
        row(D),
        _layer_spec(l, (MOD_ROWS, 6 * D)),
        _layer_spec(l, (len(POOL_WINDOWS), LANE, LANE)),
        _layer_spec(l, (1, POOL_W)),
        _layer_spec(l, (D, D)),
        _layer_spec(l, (1, D)),
    ]
    args = [attn, zg, zp, zp, zp, x, mods_all, pw_b, ps, w_o_b, n2g]
    if moe:
        in_specs.append(_layer_spec(l // 2, (D, LANE)))
        args.append(rw_p)
        out_specs = [row(D), row(HALF), row(LANE)]
        out_shape = [jax.ShapeDtypeStruct((M_TOK, D), F32), jax.ShapeDtypeStruct((M_TOK, HALF), U32),
                     jax.ShapeDtypeStruct((M_TOK, LANE), F32)]
    else:
        out_specs = [row(D), row(D)]
        out_shape = [jax.ShapeDtypeStruct((M_TOK, D), F32), jax.ShapeDtypeStruct((M_TOK, D), BF16)]
    return pl.pallas_call(
        functools.partial(_out_proj_kernel, moe),
        grid=(n_steps,),
        in_specs=in_specs,
        out_specs=out_specs,
        out_shape=out_shape,
        scratch_shapes=[pltpu.VMEM((TMO + 2 * POOL_HALO, POOL_W), F32), pltpu.VMEM((TMO, D), BF16)],
        compiler_params=_cparams(("parallel",)),
        name="out_proj_moe" if moe else "out_proj",
    )(*args)


def _ffn_up_kernel(h_ref, w1_ref, w3_ref, g_ref):
    h = h_ref[...]
    a = jnp.dot(h, w1_ref[...].astype(BF16), preferred_element_type=F32)
    b = jnp.dot(h, w3_ref[...].astype(BF16), preferred_element_type=F32)
    g_ref[...] = (a * jax.nn.sigmoid(a) * b).astype(BF16)


def _ffn_up_call(i, h, w1, w3):
    wspec = pl.BlockSpec((None, D, TF_UP), lambda j, t: (i, 0, j))
    return pl.pallas_call(
        _ffn_up_kernel,
        grid=(D_FF // TF_UP, M_TOK // TE_UP),
        in_specs=[pl.BlockSpec((TE_UP, D), lambda j, t: (t, 0)), wspec, wspec],
        out_specs=pl.BlockSpec((TE_UP, TF_UP), lambda j, t: (t, j)),
        out_shape=jax.ShapeDtypeStruct((M_TOK, D_FF), BF16),
        compiler_params=_cparams(("parallel", "parallel")),
        name="ffn_up",
    )(h, w1, w3)


def _ffn_down_kernel(g_ref, w2_ref, x_ref, mods_ref, o_ref):
    t = pl.program_id(1)
    y = jnp.dot(g_ref[...], w2_ref[...].astype(BF16), preferred_element_type=F32)
    ctx_tiles = M_P // TE_DOWN
    row = jnp.where(t < ctx_tiles, 0, 1 + (t - ctx_tiles) // (DEC_SEQ // TE_DOWN))
    o_ref[...] = x_ref[...] + mods_ref[pl.ds(row, 1), :] * y


def _ffn_down_call(l, g, w2, x, mods_all):
    i = l // 2
    return pl.pallas_call(
        _ffn_down_kernel,
        grid=(D // TN_DOWN, M_TOK // TE_DOWN),
        in_specs=[
            pl.BlockSpec((TE_DOWN, D_FF), lambda j, t: (t, 0)),
            pl.BlockSpec((None, D_FF, TN_DOWN), lambda j, t: (i, 0, j)),
            pl.BlockSpec((TE_DOWN, TN_DOWN), lambda j, t: (t, j)),
            pl.BlockSpec((None, MOD_ROWS, TN_DOWN), lambda j, t: (l, 0, 5 * (D // TN_DOWN) + j)),
        ],
        out_specs=pl.BlockSpec((TE_DOWN, TN_DOWN), lambda j, t: (t, j)),
        out_shape=jax.ShapeDtypeStruct((M_TOK, D), F32),
        compiler_params=_cparams(("parallel", "parallel")),
        name="ffn_down",
    )(g, w2, x, mods_all)


def _src_kernel(cpos_ref, src_ref):
    def body(i, c):
        src_ref[cpos_ref[i]] = lax.shift_right_logical(i, TOP_K_SHIFT)
        return c
    lax.fori_loop(0, N_SLOTS, body, 0, unroll=8)


def _src_call(cpos):
    return pl.pallas_call(
        _src_kernel,
        in_specs=[pl.BlockSpec(memory_space=pltpu.SMEM)],
        out_specs=pl.BlockSpec(memory_space=pltpu.SMEM),
        out_shape=jax.ShapeDtypeStruct((N_SLOTS,), I32),
        name="moe_src",
    )(cpos)


def _for_item_rows(cnt, live, fn):
    n_full = cnt // MC
    rem = cnt - n_full * MC
    n_mc = n_full + (rem > MC // 2).astype(I32)

    @pl.when(live)
    def _():
        def body(c, carry):
            fn(c * MC, MC)
            return carry
        lax.fori_loop(0, n_mc, body, 0)

    @pl.when(live & (rem > 0) & (rem <= MC // 2))
    def _():
        fn(n_mc * MC, MC // 2)


def _gather_copy(hp_hbm, tok, stage, slot, q, u, sem):
    return pltpu.make_async_copy(hp_hbm.at[pl.ds(tok, 1)], stage.at[slot, q, pl.ds(u, 1)], sem.at[slot])


def _moe_up_kernel(src_ref, ie_ref, ics_ref, icnt_ref, nit_ref, hp_hbm, w1_ref, w3_ref, g_ref,
                   buf, stage, sem):
    k = pl.program_id(0)
    j = pl.program_id(1)
    nit = nit_ref[0]

    def chunk_rows(item, c):
        return jnp.clip(icnt_ref[item] - c * GC, 0, GC)

    def request_rows(base, n, slot, q):
        for u in range(SUBLANES):
            tok = src_ref[base + jnp.minimum(q * SUBLANES + u, n - 1)]
            _gather_copy(hp_hbm, tok, stage, slot, q, u, sem).start()

    def issue_chunk(item, c, slot):
        n = chunk_rows(item, c)
        base = ics_ref[item] + c * GC

        @pl.when(n > 0)
        def _():
            def body(q, carry):
                request_rows(base, n, slot, q)
                return carry
            lax.fori_loop(0, GC // SUBLANES, body, 0)

    def finish_chunk(item, c, slot, b):
        @pl.when(chunk_rows(item, c) > 0)
        def _():
            def body(r, carry):
                _gather_copy(hp_hbm, 0, stage, slot, 0, 0, sem).wait()
                return carry
            lax.fori_loop(0, GC, body, 0, unroll=8)
            lo, hi = _unpack(stage[slot].reshape(GC, HALF))
            rows = pl.ds(pl.multiple_of(c * GC, GC), GC)
            buf[b, rows, 0:HALF] = lo.astype(BF16)
            buf[b, rows, HALF:D] = hi.astype(BF16)

    @pl.when((k == 0) & (j == 0))
    def _():
        buf[...] = jnp.zeros_like(buf)
        stage[...] = jnp.zeros_like(stage)
        issue_chunk(0, 0, 0)

        def body(c, carry):
            @pl.when(c + 1 < N_GC)
            def _():
                issue_chunk(0, c + 1, (c + 1) & 1)
            finish_chunk(0, c, c & 1, 0)
            return carry
        lax.fori_loop(0, N_GC, body, 0)

    nxt = k + 1

    @pl.when((nxt < nit) & (j >= 1) & (j <= N_GC))
    def _():
        finish_chunk(nxt, j - 1, (j - 1) & 1, nxt & 1)

    @pl.when((nxt < nit) & (j < N_GC))
    def _():
        issue_chunk(nxt, j, j & 1)

    g_ref[...] = jnp.zeros_like(g_ref)
    b = k & 1

    def mm_rows(start, size):
        rows = pl.ds(pl.multiple_of(start, size), size)
        h = buf[b, rows, :]
        a = jnp.dot(h, w1_ref[...].astype(BF16), preferred_element_type=F32)
        bb = jnp.dot(h, w3_ref[...].astype(BF16), preferred_element_type=F32)
        g_ref[rows, :] = (a * jax.nn.sigmoid(a) * bb).astype(BF16)

    _for_item_rows(icnt_ref[k], k < nit, mm_rows)


def _moe_up_call(layer_e0, src, item_e, item_cs, item_cnt, n_items, hp, w1, w3):
    n_f = D_FFE // TF_MOE
    wspec = pl.BlockSpec((None, D, TF_MOE), lambda k, j, s, ie, ics, ic, ni: (
        layer_e0 + ie[k], 0, jnp.where(k < ni[0], j, n_f - 1)))
    return pl.pallas_call(
        _moe_up_kernel,
        grid_spec=pltpu.PrefetchScalarGridSpec(
            num_scalar_prefetch=5,
            grid=(n_items[0], D_FFE // TF_MOE),
            in_specs=[pl.BlockSpec(memory_space=pl.ANY), wspec, wspec],
            out_specs=pl.BlockSpec((RT, TF_MOE), lambda k, j, s, ie, ics, ic, ni: (k, j)),
            scratch_shapes=[
                pltpu.VMEM((2, RT, D), BF16),
                pltpu.VMEM((2, GC // SUBLANES, SUBLANES, HALF), U32),
                pltpu.SemaphoreType.DMA((2,)),
            ],
        ),
        out_shape=jax.ShapeDtypeStruct((N_ITEMS * RT, D_FFE), BF16),
        compiler_params=_cparams(("arbitrary", "arbitrary")),
        name="moe_up",
    )(src, item_e, item_cs, item_cnt, n_items, hp, w1, w3)


def _moe_down_kernel(ie_ref, icnt_ref, nit_ref, g_ref, w2a_ref, w2b_ref, o_ref):
    k = pl.program_id(0)
    o_ref[...] = jnp.zeros_like(o_ref)

    def mm_rows(start, size):
        rows = pl.ds(pl.multiple_of(start, size), size)
        g = g_ref[rows, :]
        lo = jnp.dot(g, w2a_ref[...].astype(BF16), preferred_element_type=F32)
        hi = jnp.dot(g, w2b_ref[...].astype(BF16), preferred_element_type=F32)
        o_ref[rows, :] = _pack(lo, hi)

    _for_item_rows(icnt_ref[k], k < nit_ref[0], mm_rows)


def _moe_down_call(layer_e0, item_e, item_cnt, n_items, g, w2):
    live = lambda k, ni: jnp.minimum(k, ni[0] - 1)
    n_half = HALF // TN_MOE
    col = lambda k, n, ni: jnp.where(k < ni[0], n, n_half - 1)
    return pl.pallas_call(
        _moe_down_kernel,
        grid_spec=pltpu.PrefetchScalarGridSpec(
            num_scalar_prefetch=3,
            grid=(n_items[0], n_half),
            in_specs=[
                pl.BlockSpec((RT, D_FFE), lambda k, n, ie, ic, ni: (live(k, ni), 0)),
                pl.BlockSpec((None, D_FFE, TN_MOE),
                             lambda k, n, ie, ic, ni: (layer_e0 + ie[k], 0, col(k, n, ni))),
                pl.BlockSpec((None, D_FFE, TN_MOE),
                             lambda k, n, ie, ic, ni: (layer_e0 + ie[k], 0, n_half + col(k, n, ni))),
            ],
            out_specs=pl.BlockSpec((RT, TN_MOE), lambda k, n, ie, ic, ni: (k, n)),
        ),
        out_shape=jax.ShapeDtypeStruct((N_ITEMS * RT, HALF), U32),
        compiler_params=_cparams(("arbitrary", "arbitrary")),
        name="moe_down",
    )(item_e, item_cnt, n_items, g, w2, w2)


def _combine_copy(yo_hbm, row, buf, slot, r, sem):
    return pltpu.make_async_copy(yo_hbm.at[pl.ds(row, 1)], buf.at[slot, pl.ds(r, 1)], sem.at[slot])


def _combine_kernel(split, pos_ref, yo_hbm, x_ref, rinfo_ref, mods_ref, *rest):
    if split:
        yp_ref, ys_ref, buf, sem = rest
    else:
        o_ref, buf, sem = rest
    i = pl.program_id(0)
    n = pl.num_programs(0)
    rows = TOP_K * TM

    def issue(tile, slot):
        def body(r, c):
            _combine_copy(yo_hbm, pos_ref[tile * rows + r], buf, slot, r, sem).start()
            return c
        lax.fori_loop(0, rows, body, 0, unroll=8)

    @pl.when(i == 0)
    def _():
        issue(0, 0)

    @pl.when(i + 1 < n)
    def _():
        issue(i + 1, (i + 1) & 1)

    slot = i & 1

    def body(r, c):
        _combine_copy(yo_hbm, 0, buf, slot, r, sem).wait()
        return c
    lax.fori_loop(0, rows, body, 0, unroll=8)
    rinfo = rinfo_ref[...]
    lo0, hi0 = _unpack(buf[slot, 0:TM, :])
    lo1, hi1 = _unpack(buf[slot, TM:2 * TM, :])
    gt0, gt1 = rinfo[:, 2:3], rinfo[:, 3:4]
    g2 = mods_ref[pl.ds(_mod_row(i), 1), :]
    y_lo = x_ref[:, 0:HALF] + g2[:, 0:HALF] * (gt0 * lo0 + gt1 * lo1)
    y_hi = x_ref[:, HALF:D] + g2[:, HALF:D] * (gt0 * hi0 + gt1 * hi1)

    def write(ref):
        ref[:, 0:HALF] = y_lo
        ref[:, HALF:D] = y_hi

    if not split:
        write(o_ref)
        return

    @pl.when(i < P_TILES)
    def _():
        write(yp_ref)

    @pl.when(i >= P_TILES)
    def _():
        write(ys_ref)


def _combine_call(l, split, pos, yo, x, rinfo, mods_all):
    if split:
        out_specs = [pl.BlockSpec((TM, D), lambda i, p: (jnp.minimum(i, P_TILES - 1), 0)),
                     pl.BlockSpec((TM, D), lambda i, p: (jnp.maximum(i - P_TILES, 0), 0))]
        out_shape = [jax.ShapeDtypeStruct((M_P, D), F32), jax.ShapeDtypeStruct((M_S, D), F32)]
    else:
        out_specs = pl.BlockSpec((TM, D), lambda i, p: (i, 0))
        out_shape = jax.ShapeDtypeStruct((M_TOK, D), F32)
    return pl.pallas_call(
        functools.partial(_combine_kernel, split),
        grid_spec=pltpu.PrefetchScalarGridSpec(
            num_scalar_prefetch=1,
            grid=(N_TILES,),
            in_specs=[
                pl.BlockSpec(memory_space=pl.ANY),
                pl.BlockSpec((TM, D), lambda i, p: (i, 0)),
                pl.BlockSpec((TM, LANE), lambda i, p: (i, 0)),
                pl.BlockSpec((None, MOD_ROWS, D), lambda i, p: (l, 0, 5)),
            ],
            out_specs=out_specs,
            scratch_shapes=[pltpu.VMEM((2, TOP_K * TM, HALF), U32), pltpu.SemaphoreType.DMA((2,))],
        ),
        out_shape=out_shape,
        compiler_params=_cparams(("arbitrary",)),
        name="moe_combine",
    )(pos, yo, x, rinfo, mods_all)


def _route_plan(rinfo):
    flat_e = rinfo[:, 0:TOP_K].astype(I32).reshape(-1)
    onehot = (flat_e[:, None] == jnp.arange(N_EXP, dtype=I32)[None, :]).astype(I32)
    csum = jnp.cumsum(onehot, axis=0)
    pick = lambda table: jnp.sum(onehot * table[None, :], axis=1)
    rank = jnp.sum(onehot * csum, axis=1) - 1
    counts = csum[-1]
    cstart = jnp.cumsum(counts) - counts
    n_it = (counts + RT - 1) // RT
    it_end = jnp.cumsum(n_it)
    it_base = it_end - n_it
    n_items = it_end[-1]
    cpos = pick(cstart) + rank
    ppos = (pick(it_base) + rank // RT) * RT + rank % RT
    q = jnp.arange(N_ITEMS, dtype=I32)
    qc = jnp.minimum(q, n_items - 1)
    qe = jnp.minimum(jnp.sum((it_end[None, :] <= qc[:, None]).astype(I32), axis=1), N_EXP - 1)
    qhot = (qe[:, None] == jnp.arange(N_EXP, dtype=I32)[None, :]).astype(I32)
    qpick = lambda table: jnp.sum(qhot * table[None, :], axis=1)
    s = qc - qpick(it_base)
    item_cnt = jnp.where(q < n_items, jnp.clip(qpick(counts) - s * RT, 0, RT), 0)
    item_cs = qpick(cstart) + s * RT
    pos_tiles = ppos.reshape(N_TILES, TM, TOP_K).transpose(0, 2, 1).reshape(-1)
    return (cpos.astype(I32), qe.astype(I32), item_cs.astype(I32), item_cnt.astype(I32),
            n_items.reshape(1).astype(I32), pos_tiles.astype(I32))


def _rope_tables():
    rows = DEC_SEQ // GRID_W
    row = jnp.repeat(jnp.arange(rows), GRID_W).astype(F32)
    col = jnp.tile(jnp.arange(GRID_W), rows).astype(F32)
    nf = ROPE // 4
    inv = ROPE_BASE ** (-jnp.arange(nf, dtype=F32) / nf)
    ar = row[:, None] * inv[None, :]
    ac = col[:, None] * inv[None, :]
    ang = jnp.concatenate([ar, ar, ac, ac], axis=-1)
    cos, sin = jnp.cos(ang), jnp.sin(ang)
    first = (jnp.arange(ROPE) % 32) < 16
    pad = lambda a, v: jnp.pad(a, ((0, 0), (0, LANE - ROPE)), constant_values=v)
    cos_p = pad(cos, 1.0)
    sin_a = pad(jnp.where(first[None, :], -sin, 0.0), 0.0)
    sin_b = pad(jnp.where(first[None, :], 0.0, sin), 0.0)
    ident = jnp.stack([jnp.ones((TMI, LANE), F32), jnp.zeros((TMI, LANE), F32), jnp.zeros((TMI, LANE), F32)])
    return jnp.concatenate([ident, jnp.stack([cos_p, sin_a, sin_b])], axis=1)


def kernel(x_prompt, x_sample, cache_ckv, cache_krope, c, c_ctx, w_ada, b_ada, norm1_g, norm2_g, w_in,
           qlat_norm_g, kvlat_norm_g, w_uq, w_ukv, q_head_norm_g, k_head_norm_g, gm_norm_g, gm_w_s,
           gm_b_s, pool_w, pool_scale, w_o, ffn_w1, ffn_w3, ffn_w2, router_w, moe_w1, moe_w3, moe_w2):
    pad_w = IN_PAD - w_in.shape[-1]
    col = jnp.arange(IN_PAD)[None, None, :]
    w_in_p = jnp.where(
        col < C_KR + ROPE, jnp.pad(w_in, ((0, 0), (0, 0), (0, pad_w))),
        jnp.where(col < C_KR + LANE, 0.0, jnp.pad(w_in, ((0, 0), (0, 0), (pad_w, 0))))).astype(BF16)
    w_uq_p = jnp.pad(w_uq.reshape(DEPTH, Q_RANK, N_HEADS, QK_DIM),
                     ((0, 0), (0, 0), (0, 0), (0, HEAD_PAD - QK_DIM))
                     ).reshape(DEPTH, Q_RANK, QK_W).astype(BF16)
    w_ukv_b = w_ukv.astype(BF16)
    w_o_b = w_o.astype(BF16)
    qg_p = jnp.pad(q_head_norm_g * (1.0 / math.sqrt(QK_DIM)),
                   ((0, 0), (0, HEAD_PAD - QK_DIM))).reshape(DEPTH, 1, HEAD_PAD)
    kg_n = k_head_norm_g[:, :NOPE].reshape(DEPTH, 1, LANE)
    kg_r = jnp.pad(k_head_norm_g[:, NOPE:], ((0, 0), (0, LANE - ROPE))).reshape(DEPTH, 1, LANE)
    gmw_b = gm_w_s.astype(BF16)
    gmb = jnp.broadcast_to(gm_b_s[:, :, :, None], (DEPTH, GM_GROUPS, CHUNK, LANE))
    pw_b = pool_w.astype(BF16)
    ps = pool_scale.reshape(DEPTH, 1, POOL_W)
    n1g = norm1_g.reshape(DEPTH, 1, D)
    n2g = norm2_g.reshape(DEPTH, 1, D)
    qlg = qlat_norm_g.reshape(DEPTH, 1, Q_RANK)
    kvlg = kvlat_norm_g.reshape(DEPTH, 1, KV_RANK)
    rw_p = jnp.pad(router_w, ((0, 0), (0, 0), (0, LANE - N_EXP)))
    n_moe = moe_w1.shape[0]
    moe_w1_flat = moe_w1.reshape(n_moe * N_EXP, D, D_FFE)
    moe_w3_flat = moe_w3.reshape(n_moe * N_EXP, D, D_FFE)
    moe_w2_flat = moe_w2.reshape(n_moe * N_EXP, D_FFE, D)
    rope_tab = _rope_tables()
    cache_kr_p = jnp.pad(cache_krope, ((0, 0), (0, 0), (0, 0), (0, LANE - ROPE)))

    cond = jnp.zeros((MOD_ROWS, D), F32).at[0].set(c_ctx).at[1:1 + DEC_BATCH].set(c)
    mods_all = _ada_call(cond, w_ada, b_ada)
    k_c, v_c = _cache_kv_call(cache_ckv, cache_kr_p, w_ukv_b, kg_n, kg_r)

    x = jnp.concatenate([x_prompt.reshape(M_P, D), x_sample.reshape(M_S, D)], axis=0)
    ckv_out, kr_out = [], []
    for l in range(DEPTH):
        q, k, v, ckv, kr, zg, zp = _in_proj_call(
            l, x, mods_all, n1g, w_in_p, qlg, kvlg, w_uq_p, qg_p, w_ukv_b, kg_n, kg_r, rope_tab,
            gm_norm_g, gmw_b, gmb)
        ckv_out.append(ckv[:M_P].reshape(BATCH, SEQ, KV_RANK))
        kr_out.append(kr[:M_P, :ROPE].reshape(BATCH, SEQ, ROPE))
        attn = _attn_call(l, q, k, v, k_c, v_c)
        i = l // 2
        if l % 2 == 0:
            x, h2b = _out_proj_call(l, False, attn, zg, zp, x, mods_all, pw_b, ps, w_o_b, n2g)
            g = _ffn_up_call(i, h2b, ffn_w1, ffn_w3)
            x = _ffn_down_call(l, g, ffn_w2, x, mods_all)
        else:
            x, hp, rinfo = _out_proj_call(l, True, attn, zg, zp, x, mods_all, pw_b, ps, w_o_b, n2g, rw_p)
            cpos, item_e, item_cs, item_cnt, n_items, pos_tiles = _route_plan(rinfo)
            src = _src_call(cpos)
            g = _moe_up_call(i * N_EXP, src, item_e, item_cs, item_cnt, n_items, hp,
                             moe_w1_flat, moe_w3_flat)
            yo = _moe_down_call(i * N_EXP, item_e, item_cnt, n_items, g, moe_w2_flat)
            x = _combine_call(l, l == DEPTH - 1, pos_tiles, yo, x, rinfo, mods_all)

    if DEPTH % 2 == 0:
        y_p, y_s = x
    else:
        y_p, y_s = x[:M_P], x[M_P:]
    return (y_p.reshape(BATCH, SEQ, D), y_s.reshape(DEC_BATCH, DEC_SEQ, D),
            jnp.stack(ckv_out, axis=1), jnp.stack(kr_out, axis=1))
```

```python
import functools
import math

import jax
import jax.numpy as jnp
from jax import lax
from jax.experimental import pallas as pl
from jax.experimental.pallas import tpu as pltpu

F32 = jnp.float32
BF16 = jnp.bfloat16
U32 = jnp.uint32
I32 = jnp.int32

D = 2048
BATCH, SEQ = 16, 256
DEC_BATCH, DEC_SEQ = 2, 1024
PAST = 256
DEPTH = 4
GRID_W = 64
EPS = 1e-6
N_HEADS = 8
NOPE, ROPE, VDIM = 128, 64, 128
QK_DIM = NOPE + ROPE
HEAD_PAD = 256
QK_W = N_HEADS * HEAD_PAD
Q_RANK, KV_RANK = 512, 256
ROPE_BASE = 10000.0
ATTN_W = N_HEADS * VDIM
CHUNK, GM_GROUPS, GM_W = 128, 4, 512
POOL_WINDOWS = (2, 4, 8, 16)
POOL_W = 512
POOL_HALO = 8
D_FF, N_EXP, D_FFE, TOP_K = 5632, 8, 2816, 2
TOP_K_SHIFT = 1

C_Q, C_KV, C_KR, C_GM, C_PL = 0, 512, 768, 896, 1920
IN_PAD = 2432

LANE = 128
SUBLANES = 8
TM = 256
TMI = 512
TMO = 256
M_P, M_S = BATCH * SEQ, DEC_BATCH * DEC_SEQ
M_TOK = M_P + M_S
N_TILES = M_TOK // TM
P_TILES = M_P // TM
S_TILES = DEC_SEQ // TM
MOD_ROWS = 16
ADA_TN = 1024
TE_UP, TF_UP = 1024, 512
TE_DOWN, TN_DOWN = 512, 512
HALF = D // 2
N_SLOTS = TOP_K * M_TOK
RT = 2048
N_ITEMS = N_EXP + N_SLOTS // RT
GC = 256
N_GC = RT // GC
MC = 512
TF_MOE = 256
TN_MOE = 256
VMEM_LIMIT = 56 * 1024 * 1024


def _cparams(sem):
    return pltpu.CompilerParams(dimension_semantics=sem, vmem_limit_bytes=VMEM_LIMIT)


def _const_spec(shape):
    nd = len(shape)
    return pl.BlockSpec(shape, lambda *_: (0,) * nd)


def _layer_spec(l, shape):
    nd = len(shape)
    return pl.BlockSpec((None,) + tuple(shape), lambda *_: (l,) + (0,) * nd)


def _x_operands(x, tile):
    if not isinstance(x, tuple):
        return [x], [pl.BlockSpec((tile, D), lambda i: (i, 0))]
    ctx = M_P // tile
    return list(x), [pl.BlockSpec((tile, D), lambda i: (jnp.minimum(i, ctx - 1), 0)),
                     pl.BlockSpec((tile, D), lambda i: (jnp.maximum(i - ctx, 0), 0))]


def _mod_row(i):
    return jnp.where(i < P_TILES, 0, 1 + (i - P_TILES) // S_TILES)


def _rms(x, g, n):
    ms = jnp.sum(x * x, axis=-1, keepdims=True) * (1.0 / n)
    return x * lax.rsqrt(ms + EPS) * g


def _rope(r, cos, sin_a, sin_b):
    return r * cos + pltpu.roll(r, LANE - 16, 1) * sin_a + pltpu.roll(r, 16, 1) * sin_b


def _pack(lo, hi):
    lo_b = lax.bitcast_convert_type(lo.astype(BF16).astype(F32), U32)
    hi_b = lax.bitcast_convert_type(hi.astype(BF16).astype(F32), U32)
    return (lo_b >> 16) | hi_b


def _unpack(w):
    lo = lax.bitcast_convert_type(w << 16, F32)
    hi = lax.bitcast_convert_type(w & jnp.uint32(0xFFFF0000), F32)
    return lo, hi


def _ada_kernel(cond_ref, w_ref, b_ref, o_ref):
    c = cond_ref[...]
    s = c * jax.nn.sigmoid(c)
    s_hi = s.astype(BF16)
    s_lo = (s - s_hi.astype(F32)).astype(BF16)
    w = w_ref[...]
    w_hi = w.astype(BF16)
    w_lo = (w - w_hi.astype(F32)).astype(BF16)
    lhs = jnp.concatenate([s_hi, s_lo], axis=0)
    r1 = jnp.dot(lhs, w_hi, preferred_element_type=F32)
    r2 = jnp.dot(s_hi, w_lo, preferred_element_type=F32)
    o_ref[...] = r1[:MOD_ROWS] + r1[MOD_ROWS:] + r2 + b_ref[...]


def _ada_call(cond, w_ada, b_ada):
    n = 6 * D
    return pl.pallas_call(
        _ada_kernel,
        grid=(DEPTH, n // ADA_TN),
        in_specs=[
            pl.BlockSpec((MOD_ROWS, D), lambda l, j: (0, 0)),
            pl.BlockSpec((None, D, ADA_TN), lambda l, j: (l, 0, j)),
            pl.BlockSpec((None, 1, ADA_TN), lambda l, j: (l, 0, j)),
        ],
        out_specs=pl.BlockSpec((None, MOD_ROWS, ADA_TN), lambda l, j: (l, 0, j)),
        out_shape=jax.ShapeDtypeStruct((DEPTH, MOD_ROWS, n), F32),
        compiler_params=_cparams(("parallel", "parallel")),
        name="ada",
    )(cond, w_ada, b_ada.reshape(DEPTH, 1, n))


def _write_kv(ckv_b, kr, w_ukv_ref, kg_n, kg_r, rope, k_ref, v_ref, rows=slice(None)):
    kv = jnp.dot(ckv_b, w_ukv_ref[...], preferred_element_type=F32)
    ssr = jnp.sum(kr * kr, axis=-1, keepdims=True)
    shared = kr * kg_r
    if rope is not None:
        shared = _rope(shared, *rope)
    for h in range(N_HEADS):
        kn = kv[:, h * HEAD_PAD:h * HEAD_PAD + NOPE]
        vv = kv[:, h * HEAD_PAD + NOPE:(h + 1) * HEAD_PAD]
        ms = (jnp.sum(kn * kn, axis=-1, keepdims=True) + ssr) * (1.0 / QK_DIM)
        inv = lax.rsqrt(ms + EPS)
        k_ref[rows, h * HEAD_PAD:h * HEAD_PAD + NOPE] = (kn * inv * kg_n).astype(BF16)
        k_ref[rows, h * HEAD_PAD + NOPE:(h + 1) * HEAD_PAD] = (shared * inv).astype(BF16)
        v_ref[rows, h * VDIM:(h + 1) * VDIM] = vv.astype(BF16)


def _cache_kv_kernel(ckv_ref, kr_ref, w_ukv_ref, kgn_ref, kgr_ref, k_ref, v_ref):
    _write_kv(ckv_ref[...].astype(BF16), kr_ref[...], w_ukv_ref, kgn_ref[...], kgr_ref[...],
              None, k_ref, v_ref)


def _cache_kv_call(cache_ckv, cache_kr_p, w_ukv_b, kg_n, kg_r):
    return pl.pallas_call(
        _cache_kv_kernel,
        grid=(DEPTH, DEC_BATCH),
        in_specs=[
            pl.BlockSpec((None, None, PAST, KV_RANK), lambda l, b: (b, l, 0, 0)),
            pl.BlockSpec((None, None, PAST, LANE), lambda l, b: (b, l, 0, 0)),
            pl.BlockSpec((None, KV_RANK, QK_W), lambda l, b: (l, 0, 0)),
            pl.BlockSpec((None, 1, LANE), lambda l, b: (l, 0, 0)),
            pl.BlockSpec((None, 1, LANE), lambda l, b: (l, 0, 0)),
        ],
        out_specs=[
            pl.BlockSpec((None, None, PAST, QK_W), lambda l, b: (l, b, 0, 0)),
            pl.BlockSpec((None, None, PAST, ATTN_W), lambda l, b: (l, b, 0, 0)),
        ],
        out_shape=[
            jax.ShapeDtypeStruct((DEPTH, DEC_BATCH, PAST, QK_W), BF16),
            jax.ShapeDtypeStruct((DEPTH, DEC_BATCH, PAST, ATTN_W), BF16),
        ],
        compiler_params=_cparams(("parallel", "parallel")),
        name="cache_kv",
    )(cache_ckv, cache_kr_p, w_ukv_b, kg_n, kg_r)


def _in_proj_kernel(two_x, *refs):
    x_refs, refs = refs[:1 + two_x], refs[1 + two_x:]
    _in_proj_body(x_refs, *refs)


def _in_proj_body(x_refs, mods_ref, n1g_ref, w_in_ref, qlg_ref, kvlg_ref, w_uq_ref, qg_ref,
                    w_ukv_ref, kgn_ref, kgr_ref, rope_ref, gmg_ref, gmw_ref, gmb_ref,
                    q_ref, k_ref, v_ref, ckv_ref, kr_ref, zg_ref, zp_ref):
    i = pl.program_id(0)
    row = jnp.where(i < M_P // TMI, 0, 1 + (i - M_P // TMI) // (DEC_SEQ // TMI))
    m = mods_ref[pl.ds(row, 1), :]
    sh1, sc1 = m[:, 0:D], m[:, D:2 * D]
    for sub in range(TMI // TM):
        rs = slice(sub * TM, (sub + 1) * TM)
        x = x_refs[0][rs, :] if len(x_refs) == 1 else jnp.where(i < M_P // TMI, x_refs[0][rs, :], x_refs[1][rs, :])
        h = _rms(x, n1g_ref[...], D) * (1.0 + sc1) + sh1
        z = jnp.dot(h.astype(BF16), w_in_ref[...], preferred_element_type=F32)
        rope = (rope_ref[0, rs, :], rope_ref[1, rs, :], rope_ref[2, rs, :])

        ql = _rms(z[:, C_Q:C_Q + Q_RANK], qlg_ref[...], Q_RANK)
        qf = jnp.dot(ql.astype(BF16), w_uq_ref[...], preferred_element_type=F32)
        qg = qg_ref[...]
        for hd in range(N_HEADS):
            seg = qf[:, hd * HEAD_PAD:(hd + 1) * HEAD_PAD]
            seg = _rms(seg, qg, QK_DIM)
            q_ref[rs, hd * HEAD_PAD:hd * HEAD_PAD + NOPE] = seg[:, :NOPE].astype(BF16)
            q_ref[rs, hd * HEAD_PAD + NOPE:(hd + 1) * HEAD_PAD] = _rope(seg[:, NOPE:], *rope).astype(BF16)

        ckv = _rms(z[:, C_KV:C_KV + KV_RANK], kvlg_ref[...], KV_RANK)
        kr = z[:, C_KR:C_KR + LANE]
        ckv_ref[rs, :] = ckv
        kr_ref[rs, :] = kr
        _write_kv(ckv.astype(BF16), kr, w_ukv_ref, kgn_ref[...], kgr_ref[...], rope, k_ref, v_ref, rs)

        zg = jax.nn.gelu(z[:, C_GM:C_GM + 2 * GM_W], approximate=True)
        for g in range(GM_GROUPS):
            u = zg[:, g * LANE:(g + 1) * LANE]
            vg = _rms(zg[:, GM_W + g * LANE:GM_W + (g + 1) * LANE], gmg_ref[g:g + 1, :], LANE)
            vb = vg.astype(BF16)
            for c in range(TM // CHUNK):
                sv = jnp.dot(gmw_ref[g], vb[c * CHUNK:(c + 1) * CHUNK, :], preferred_element_type=F32)
                sv = sv + gmb_ref[g]
                r0 = sub * TM + c * CHUNK
                zg_ref[r0:r0 + CHUNK, g * LANE:(g + 1) * LANE] = (
                    u[c * CHUNK:(c + 1) * CHUNK, :] * sv).astype(BF16)

        zp_ref[rs, :] = z[:, C_PL:C_PL + POOL_W]


def _in_proj_call(l, x, mods_all, n1g, w_in_b, qlg, kvlg, w_uq_b, qg, w_ukv_b, kg_n, kg_r, rope_tab,
                  gmg, gmw_b, gmb):
    def rope_idx(i):
        return (0, jnp.where(i < M_P // TMI, 0, 1 + (i - M_P // TMI) % (DEC_SEQ // TMI)), 0)

    row = lambda w: pl.BlockSpec((TMI, w), lambda i: (i, 0))
    two_x = isinstance(x, tuple)
    x_args, x_specs = _x_operands(x, TMI)
    return pl.pallas_call(
        functools.partial(_in_proj_kernel, two_x),
        grid=(M_TOK // TMI,),
        in_specs=x_specs + [
            _layer_spec(l, (MOD_ROWS, 6 * D)),
            _layer_spec(l, (1, D)),
            _layer_spec(l, (D, IN_PAD)),
            _layer_spec(l, (1, Q_RANK)),
            _layer_spec(l, (1, KV_RANK)),
            _layer_spec(l, (Q_RANK, QK_W)),
            _layer_spec(l, (1, HEAD_PAD)),
            _layer_spec(l, (KV_RANK, QK_W)),
            _layer_spec(l, (1, LANE)),
            _layer_spec(l, (1, LANE)),
            pl.BlockSpec((3, TMI, LANE), rope_idx),
            _layer_spec(l, (GM_GROUPS, LANE)),
            _layer_spec(l, (GM_GROUPS, CHUNK, CHUNK)),
            _layer_spec(l, (GM_GROUPS, CHUNK, LANE)),
        ],
        out_specs=[row(QK_W), row(QK_W), row(ATTN_W), row(KV_RANK), row(LANE), row(GM_W), row(POOL_W)],
        out_shape=[
            jax.ShapeDtypeStruct((M_TOK, QK_W), BF16),
            jax.ShapeDtypeStruct((M_TOK, QK_W), BF16),
            jax.ShapeDtypeStruct((M_TOK, ATTN_W), BF16),
            jax.ShapeDtypeStruct((M_TOK, KV_RANK), F32),
            jax.ShapeDtypeStruct((M_TOK, LANE), F32),
            jax.ShapeDtypeStruct((M_TOK, GM_W), BF16),
            jax.ShapeDtypeStruct((M_TOK, POOL_W), F32),
        ],
        compiler_params=_cparams(("parallel",)),
        name="in_proj",
    )(*x_args, mods_all, n1g, w_in_b, qlg, kvlg, w_uq_b, qg, w_ukv_b, kg_n, kg_r, rope_tab, gmg, gmw_b, gmb)


def _attend_heads(q_ref, kv_pairs, o_ref):
    nt = (((1,), (1,)), ((), ()))
    for h in range(N_HEADS):
        qh = q_ref[:, h * HEAD_PAD:(h + 1) * HEAD_PAD]
        ss = [lax.dot_general(qh, k_ref[:, h * HEAD_PAD:(h + 1) * HEAD_PAD], nt,
                              preferred_element_type=F32) for k_ref, _ in kv_pairs]
        mx = ss[0].max(axis=-1, keepdims=True)
        for s in ss[1:]:
            mx = jnp.maximum(mx, s.max(axis=-1, keepdims=True))
        den = None
        acc = None
        for s, (_, v_ref) in zip(ss, kv_pairs):
            p = jnp.exp(s - mx)
            d = jnp.sum(p, axis=-1, keepdims=True)
            a = jnp.dot(p.astype(BF16), v_ref[:, h * VDIM:(h + 1) * VDIM], preferred_element_type=F32)
            den = d if den is None else den + d
            acc = a if acc is None else acc + a
        o_ref[:, h * VDIM:(h + 1) * VDIM] = (acc / den).astype(BF16)


def _attn_kernel(q_ref, kp_ref, vp_ref, ks_ref, vs_ref, kc_ref, vc_ref, o_ref):
    i = pl.program_id(0)

    @pl.when(i < P_TILES)
    def _():
        _attend_heads(q_ref, [(kp_ref, vp_ref)], o_ref)

    @pl.when(i >= P_TILES)
    def _():
        _attend_heads(q_ref, [(ks_ref, vs_ref), (kc_ref, vc_ref)], o_ref)


def _attn_call(l, q, k, v, k_c, v_c):
    def seq_b(i):
        return jnp.maximum(i - P_TILES, 0) // S_TILES

    small = lambda w: pl.BlockSpec((TM, w), lambda i: (jnp.minimum(i, P_TILES - 1), 0))
    big = lambda w: pl.BlockSpec((DEC_SEQ, w), lambda i: (M_P // DEC_SEQ + seq_b(i), 0))
    cache = lambda w: pl.BlockSpec((None, None, PAST, w), lambda i: (l, seq_b(i), 0, 0))
    return pl.pallas_call(
        _attn_kernel,
        grid=(N_TILES,),
        in_specs=[
            pl.BlockSpec((TM, QK_W), lambda i: (i, 0)),
            small(QK_W), small(ATTN_W),
            big(QK_W), big(ATTN_W),
            cache(QK_W), cache(ATTN_W),
        ],
        out_specs=pl.BlockSpec((TM, ATTN_W), lambda i: (i, 0)),
        out_shape=jax.ShapeDtypeStruct((M_TOK, ATTN_W), BF16),
        compiler_params=_cparams(("parallel",)),
        name="attention",
    )(q, k, v, k, v, k_c, v_c)


def _out_proj_kernel(moe, two_x, attn_ref, zg_ref, zprev_ref, zcur_ref, znext_ref, *rest):
    x_refs, rest = rest[:1 + two_x], rest[1 + two_x:]
    mods_ref, pw_ref, ps_ref, w_o_ref, n2g_ref = rest[:5]
    rest = rest[5:]
    if moe:
        rw_ref, xo_ref, hp_ref, rinfo_ref, ext_ref, mix_ref = rest
    else:
        xo_ref, h2b_ref, ext_ref, mix_ref = rest
    i = pl.program_id(0)
    ctx_steps = M_P // TMO
    is_ctx = i < ctx_steps
    seq_len = jnp.where(is_ctx, SEQ, DEC_SEQ)
    m = mods_ref[pl.ds(jnp.where(is_ctx, 0, 1 + (i - ctx_steps) // (DEC_SEQ // TMO)), 1), :]
    g1, sh2, sc2 = m[:, 2 * D:3 * D], m[:, 3 * D:4 * D], m[:, 4 * D:5 * D]

    pos0 = jnp.where(is_ctx, 0, ((i - ctx_steps) % (DEC_SEQ // TMO)) * TMO)
    zero_halo = jnp.zeros((POOL_HALO, POOL_W), F32)
    ext_ref[0:POOL_HALO, :] = jnp.where(pos0 == 0, zero_halo, zprev_ref[TMO - POOL_HALO:, :])
    ext_ref[POOL_HALO:POOL_HALO + TMO, :] = zcur_ref[...]
    ext_ref[POOL_HALO + TMO:, :] = jnp.where(pos0 + TMO == seq_len, zero_halo, znext_ref[0:POOL_HALO, :])
    mix_ref[:, 0:ATTN_W] = attn_ref[...]
    mix_ref[:, ATTN_W:ATTN_W + GM_W] = zg_ref[...]

    assert TMO == TM and SEQ % TMO == 0 and DEC_SEQ % TMO == 0
    for sub in range(TMO // TM):
        rs = slice(sub * TM, (sub + 1) * TM)
        pos = pos0 + lax.broadcasted_iota(I32, (TM, 1), 0)
        for gi, w in enumerate(POOL_WINDOWS):
            cols = slice(gi * LANE, (gi + 1) * LANE)
            win = None
            for dlt in range(-(w // 2), w - w // 2):
                r0 = POOL_HALO + dlt
                part = ext_ref[r0:r0 + TM, cols]
                win = part if win is None else win + part
            lo = jnp.maximum(pos - w // 2, 0)
            hi = jnp.minimum(pos + (w - 1 - w // 2), seq_len - 1)
            cnt = (hi - lo + 1).astype(F32)
            pooled = win / cnt - zcur_ref[rs, cols]
            mixed = jnp.dot(pooled.astype(BF16), pw_ref[gi], preferred_element_type=F32)
            mix_ref[rs, ATTN_W + GM_W + gi * LANE:ATTN_W + GM_W + (gi + 1) * LANE] = (
                mixed * ps_ref[:, cols]).astype(BF16)

        out = jnp.dot(mix_ref[rs, :], w_o_ref[...], preferred_element_type=F32)
        x = x_refs[0][rs, :] if len(x_refs) == 1 else jnp.where(is_ctx, x_refs[0][rs, :], x_refs[1][rs, :])
        xn = x + g1 * out
        xo_ref[rs, :] = xn
        h2 = _rms(xn, n2g_ref[...], D) * (1.0 + sc2) + sh2

        if not moe:
            h2b_ref[rs, :] = h2.astype(BF16)
            continue

        hp_ref[rs, :] = _pack(h2[:, :HALF], h2[:, HALF:])
        h_hi = h2.astype(BF16)
        h_lo = (h2 - h_hi.astype(F32)).astype(BF16)
        rw = rw_ref[...]
        w_hi = rw.astype(BF16)
        w_lo = (rw - w_hi.astype(F32)).astype(BF16)
        logits = (jnp.dot(h_hi, w_hi, preferred_element_type=F32)
                  + jnp.dot(h_lo, w_hi, preferred_element_type=F32)
                  + jnp.dot(h_hi, w_lo, preferred_element_type=F32))
        lane = lax.broadcasted_iota(I32, (TM, LANE), 1).astype(F32)
        neg = jnp.float32(-jnp.inf)
        logits = jnp.where(lane < N_EXP, logits, neg)
        v1 = logits.max(axis=-1, keepdims=True)
        i1 = jnp.where(logits == v1, lane, float(LANE)).min(axis=-1, keepdims=True)
        l2 = jnp.where(lane == i1, neg, logits)
        v2 = l2.max(axis=-1, keepdims=True)
        i2 = jnp.where(l2 == v2, lane, float(LANE)).min(axis=-1, keepdims=True)
        e2 = jnp.exp(v2 - v1)
        gate1 = 1.0 / (1.0 + e2)
        gate2 = e2 / (1.0 + e2)
        rinfo_ref[rs, :] = jnp.where(lane == 0, i1, jnp.where(lane == 1, i2, jnp.where(
            lane == 2, gate1, jnp.where(lane == 3, gate2, 0.0))))


def _out_proj_call(l, moe, attn, zg, zp, x, mods_all, pw_b, ps, w_o_b, n2g, rw_p=None):
    n_steps = M_TOK // TMO
    row = lambda w: pl.BlockSpec((TMO, w), lambda i: (i, 0))
    two_x = isinstance(x, tuple)
    x_args, x_specs = _x_operands(x, TMO)
    in_specs = [
        row(ATTN_W), row(GM_W),
        pl.BlockSpec((TMO, POOL_W), lambda i: (jnp.maximum(i - 1, 0), 0)),
        row(POOL_W),
        pl.BlockSpec((TMO, POOL_W), lambda i: (jnp.minimum(i + 1, n_steps - 1), 0)),
    ] + x_specs + [
        _layer_spec(l, (MOD_ROWS, 6 * D)),
        _layer_spec(l, (len(POOL_WINDOWS), LANE, LANE)),
        _layer_spec(l, (1, POOL_W)),
        _layer_spec(l, (D, D)),
        _layer_spec(l, (1, D)),
    ]
    args = [attn, zg, zp, zp, zp, *x_args, mods_all, pw_b, ps, w_o_b, n2g]
    if moe:
        in_specs.append(_layer_spec(l // 2, (D, LANE)))
        args.append(rw_p)
        out_specs = [row(D), row(HALF), row(LANE)]
        out_shape = [jax.ShapeDtypeStruct((M_TOK, D), F32), jax.ShapeDtypeStruct((M_TOK, HALF), U32),
                     jax.ShapeDtypeStruct((M_TOK, LANE), F32)]
    else:
        out_specs = [row(D), row(D)]
        out_shape = [jax.ShapeDtypeStruct((M_TOK, D), F32), jax.ShapeDtypeStruct((M_TOK, D), BF16)]
    return pl.pallas_call(
        functools.partial(_out_proj_kernel, moe, two_x),
        grid=(n_steps,),
        in_specs=in_specs,
        out_specs=out_specs,
        out_shape=out_shape,
        scratch_shapes=[pltpu.VMEM((TMO + 2 * POOL_HALO, POOL_W), F32), pltpu.VMEM((TMO, D), BF16)],
        compiler_params=_cparams(("parallel",)),
        name="out_proj_moe" if moe else "out_proj",
    )(*args)


def _ffn_up_kernel(h_ref, w1_ref, w3_ref, g_ref):
    h = h_ref[...]
    a = jnp.dot(h, w1_ref[...].astype(BF16), preferred_element_type=F32)
    b = jnp.dot(h, w3_ref[...].astype(BF16), preferred_element_type=F32)
    g_ref[...] = (a * jax.nn.sigmoid(a) * b).astype(BF16)


def _ffn_up_call(i, h, w1, w3):
    wspec = pl.BlockSpec((None, D, TF_UP), lambda j, t: (i, 0, j))
    return pl.pallas_call(
        _ffn_up_kernel,
        grid=(D_FF // TF_UP, M_TOK // TE_UP),
        in_specs=[pl.BlockSpec((TE_UP, D), lambda j, t: (t, 0)), wspec, wspec],
        out_specs=pl.BlockSpec((TE_UP, TF_UP), lambda j, t: (t, j)),
        out_shape=jax.ShapeDtypeStruct((M_TOK, D_FF), BF16),
        compiler_params=_cparams(("parallel", "parallel")),
        name="ffn_up",
    )(h, w1, w3)


def _ffn_down_kernel(g_ref, w2_ref, x_ref, mods_ref, o_ref):
    t = pl.program_id(1)
    y = jnp.dot(g_ref[...], w2_ref[...].astype(BF16), preferred_element_type=F32)
    ctx_tiles = M_P // TE_DOWN
    row = jnp.where(t < ctx_tiles, 0, 1 + (t - ctx_tiles) // (DEC_SEQ // TE_DOWN))
    o_ref[...] = x_ref[...] + mods_ref[pl.ds(row, 1), :] * y


def _ffn_down_call(l, g, w2, x, mods_all):
    i = l // 2
    return pl.pallas_call(
        _ffn_down_kernel,
        grid=(D // TN_DOWN, M_TOK // TE_DOWN),
        in_specs=[
            pl.BlockSpec((TE_DOWN, D_FF), lambda j, t: (t, 0)),
            pl.BlockSpec((None, D_FF, TN_DOWN), lambda j, t: (i, 0, j)),
            pl.BlockSpec((TE_DOWN, TN_DOWN), lambda j, t: (t, j)),
            pl.BlockSpec((None, MOD_ROWS, TN_DOWN), lambda j, t: (l, 0, 5 * (D // TN_DOWN) + j)),
        ],
        out_specs=pl.BlockSpec((TE_DOWN, TN_DOWN), lambda j, t: (t, j)),
        out_shape=jax.ShapeDtypeStruct((M_TOK, D), F32),
        compiler_params=_cparams(("parallel", "parallel")),
        name="ffn_down",
    )(g, w2, x, mods_all)


def _src_kernel(cpos_ref, src_ref):
    def body(i, c):
        src_ref[cpos_ref[i]] = lax.shift_right_logical(i, TOP_K_SHIFT)
        return c
    lax.fori_loop(0, N_SLOTS, body, 0, unroll=8)


def _src_call(cpos):
    return pl.pallas_call(
        _src_kernel,
        in_specs=[pl.BlockSpec(memory_space=pltpu.SMEM)],
        out_specs=pl.BlockSpec(memory_space=pltpu.SMEM),
        out_shape=jax.ShapeDtypeStruct((N_SLOTS,), I32),
        name="moe_src",
    )(cpos)


def _for_item_rows(cnt, live, fn):
    n_full = cnt // MC
    rem = cnt - n_full * MC
    n_mc = n_full + (rem > MC // 2).astype(I32)

    @pl.when(live)
    def _():
        def body(c, carry):
            fn(c * MC, MC)
            return carry
        lax.fori_loop(0, n_mc, body, 0)

    @pl.when(live & (rem > 0) & (rem <= MC // 2))
    def _():
        fn(n_mc * MC, MC // 2)


def _gather_copy(hp_hbm, tok, stage, slot, q, u, sem):
    return pltpu.make_async_copy(hp_hbm.at[pl.ds(tok, 1)], stage.at[slot, q, pl.ds(u, 1)], sem.at[slot])


def _moe_up_kernel(src_ref, ie_ref, ics_ref, icnt_ref, nit_ref, hp_hbm, w1_ref, w3_ref, g_ref,
                   buf, stage, sem):
    k = pl.program_id(0)
    j = pl.program_id(1)
    nit = nit_ref[0]

    def chunk_rows(item, c):
        return jnp.clip(icnt_ref[item] - c * GC, 0, GC)

    def request_rows(base, n, slot, q):
        for u in range(SUBLANES):
            tok = src_ref[base + jnp.minimum(q * SUBLANES + u, n - 1)]
            _gather_copy(hp_hbm, tok, stage, slot, q, u, sem).start()

    def issue_chunk(item, c, slot):
        n = chunk_rows(item, c)
        base = ics_ref[item] + c * GC

        @pl.when(n > 0)
        def _():
            def body(q, carry):
                request_rows(base, n, slot, q)
                return carry
            lax.fori_loop(0, GC // SUBLANES, body, 0)

    def finish_chunk(item, c, slot, b):
        @pl.when(chunk_rows(item, c) > 0)
        def _():
            def body(r, carry):
                _gather_copy(hp_hbm, 0, stage, slot, 0, 0, sem).wait()
                return carry
            lax.fori_loop(0, GC, body, 0, unroll=8)
            lo, hi = _unpack(stage[slot].reshape(GC, HALF))
            rows = pl.ds(pl.multiple_of(c * GC, GC), GC)
            buf[b, rows, 0:HALF] = lo.astype(BF16)
            buf[b, rows, HALF:D] = hi.astype(BF16)

    @pl.when((k == 0) & (j == 0))
    def _():
        buf[...] = jnp.zeros_like(buf)
        stage[...] = jnp.zeros_like(stage)
        issue_chunk(0, 0, 0)

        def body(c, carry):
            @pl.when(c + 1 < N_GC)
            def _():
                issue_chunk(0, c + 1, (c + 1) & 1)
            finish_chunk(0, c, c & 1, 0)
            return carry
        lax.fori_loop(0, N_GC, body, 0)

    nxt = k + 1

    @pl.when((nxt < nit) & (j >= 1) & (j <= N_GC))
    def _():
        finish_chunk(nxt, j - 1, (j - 1) & 1, nxt & 1)

    @pl.when((nxt < nit) & (j < N_GC))
    def _():
        issue_chunk(nxt, j, j & 1)

    g_ref[...] = jnp.zeros_like(g_ref)
    b = k & 1

    def mm_rows(start, size):
        rows = pl.ds(pl.multiple_of(start, size), size)
        h = buf[b, rows, :]
        a = jnp.dot(h, w1_ref[...].astype(BF16), preferred_element_type=F32)
        bb = jnp.dot(h, w3_ref[...].astype(BF16), preferred_element_type=F32)
        g_ref[rows, :] = (a * jax.nn.sigmoid(a) * bb).astype(BF16)

    _for_item_rows(icnt_ref[k], k < nit, mm_rows)


def _moe_up_call(layer_e0, src, item_e, item_cs, item_cnt, n_items, hp, w1, w3):
    n_f = D_FFE // TF_MOE
    wspec = pl.BlockSpec((None, D, TF_MOE), lambda k, j, s, ie, ics, ic, ni: (
        layer_e0 + ie[k], 0, jnp.where(k < ni[0], j, n_f - 1)))
    return pl.pallas_call(
        _moe_up_kernel,
        grid_spec=pltpu.PrefetchScalarGridSpec(
            num_scalar_prefetch=5,
            grid=(N_ITEMS, D_FFE // TF_MOE),
            in_specs=[pl.BlockSpec(memory_space=pl.ANY), wspec, wspec],
            out_specs=pl.BlockSpec((RT, TF_MOE), lambda k, j, s, ie, ics, ic, ni: (k, j)),
            scratch_shapes=[
                pltpu.VMEM((2, RT, D), BF16),
                pltpu.VMEM((2, GC // SUBLANES, SUBLANES, HALF), U32),
                pltpu.SemaphoreType.DMA((2,)),
            ],
        ),
        out_shape=jax.ShapeDtypeStruct((N_ITEMS * RT, D_FFE), BF16),
        compiler_params=_cparams(("arbitrary", "arbitrary")),
        name="moe_up",
    )(src, item_e, item_cs, item_cnt, n_items, hp, w1, w3)


def _moe_down_kernel(ie_ref, icnt_ref, nit_ref, g_ref, w2a_ref, w2b_ref, o_ref):
    k = pl.program_id(0)
    o_ref[...] = jnp.zeros_like(o_ref)

    def mm_rows(start, size):
        rows = pl.ds(pl.multiple_of(start, size), size)
        g = g_ref[rows, :]
        lo = jnp.dot(g, w2a_ref[...].astype(BF16), preferred_element_type=F32)
        hi = jnp.dot(g, w2b_ref[...].astype(BF16), preferred_element_type=F32)
        o_ref[rows, :] = _pack(lo, hi)

    _for_item_rows(icnt_ref[k], k < nit_ref[0], mm_rows)


def _moe_down_call(layer_e0, item_e, item_cnt, n_items, g, w2):
    live = lambda k, ni: jnp.minimum(k, ni[0] - 1)
    n_half = HALF // TN_MOE
    col = lambda k, n, ni: jnp.where(k < ni[0], n, n_half - 1)
    return pl.pallas_call(
        _moe_down_kernel,
        grid_spec=pltpu.PrefetchScalarGridSpec(
            num_scalar_prefetch=3,
            grid=(N_ITEMS, n_half),
            in_specs=[
                pl.BlockSpec((RT, D_FFE), lambda k, n, ie, ic, ni: (live(k, ni), 0)),
                pl.BlockSpec((None, D_FFE, TN_MOE),
                             lambda k, n, ie, ic, ni: (layer_e0 + ie[k], 0, col(k, n, ni))),
                pl.BlockSpec((None, D_FFE, TN_MOE),
                             lambda k, n, ie, ic, ni: (layer_e0 + ie[k], 0, n_half + col(k, n, ni))),
            ],
            out_specs=pl.BlockSpec((RT, TN_MOE), lambda k, n, ie, ic, ni: (k, n)),
        ),
        out_shape=jax.ShapeDtypeStruct((N_ITEMS * RT, HALF), U32),
        compiler_params=_cparams(("arbitrary", "arbitrary")),
        name="moe_down",
    )(item_e, item_cnt, n_items, g, w2, w2)


def _combine_copy(yo_hbm, row, buf, slot, r, sem):
    return pltpu.make_async_copy(yo_hbm.at[pl.ds(row, 1)], buf.at[slot, pl.ds(r, 1)], sem.at[slot])


def _combine_kernel(split, pos_ref, yo_hbm, x_ref, rinfo_ref, mods_ref, *rest):
    if split:
        yp_ref, ys_ref, buf, sem = rest
    else:
        o_ref, buf, sem = rest
    i = pl.program_id(0)
    n = pl.num_programs(0)
    rows = TOP_K * TM

    def issue(tile, slot):
        def body(r, c):
            _combine_copy(yo_hbm, pos_ref[tile * rows + r], buf, slot, r, sem).start()
            return c
        lax.fori_loop(0, rows, body, 0, unroll=8)

    @pl.when(i == 0)
    def _():
        issue(0, 0)

    @pl.when(i + 1 < n)
    def _():
        issue(i + 1, (i + 1) & 1)

    slot = i & 1

    def body(r, c):
        _combine_copy(yo_hbm, 0, buf, slot, r, sem).wait()
        return c
    lax.fori_loop(0, rows, body, 0, unroll=8)
    rinfo = rinfo_ref[...]
    lo0, hi0 = _unpack(buf[slot, 0:TM, :])
    lo1, hi1 = _unpack(buf[slot, TM:2 * TM, :])
    gt0, gt1 = rinfo[:, 2:3], rinfo[:, 3:4]
    g2 = mods_ref[pl.ds(_mod_row(i), 1), :]
    y_lo = x_ref[:, 0:HALF] + g2[:, 0:HALF] * (gt0 * lo0 + gt1 * lo1)
    y_hi = x_ref[:, HALF:D] + g2[:, HALF:D] * (gt0 * hi0 + gt1 * hi1)

    def write(ref):
        ref[:, 0:HALF] = y_lo
        ref[:, HALF:D] = y_hi

    if not split:
        write(o_ref)
        return

    @pl.when(i < P_TILES)
    def _():
        write(yp_ref)

    @pl.when(i >= P_TILES)
    def _():
        write(ys_ref)


def _combine_call(l, split, pos, yo, x, rinfo, mods_all):
    if split:
        out_specs = [pl.BlockSpec((TM, D), lambda i, p: (jnp.minimum(i, P_TILES - 1), 0)),
                     pl.BlockSpec((TM, D), lambda i, p: (jnp.maximum(i - P_TILES, 0), 0))]
        out_shape = [jax.ShapeDtypeStruct((M_P, D), F32), jax.ShapeDtypeStruct((M_S, D), F32)]
    else:
        out_specs = pl.BlockSpec((TM, D), lambda i, p: (i, 0))
        out_shape = jax.ShapeDtypeStruct((M_TOK, D), F32)
    return pl.pallas_call(
        functools.partial(_combine_kernel, split),
        grid_spec=pltpu.PrefetchScalarGridSpec(
            num_scalar_prefetch=1,
            grid=(N_TILES,),
            in_specs=[
                pl.BlockSpec(memory_space=pl.ANY),
                pl.BlockSpec((TM, D), lambda i, p: (i, 0)),
                pl.BlockSpec((TM, LANE), lambda i, p: (i, 0)),
                pl.BlockSpec((None, MOD_ROWS, D), lambda i, p: (l, 0, 5)),
            ],
            out_specs=out_specs,
            scratch_shapes=[pltpu.VMEM((2, TOP_K * TM, HALF), U32), pltpu.SemaphoreType.DMA((2,))],
        ),
        out_shape=out_shape,
        compiler_params=_cparams(("arbitrary",)),
        name="moe_combine",
    )(pos, yo, x, rinfo, mods_all)


def _route_plan(rinfo):
    flat_e = rinfo[:, 0:TOP_K].astype(I32).reshape(-1)
    onehot = (flat_e[:, None] == jnp.arange(N_EXP, dtype=I32)[None, :]).astype(I32)
    csum = jnp.cumsum(onehot, axis=0)
    pick = lambda table: jnp.sum(onehot * table[None, :], axis=1)
    rank = jnp.sum(onehot * csum, axis=1) - 1
    counts = csum[-1]
    cstart = jnp.cumsum(counts) - counts
    n_it = (counts + RT - 1) // RT
    it_end = jnp.cumsum(n_it)
    it_base = it_end - n_it
    n_items = it_end[-1]
    cpos = pick(cstart) + rank
    ppos = (pick(it_base) + rank // RT) * RT + rank % RT
    q = jnp.arange(N_ITEMS, dtype=I32)
    qc = jnp.minimum(q, n_items - 1)
    qe = jnp.minimum(jnp.sum((it_end[None, :] <= qc[:, None]).astype(I32), axis=1), N_EXP - 1)
    qhot = (qe[:, None] == jnp.arange(N_EXP, dtype=I32)[None, :]).astype(I32)
    qpick = lambda table: jnp.sum(qhot * table[None, :], axis=1)
    s = qc - qpick(it_base)
    item_cnt = jnp.where(q < n_items, jnp.clip(qpick(counts) - s * RT, 0, RT), 0)
    item_cs = qpick(cstart) + s * RT
    pos_tiles = ppos.reshape(N_TILES, TM, TOP_K).transpose(0, 2, 1).reshape(-1)
    return (cpos.astype(I32), qe.astype(I32), item_cs.astype(I32), item_cnt.astype(I32),
            n_items.reshape(1).astype(I32), pos_tiles.astype(I32))


def _prep_w_in_kernel(w_ref, o_ref):
    w = w_ref[...]
    gap = jnp.zeros((w.shape[0], LANE - ROPE), F32)
    o_ref[:, 0:C_GM] = jnp.concatenate([w[:, 0:C_KR + ROPE], gap], axis=1).astype(BF16)
    o_ref[:, C_GM:IN_PAD] = w[:, C_KR + ROPE:].astype(BF16)


def _prep_w_in_call(w_in):
    n_in = w_in.shape[-1]
    return pl.pallas_call(
        _prep_w_in_kernel,
        grid=(DEPTH, D // TM),
        in_specs=[pl.BlockSpec((None, TM, n_in), lambda l, i: (l, i, 0))],
        out_specs=pl.BlockSpec((None, TM, IN_PAD), lambda l, i: (l, i, 0)),
        out_shape=jax.ShapeDtypeStruct((DEPTH, D, IN_PAD), BF16),
        compiler_params=_cparams(("parallel", "parallel")),
        name="prep_w_in",
    )(w_in)


def _rope_tables():
    rows = DEC_SEQ // GRID_W
    row = jnp.repeat(jnp.arange(rows), GRID_W).astype(F32)
    col = jnp.tile(jnp.arange(GRID_W), rows).astype(F32)
    nf = ROPE // 4
    inv = ROPE_BASE ** (-jnp.arange(nf, dtype=F32) / nf)
    ar = row[:, None] * inv[None, :]
    ac = col[:, None] * inv[None, :]
    ang = jnp.concatenate([ar, ar, ac, ac], axis=-1)
    cos, sin = jnp.cos(ang), jnp.sin(ang)
    first = (jnp.arange(ROPE) % 32) < 16
    pad = lambda a, v: jnp.pad(a, ((0, 0), (0, LANE - ROPE)), constant_values=v)
    cos_p = pad(cos, 1.0)
    sin_a = pad(jnp.where(first[None, :], -sin, 0.0), 0.0)
    sin_b = pad(jnp.where(first[None, :], 0.0, sin), 0.0)
    ident = jnp.stack([jnp.ones((TMI, LANE), F32), jnp.zeros((TMI, LANE), F32), jnp.zeros((TMI, LANE), F32)])
    return jnp.concatenate([ident, jnp.stack([cos_p, sin_a, sin_b])], axis=1)


def kernel(x_prompt, x_sample, cache_ckv, cache_krope, c, c_ctx, w_ada, b_ada, norm1_g, norm2_g, w_in,
           qlat_norm_g, kvlat_norm_g, w_uq, w_ukv, q_head_norm_g, k_head_norm_g, gm_norm_g, gm_w_s,
           gm_b_s, pool_w, pool_scale, w_o, ffn_w1, ffn_w3, ffn_w2, router_w, moe_w1, moe_w3, moe_w2):
    w_in_p = _prep_w_in_call(w_in)
    w_uq_p = jnp.pad(w_uq.reshape(DEPTH, Q_RANK, N_HEADS, QK_DIM),
                     ((0, 0), (0, 0), (0, 0), (0, HEAD_PAD - QK_DIM))
                     ).reshape(DEPTH, Q_RANK, QK_W).astype(BF16)
    w_ukv_b = w_ukv.astype(BF16)
    w_o_b = w_o.astype(BF16)
    qg_p = jnp.pad(q_head_norm_g * (1.0 / math.sqrt(QK_DIM)),
                   ((0, 0), (0, HEAD_PAD - QK_DIM))).reshape(DEPTH, 1, HEAD_PAD)
    kg_n = k_head_norm_g[:, :NOPE].reshape(DEPTH, 1, LANE)
    kg_r = jnp.pad(k_head_norm_g[:, NOPE:], ((0, 0), (0, LANE - ROPE))).reshape(DEPTH, 1, LANE)
    gmw_b = gm_w_s.astype(BF16)
    gmb = jnp.broadcast_to(gm_b_s[:, :, :, None], (DEPTH, GM_GROUPS, CHUNK, LANE))
    pw_b = pool_w.astype(BF16)
    ps = pool_scale.reshape(DEPTH, 1, POOL_W)
    n1g = norm1_g.reshape(DEPTH, 1, D)
    n2g = norm2_g.reshape(DEPTH, 1, D)
    qlg = qlat_norm_g.reshape(DEPTH, 1, Q_RANK)
    kvlg = kvlat_norm_g.reshape(DEPTH, 1, KV_RANK)
    rw_p = jnp.pad(router_w, ((0, 0), (0, 0), (0, LANE - N_EXP)))
    n_moe = moe_w1.shape[0]
    moe_w1_flat = moe_w1.reshape(n_moe * N_EXP, D, D_FFE)
    moe_w3_flat = moe_w3.reshape(n_moe * N_EXP, D, D_FFE)
    moe_w2_flat = moe_w2.reshape(n_moe * N_EXP, D_FFE, D)
    rope_tab = _rope_tables()
    cache_kr_p = jnp.pad(cache_krope, ((0, 0), (0, 0), (0, 0), (0, LANE - ROPE)))

    cond = jnp.zeros((MOD_ROWS, D), F32).at[0].set(c_ctx).at[1:1 + DEC_BATCH].set(c)
    mods_all = _ada_call(cond, w_ada, b_ada)
    k_c, v_c = _cache_kv_call(cache_ckv, cache_kr_p, w_ukv_b, kg_n, kg_r)

    x = (x_prompt.reshape(M_P, D), x_sample.reshape(M_S, D))
    ckv_out, kr_out = [], []
    for l in range(DEPTH):
        q, k, v, ckv, kr, zg, zp = _in_proj_call(
            l, x, mods_all, n1g, w_in_p, qlg, kvlg, w_uq_p, qg_p, w_ukv_b, kg_n, kg_r, rope_tab,
            gm_norm_g, gmw_b, gmb)
        ckv_out.append(ckv[:M_P].reshape(BATCH, SEQ, KV_RANK))
        kr_out.append(kr[:M_P, :ROPE].reshape(BATCH, SEQ, ROPE))
        attn = _attn_call(l, q, k, v, k_c, v_c)
        i = l // 2
        if l % 2 == 0:
            x, h2b = _out_proj_call(l, False, attn, zg, zp, x, mods_all, pw_b, ps, w_o_b, n2g)
            g = _ffn_up_call(i, h2b, ffn_w1, ffn_w3)
            x = _ffn_down_call(l, g, ffn_w2, x, mods_all)
        else:
            x, hp, rinfo = _out_proj_call(l, True, attn, zg, zp, x, mods_all, pw_b, ps, w_o_b, n2g, rw_p)
            cpos, item_e, item_cs, item_cnt, n_items, pos_tiles = _route_plan(rinfo)
            src = _src_call(cpos)
            g = _moe_up_call(i * N_EXP, src, item_e, item_cs, item_cnt, n_items, hp,
                             moe_w1_flat, moe_w3_flat)
            yo = _moe_down_call(i * N_EXP, item_e, item_cnt, n_items, g, moe_w2_flat)
            x = _combine_call(l, l == DEPTH - 1, pos_tiles, yo, x, rinfo, mods_all)

    if DEPTH % 2 == 0:
        y_p, y_s = x
    else:
        y_p, y_s = x[:M_P], x[M_P:]
    return (y_p.reshape(BATCH, SEQ, D), y_s.reshape(DEC_BATCH, DEC_SEQ, D),
            jnp.stack(ckv_out, axis=1), jnp.stack(kr_out, axis=1))
```

```python
import functools
import math

import jax
import jax.numpy as jnp
from jax import lax
from jax.experimental import pallas as pl
from jax.experimental.pallas import tpu as pltpu

F32 = jnp.float32
BF16 = jnp.bfloat16
U32 = jnp.uint32
I32 = jnp.int32

D = 2048
BATCH, SEQ = 16, 256
DEC_BATCH, DEC_SEQ = 2, 1024
PAST = 256
DEPTH = 4
GRID_W = 64
EPS = 1e-6
N_HEADS = 8
NOPE, ROPE, VDIM = 128, 64, 128
QK_DIM = NOPE + ROPE
HEAD_PAD = 256
QK_W = N_HEADS * HEAD_PAD
Q_RANK, KV_RANK = 512, 256
ROPE_BASE = 10000.0
ATTN_W = N_HEADS * VDIM
CHUNK, GM_GROUPS, GM_W = 128, 4, 512
POOL_WINDOWS = (2, 4, 8, 16)
POOL_W = 512
POOL_HALO = 8
D_FF, N_EXP, D_FFE, TOP_K = 5632, 8, 2816, 2
TOP_K_SHIFT = 1

C_Q, C_KV, C_KR, C_GM, C_PL = 0, 512, 768, 896, 1920
IN_PAD = 2432

LANE = 128
SUBLANES = 8
TM = 256
TMI = 512
TMO = 256
M_P, M_S = BATCH * SEQ, DEC_BATCH * DEC_SEQ
M_TOK = M_P + M_S
N_TILES = M_TOK // TM
P_TILES = M_P // TM
S_TILES = DEC_SEQ // TM
MOD_ROWS = 16
ADA_TN = 1024
TE_UP, TF_UP = 1024, 512
TE_DOWN, TN_DOWN = 512, 512
HALF = D // 2
N_SLOTS = TOP_K * M_TOK
RT = 2048
N_ITEMS = N_EXP + N_SLOTS // RT
GC = 256
N_GC = RT // GC
MC = 1024
TF_MOE = 256
TN_MOE = 256
VMEM_LIMIT = 56 * 1024 * 1024


def _cparams(sem):
    return pltpu.CompilerParams(dimension_semantics=sem, vmem_limit_bytes=VMEM_LIMIT)


def _const_spec(shape):
    nd = len(shape)
    return pl.BlockSpec(shape, lambda *_: (0,) * nd)


def _layer_spec(l, shape):
    nd = len(shape)
    return pl.BlockSpec((None,) + tuple(shape), lambda *_: (l,) + (0,) * nd)


def _x_operands(x, tile):
    if not isinstance(x, tuple):
        return [x], [pl.BlockSpec((tile, D), lambda i: (i, 0))]
    ctx = M_P // tile
    return list(x), [pl.BlockSpec((tile, D), lambda i: (jnp.minimum(i, ctx - 1), 0)),
                     pl.BlockSpec((tile, D), lambda i: (jnp.maximum(i - ctx, 0), 0))]


def _mod_row(i):
    return jnp.where(i < P_TILES, 0, 1 + (i - P_TILES) // S_TILES)


def _rms(x, g, n):
    ms = jnp.sum(x * x, axis=-1, keepdims=True) * (1.0 / n)
    return x * lax.rsqrt(ms + EPS) * g


def _rope(r, cos, sin_a, sin_b):
    return r * cos + pltpu.roll(r, LANE - 16, 1) * sin_a + pltpu.roll(r, 16, 1) * sin_b


def _pack(lo, hi):
    lo_b = lax.bitcast_convert_type(lo.astype(BF16).astype(F32), U32)
    hi_b = lax.bitcast_convert_type(hi.astype(BF16).astype(F32), U32)
    return (lo_b >> 16) | hi_b


def _unpack(w):
    lo = lax.bitcast_convert_type(w << 16, F32)
    hi = lax.bitcast_convert_type(w & jnp.uint32(0xFFFF0000), F32)
    return lo, hi


def _ada_kernel(cond_ref, w_ref, b_ref, o_ref):
    c = cond_ref[...]
    s = c * jax.nn.sigmoid(c)
    s_hi = s.astype(BF16)
    s_lo = (s - s_hi.astype(F32)).astype(BF16)
    w = w_ref[...]
    w_hi = w.astype(BF16)
    w_lo = (w - w_hi.astype(F32)).astype(BF16)
    lhs = jnp.concatenate([s_hi, s_lo], axis=0)
    r1 = jnp.dot(lhs, w_hi, preferred_element_type=F32)
    r2 = jnp.dot(s_hi, w_lo, preferred_element_type=F32)
    o_ref[...] = r1[:MOD_ROWS] + r1[MOD_ROWS:] + r2 + b_ref[...]


def _ada_call(cond, w_ada, b_ada):
    n = 6 * D
    return pl.pallas_call(
        _ada_kernel,
        grid=(DEPTH, n // ADA_TN),
        in_specs=[
            pl.BlockSpec((MOD_ROWS, D), lambda l, j: (0, 0)),
            pl.BlockSpec((None, D, ADA_TN), lambda l, j: (l, 0, j)),
            pl.BlockSpec((None, 1, ADA_TN), lambda l, j: (l, 0, j)),
        ],
        out_specs=pl.BlockSpec((None, MOD_ROWS, ADA_TN), lambda l, j: (l, 0, j)),
        out_shape=jax.ShapeDtypeStruct((DEPTH, MOD_ROWS, n), F32),
        compiler_params=_cparams(("parallel", "parallel")),
        name="ada",
    )(cond, w_ada, b_ada.reshape(DEPTH, 1, n))


def _write_kv(ckv_b, kr, w_ukv_ref, kg_n, kg_r, rope, k_ref, v_ref, rows=slice(None)):
    kv = jnp.dot(ckv_b, w_ukv_ref[...], preferred_element_type=F32)
    ssr = jnp.sum(kr * kr, axis=-1, keepdims=True)
    shared = kr * kg_r
    if rope is not None:
        shared = _rope(shared, *rope)
    for h in range(N_HEADS):
        kn = kv[:, h * HEAD_PAD:h * HEAD_PAD + NOPE]
        vv = kv[:, h * HEAD_PAD + NOPE:(h + 1) * HEAD_PAD]
        ms = (jnp.sum(kn * kn, axis=-1, keepdims=True) + ssr) * (1.0 / QK_DIM)
        inv = lax.rsqrt(ms + EPS)
        k_ref[rows, h * HEAD_PAD:h * HEAD_PAD + NOPE] = (kn * inv * kg_n).astype(BF16)
        k_ref[rows, h * HEAD_PAD + NOPE:(h + 1) * HEAD_PAD] = (shared * inv).astype(BF16)
        v_ref[rows, h * VDIM:(h + 1) * VDIM] = vv.astype(BF16)


def _cache_kv_kernel(ckv_ref, kr_ref, w_ukv_ref, kgn_ref, kgr_ref, k_ref, v_ref):
    _write_kv(ckv_ref[...].astype(BF16), kr_ref[...], w_ukv_ref, kgn_ref[...], kgr_ref[...],
              None, k_ref, v_ref)


def _cache_kv_call(cache_ckv, cache_kr_p, w_ukv_b, kg_n, kg_r):
    return pl.pallas_call(
        _cache_kv_kernel,
        grid=(DEPTH, DEC_BATCH),
        in_specs=[
            pl.BlockSpec((None, None, PAST, KV_RANK), lambda l, b: (b, l, 0, 0)),
            pl.BlockSpec((None, None, PAST, LANE), lambda l, b: (b, l, 0, 0)),
            pl.BlockSpec((None, KV_RANK, QK_W), lambda l, b: (l, 0, 0)),
            pl.BlockSpec((None, 1, LANE), lambda l, b: (l, 0, 0)),
            pl.BlockSpec((None, 1, LANE), lambda l, b: (l, 0, 0)),
        ],
        out_specs=[
            pl.BlockSpec((None, None, PAST, QK_W), lambda l, b: (l, b, 0, 0)),
            pl.BlockSpec((None, None, PAST, ATTN_W), lambda l, b: (l, b, 0, 0)),
        ],
        out_shape=[
            jax.ShapeDtypeStruct((DEPTH, DEC_BATCH, PAST, QK_W), BF16),
            jax.ShapeDtypeStruct((DEPTH, DEC_BATCH, PAST, ATTN_W), BF16),
        ],
        compiler_params=_cparams(("parallel", "parallel")),
        name="cache_kv",
    )(cache_ckv, cache_kr_p, w_ukv_b, kg_n, kg_r)


def _in_proj_kernel(two_x, *refs):
    x_refs, refs = refs[:1 + two_x], refs[1 + two_x:]
    _in_proj_body(x_refs, *refs)


def _in_proj_body(x_refs, mods_ref, n1g_ref, w_in_ref, qlg_ref, kvlg_ref, w_uq_ref, qg_ref,
                    w_ukv_ref, kgn_ref, kgr_ref, rope_ref, gmg_ref, gmw_ref, gmb_ref,
                    q_ref, k_ref, v_ref, ckv_ref, kr_ref, zg_ref, zp_ref):
    i = pl.program_id(0)
    row = jnp.where(i < M_P // TMI, 0, 1 + (i - M_P // TMI) // (DEC_SEQ // TMI))
    m = mods_ref[pl.ds(row, 1), :]
    sh1, sc1 = m[:, 0:D], m[:, D:2 * D]
    for sub in range(TMI // TM):
        rs = slice(sub * TM, (sub + 1) * TM)
        x = x_refs[0][rs, :] if len(x_refs) == 1 else jnp.where(i < M_P // TMI, x_refs[0][rs, :], x_refs[1][rs, :])
        h = _rms(x, n1g_ref[...], D) * (1.0 + sc1) + sh1
        z = jnp.dot(h.astype(BF16), w_in_ref[...], preferred_element_type=F32)
        rope = (rope_ref[0, rs, :], rope_ref[1, rs, :], rope_ref[2, rs, :])

        ql = _rms(z[:, C_Q:C_Q + Q_RANK], qlg_ref[...], Q_RANK)
        qf = jnp.dot(ql.astype(BF16), w_uq_ref[...], preferred_element_type=F32)
        qg = qg_ref[...]
        for hd in range(N_HEADS):
            seg = qf[:, hd * HEAD_PAD:(hd + 1) * HEAD_PAD]
            seg = _rms(seg, qg, QK_DIM)
            q_ref[rs, hd * HEAD_PAD:hd * HEAD_PAD + NOPE] = seg[:, :NOPE].astype(BF16)
            q_ref[rs, hd * HEAD_PAD + NOPE:(hd + 1) * HEAD_PAD] = _rope(seg[:, NOPE:], *rope).astype(BF16)

        ckv = _rms(z[:, C_KV:C_KV + KV_RANK], kvlg_ref[...], KV_RANK)
        kr = z[:, C_KR:C_KR + LANE]
        ckv_ref[rs, :] = ckv
        kr_ref[rs, :] = kr
        _write_kv(ckv.astype(BF16), kr, w_ukv_ref, kgn_ref[...], kgr_ref[...], rope, k_ref, v_ref, rs)

        zg = jax.nn.gelu(z[:, C_GM:C_GM + 2 * GM_W], approximate=True)
        for g in range(GM_GROUPS):
            u = zg[:, g * LANE:(g + 1) * LANE]
            vg = _rms(zg[:, GM_W + g * LANE:GM_W + (g + 1) * LANE], gmg_ref[g:g + 1, :], LANE)
            vb = vg.astype(BF16)
            for c in range(TM // CHUNK):
                sv = jnp.dot(gmw_ref[g], vb[c * CHUNK:(c + 1) * CHUNK, :], preferred_element_type=F32)
                sv = sv + gmb_ref[g]
                r0 = sub * TM + c * CHUNK
                zg_ref[r0:r0 + CHUNK, g * LANE:(g + 1) * LANE] = (
                    u[c * CHUNK:(c + 1) * CHUNK, :] * sv).astype(BF16)

        zp_ref[rs, :] = z[:, C_PL:C_PL + POOL_W]


def _in_proj_call(l, x, mods_all, n1g, w_in_b, qlg, kvlg, w_uq_b, qg, w_ukv_b, kg_n, kg_r, rope_tab,
                  gmg, gmw_b, gmb):
    def rope_idx(i):
        return (0, jnp.where(i < M_P // TMI, 0, 1 + (i - M_P // TMI) % (DEC_SEQ // TMI)), 0)

    row = lambda w: pl.BlockSpec((TMI, w), lambda i: (i, 0))
    two_x = isinstance(x, tuple)
    x_args, x_specs = _x_operands(x, TMI)
    return pl.pallas_call(
        functools.partial(_in_proj_kernel, two_x),
        grid=(M_TOK // TMI,),
        in_specs=x_specs + [
            _layer_spec(l, (MOD_ROWS, 6 * D)),
            _layer_spec(l, (1, D)),
            _layer_spec(l, (D, IN_PAD)),
            _layer_spec(l, (1, Q_RANK)),
            _layer_spec(l, (1, KV_RANK)),
            _layer_spec(l, (Q_RANK, QK_W)),
            _layer_spec(l, (1, HEAD_PAD)),
            _layer_spec(l, (KV_RANK, QK_W)),
            _layer_spec(l, (1, LANE)),
            _layer_spec(l, (1, LANE)),
            pl.BlockSpec((3, TMI, LANE), rope_idx),
            _layer_spec(l, (GM_GROUPS, LANE)),
            _layer_spec(l, (GM_GROUPS, CHUNK, CHUNK)),
            _layer_spec(l, (GM_GROUPS, CHUNK, LANE)),
        ],
        out_specs=[row(QK_W), row(QK_W), row(ATTN_W), row(KV_RANK), row(LANE), row(GM_W), row(POOL_W)],
        out_shape=[
            jax.ShapeDtypeStruct((M_TOK, QK_W), BF16),
            jax.ShapeDtypeStruct((M_TOK, QK_W), BF16),
            jax.ShapeDtypeStruct((M_TOK, ATTN_W), BF16),
            jax.ShapeDtypeStruct((M_TOK, KV_RANK), F32),
            jax.ShapeDtypeStruct((M_TOK, LANE), F32),
            jax.ShapeDtypeStruct((M_TOK, GM_W), BF16),
            jax.ShapeDtypeStruct((M_TOK, POOL_W), F32),
        ],
        compiler_params=_cparams(("parallel",)),
        name="in_proj",
    )(*x_args, mods_all, n1g, w_in_b, qlg, kvlg, w_uq_b, qg, w_ukv_b, kg_n, kg_r, rope_tab, gmg, gmw_b, gmb)


def _attend_heads(q_ref, kv_pairs, o_ref):
    nt = (((1,), (1,)), ((), ()))
    for h in range(N_HEADS):
        qh = q_ref[:, h * HEAD_PAD:(h + 1) * HEAD_PAD]
        ss = [lax.dot_general(qh, k_ref[:, h * HEAD_PAD:(h + 1) * HEAD_PAD], nt,
                              preferred_element_type=F32) for k_ref, _ in kv_pairs]
        mx = ss[0].max(axis=-1, keepdims=True)
        for s in ss[1:]:
            mx = jnp.maximum(mx, s.max(axis=-1, keepdims=True))
        den = None
        acc = None
        for s, (_, v_ref) in zip(ss, kv_pairs):
            p = jnp.exp(s - mx)
            d = jnp.sum(p, axis=-1, keepdims=True)
            a = jnp.dot(p.astype(BF16), v_ref[:, h * VDIM:(h + 1) * VDIM], preferred_element_type=F32)
            den = d if den is None else den + d
            acc = a if acc is None else acc + a
        o_ref[:, h * VDIM:(h + 1) * VDIM] = (acc / den).astype(BF16)


def _attn_kernel(q_ref, kp_ref, vp_ref, ks_ref, vs_ref, kc_ref, vc_ref, o_ref):
    i = pl.program_id(0)

    @pl.when(i < P_TILES)
    def _():
        _attend_heads(q_ref, [(kp_ref, vp_ref)], o_ref)

    @pl.when(i >= P_TILES)
    def _():
        _attend_heads(q_ref, [(ks_ref, vs_ref), (kc_ref, vc_ref)], o_ref)


def _attn_call(l, q, k, v, k_c, v_c):
    def seq_b(i):
        return jnp.maximum(i - P_TILES, 0) // S_TILES

    small = lambda w: pl.BlockSpec((TM, w), lambda i: (jnp.minimum(i, P_TILES - 1), 0))
    big = lambda w: pl.BlockSpec((DEC_SEQ, w), lambda i: (M_P // DEC_SEQ + seq_b(i), 0))
    cache = lambda w: pl.BlockSpec((None, None, PAST, w), lambda i: (l, seq_b(i), 0, 0))
    return pl.pallas_call(
        _attn_kernel,
        grid=(N_TILES,),
        in_specs=[
            pl.BlockSpec((TM, QK_W), lambda i: (i, 0)),
            small(QK_W), small(ATTN_W),
            big(QK_W), big(ATTN_W),
            cache(QK_W), cache(ATTN_W),
        ],
        out_specs=pl.BlockSpec((TM, ATTN_W), lambda i: (i, 0)),
        out_shape=jax.ShapeDtypeStruct((M_TOK, ATTN_W), BF16),
        compiler_params=_cparams(("parallel",)),
        name="attention",
    )(q, k, v, k, v, k_c, v_c)


def _out_proj_kernel(moe, two_x, attn_ref, zg_ref, zprev_ref, zcur_ref, znext_ref, *rest):
    x_refs, rest = rest[:1 + two_x], rest[1 + two_x:]
    mods_ref, pw_ref, ps_ref, w_o_ref, n2g_ref = rest[:5]
    rest = rest[5:]
    if moe:
        rw_ref, xo_ref, hp_ref, rinfo_ref, ext_ref, mix_ref = rest
    else:
        xo_ref, h2b_ref, ext_ref, mix_ref = rest
    i = pl.program_id(0)
    ctx_steps = M_P // TMO
    is_ctx = i < ctx_steps
    seq_len = jnp.where(is_ctx, SEQ, DEC_SEQ)
    m = mods_ref[pl.ds(jnp.where(is_ctx, 0, 1 + (i - ctx_steps) // (DEC_SEQ // TMO)), 1), :]
    g1, sh2, sc2 = m[:, 2 * D:3 * D], m[:, 3 * D:4 * D], m[:, 4 * D:5 * D]

    pos0 = jnp.where(is_ctx, 0, ((i - ctx_steps) % (DEC_SEQ // TMO)) * TMO)
    zero_halo = jnp.zeros((POOL_HALO, POOL_W), F32)
    ext_ref[0:POOL_HALO, :] = jnp.where(pos0 == 0, zero_halo, zprev_ref[TMO - POOL_HALO:, :])
    ext_ref[POOL_HALO:POOL_HALO + TMO, :] = zcur_ref[...]
    ext_ref[POOL_HALO + TMO:, :] = jnp.where(pos0 + TMO == seq_len, zero_halo, znext_ref[0:POOL_HALO, :])
    mix_ref[:, 0:ATTN_W] = attn_ref[...]
    mix_ref[:, ATTN_W:ATTN_W + GM_W] = zg_ref[...]

    assert TMO == TM and SEQ % TMO == 0 and DEC_SEQ % TMO == 0
    for sub in range(TMO // TM):
        rs = slice(sub * TM, (sub + 1) * TM)
        pos = pos0 + lax.broadcasted_iota(I32, (TM, 1), 0)
        for gi, w in enumerate(POOL_WINDOWS):
            cols = slice(gi * LANE, (gi + 1) * LANE)
            win = None
            for dlt in range(-(w // 2), w - w // 2):
                r0 = POOL_HALO + dlt
                part = ext_ref[r0:r0 + TM, cols]
                win = part if win is None else win + part
            lo = jnp.maximum(pos - w // 2, 0)
            hi = jnp.minimum(pos + (w - 1 - w // 2), seq_len - 1)
            cnt = (hi - lo + 1).astype(F32)
            pooled = win / cnt - zcur_ref[rs, cols]
            mixed = jnp.dot(pooled.astype(BF16), pw_ref[gi], preferred_element_type=F32)
            mix_ref[rs, ATTN_W + GM_W + gi * LANE:ATTN_W + GM_W + (gi + 1) * LANE] = (
                mixed * ps_ref[:, cols]).astype(BF16)

        out = jnp.dot(mix_ref[rs, :], w_o_ref[...], preferred_element_type=F32)
        x = x_refs[0][rs, :] if len(x_refs) == 1 else jnp.where(is_ctx, x_refs[0][rs, :], x_refs[1][rs, :])
        xn = x + g1 * out
        xo_ref[rs, :] = xn
        h2 = _rms(xn, n2g_ref[...], D) * (1.0 + sc2) + sh2

        if not moe:
            h2b_ref[rs, :] = h2.astype(BF16)
            continue

        hp_ref[rs, :] = _pack(h2[:, :HALF], h2[:, HALF:])
        h_hi = h2.astype(BF16)
        h_lo = (h2 - h_hi.astype(F32)).astype(BF16)
        rw = rw_ref[...]
        w_hi = rw.astype(BF16)
        w_lo = (rw - w_hi.astype(F32)).astype(BF16)
        logits = (jnp.dot(h_hi, w_hi, preferred_element_type=F32)
                  + jnp.dot(h_lo, w_hi, preferred_element_type=F32)
                  + jnp.dot(h_hi, w_lo, preferred_element_type=F32))
        lane = lax.broadcasted_iota(I32, (TM, LANE), 1).astype(F32)
        neg = jnp.float32(-jnp.inf)
        logits = jnp.where(lane < N_EXP, logits, neg)
        v1 = logits.max(axis=-1, keepdims=True)
        i1 = jnp.where(logits == v1, lane, float(LANE)).min(axis=-1, keepdims=True)
        l2 = jnp.where(lane == i1, neg, logits)
        v2 = l2.max(axis=-1, keepdims=True)
        i2 = jnp.where(l2 == v2, lane, float(LANE)).min(axis=-1, keepdims=True)
        e2 = jnp.exp(v2 - v1)
        gate1 = 1.0 / (1.0 + e2)
        gate2 = e2 / (1.0 + e2)
        rinfo_ref[rs, :] = jnp.where(lane == 0, i1, jnp.where(lane == 1, i2, jnp.where(
            lane == 2, gate1, jnp.where(lane == 3, gate2, 0.0))))


def _out_proj_call(l, moe, attn, zg, zp, x, mods_all, pw_b, ps, w_o_b, n2g, rw_p=None):
    n_steps = M_TOK // TMO
    row = lambda w: pl.BlockSpec((TMO, w), lambda i: (i, 0))
    two_x = isinstance(x, tuple)
    x_args, x_specs = _x_operands(x, TMO)
    in_specs = [
        row(ATTN_W), row(GM_W),
        pl.BlockSpec((TMO, POOL_W), lambda i: (jnp.maximum(i - 1, 0), 0)),
        row(POOL_W),
        pl.BlockSpec((TMO, POOL_W), lambda i: (jnp.minimum(i + 1, n_steps - 1), 0)),
    ] + x_specs + [
        _layer_spec(l, (MOD_ROWS, 6 * D)),
        _layer_spec(l, (len(POOL_WINDOWS), LANE, LANE)),
        _layer_spec(l, (1, POOL_W)),
        _layer_spec(l, (D, D)),
        _layer_spec(l, (1, D)),
    ]
    args = [attn, zg, zp, zp, zp, *x_args, mods_all, pw_b, ps, w_o_b, n2g]
    if moe:
        in_specs.append(_layer_spec(l // 2, (D, LANE)))
        args.append(rw_p)
        out_specs = [row(D), row(HALF), row(LANE)]
        out_shape = [jax.ShapeDtypeStruct((M_TOK, D), F32), jax.ShapeDtypeStruct((M_TOK, HALF), U32),
                     jax.ShapeDtypeStruct((M_TOK, LANE), F32)]
    else:
        out_specs = [row(D), row(D)]
        out_shape = [jax.ShapeDtypeStruct((M_TOK, D), F32), jax.ShapeDtypeStruct((M_TOK, D), BF16)]
    return pl.pallas_call(
        functools.partial(_out_proj_kernel, moe, two_x),
        grid=(n_steps,),
        in_specs=in_specs,
        out_specs=out_specs,
        out_shape=out_shape,
        scratch_shapes=[pltpu.VMEM((TMO + 2 * POOL_HALO, POOL_W), F32), pltpu.VMEM((TMO, D), BF16)],
        compiler_params=_cparams(("parallel",)),
        name="out_proj_moe" if moe else "out_proj",
    )(*args)


def _ffn_up_kernel(h_ref, w1_ref, w3_ref, g_ref):
    h = h_ref[...]
    a = jnp.dot(h, w1_ref[...].astype(BF16), preferred_element_type=F32)
    b = jnp.dot(h, w3_ref[...].astype(BF16), preferred_element_type=F32)
    g_ref[...] = (a * jax.nn.sigmoid(a) * b).astype(BF16)


def _ffn_up_call(i, h, w1, w3):
    wspec = pl.BlockSpec((None, D, TF_UP), lambda j, t: (i, 0, j))
    return pl.pallas_call(
        _ffn_up_kernel,
        grid=(D_FF // TF_UP, M_TOK // TE_UP),
        in_specs=[pl.BlockSpec((TE_UP, D), lambda j, t: (t, 0)), wspec, wspec],
        out_specs=pl.BlockSpec((TE_UP, TF_UP), lambda j, t: (t, j)),
        out_shape=jax.ShapeDtypeStruct((M_TOK, D_FF), BF16),
        compiler_params=_cparams(("parallel", "parallel")),
        name="ffn_up",
    )(h, w1, w3)


def _ffn_down_kernel(g_ref, w2_ref, x_ref, mods_ref, o_ref):
    t = pl.program_id(1)
    y = jnp.dot(g_ref[...], w2_ref[...].astype(BF16), preferred_element_type=F32)
    ctx_tiles = M_P // TE_DOWN
    row = jnp.where(t < ctx_tiles, 0, 1 + (t - ctx_tiles) // (DEC_SEQ // TE_DOWN))
    o_ref[...] = x_ref[...] + mods_ref[pl.ds(row, 1), :] * y


def _ffn_down_call(l, g, w2, x, mods_all):
    i = l // 2
    return pl.pallas_call(
        _ffn_down_kernel,
        grid=(D // TN_DOWN, M_TOK // TE_DOWN),
        in_specs=[
            pl.BlockSpec((TE_DOWN, D_FF), lambda j, t: (t, 0)),
            pl.BlockSpec((None, D_FF, TN_DOWN), lambda j, t: (i, 0, j)),
            pl.BlockSpec((TE_DOWN, TN_DOWN), lambda j, t: (t, j)),
            pl.BlockSpec((None, MOD_ROWS, TN_DOWN), lambda j, t: (l, 0, 5 * (D // TN_DOWN) + j)),
        ],
        out_specs=pl.BlockSpec((TE_DOWN, TN_DOWN), lambda j, t: (t, j)),
        out_shape=jax.ShapeDtypeStruct((M_TOK, D), F32),
        compiler_params=_cparams(("parallel", "parallel")),
        name="ffn_down",
    )(g, w2, x, mods_all)


def _src_kernel(cpos_ref, src_ref):
    def body(i, c):
        src_ref[cpos_ref[i]] = lax.shift_right_logical(i, TOP_K_SHIFT)
        return c
    lax.fori_loop(0, N_SLOTS, body, 0, unroll=8)


def _src_call(cpos):
    return pl.pallas_call(
        _src_kernel,
        in_specs=[pl.BlockSpec(memory_space=pltpu.SMEM)],
        out_specs=pl.BlockSpec(memory_space=pltpu.SMEM),
        out_shape=jax.ShapeDtypeStruct((N_SLOTS,), I32),
        name="moe_src",
    )(cpos)


def _for_item_rows(cnt, live, fn):
    half, quarter = MC // 2, MC // 4
    n_full = cnt // MC
    rem = cnt - n_full * MC
    whole = rem > half + quarter
    n_mc = n_full + whole.astype(I32)
    use_half = ~whole & (rem > quarter)
    use_quarter = ~whole & (rem > 0) & ((rem <= quarter) | (rem > half))

    @pl.when(live)
    def _():
        def body(c, carry):
            fn(c * MC, MC)
            return carry
        lax.fori_loop(0, n_mc, body, 0)

    @pl.when(live & use_half)
    def _():
        fn(n_mc * MC, half)

    @pl.when(live & use_quarter)
    def _():
        fn(n_mc * MC + jnp.where(use_half, half, 0), quarter)


def _gather_copy(hp_hbm, tok, stage, slot, q, u, sem):
    return pltpu.make_async_copy(hp_hbm.at[pl.ds(tok, 1)], stage.at[slot, q, pl.ds(u, 1)], sem.at[slot])


def _moe_up_kernel(src_ref, ie_ref, ics_ref, icnt_ref, nit_ref, hp_hbm, w1_ref, w3_ref, g_ref,
                   buf, stage, sem):
    k = pl.program_id(0)
    j = pl.program_id(1)
    nit = nit_ref[0]

    def chunk_rows(item, c):
        return jnp.clip(icnt_ref[item] - c * GC, 0, GC)

    def request_rows(base, n, slot, q):
        for u in range(SUBLANES):
            tok = src_ref[base + jnp.minimum(q * SUBLANES + u, n - 1)]
            _gather_copy(hp_hbm, tok, stage, slot, q, u, sem).start()

    def issue_chunk(item, c, slot):
        n = chunk_rows(item, c)
        base = ics_ref[item] + c * GC

        @pl.when(n > 0)
        def _():
            def body(q, carry):
                request_rows(base, n, slot, q)
                return carry
            lax.fori_loop(0, GC // SUBLANES, body, 0)

    def finish_chunk(item, c, slot, b):
        @pl.when(chunk_rows(item, c) > 0)
        def _():
            def body(r, carry):
                _gather_copy(hp_hbm, 0, stage, slot, 0, 0, sem).wait()
                return carry
            lax.fori_loop(0, GC, body, 0, unroll=8)
            lo, hi = _unpack(stage[slot].reshape(GC, HALF))
            rows = pl.ds(pl.multiple_of(c * GC, GC), GC)
            buf[b, rows, 0:HALF] = lo.astype(BF16)
            buf[b, rows, HALF:D] = hi.astype(BF16)

    @pl.when((k == 0) & (j == 0))
    def _():
        buf[...] = jnp.zeros_like(buf)
        stage[...] = jnp.zeros_like(stage)
        issue_chunk(0, 0, 0)

        def body(c, carry):
            @pl.when(c + 1 < N_GC)
            def _():
                issue_chunk(0, c + 1, (c + 1) & 1)
            finish_chunk(0, c, c & 1, 0)
            return carry
        lax.fori_loop(0, N_GC, body, 0)

    nxt = k + 1

    @pl.when((nxt < nit) & (j >= 1) & (j <= N_GC))
    def _():
        finish_chunk(nxt, j - 1, (j - 1) & 1, nxt & 1)

    @pl.when((nxt < nit) & (j < N_GC))
    def _():
        issue_chunk(nxt, j, j & 1)

    g_ref[...] = jnp.zeros_like(g_ref)
    b = k & 1

    def mm_rows(start, size):
        rows = pl.ds(pl.multiple_of(start, size), size)
        h = buf[b, rows, :]
        a = jnp.dot(h, w1_ref[...].astype(BF16), preferred_element_type=F32)
        bb = jnp.dot(h, w3_ref[...].astype(BF16), preferred_element_type=F32)
        g_ref[rows, :] = (a * jax.nn.sigmoid(a) * bb).astype(BF16)

    _for_item_rows(icnt_ref[k], k < nit, mm_rows)


def _moe_up_call(layer_e0, src, item_e, item_cs, item_cnt, n_items, hp, w1, w3):
    n_f = D_FFE // TF_MOE
    wspec = pl.BlockSpec((None, D, TF_MOE), lambda k, j, s, ie, ics, ic, ni: (
        layer_e0 + ie[k], 0, jnp.where(k < ni[0], j, n_f - 1)))
    return pl.pallas_call(
        _moe_up_kernel,
        grid_spec=pltpu.PrefetchScalarGridSpec(
            num_scalar_prefetch=5,
            grid=(N_ITEMS, D_FFE // TF_MOE),
            in_specs=[pl.BlockSpec(memory_space=pl.ANY), wspec, wspec],
            out_specs=pl.BlockSpec((RT, TF_MOE), lambda k, j, s, ie, ics, ic, ni: (k, j)),
            scratch_shapes=[
                pltpu.VMEM((2, RT, D), BF16),
                pltpu.VMEM((2, GC // SUBLANES, SUBLANES, HALF), U32),
                pltpu.SemaphoreType.DMA((2,)),
            ],
        ),
        out_shape=jax.ShapeDtypeStruct((N_ITEMS * RT, D_FFE), BF16),
        compiler_params=_cparams(("arbitrary", "arbitrary")),
        name="moe_up",
    )(src, item_e, item_cs, item_cnt, n_items, hp, w1, w3)


def _moe_down_kernel(ie_ref, icnt_ref, nit_ref, g_ref, w2a_ref, w2b_ref, o_ref):
    k = pl.program_id(0)
    o_ref[...] = jnp.zeros_like(o_ref)

    def mm_rows(start, size):
        rows = pl.ds(pl.multiple_of(start, size), size)
        g = g_ref[rows, :]
        lo = jnp.dot(g, w2a_ref[...].astype(BF16), preferred_element_type=F32)
        hi = jnp.dot(g, w2b_ref[...].astype(BF16), preferred_element_type=F32)
        o_ref[rows, :] = _pack(lo, hi)

    _for_item_rows(icnt_ref[k], k < nit_ref[0], mm_rows)


def _moe_down_call(layer_e0, item_e, item_cnt, n_items, g, w2):
    live = lambda k, ni: jnp.minimum(k, ni[0] - 1)
    n_half = HALF // TN_MOE
    col = lambda k, n, ni: jnp.where(k < ni[0], n, n_half - 1)
    return pl.pallas_call(
        _moe_down_kernel,
        grid_spec=pltpu.PrefetchScalarGridSpec(
            num_scalar_prefetch=3,
            grid=(N_ITEMS, n_half),
            in_specs=[
                pl.BlockSpec((RT, D_FFE), lambda k, n, ie, ic, ni: (live(k, ni), 0)),
                pl.BlockSpec((None, D_FFE, TN_MOE),
                             lambda k, n, ie, ic, ni: (layer_e0 + ie[k], 0, col(k, n, ni))),
                pl.BlockSpec((None, D_FFE, TN_MOE),
                             lambda k, n, ie, ic, ni: (layer_e0 + ie[k], 0, n_half + col(k, n, ni))),
            ],
            out_specs=pl.BlockSpec((RT, TN_MOE), lambda k, n, ie, ic, ni: (k, n)),
        ),
        out_shape=jax.ShapeDtypeStruct((N_ITEMS * RT, HALF), U32),
        compiler_params=_cparams(("arbitrary", "arbitrary")),
        name="moe_down",
    )(item_e, item_cnt, n_items, g, w2, w2)


def _combine_copy(yo_hbm, row, buf, slot, q, u, sem):
    return pltpu.make_async_copy(yo_hbm.at[pl.ds(row, 1)], buf.at[slot, q, pl.ds(u, 1)], sem.at[slot])


def _combine_kernel(split, pos_ref, yo_hbm, x_ref, rinfo_ref, mods_ref, *rest):
    if split:
        yp_ref, ys_ref, buf, sem = rest
    else:
        o_ref, buf, sem = rest
    i = pl.program_id(0)
    n = pl.num_programs(0)
    rows = TOP_K * TM

    def issue(tile, slot):
        def body(q, c):
            for u in range(SUBLANES):
                _combine_copy(yo_hbm, pos_ref[tile * rows + q * SUBLANES + u], buf, slot, q, u, sem).start()
            return c
        lax.fori_loop(0, rows // SUBLANES, body, 0)

    @pl.when(i == 0)
    def _():
        issue(0, 0)

    @pl.when(i + 1 < n)
    def _():
        issue(i + 1, (i + 1) & 1)

    slot = i & 1

    def body(r, c):
        _combine_copy(yo_hbm, 0, buf, slot, 0, 0, sem).wait()
        return c
    lax.fori_loop(0, rows, body, 0, unroll=8)
    rinfo = rinfo_ref[...]
    tiles = TM // SUBLANES
    lo0, hi0 = _unpack(buf[slot, 0:tiles].reshape(TM, HALF))
    lo1, hi1 = _unpack(buf[slot, tiles:2 * tiles].reshape(TM, HALF))
    gt0, gt1 = rinfo[:, 2:3], rinfo[:, 3:4]
    g2 = mods_ref[pl.ds(_mod_row(i), 1), :]
    y_lo = x_ref[:, 0:HALF] + g2[:, 0:HALF] * (gt0 * lo0 + gt1 * lo1)
    y_hi = x_ref[:, HALF:D] + g2[:, HALF:D] * (gt0 * hi0 + gt1 * hi1)

    def write(ref):
        ref[:, 0:HALF] = y_lo
        ref[:, HALF:D] = y_hi

    if not split:
        write(o_ref)
        return

    @pl.when(i < P_TILES)
    def _():
        write(yp_ref)

    @pl.when(i >= P_TILES)
    def _():
        write(ys_ref)


def _combine_call(l, split, pos, yo, x, rinfo, mods_all):
    if split:
        out_specs = [pl.BlockSpec((TM, D), lambda i, p: (jnp.minimum(i, P_TILES - 1), 0)),
                     pl.BlockSpec((TM, D), lambda i, p: (jnp.maximum(i - P_TILES, 0), 0))]
        out_shape = [jax.ShapeDtypeStruct((M_P, D), F32), jax.ShapeDtypeStruct((M_S, D), F32)]
    else:
        out_specs = pl.BlockSpec((TM, D), lambda i, p: (i, 0))
        out_shape = jax.ShapeDtypeStruct((M_TOK, D), F32)
    return pl.pallas_call(
        functools.partial(_combine_kernel, split),
        grid_spec=pltpu.PrefetchScalarGridSpec(
            num_scalar_prefetch=1,
            grid=(N_TILES,),
            in_specs=[
                pl.BlockSpec(memory_space=pl.ANY),
                pl.BlockSpec((TM, D), lambda i, p: (i, 0)),
                pl.BlockSpec((TM, LANE), lambda i, p: (i, 0)),
                pl.BlockSpec((None, MOD_ROWS, D), lambda i, p: (l, 0, 5)),
            ],
            out_specs=out_specs,
            scratch_shapes=[pltpu.VMEM((2, TOP_K * TM // SUBLANES, SUBLANES, HALF), U32),
                            pltpu.SemaphoreType.DMA((2,))],
        ),
        out_shape=out_shape,
        compiler_params=_cparams(("arbitrary",)),
        name="moe_combine",
    )(pos, yo, x, rinfo, mods_all)


def _route_plan(rinfo):
    flat_e = rinfo[:, 0:TOP_K].astype(I32).reshape(-1)
    onehot = (flat_e[:, None] == jnp.arange(N_EXP, dtype=I32)[None, :]).astype(I32)
    csum = jnp.cumsum(onehot, axis=0)
    pick = lambda table: jnp.sum(onehot * table[None, :], axis=1)
    rank = jnp.sum(onehot * csum, axis=1) - 1
    counts = csum[-1]
    cstart = jnp.cumsum(counts) - counts
    n_it = (counts + RT - 1) // RT
    it_end = jnp.cumsum(n_it)
    it_base = it_end - n_it
    n_items = it_end[-1]
    cpos = pick(cstart) + rank
    ppos = (pick(it_base) + rank // RT) * RT + rank % RT
    q = jnp.arange(N_ITEMS, dtype=I32)
    qc = jnp.minimum(q, n_items - 1)
    qe = jnp.minimum(jnp.sum((it_end[None, :] <= qc[:, None]).astype(I32), axis=1), N_EXP - 1)
    qhot = (qe[:, None] == jnp.arange(N_EXP, dtype=I32)[None, :]).astype(I32)
    qpick = lambda table: jnp.sum(qhot * table[None, :], axis=1)
    s = qc - qpick(it_base)
    item_cnt = jnp.where(q < n_items, jnp.clip(qpick(counts) - s * RT, 0, RT), 0)
    item_cs = qpick(cstart) + s * RT
    pos_tiles = ppos.reshape(N_TILES, TM, TOP_K).transpose(0, 2, 1).reshape(-1)
    return (cpos.astype(I32), qe.astype(I32), item_cs.astype(I32), item_cnt.astype(I32),
            n_items.reshape(1).astype(I32), pos_tiles.astype(I32))


def _rope_tables():
    rows = DEC_SEQ // GRID_W
    row = jnp.repeat(jnp.arange(rows), GRID_W).astype(F32)
    col = jnp.tile(jnp.arange(GRID_W), rows).astype(F32)
    nf = ROPE // 4
    inv = ROPE_BASE ** (-jnp.arange(nf, dtype=F32) / nf)
    ar = row[:, None] * inv[None, :]
    ac = col[:, None] * inv[None, :]
    ang = jnp.concatenate([ar, ar, ac, ac], axis=-1)
    cos, sin = jnp.cos(ang), jnp.sin(ang)
    first = (jnp.arange(ROPE) % 32) < 16
    pad = lambda a, v: jnp.pad(a, ((0, 0), (0, LANE - ROPE)), constant_values=v)
    cos_p = pad(cos, 1.0)
    sin_a = pad(jnp.where(first[None, :], -sin, 0.0), 0.0)
    sin_b = pad(jnp.where(first[None, :], 0.0, sin), 0.0)
    ident = jnp.stack([jnp.ones((TMI, LANE), F32), jnp.zeros((TMI, LANE), F32), jnp.zeros((TMI, LANE), F32)])
    return jnp.concatenate([ident, jnp.stack([cos_p, sin_a, sin_b])], axis=1)


def kernel(x_prompt, x_sample, cache_ckv, cache_krope, c, c_ctx, w_ada, b_ada, norm1_g, norm2_g, w_in,
           qlat_norm_g, kvlat_norm_g, w_uq, w_ukv, q_head_norm_g, k_head_norm_g, gm_norm_g, gm_w_s,
           gm_b_s, pool_w, pool_scale, w_o, ffn_w1, ffn_w3, ffn_w2, router_w, moe_w1, moe_w3, moe_w2):
    pad_w = IN_PAD - w_in.shape[-1]
    col = jnp.arange(IN_PAD)[None, None, :]
    w_in_p = jnp.where(
        col < C_KR + ROPE, jnp.pad(w_in, ((0, 0), (0, 0), (0, pad_w))),
        jnp.where(col < C_KR + LANE, 0.0, jnp.pad(w_in, ((0, 0), (0, 0), (pad_w, 0))))).astype(BF16)
    w_uq_p = jnp.pad(w_uq.reshape(DEPTH, Q_RANK, N_HEADS, QK_DIM),
                     ((0, 0), (0, 0), (0, 0), (0, HEAD_PAD - QK_DIM))
                     ).reshape(DEPTH, Q_RANK, QK_W).astype(BF16)
    w_ukv_b = w_ukv.astype(BF16)
    w_o_b = w_o.astype(BF16)
    qg_p = jnp.pad(q_head_norm_g * (1.0 / math.sqrt(QK_DIM)),
                   ((0, 0), (0, HEAD_PAD - QK_DIM))).reshape(DEPTH, 1, HEAD_PAD)
    kg_n = k_head_norm_g[:, :NOPE].reshape(DEPTH, 1, LANE)
    kg_r = jnp.pad(k_head_norm_g[:, NOPE:], ((0, 0), (0, LANE - ROPE))).reshape(DEPTH, 1, LANE)
    gmw_b = gm_w_s.astype(BF16)
    gmb = jnp.broadcast_to(gm_b_s[:, :, :, None], (DEPTH, GM_GROUPS, CHUNK, LANE))
    pw_b = pool_w.astype(BF16)
    ps = pool_scale.reshape(DEPTH, 1, POOL_W)
    n1g = norm1_g.reshape(DEPTH, 1, D)
    n2g = norm2_g.reshape(DEPTH, 1, D)
    qlg = qlat_norm_g.reshape(DEPTH, 1, Q_RANK)
    kvlg = kvlat_norm_g.reshape(DEPTH, 1, KV_RANK)
    rw_p = jnp.pad(router_w, ((0, 0), (0, 0), (0, LANE - N_EXP)))
    n_moe = moe_w1.shape[0]
    moe_w1_flat = moe_w1.reshape(n_moe * N_EXP, D, D_FFE)
    moe_w3_flat = moe_w3.reshape(n_moe * N_EXP, D, D_FFE)
    moe_w2_flat = moe_w2.reshape(n_moe * N_EXP, D_FFE, D)
    rope_tab = _rope_tables()
    cache_kr_p = jnp.pad(cache_krope, ((0, 0), (0, 0), (0, 0), (0, LANE - ROPE)))

    cond = jnp.zeros((MOD_ROWS, D), F32).at[0].set(c_ctx).at[1:1 + DEC_BATCH].set(c)
    mods_all = _ada_call(cond, w_ada, b_ada)
    k_c, v_c = _cache_kv_call(cache_ckv, cache_kr_p, w_ukv_b, kg_n, kg_r)

    x = (x_prompt.reshape(M_P, D), x_sample.reshape(M_S, D))
    ckv_out, kr_out = [], []
    for l in range(DEPTH):
        q, k, v, ckv, kr, zg, zp = _in_proj_call(
            l, x, mods_all, n1g, w_in_p, qlg, kvlg, w_uq_p, qg_p, w_ukv_b, kg_n, kg_r, rope_tab,
            gm_norm_g, gmw_b, gmb)
        ckv_out.append(ckv[:M_P].reshape(BATCH, SEQ, KV_RANK))
        kr_out.append(kr[:M_P, :ROPE].reshape(BATCH, SEQ, ROPE))
        attn = _attn_call(l, q, k, v, k_c, v_c)
        i = l // 2
        if l % 2 == 0:
            x, h2b = _out_proj_call(l, False, attn, zg, zp, x, mods_all, pw_b, ps, w_o_b, n2g)
            g = _ffn_up_call(i, h2b, ffn_w1, ffn_w3)
            x = _ffn_down_call(l, g, ffn_w2, x, mods_all)
        else:
            x, hp, rinfo = _out_proj_call(l, True, attn, zg, zp, x, mods_all, pw_b, ps, w_o_b, n2g, rw_p)
            cpos, item_e, item_cs, item_cnt, n_items, pos_tiles = _route_plan(rinfo)
            src = _src_call(cpos)
            g = _moe_up_call(i * N_EXP, src, item_e, item_cs, item_cnt, n_items, hp,
                             moe_w1_flat, moe_w3_flat)
            yo = _moe_down_call(i * N_EXP, item_e, item_cnt, n_items, g, moe_w2_flat)
            x = _combine_call(l, l == DEPTH - 1, pos_tiles, yo, x, rinfo, mods_all)

    if DEPTH % 2 == 0:
        y_p, y_s = x
    else:
        y_p, y_s = x[:M_P], x[M_P:]
    return (y_p.reshape(BATCH, SEQ, D), y_s.reshape(DEC_BATCH, DEC_SEQ, D),
            jnp.stack(ckv_out, axis=1), jnp.stack(kr_out, axis=1))
```

```python
import functools
import math

import jax
import jax.numpy as jnp
from jax import lax
from jax.experimental import pallas as pl
from jax.experimental.pallas import tpu as pltpu

F32 = jnp.float32
BF16 = jnp.bfloat16
U32 = jnp.uint32
I32 = jnp.int32

D = 2048
BATCH, SEQ = 16, 256
DEC_BATCH, DEC_SEQ = 2, 1024
PAST = 256
DEPTH = 4
GRID_W = 64
EPS = 1e-6
N_HEADS = 8
NOPE, ROPE, VDIM = 128, 64, 128
QK_DIM = NOPE + ROPE
HEAD_PAD = 256
QK_W = N_HEADS * HEAD_PAD
Q_RANK, KV_RANK = 512, 256
ROPE_BASE = 10000.0
ATTN_W = N_HEADS * VDIM
CHUNK, GM_GROUPS, GM_W = 128, 4, 512
POOL_WINDOWS = (2, 4, 8, 16)
POOL_W = 512
POOL_HALO = 8
D_FF, N_EXP, D_FFE, TOP_K = 5632, 8, 2816, 2
TOP_K_SHIFT = 1

C_Q, C_KV, C_KR, C_GM, C_PL = 0, 512, 768, 896, 1920
IN_PAD = 2432

LANE = 128
SUBLANES = 8
TM = 256
TMI = 512
TMO = 256
M_P, M_S = BATCH * SEQ, DEC_BATCH * DEC_SEQ
M_TOK = M_P + M_S
N_TILES = M_TOK // TM
P_TILES = M_P // TM
S_TILES = DEC_SEQ // TM
MOD_ROWS = 16
ADA_TN = 1024
TE_UP, TF_UP = 1024, 512
TE_DOWN, TN_DOWN = 512, 512
HALF = D // 2
N_SLOTS = TOP_K * M_TOK
RT = 2048
N_ITEMS = N_EXP + N_SLOTS // RT
GC = 256
N_GC = RT // GC
MC = 1024
TF_MOE = 256
TN_MOE = 256
VMEM_LIMIT = 56 * 1024 * 1024


def _cparams(sem):
    return pltpu.CompilerParams(dimension_semantics=sem, vmem_limit_bytes=VMEM_LIMIT)


def _const_spec(shape):
    nd = len(shape)
    return pl.BlockSpec(shape, lambda *_: (0,) * nd)


def _layer_spec(l, shape):
    nd = len(shape)
    return pl.BlockSpec((None,) + tuple(shape), lambda *_: (l,) + (0,) * nd)


def _x_operands(x, tile):
    if not isinstance(x, tuple):
        return [x], [pl.BlockSpec((tile, D), lambda i: (i, 0))]
    ctx = M_P // tile
    return list(x), [pl.BlockSpec((tile, D), lambda i: (jnp.minimum(i, ctx - 1), 0)),
                     pl.BlockSpec((tile, D), lambda i: (jnp.maximum(i - ctx, 0), 0))]


def _mod_row(i):
    return jnp.where(i < P_TILES, 0, 1 + (i - P_TILES) // S_TILES)


def _rms(x, g, n):
    ms = jnp.sum(x * x, axis=-1, keepdims=True) * (1.0 / n)
    return x * lax.rsqrt(ms + EPS) * g


def _rope(r, cos, sin_a, sin_b):
    return r * cos + pltpu.roll(r, LANE - 16, 1) * sin_a + pltpu.roll(r, 16, 1) * sin_b


def _pack(lo, hi):
    lo_b = lax.bitcast_convert_type(lo.astype(BF16).astype(F32), U32)
    hi_b = lax.bitcast_convert_type(hi.astype(BF16).astype(F32), U32)
    return (lo_b >> 16) | hi_b


def _unpack(w):
    lo = lax.bitcast_convert_type(w << 16, F32)
    hi = lax.bitcast_convert_type(w & jnp.uint32(0xFFFF0000), F32)
    return lo, hi


def _ada_kernel(cond_ref, w_ref, b_ref, o_ref):
    c = cond_ref[...]
    s = c * jax.nn.sigmoid(c)
    s_hi = s.astype(BF16)
    s_lo = (s - s_hi.astype(F32)).astype(BF16)
    w = w_ref[...]
    w_hi = w.astype(BF16)
    w_lo = (w - w_hi.astype(F32)).astype(BF16)
    lhs = jnp.concatenate([s_hi, s_lo], axis=0)
    r1 = jnp.dot(lhs, w_hi, preferred_element_type=F32)
    r2 = jnp.dot(s_hi, w_lo, preferred_element_type=F32)
    o_ref[...] = r1[:MOD_ROWS] + r1[MOD_ROWS:] + r2 + b_ref[...]


def _ada_call(cond, w_ada, b_ada):
    n = 6 * D
    return pl.pallas_call(
        _ada_kernel,
        grid=(DEPTH, n // ADA_TN),
        in_specs=[
            pl.BlockSpec((MOD_ROWS, D), lambda l, j: (0, 0)),
            pl.BlockSpec((None, D, ADA_TN), lambda l, j: (l, 0, j)),
            pl.BlockSpec((None, 1, ADA_TN), lambda l, j: (l, 0, j)),
        ],
        out_specs=pl.BlockSpec((None, MOD_ROWS, ADA_TN), lambda l, j: (l, 0, j)),
        out_shape=jax.ShapeDtypeStruct((DEPTH, MOD_ROWS, n), F32),
        compiler_params=_cparams(("parallel", "parallel")),
        name="ada",
    )(cond, w_ada, b_ada.reshape(DEPTH, 1, n))


def _write_kv(ckv_b, kr, w_ukv_ref, kg_n, kg_r, rope, k_ref, v_ref, rows=slice(None)):
    kv = jnp.dot(ckv_b, w_ukv_ref[...], preferred_element_type=F32)
    ssr = jnp.sum(kr * kr, axis=-1, keepdims=True)
    shared = kr * kg_r
    if rope is not None:
        shared = _rope(shared, *rope)
    for h in range(N_HEADS):
        kn = kv[:, h * HEAD_PAD:h * HEAD_PAD + NOPE]
        vv = kv[:, h * HEAD_PAD + NOPE:(h + 1) * HEAD_PAD]
        ms = (jnp.sum(kn * kn, axis=-1, keepdims=True) + ssr) * (1.0 / QK_DIM)
        inv = lax.rsqrt(ms + EPS)
        k_ref[rows, h * HEAD_PAD:h * HEAD_PAD + NOPE] = (kn * inv * kg_n).astype(BF16)
        k_ref[rows, h * HEAD_PAD + NOPE:(h + 1) * HEAD_PAD] = (shared * inv).astype(BF16)
        v_ref[rows, h * VDIM:(h + 1) * VDIM] = vv.astype(BF16)


def _cache_kv_kernel(ckv_ref, kr_ref, w_ukv_ref, kgn_ref, kgr_ref, k_ref, v_ref):
    _write_kv(ckv_ref[...].astype(BF16), kr_ref[...], w_ukv_ref, kgn_ref[...], kgr_ref[...],
              None, k_ref, v_ref)


def _cache_kv_call(cache_ckv, cache_kr_p, w_ukv_b, kg_n, kg_r):
    return pl.pallas_call(
        _cache_kv_kernel,
        grid=(DEPTH, DEC_BATCH),
        in_specs=[
            pl.BlockSpec((None, None, PAST, KV_RANK), lambda l, b: (b, l, 0, 0)),
            pl.BlockSpec((None, None, PAST, LANE), lambda l, b: (b, l, 0, 0)),
            pl.BlockSpec((None, KV_RANK, QK_W), lambda l, b: (l, 0, 0)),
            pl.BlockSpec((None, 1, LANE), lambda l, b: (l, 0, 0)),
            pl.BlockSpec((None, 1, LANE), lambda l, b: (l, 0, 0)),
        ],
        out_specs=[
            pl.BlockSpec((None, None, PAST, QK_W), lambda l, b: (l, b, 0, 0)),
            pl.BlockSpec((None, None, PAST, ATTN_W), lambda l, b: (l, b, 0, 0)),
        ],
        out_shape=[
            jax.ShapeDtypeStruct((DEPTH, DEC_BATCH, PAST, QK_W), BF16),
            jax.ShapeDtypeStruct((DEPTH, DEC_BATCH, PAST, ATTN_W), BF16),
        ],
        compiler_params=_cparams(("parallel", "parallel")),
        name="cache_kv",
    )(cache_ckv, cache_kr_p, w_ukv_b, kg_n, kg_r)


def _in_proj_kernel(two_x, *refs):
    x_refs, refs = refs[:1 + two_x], refs[1 + two_x:]
    _in_proj_body(x_refs, *refs)


def _in_proj_body(x_refs, mods_ref, n1g_ref, w_in_ref, qlg_ref, kvlg_ref, w_uq_ref, qg_ref,
                    w_ukv_ref, kgn_ref, kgr_ref, rope_ref, gmg_ref, gmw_ref, gmb_ref,
                    q_ref, k_ref, v_ref, ckv_ref, kr_ref, zg_ref, zp_ref):
    i = pl.program_id(0)
    row = jnp.where(i < M_P // TMI, 0, 1 + (i - M_P // TMI) // (DEC_SEQ // TMI))
    m = mods_ref[pl.ds(row, 1), :]
    sh1, sc1 = m[:, 0:D], m[:, D:2 * D]
    for sub in range(TMI // TM):
        rs = slice(sub * TM, (sub + 1) * TM)
        x = x_refs[0][rs, :] if len(x_refs) == 1 else jnp.where(i < M_P // TMI, x_refs[0][rs, :], x_refs[1][rs, :])
        h = _rms(x, n1g_ref[...], D) * (1.0 + sc1) + sh1
        z = jnp.dot(h.astype(BF16), w_in_ref[...], preferred_element_type=F32)
        rope = (rope_ref[0, rs, :], rope_ref[1, rs, :], rope_ref[2, rs, :])

        ql = _rms(z[:, C_Q:C_Q + Q_RANK], qlg_ref[...], Q_RANK)
        qf = jnp.dot(ql.astype(BF16), w_uq_ref[...], preferred_element_type=F32)
        qg = qg_ref[...]
        for hd in range(N_HEADS):
            seg = qf[:, hd * HEAD_PAD:(hd + 1) * HEAD_PAD]
            seg = _rms(seg, qg, QK_DIM)
            q_ref[rs, hd * HEAD_PAD:hd * HEAD_PAD + NOPE] = seg[:, :NOPE].astype(BF16)
            q_ref[rs, hd * HEAD_PAD + NOPE:(hd + 1) * HEAD_PAD] = _rope(seg[:, NOPE:], *rope).astype(BF16)

        ckv = _rms(z[:, C_KV:C_KV + KV_RANK], kvlg_ref[...], KV_RANK)
        kr = z[:, C_KR:C_KR + LANE]
        ckv_ref[rs, :] = ckv
        kr_ref[rs, :] = kr
        _write_kv(ckv.astype(BF16), kr, w_ukv_ref, kgn_ref[...], kgr_ref[...], rope, k_ref, v_ref, rs)

        zg = jax.nn.gelu(z[:, C_GM:C_GM + 2 * GM_W], approximate=True)
        for g in range(GM_GROUPS):
            u = zg[:, g * LANE:(g + 1) * LANE]
            vg = _rms(zg[:, GM_W + g * LANE:GM_W + (g + 1) * LANE], gmg_ref[g:g + 1, :], LANE)
            vb = vg.astype(BF16)
            for c in range(TM // CHUNK):
                sv = jnp.dot(gmw_ref[g], vb[c * CHUNK:(c + 1) * CHUNK, :], preferred_element_type=F32)
                sv = sv + gmb_ref[g]
                r0 = sub * TM + c * CHUNK
                zg_ref[r0:r0 + CHUNK, g * LANE:(g + 1) * LANE] = (
                    u[c * CHUNK:(c + 1) * CHUNK, :] * sv).astype(BF16)

        zp_ref[rs, :] = z[:, C_PL:C_PL + POOL_W]


def _in_proj_call(l, x, mods_all, n1g, w_in_b, qlg, kvlg, w_uq_b, qg, w_ukv_b, kg_n, kg_r, rope_tab,
                  gmg, gmw_b, gmb):
    def rope_idx(i):
        return (0, jnp.where(i < M_P // TMI, 0, 1 + (i - M_P // TMI) % (DEC_SEQ // TMI)), 0)

    row = lambda w: pl.BlockSpec((TMI, w), lambda i: (i, 0))
    two_x = isinstance(x, tuple)
    x_args, x_specs = _x_operands(x, TMI)
    return pl.pallas_call(
        functools.partial(_in_proj_kernel, two_x),
        grid=(M_TOK // TMI,),
        in_specs=x_specs + [
            _layer_spec(l, (MOD_ROWS, 6 * D)),
            _layer_spec(l, (1, D)),
            _layer_spec(l, (D, IN_PAD)),
            _layer_spec(l, (1, Q_RANK)),
            _layer_spec(l, (1, KV_RANK)),
            _layer_spec(l, (Q_RANK, QK_W)),
            _layer_spec(l, (1, HEAD_PAD)),
            _layer_spec(l, (KV_RANK, QK_W)),
            _layer_spec(l, (1, LANE)),
            _layer_spec(l, (1, LANE)),
            pl.BlockSpec((3, TMI, LANE), rope_idx),
            _layer_spec(l, (GM_GROUPS, LANE)),
            _layer_spec(l, (GM_GROUPS, CHUNK, CHUNK)),
            _layer_spec(l, (GM_GROUPS, CHUNK, LANE)),
        ],
        out_specs=[row(QK_W), row(QK_W), row(ATTN_W), row(KV_RANK), row(LANE), row(GM_W), row(POOL_W)],
        out_shape=[
            jax.ShapeDtypeStruct((M_TOK, QK_W), BF16),
            jax.ShapeDtypeStruct((M_TOK, QK_W), BF16),
            jax.ShapeDtypeStruct((M_TOK, ATTN_W), BF16),
            jax.ShapeDtypeStruct((M_TOK, KV_RANK), F32),
            jax.ShapeDtypeStruct((M_TOK, LANE), F32),
            jax.ShapeDtypeStruct((M_TOK, GM_W), BF16),
            jax.ShapeDtypeStruct((M_TOK, POOL_W), F32),
        ],
        compiler_params=_cparams(("parallel",)),
        name="in_proj",
    )(*x_args, mods_all, n1g, w_in_b, qlg, kvlg, w_uq_b, qg, w_ukv_b, kg_n, kg_r, rope_tab, gmg, gmw_b, gmb)


def _attend_heads(q_ref, kv_pairs, o_ref):
    nt = (((1,), (1,)), ((), ()))
    for h in range(N_HEADS):
        qh = q_ref[:, h * HEAD_PAD:(h + 1) * HEAD_PAD]
        ss = [lax.dot_general(qh, k_ref[:, h * HEAD_PAD:(h + 1) * HEAD_PAD], nt,
                              preferred_element_type=F32) for k_ref, _ in kv_pairs]
        mx = ss[0].max(axis=-1, keepdims=True)
        for s in ss[1:]:
            mx = jnp.maximum(mx, s.max(axis=-1, keepdims=True))
        den = None
        acc = None
        for s, (_, v_ref) in zip(ss, kv_pairs):
            p = jnp.exp(s - mx)
            d = jnp.sum(p, axis=-1, keepdims=True)
            a = jnp.dot(p.astype(BF16), v_ref[:, h * VDIM:(h + 1) * VDIM], preferred_element_type=F32)
            den = d if den is None else den + d
            acc = a if acc is None else acc + a
        o_ref[:, h * VDIM:(h + 1) * VDIM] = (acc / den).astype(BF16)


def _attn_kernel(q_ref, kp_ref, vp_ref, ks_ref, vs_ref, kc_ref, vc_ref, o_ref):
    i = pl.program_id(0)

    @pl.when(i < P_TILES)
    def _():
        _attend_heads(q_ref, [(kp_ref, vp_ref)], o_ref)

    @pl.when(i >= P_TILES)
    def _():
        _attend_heads(q_ref, [(ks_ref, vs_ref), (kc_ref, vc_ref)], o_ref)


def _attn_call(l, q, k, v, k_c, v_c):
    def seq_b(i):
        return jnp.maximum(i - P_TILES, 0) // S_TILES

    small = lambda w: pl.BlockSpec((TM, w), lambda i: (jnp.minimum(i, P_TILES - 1), 0))
    big = lambda w: pl.BlockSpec((DEC_SEQ, w), lambda i: (M_P // DEC_SEQ + seq_b(i), 0))
    cache = lambda w: pl.BlockSpec((None, None, PAST, w), lambda i: (l, seq_b(i), 0, 0))
    return pl.pallas_call(
        _attn_kernel,
        grid=(N_TILES,),
        in_specs=[
            pl.BlockSpec((TM, QK_W), lambda i: (i, 0)),
            small(QK_W), small(ATTN_W),
            big(QK_W), big(ATTN_W),
            cache(QK_W), cache(ATTN_W),
        ],
        out_specs=pl.BlockSpec((TM, ATTN_W), lambda i: (i, 0)),
        out_shape=jax.ShapeDtypeStruct((M_TOK, ATTN_W), BF16),
        compiler_params=_cparams(("parallel",)),
        name="attention",
    )(q, k, v, k, v, k_c, v_c)


def _out_proj_kernel(moe, two_x, attn_ref, zg_ref, zprev_ref, zcur_ref, znext_ref, *rest):
    x_refs, rest = rest[:1 + two_x], rest[1 + two_x:]
    mods_ref, pw_ref, ps_ref, w_o_ref, n2g_ref = rest[:5]
    rest = rest[5:]
    if moe:
        rw_ref, xo_ref, hp_ref, rinfo_ref, ext_ref, mix_ref = rest
    else:
        xo_ref, h2b_ref, ext_ref, mix_ref = rest
    i = pl.program_id(0)
    ctx_steps = M_P // TMO
    is_ctx = i < ctx_steps
    seq_len = jnp.where(is_ctx, SEQ, DEC_SEQ)
    m = mods_ref[pl.ds(jnp.where(is_ctx, 0, 1 + (i - ctx_steps) // (DEC_SEQ // TMO)), 1), :]
    g1, sh2, sc2 = m[:, 2 * D:3 * D], m[:, 3 * D:4 * D], m[:, 4 * D:5 * D]

    pos0 = jnp.where(is_ctx, 0, ((i - ctx_steps) % (DEC_SEQ // TMO)) * TMO)
    zero_halo = jnp.zeros((POOL_HALO, POOL_W), F32)
    ext_ref[0:POOL_HALO, :] = jnp.where(pos0 == 0, zero_halo, zprev_ref[TMO - POOL_HALO:, :])
    ext_ref[POOL_HALO:POOL_HALO + TMO, :] = zcur_ref[...]
    ext_ref[POOL_HALO + TMO:, :] = jnp.where(pos0 + TMO == seq_len, zero_halo, znext_ref[0:POOL_HALO, :])
    mix_ref[:, 0:ATTN_W] = attn_ref[...]
    mix_ref[:, ATTN_W:ATTN_W + GM_W] = zg_ref[...]

    assert TMO == TM and SEQ % TMO == 0 and DEC_SEQ % TMO == 0
    for sub in range(TMO // TM):
        rs = slice(sub * TM, (sub + 1) * TM)
        pos = pos0 + lax.broadcasted_iota(I32, (TM, 1), 0)
        for gi, w in enumerate(POOL_WINDOWS):
            cols = slice(gi * LANE, (gi + 1) * LANE)
            win = None
            for dlt in range(-(w // 2), w - w // 2):
                r0 = POOL_HALO + dlt
                part = ext_ref[r0:r0 + TM, cols]
                win = part if win is None else win + part
            lo = jnp.maximum(pos - w // 2, 0)
            hi = jnp.minimum(pos + (w - 1 - w // 2), seq_len - 1)
            cnt = (hi - lo + 1).astype(F32)
            pooled = win / cnt - zcur_ref[rs, cols]
            mixed = jnp.dot(pooled.astype(BF16), pw_ref[gi], preferred_element_type=F32)
            mix_ref[rs, ATTN_W + GM_W + gi * LANE:ATTN_W + GM_W + (gi + 1) * LANE] = (
                mixed * ps_ref[:, cols]).astype(BF16)

        out = jnp.dot(mix_ref[rs, :], w_o_ref[...], preferred_element_type=F32)
        x = x_refs[0][rs, :] if len(x_refs) == 1 else jnp.where(is_ctx, x_refs[0][rs, :], x_refs[1][rs, :])
        xn = x + g1 * out
        xo_ref[rs, :] = xn
        h2 = _rms(xn, n2g_ref[...], D) * (1.0 + sc2) + sh2

        if not moe:
            h2b_ref[rs, :] = h2.astype(BF16)
            continue

        hp_ref[rs, :] = _pack(h2[:, :HALF], h2[:, HALF:])
        h_hi = h2.astype(BF16)
        h_lo = (h2 - h_hi.astype(F32)).astype(BF16)
        rw = rw_ref[...]
        w_hi = rw.astype(BF16)
        w_lo = (rw - w_hi.astype(F32)).astype(BF16)
        logits = (jnp.dot(h_hi, w_hi, preferred_element_type=F32)
                  + jnp.dot(h_lo, w_hi, preferred_element_type=F32)
                  + jnp.dot(h_hi, w_lo, preferred_element_type=F32))
        lane = lax.broadcasted_iota(I32, (TM, LANE), 1).astype(F32)
        neg = jnp.float32(-jnp.inf)
        logits = jnp.where(lane < N_EXP, logits, neg)
        v1 = logits.max(axis=-1, keepdims=True)
        i1 = jnp.where(logits == v1, lane, float(LANE)).min(axis=-1, keepdims=True)
        l2 = jnp.where(lane == i1, neg, logits)
        v2 = l2.max(axis=-1, keepdims=True)
        i2 = jnp.where(l2 == v2, lane, float(LANE)).min(axis=-1, keepdims=True)
        e2 = jnp.exp(v2 - v1)
        gate1 = 1.0 / (1.0 + e2)
        gate2 = e2 / (1.0 + e2)
        rinfo_ref[rs, :] = jnp.where(lane == 0, i1, jnp.where(lane == 1, i2, jnp.where(
            lane == 2, gate1, jnp.where(lane == 3, gate2, 0.0))))


def _out_proj_call(l, moe, attn, zg, zp, x, mods_all, pw_b, ps, w_o_b, n2g, rw_p=None):
    n_steps = M_TOK // TMO
    row = lambda w: pl.BlockSpec((TMO, w), lambda i: (i, 0))
    two_x = isinstance(x, tuple)
    x_args, x_specs = _x_operands(x, TMO)
    in_specs = [
        row(ATTN_W), row(GM_W),
        pl.BlockSpec((TMO, POOL_W), lambda i: (jnp.maximum(i - 1, 0), 0)),
        row(POOL_W),
        pl.BlockSpec((TMO, POOL_W), lambda i: (jnp.minimum(i + 1, n_steps - 1), 0)),
    ] + x_specs + [
        _layer_spec(l, (MOD_ROWS, 6 * D)),
        _layer_spec(l, (len(POOL_WINDOWS), LANE, LANE)),
        _layer_spec(l, (1, POOL_W)),
        _layer_spec(l, (D, D)),
        _layer_spec(l, (1, D)),
    ]
    args = [attn, zg, zp, zp, zp, *x_args, mods_all, pw_b, ps, w_o_b, n2g]
    if moe:
        in_specs.append(_layer_spec(l // 2, (D, LANE)))
        args.append(rw_p)
        out_specs = [row(D), row(HALF), row(LANE)]
        out_shape = [jax.ShapeDtypeStruct((M_TOK, D), F32), jax.ShapeDtypeStruct((M_TOK, HALF), U32),
                     jax.ShapeDtypeStruct((M_TOK, LANE), F32)]
    else:
        out_specs = [row(D), row(D)]
        out_shape = [jax.ShapeDtypeStruct((M_TOK, D), F32), jax.ShapeDtypeStruct((M_TOK, D), BF16)]
    return pl.pallas_call(
        functools.partial(_out_proj_kernel, moe, two_x),
        grid=(n_steps,),
        in_specs=in_specs,
        out_specs=out_specs,
        out_shape=out_shape,
        scratch_shapes=[pltpu.VMEM((TMO + 2 * POOL_HALO, POOL_W), F32), pltpu.VMEM((TMO, D), BF16)],
        compiler_params=_cparams(("parallel",)),
        name="out_proj_moe" if moe else "out_proj",
    )(*args)


def _ffn_up_kernel(h_ref, w1_ref, w3_ref, g_ref):
    h = h_ref[...]
    a = jnp.dot(h, w1_ref[...].astype(BF16), preferred_element_type=F32)
    b = jnp.dot(h, w3_ref[...].astype(BF16), preferred_element_type=F32)
    g_ref[...] = (a * jax.nn.sigmoid(a) * b).astype(BF16)


def _ffn_up_call(i, h, w1, w3):
    wspec = pl.BlockSpec((None, D, TF_UP), lambda j, t: (i, 0, j))
    return pl.pallas_call(
        _ffn_up_kernel,
        grid=(D_FF // TF_UP, M_TOK // TE_UP),
        in_specs=[pl.BlockSpec((TE_UP, D), lambda j, t: (t, 0)), wspec, wspec],
        out_specs=pl.BlockSpec((TE_UP, TF_UP), lambda j, t: (t, j)),
        out_shape=jax.ShapeDtypeStruct((M_TOK, D_FF), BF16),
        compiler_params=_cparams(("parallel", "parallel")),
        name="ffn_up",
    )(h, w1, w3)


def _ffn_down_kernel(g_ref, w2_ref, x_ref, mods_ref, o_ref):
    t = pl.program_id(1)
    y = jnp.dot(g_ref[...], w2_ref[...].astype(BF16), preferred_element_type=F32)
    ctx_tiles = M_P // TE_DOWN
    row = jnp.where(t < ctx_tiles, 0, 1 + (t - ctx_tiles) // (DEC_SEQ // TE_DOWN))
    o_ref[...] = x_ref[...] + mods_ref[pl.ds(row, 1), :] * y


def _ffn_down_call(l, g, w2, x, mods_all):
    i = l // 2
    return pl.pallas_call(
        _ffn_down_kernel,
        grid=(D // TN_DOWN, M_TOK // TE_DOWN),
        in_specs=[
            pl.BlockSpec((TE_DOWN, D_FF), lambda j, t: (t, 0)),
            pl.BlockSpec((None, D_FF, TN_DOWN), lambda j, t: (i, 0, j)),
            pl.BlockSpec((TE_DOWN, TN_DOWN), lambda j, t: (t, j)),
            pl.BlockSpec((None, MOD_ROWS, TN_DOWN), lambda j, t: (l, 0, 5 * (D // TN_DOWN) + j)),
        ],
        out_specs=pl.BlockSpec((TE_DOWN, TN_DOWN), lambda j, t: (t, j)),
        out_shape=jax.ShapeDtypeStruct((M_TOK, D), F32),
        compiler_params=_cparams(("parallel", "parallel")),
        name="ffn_down",
    )(g, w2, x, mods_all)


def _src_kernel(cpos_ref, src_ref):
    def body(i, c):
        src_ref[cpos_ref[i]] = lax.shift_right_logical(i, TOP_K_SHIFT)
        return c
    lax.fori_loop(0, N_SLOTS, body, 0, unroll=8)


def _src_call(cpos):
    return pl.pallas_call(
        _src_kernel,
        in_specs=[pl.BlockSpec(memory_space=pltpu.SMEM)],
        out_specs=pl.BlockSpec(memory_space=pltpu.SMEM),
        out_shape=jax.ShapeDtypeStruct((N_SLOTS,), I32),
        name="moe_src",
    )(cpos)


def _for_item_rows(cnt, live, fn):
    half, quarter = MC // 2, MC // 4
    n_full = cnt // MC
    rem = cnt - n_full * MC
    whole = rem > half + quarter
    n_mc = n_full + whole.astype(I32)
    use_half = ~whole & (rem > quarter)
    use_quarter = ~whole & (rem > 0) & ((rem <= quarter) | (rem > half))

    @pl.when(live)
    def _():
        def body(c, carry):
            fn(c * MC, MC)
            return carry
        lax.fori_loop(0, n_mc, body, 0)

    @pl.when(live & use_half)
    def _():
        fn(n_mc * MC, half)

    @pl.when(live & use_quarter)
    def _():
        fn(n_mc * MC + jnp.where(use_half, half, 0), quarter)


def _gather_copy(hp_hbm, tok, buf, b, q, u, sem, slot):
    return pltpu.make_async_copy(hp_hbm.at[pl.ds(tok, 1)], buf.at[b, q, pl.ds(u, 1)], sem.at[slot])


def _moe_up_kernel(src_ref, ie_ref, ics_ref, icnt_ref, nit_ref, hp_hbm, w1_ref, w3_ref, g_ref, buf, sem):
    k = pl.program_id(0)
    j = pl.program_id(1)
    nit = nit_ref[0]
    groups = GC // SUBLANES

    def chunk_rows(item, c):
        return jnp.clip(icnt_ref[item] - c * GC, 0, GC)

    def issue_chunk(item, c, b):
        n = chunk_rows(item, c)
        base = ics_ref[item] + c * GC

        @pl.when(n > 0)
        def _():
            def body(q, carry):
                for u in range(SUBLANES):
                    tok = src_ref[base + jnp.minimum(q * SUBLANES + u, n - 1)]
                    _gather_copy(hp_hbm, tok, buf, b, c * groups + q, u, sem, c & 1).start()
                return carry
            lax.fori_loop(0, groups, body, 0)

    def finish_chunk(item, c, b):
        @pl.when(chunk_rows(item, c) > 0)
        def _():
            def body(r, carry):
                _gather_copy(hp_hbm, 0, buf, b, 0, 0, sem, c & 1).wait()
                return carry
            lax.fori_loop(0, GC, body, 0, unroll=8)

    @pl.when((k == 0) & (j == 0))
    def _():
        buf[...] = jnp.zeros_like(buf)
        issue_chunk(0, 0, 0)

        def body(c, carry):
            @pl.when(c + 1 < N_GC)
            def _():
                issue_chunk(0, c + 1, 0)
            finish_chunk(0, c, 0)
            return carry
        lax.fori_loop(0, N_GC, body, 0)

    nxt = k + 1

    @pl.when((nxt < nit) & (j >= 1) & (j <= N_GC))
    def _():
        finish_chunk(nxt, j - 1, nxt & 1)

    @pl.when((nxt < nit) & (j < N_GC))
    def _():
        issue_chunk(nxt, j, nxt & 1)

    g_ref[...] = jnp.zeros_like(g_ref)
    b = k & 1

    def mm_rows(start, size):
        tiles = pl.ds(pl.multiple_of(start // SUBLANES, size // SUBLANES), size // SUBLANES)
        lo, hi = _unpack(buf[b, tiles].reshape(size, HALF))
        lo, hi = lo.astype(BF16), hi.astype(BF16)

        def up(w_ref):
            return (jnp.dot(lo, w_ref[0:HALF, :].astype(BF16), preferred_element_type=F32)
                    + jnp.dot(hi, w_ref[HALF:D, :].astype(BF16), preferred_element_type=F32))
        a = up(w1_ref)
        rows = pl.ds(pl.multiple_of(start, size), size)
        g_ref[rows, :] = (a * jax.nn.sigmoid(a) * up(w3_ref)).astype(BF16)

    _for_item_rows(icnt_ref[k], k < nit, mm_rows)


def _moe_up_call(layer_e0, src, item_e, item_cs, item_cnt, n_items, hp, w1, w3):
    n_f = D_FFE // TF_MOE
    wspec = pl.BlockSpec((None, D, TF_MOE), lambda k, j, s, ie, ics, ic, ni: (
        layer_e0 + ie[k], 0, jnp.where(k < ni[0], j, n_f - 1)))
    return pl.pallas_call(
        _moe_up_kernel,
        grid_spec=pltpu.PrefetchScalarGridSpec(
            num_scalar_prefetch=5,
            grid=(N_ITEMS, D_FFE // TF_MOE),
            in_specs=[pl.BlockSpec(memory_space=pl.ANY), wspec, wspec],
            out_specs=pl.BlockSpec((RT, TF_MOE), lambda k, j, s, ie, ics, ic, ni: (k, j)),
            scratch_shapes=[
                pltpu.VMEM((2, RT // SUBLANES, SUBLANES, HALF), U32),
                pltpu.SemaphoreType.DMA((2,)),
            ],
        ),
        out_shape=jax.ShapeDtypeStruct((N_ITEMS * RT, D_FFE), BF16),
        compiler_params=_cparams(("arbitrary", "arbitrary")),
        name="moe_up",
    )(src, item_e, item_cs, item_cnt, n_items, hp, w1, w3)


def _moe_down_kernel(ie_ref, icnt_ref, nit_ref, g_ref, w2a_ref, w2b_ref, o_ref):
    k = pl.program_id(0)
    o_ref[...] = jnp.zeros_like(o_ref)

    def mm_rows(start, size):
        rows = pl.ds(pl.multiple_of(start, size), size)
        g = g_ref[rows, :]
        lo = jnp.dot(g, w2a_ref[...].astype(BF16), preferred_element_type=F32)
        hi = jnp.dot(g, w2b_ref[...].astype(BF16), preferred_element_type=F32)
        o_ref[rows, :] = _pack(lo, hi)

    _for_item_rows(icnt_ref[k], k < nit_ref[0], mm_rows)


def _moe_down_call(layer_e0, item_e, item_cnt, n_items, g, w2):
    live = lambda k, ni: jnp.minimum(k, ni[0] - 1)
    n_half = HALF // TN_MOE
    col = lambda k, n, ni: jnp.where(k < ni[0], n, n_half - 1)
    return pl.pallas_call(
        _moe_down_kernel,
        grid_spec=pltpu.PrefetchScalarGridSpec(
            num_scalar_prefetch=3,
            grid=(N_ITEMS, n_half),
            in_specs=[
                pl.BlockSpec((RT, D_FFE), lambda k, n, ie, ic, ni: (live(k, ni), 0)),
                pl.BlockSpec((None, D_FFE, TN_MOE),
                             lambda k, n, ie, ic, ni: (layer_e0 + ie[k], 0, col(k, n, ni))),
                pl.BlockSpec((None, D_FFE, TN_MOE),
                             lambda k, n, ie, ic, ni: (layer_e0 + ie[k], 0, n_half + col(k, n, ni))),
            ],
            out_specs=pl.BlockSpec((RT, TN_MOE), lambda k, n, ie, ic, ni: (k, n)),
        ),
        out_shape=jax.ShapeDtypeStruct((N_ITEMS * RT, HALF), U32),
        compiler_params=_cparams(("arbitrary", "arbitrary")),
        name="moe_down",
    )(item_e, item_cnt, n_items, g, w2, w2)


def _combine_copy(yo_hbm, row, buf, slot, q, u, sem):
    return pltpu.make_async_copy(yo_hbm.at[pl.ds(row, 1)], buf.at[slot, q, pl.ds(u, 1)], sem.at[slot])


def _combine_kernel(split, pos_ref, yo_hbm, x_ref, rinfo_ref, mods_ref, *rest):
    if split:
        yp_ref, ys_ref, buf, sem = rest
    else:
        o_ref, buf, sem = rest
    i = pl.program_id(0)
    n = pl.num_programs(0)
    rows = TOP_K * TM

    def issue(tile, slot):
        def body(q, c):
            for u in range(SUBLANES):
                _combine_copy(yo_hbm, pos_ref[tile * rows + q * SUBLANES + u], buf, slot, q, u, sem).start()
            return c
        lax.fori_loop(0, rows // SUBLANES, body, 0)

    @pl.when(i == 0)
    def _():
        issue(0, 0)

    @pl.when(i + 1 < n)
    def _():
        issue(i + 1, (i + 1) & 1)

    slot = i & 1

    def body(r, c):
        _combine_copy(yo_hbm, 0, buf, slot, 0, 0, sem).wait()
        return c
    lax.fori_loop(0, rows, body, 0, unroll=8)
    rinfo = rinfo_ref[...]
    tiles = TM // SUBLANES
    lo0, hi0 = _unpack(buf[slot, 0:tiles].reshape(TM, HALF))
    lo1, hi1 = _unpack(buf[slot, tiles:2 * tiles].reshape(TM, HALF))
    gt0, gt1 = rinfo[:, 2:3], rinfo[:, 3:4]
    g2 = mods_ref[pl.ds(_mod_row(i), 1), :]
    y_lo = x_ref[:, 0:HALF] + g2[:, 0:HALF] * (gt0 * lo0 + gt1 * lo1)
    y_hi = x_ref[:, HALF:D] + g2[:, HALF:D] * (gt0 * hi0 + gt1 * hi1)

    def write(ref):
        ref[:, 0:HALF] = y_lo
        ref[:, HALF:D] = y_hi

    if not split:
        write(o_ref)
        return

    @pl.when(i < P_TILES)
    def _():
        write(yp_ref)

    @pl.when(i >= P_TILES)
    def _():
        write(ys_ref)


def _combine_call(l, split, pos, yo, x, rinfo, mods_all):
    if split:
        out_specs = [pl.BlockSpec((TM, D), lambda i, p: (jnp.minimum(i, P_TILES - 1), 0)),
                     pl.BlockSpec((TM, D), lambda i, p: (jnp.maximum(i - P_TILES, 0), 0))]
        out_shape = [jax.ShapeDtypeStruct((M_P, D), F32), jax.ShapeDtypeStruct((M_S, D), F32)]
    else:
        out_specs = pl.BlockSpec((TM, D), lambda i, p: (i, 0))
        out_shape = jax.ShapeDtypeStruct((M_TOK, D), F32)
    return pl.pallas_call(
        functools.partial(_combine_kernel, split),
        grid_spec=pltpu.PrefetchScalarGridSpec(
            num_scalar_prefetch=1,
            grid=(N_TILES,),
            in_specs=[
                pl.BlockSpec(memory_space=pl.ANY),
                pl.BlockSpec((TM, D), lambda i, p: (i, 0)),
                pl.BlockSpec((TM, LANE), lambda i, p: (i, 0)),
                pl.BlockSpec((None, MOD_ROWS, D), lambda i, p: (l, 0, 5)),
            ],
            out_specs=out_specs,
            scratch_shapes=[pltpu.VMEM((2, TOP_K * TM // SUBLANES, SUBLANES, HALF), U32),
                            pltpu.SemaphoreType.DMA((2,))],
        ),
        out_shape=out_shape,
        compiler_params=_cparams(("arbitrary",)),
        name="moe_combine",
    )(pos, yo, x, rinfo, mods_all)


def _route_plan(rinfo):
    flat_e = rinfo[:, 0:TOP_K].astype(I32).reshape(-1)
    onehot = (flat_e[:, None] == jnp.arange(N_EXP, dtype=I32)[None, :]).astype(I32)
    csum = jnp.cumsum(onehot, axis=0)
    pick = lambda table: jnp.sum(onehot * table[None, :], axis=1)
    rank = jnp.sum(onehot * csum, axis=1) - 1
    counts = csum[-1]
    cstart = jnp.cumsum(counts) - counts
    n_it = (counts + RT - 1) // RT
    it_end = jnp.cumsum(n_it)
    it_base = it_end - n_it
    n_items = it_end[-1]
    cpos = pick(cstart) + rank
    ppos = (pick(it_base) + rank // RT) * RT + rank % RT
    q = jnp.arange(N_ITEMS, dtype=I32)
    qc = jnp.minimum(q, n_items - 1)
    qe = jnp.minimum(jnp.sum((it_end[None, :] <= qc[:, None]).astype(I32), axis=1), N_EXP - 1)
    qhot = (qe[:, None] == jnp.arange(N_EXP, dtype=I32)[None, :]).astype(I32)
    qpick = lambda table: jnp.sum(qhot * table[None, :], axis=1)
    s = qc - qpick(it_base)
    item_cnt = jnp.where(q < n_items, jnp.clip(qpick(counts) - s * RT, 0, RT), 0)
    item_cs = qpick(cstart) + s * RT
    pos_tiles = ppos.reshape(N_TILES, TM, TOP_K).transpose(0, 2, 1).reshape(-1)
    return (cpos.astype(I32), qe.astype(I32), item_cs.astype(I32), item_cnt.astype(I32),
            n_items.reshape(1).astype(I32), pos_tiles.astype(I32))


def _rope_tables():
    rows = DEC_SEQ // GRID_W
    row = jnp.repeat(jnp.arange(rows), GRID_W).astype(F32)
    col = jnp.tile(jnp.arange(GRID_W), rows).astype(F32)
    nf = ROPE // 4
    inv = ROPE_BASE ** (-jnp.arange(nf, dtype=F32) / nf)
    ar = row[:, None] * inv[None, :]
    ac = col[:, None] * inv[None, :]
    ang = jnp.concatenate([ar, ar, ac, ac], axis=-1)
    cos, sin = jnp.cos(ang), jnp.sin(ang)
    first = (jnp.arange(ROPE) % 32) < 16
    pad = lambda a, v: jnp.pad(a, ((0, 0), (0, LANE - ROPE)), constant_values=v)
    cos_p = pad(cos, 1.0)
    sin_a = pad(jnp.where(first[None, :], -sin, 0.0), 0.0)
    sin_b = pad(jnp.where(first[None, :], 0.0, sin), 0.0)
    ident = jnp.stack([jnp.ones((TMI, LANE), F32), jnp.zeros((TMI, LANE), F32), jnp.zeros((TMI, LANE), F32)])
    return jnp.concatenate([ident, jnp.stack([cos_p, sin_a, sin_b])], axis=1)


def kernel(x_prompt, x_sample, cache_ckv, cache_krope, c, c_ctx, w_ada, b_ada, norm1_g, norm2_g, w_in,
           qlat_norm_g, kvlat_norm_g, w_uq, w_ukv, q_head_norm_g, k_head_norm_g, gm_norm_g, gm_w_s,
           gm_b_s, pool_w, pool_scale, w_o, ffn_w1, ffn_w3, ffn_w2, router_w, moe_w1, moe_w3, moe_w2):
    pad_w = IN_PAD - w_in.shape[-1]
    col = jnp.arange(IN_PAD)[None, None, :]
    w_in_p = jnp.where(
        col < C_KR + ROPE, jnp.pad(w_in, ((0, 0), (0, 0), (0, pad_w))),
        jnp.where(col < C_KR + LANE, 0.0, jnp.pad(w_in, ((0, 0), (0, 0), (pad_w, 0))))).astype(BF16)
    w_uq_p = jnp.pad(w_uq.reshape(DEPTH, Q_RANK, N_HEADS, QK_DIM),
                     ((0, 0), (0, 0), (0, 0), (0, HEAD_PAD - QK_DIM))
                     ).reshape(DEPTH, Q_RANK, QK_W).astype(BF16)
    w_ukv_b = w_ukv.astype(BF16)
    w_o_b = w_o.astype(BF16)
    qg_p = jnp.pad(q_head_norm_g * (1.0 / math.sqrt(QK_DIM)),
                   ((0, 0), (0, HEAD_PAD - QK_DIM))).reshape(DEPTH, 1, HEAD_PAD)
    kg_n = k_head_norm_g[:, :NOPE].reshape(DEPTH, 1, LANE)
    kg_r = jnp.pad(k_head_norm_g[:, NOPE:], ((0, 0), (0, LANE - ROPE))).reshape(DEPTH, 1, LANE)
    gmw_b = gm_w_s.astype(BF16)
    gmb = jnp.broadcast_to(gm_b_s[:, :, :, None], (DEPTH, GM_GROUPS, CHUNK, LANE))
    pw_b = pool_w.astype(BF16)
    ps = pool_scale.reshape(DEPTH, 1, POOL_W)
    n1g = norm1_g.reshape(DEPTH, 1, D)
    n2g = norm2_g.reshape(DEPTH, 1, D)
    qlg = qlat_norm_g.reshape(DEPTH, 1, Q_RANK)
    kvlg = kvlat_norm_g.reshape(DEPTH, 1, KV_RANK)
    rw_p = jnp.pad(router_w, ((0, 0), (0, 0), (0, LANE - N_EXP)))
    n_moe = moe_w1.shape[0]
    moe_w1_flat = moe_w1.reshape(n_moe * N_EXP, D, D_FFE)
    moe_w3_flat = moe_w3.reshape(n_moe * N_EXP, D, D_FFE)
    moe_w2_flat = moe_w2.reshape(n_moe * N_EXP, D_FFE, D)
    rope_tab = _rope_tables()
    cache_kr_p = jnp.pad(cache_krope, ((0, 0), (0, 0), (0, 0), (0, LANE - ROPE)))

    cond = jnp.zeros((MOD_ROWS, D), F32).at[0].set(c_ctx).at[1:1 + DEC_BATCH].set(c)
    mods_all = _ada_call(cond, w_ada, b_ada)
    k_c, v_c = _cache_kv_call(cache_ckv, cache_kr_p, w_ukv_b, kg_n, kg_r)

    x = (x_prompt.reshape(M_P, D), x_sample.reshape(M_S, D))
    ckv_out, kr_out = [], []
    for l in range(DEPTH):
        q, k, v, ckv, kr, zg, zp = _in_proj_call(
            l, x, mods_all, n1g, w_in_p, qlg, kvlg, w_uq_p, qg_p, w_ukv_b, kg_n, kg_r, rope_tab,
            gm_norm_g, gmw_b, gmb)
        ckv_out.append(ckv[:M_P].reshape(BATCH, SEQ, KV_RANK))
        kr_out.append(kr[:M_P, :ROPE].reshape(BATCH, SEQ, ROPE))
        attn = _attn_call(l, q, k, v, k_c, v_c)
        i = l // 2
        if l % 2 == 0:
            x, h2b = _out_proj_call(l, False, attn, zg, zp, x, mods_all, pw_b, ps, w_o_b, n2g)
            g = _ffn_up_call(i, h2b, ffn_w1, ffn_w3)
            x = _ffn_down_call(l, g, ffn_w2, x, mods_all)
        else:
            x, hp, rinfo = _out_proj_call(l, True, attn, zg, zp, x, mods_all, pw_b, ps, w_o_b, n2g, rw_p)
            cpos, item_e, item_cs, item_cnt, n_items, pos_tiles = _route_plan(rinfo)
            src = _src_call(cpos)
            g = _moe_up_call(i * N_EXP, src, item_e, item_cs, item_cnt, n_items, hp,
                             moe_w1_flat, moe_w3_flat)
            yo = _moe_down_call(i * N_EXP, item_e, item_cnt, n_items, g, moe_w2_flat)
            x = _combine_call(l, l == DEPTH - 1, pos_tiles, yo, x, rinfo, mods_all)

    if DEPTH % 2 == 0:
        y_p, y_s = x
    else:
        y_p, y_s = x[:M_P], x[M_P:]
    return (y_p.reshape(BATCH, SEQ, D), y_s.reshape(DEC_BATCH, DEC_SEQ, D),
            jnp.stack(ckv_out, axis=1), jnp.stack(kr_out, axis=1))
```

```python
import functools
import math

import jax
import jax.numpy as jnp
from jax import lax
from jax.experimental import pallas as pl
from jax.experimental.pallas import tpu as pltpu

F32 = jnp.float32
BF16 = jnp.bfloat16
U32 = jnp.uint32
I32 = jnp.int32

D = 2048
BATCH, SEQ = 16, 256
DEC_BATCH, DEC_SEQ = 2, 1024
PAST = 256
DEPTH = 4
GRID_W = 64
EPS = 1e-6
N_HEADS = 8
NOPE, ROPE, VDIM = 128, 64, 128
QK_DIM = NOPE + ROPE
HEAD_PAD = 256
QK_W = N_HEADS * HEAD_PAD
Q_RANK, KV_RANK = 512, 256
ROPE_BASE = 10000.0
ATTN_W = N_HEADS * VDIM
CHUNK, GM_GROUPS, GM_W = 128, 4, 512
POOL_WINDOWS = (2, 4, 8, 16)
POOL_W = 512
POOL_HALO = 8
D_FF, N_EXP, D_FFE, TOP_K = 5632, 8, 2816, 2
TOP_K_SHIFT = 1

C_Q, C_KV, C_KR, C_GM, C_PL = 0, 512, 768, 896, 1920
IN_PAD = 2432

LANE = 128
SUBLANES = 8
TM = 256
TMI = 512
TMO = 256
M_P, M_S = BATCH * SEQ, DEC_BATCH * DEC_SEQ
M_TOK = M_P + M_S
N_TILES = M_TOK // TM
P_TILES = M_P // TM
S_TILES = DEC_SEQ // TM
MOD_ROWS = 16
ADA_TN = 1024
TE_UP, TF_UP = 1024, 512
TE_DOWN, TN_DOWN = 512, 512
HALF = D // 2
N_SLOTS = TOP_K * M_TOK
RT = 2048
N_ITEMS = N_EXP + (N_SLOTS - N_EXP) // RT
GC = 256
N_GC = RT // GC
MC = 1024
TF_MOE = 256
TN_MOE = 256
VMEM_LIMIT = 56 * 1024 * 1024


def _cparams(sem):
    return pltpu.CompilerParams(dimension_semantics=sem, vmem_limit_bytes=VMEM_LIMIT)


def _const_spec(shape):
    nd = len(shape)
    return pl.BlockSpec(shape, lambda *_: (0,) * nd)


def _layer_spec(l, shape):
    nd = len(shape)
    return pl.BlockSpec((None,) + tuple(shape), lambda *_: (l,) + (0,) * nd)


def _x_operands(x, tile):
    if not isinstance(x, tuple):
        return [x], [pl.BlockSpec((tile, D), lambda i: (i, 0))]
    ctx = M_P // tile
    return list(x), [pl.BlockSpec((tile, D), lambda i: (jnp.minimum(i, ctx - 1), 0)),
                     pl.BlockSpec((tile, D), lambda i: (jnp.maximum(i - ctx, 0), 0))]


def _mod_row(i):
    return jnp.where(i < P_TILES, 0, 1 + (i - P_TILES) // S_TILES)


def _rms(x, g, n):
    ms = jnp.sum(x * x, axis=-1, keepdims=True) * (1.0 / n)
    return x * lax.rsqrt(ms + EPS) * g


def _rope(r, cos, sin_a, sin_b):
    return r * cos + pltpu.roll(r, LANE - 16, 1) * sin_a + pltpu.roll(r, 16, 1) * sin_b


def _pack(lo, hi):
    lo_b = lax.bitcast_convert_type(lo.astype(BF16).astype(F32), U32)
    hi_b = lax.bitcast_convert_type(hi.astype(BF16).astype(F32), U32)
    return (lo_b >> 16) | hi_b


def _unpack(w):
    lo = lax.bitcast_convert_type(w << 16, F32)
    hi = lax.bitcast_convert_type(w & jnp.uint32(0xFFFF0000), F32)
    return lo, hi


def _ada_kernel(cond_ref, w_ref, b_ref, o_ref):
    c = cond_ref[...]
    s = c * jax.nn.sigmoid(c)
    s_hi = s.astype(BF16)
    s_lo = (s - s_hi.astype(F32)).astype(BF16)
    w = w_ref[...]
    w_hi = w.astype(BF16)
    w_lo = (w - w_hi.astype(F32)).astype(BF16)
    lhs = jnp.concatenate([s_hi, s_lo], axis=0)
    r1 = jnp.dot(lhs, w_hi, preferred_element_type=F32)
    r2 = jnp.dot(s_hi, w_lo, preferred_element_type=F32)
    o_ref[...] = r1[:MOD_ROWS] + r1[MOD_ROWS:] + r2 + b_ref[...]


def _ada_call(cond, w_ada, b_ada):
    n = 6 * D
    return pl.pallas_call(
        _ada_kernel,
        grid=(DEPTH, n // ADA_TN),
        in_specs=[
            pl.BlockSpec((MOD_ROWS, D), lambda l, j: (0, 0)),
            pl.BlockSpec((None, D, ADA_TN), lambda l, j: (l, 0, j)),
            pl.BlockSpec((None, 1, ADA_TN), lambda l, j: (l, 0, j)),
        ],
        out_specs=pl.BlockSpec((None, MOD_ROWS, ADA_TN), lambda l, j: (l, 0, j)),
        out_shape=jax.ShapeDtypeStruct((DEPTH, MOD_ROWS, n), F32),
        compiler_params=_cparams(("parallel", "parallel")),
        name="ada",
    )(cond, w_ada, b_ada.reshape(DEPTH, 1, n))


def _write_kv(ckv_b, kr, w_ukv_ref, kg_n, kg_r, rope, k_ref, v_ref, rows=slice(None)):
    kv = jnp.dot(ckv_b, w_ukv_ref[...], preferred_element_type=F32)
    ssr = jnp.sum(kr * kr, axis=-1, keepdims=True)
    shared = kr * kg_r
    if rope is not None:
        shared = _rope(shared, *rope)
    for h in range(N_HEADS):
        kn = kv[:, h * HEAD_PAD:h * HEAD_PAD + NOPE]
        vv = kv[:, h * HEAD_PAD + NOPE:(h + 1) * HEAD_PAD]
        ms = (jnp.sum(kn * kn, axis=-1, keepdims=True) + ssr) * (1.0 / QK_DIM)
        inv = lax.rsqrt(ms + EPS)
        k_ref[rows, h * HEAD_PAD:h * HEAD_PAD + NOPE] = (kn * inv * kg_n).astype(BF16)
        k_ref[rows, h * HEAD_PAD + NOPE:(h + 1) * HEAD_PAD] = (shared * inv).astype(BF16)
        v_ref[rows, h * VDIM:(h + 1) * VDIM] = vv.astype(BF16)


def _cache_kv_kernel(ckv_ref, kr_ref, w_ukv_ref, kgn_ref, kgr_ref, k_ref, v_ref):
    _write_kv(ckv_ref[...].astype(BF16), kr_ref[...], w_ukv_ref, kgn_ref[...], kgr_ref[...],
              None, k_ref, v_ref)


def _cache_kv_call(cache_ckv, cache_kr_p, w_ukv_b, kg_n, kg_r):
    return pl.pallas_call(
        _cache_kv_kernel,
        grid=(DEPTH, DEC_BATCH),
        in_specs=[
            pl.BlockSpec((None, None, PAST, KV_RANK), lambda l, b: (b, l, 0, 0)),
            pl.BlockSpec((None, None, PAST, LANE), lambda l, b: (b, l, 0, 0)),
            pl.BlockSpec((None, KV_RANK, QK_W), lambda l, b: (l, 0, 0)),
            pl.BlockSpec((None, 1, LANE), lambda l, b: (l, 0, 0)),
            pl.BlockSpec((None, 1, LANE), lambda l, b: (l, 0, 0)),
        ],
        out_specs=[
            pl.BlockSpec((None, None, PAST, QK_W), lambda l, b: (l, b, 0, 0)),
            pl.BlockSpec((None, None, PAST, ATTN_W), lambda l, b: (l, b, 0, 0)),
        ],
        out_shape=[
            jax.ShapeDtypeStruct((DEPTH, DEC_BATCH, PAST, QK_W), BF16),
            jax.ShapeDtypeStruct((DEPTH, DEC_BATCH, PAST, ATTN_W), BF16),
        ],
        compiler_params=_cparams(("parallel", "parallel")),
        name="cache_kv",
    )(cache_ckv, cache_kr_p, w_ukv_b, kg_n, kg_r)


def _in_proj_kernel(two_x, *refs):
    x_refs, refs = refs[:1 + two_x], refs[1 + two_x:]
    _in_proj_body(x_refs, *refs)


def _in_proj_body(x_refs, mods_ref, n1g_ref, w_in_ref, qlg_ref, kvlg_ref, w_uq_ref, qg_ref,
                    w_ukv_ref, kgn_ref, kgr_ref, rope_ref, gmg_ref, gmw_ref, gmb_ref,
                    q_ref, k_ref, v_ref, ckv_ref, kr_ref, zg_ref, zp_ref):
    i = pl.program_id(0)
    row = jnp.where(i < M_P // TMI, 0, 1 + (i - M_P // TMI) // (DEC_SEQ // TMI))
    m = mods_ref[pl.ds(row, 1), :]
    sh1, sc1 = m[:, 0:D], m[:, D:2 * D]
    for sub in range(TMI // TM):
        rs = slice(sub * TM, (sub + 1) * TM)
        x = x_refs[0][rs, :] if len(x_refs) == 1 else jnp.where(i < M_P // TMI, x_refs[0][rs, :], x_refs[1][rs, :])
        h = _rms(x, n1g_ref[...], D) * (1.0 + sc1) + sh1
        z = jnp.dot(h.astype(BF16), w_in_ref[...], preferred_element_type=F32)
        rope = (rope_ref[0, rs, :], rope_ref[1, rs, :], rope_ref[2, rs, :])

        ql = _rms(z[:, C_Q:C_Q + Q_RANK], qlg_ref[...], Q_RANK)
        qf = jnp.dot(ql.astype(BF16), w_uq_ref[...], preferred_element_type=F32)
        qg = qg_ref[...]
        for hd in range(N_HEADS):
            seg = qf[:, hd * HEAD_PAD:(hd + 1) * HEAD_PAD]
            seg = _rms(seg, qg, QK_DIM)
            q_ref[rs, hd * HEAD_PAD:hd * HEAD_PAD + NOPE] = seg[:, :NOPE].astype(BF16)
            q_ref[rs, hd * HEAD_PAD + NOPE:(hd + 1) * HEAD_PAD] = _rope(seg[:, NOPE:], *rope).astype(BF16)

        ckv = _rms(z[:, C_KV:C_KV + KV_RANK], kvlg_ref[...], KV_RANK)
        kr = z[:, C_KR:C_KR + LANE]
        ckv_ref[rs, :] = ckv
        kr_ref[rs, :] = kr
        _write_kv(ckv.astype(BF16), kr, w_ukv_ref, kgn_ref[...], kgr_ref[...], rope, k_ref, v_ref, rs)

        zg = jax.nn.gelu(z[:, C_GM:C_GM + 2 * GM_W], approximate=True)
        for g in range(GM_GROUPS):
            u = zg[:, g * LANE:(g + 1) * LANE]
            vg = _rms(zg[:, GM_W + g * LANE:GM_W + (g + 1) * LANE], gmg_ref[g:g + 1, :], LANE)
            vb = vg.astype(BF16)
            for c in range(TM // CHUNK):
                sv = jnp.dot(gmw_ref[g], vb[c * CHUNK:(c + 1) * CHUNK, :], preferred_element_type=F32)
                sv = sv + gmb_ref[g]
                r0 = sub * TM + c * CHUNK
                zg_ref[r0:r0 + CHUNK, g * LANE:(g + 1) * LANE] = (
                    u[c * CHUNK:(c + 1) * CHUNK, :] * sv).astype(BF16)

        zp_ref[rs, :] = z[:, C_PL:C_PL + POOL_W]


def _in_proj_call(l, x, mods_all, n1g, w_in_b, qlg, kvlg, w_uq_b, qg, w_ukv_b, kg_n, kg_r, rope_tab,
                  gmg, gmw_b, gmb):
    def rope_idx(i):
        return (0, jnp.where(i < M_P // TMI, 0, 1 + (i - M_P // TMI) % (DEC_SEQ // TMI)), 0)

    row = lambda w: pl.BlockSpec((TMI, w), lambda i: (i, 0))
    two_x = isinstance(x, tuple)
    x_args, x_specs = _x_operands(x, TMI)
    return pl.pallas_call(
        functools.partial(_in_proj_kernel, two_x),
        grid=(M_TOK // TMI,),
        in_specs=x_specs + [
            _layer_spec(l, (MOD_ROWS, 6 * D)),
            _layer_spec(l, (1, D)),
            _layer_spec(l, (D, IN_PAD)),
            _layer_spec(l, (1, Q_RANK)),
            _layer_spec(l, (1, KV_RANK)),
            _layer_spec(l, (Q_RANK, QK_W)),
            _layer_spec(l, (1, HEAD_PAD)),
            _layer_spec(l, (KV_RANK, QK_W)),
            _layer_spec(l, (1, LANE)),
            _layer_spec(l, (1, LANE)),
            pl.BlockSpec((3, TMI, LANE), rope_idx),
            _layer_spec(l, (GM_GROUPS, LANE)),
            _layer_spec(l, (GM_GROUPS, CHUNK, CHUNK)),
            _layer_spec(l, (GM_GROUPS, CHUNK, LANE)),
        ],
        out_specs=[row(QK_W), row(QK_W), row(ATTN_W), row(KV_RANK), row(LANE), row(GM_W), row(POOL_W)],
        out_shape=[
            jax.ShapeDtypeStruct((M_TOK, QK_W), BF16),
            jax.ShapeDtypeStruct((M_TOK, QK_W), BF16),
            jax.ShapeDtypeStruct((M_TOK, ATTN_W), BF16),
            jax.ShapeDtypeStruct((M_TOK, KV_RANK), F32),
            jax.ShapeDtypeStruct((M_TOK, LANE), F32),
            jax.ShapeDtypeStruct((M_TOK, GM_W), BF16),
            jax.ShapeDtypeStruct((M_TOK, POOL_W), F32),
        ],
        compiler_params=_cparams(("parallel",)),
        name="in_proj",
    )(*x_args, mods_all, n1g, w_in_b, qlg, kvlg, w_uq_b, qg, w_ukv_b, kg_n, kg_r, rope_tab, gmg, gmw_b, gmb)


def _attend_heads(q_ref, kv_pairs, o_ref):
    nt = (((1,), (1,)), ((), ()))
    for h in range(N_HEADS):
        qh = q_ref[:, h * HEAD_PAD:(h + 1) * HEAD_PAD]
        ss = [lax.dot_general(qh, k_ref[:, h * HEAD_PAD:(h + 1) * HEAD_PAD], nt,
                              preferred_element_type=F32) for k_ref, _ in kv_pairs]
        mx = ss[0].max(axis=-1, keepdims=True)
        for s in ss[1:]:
            mx = jnp.maximum(mx, s.max(axis=-1, keepdims=True))
        den = None
        acc = None
        for s, (_, v_ref) in zip(ss, kv_pairs):
            p = jnp.exp(s - mx)
            d = jnp.sum(p, axis=-1, keepdims=True)
            a = jnp.dot(p.astype(BF16), v_ref[:, h * VDIM:(h + 1) * VDIM], preferred_element_type=F32)
            den = d if den is None else den + d
            acc = a if acc is None else acc + a
        o_ref[:, h * VDIM:(h + 1) * VDIM] = (acc / den).astype(BF16)


def _attn_kernel(q_ref, kp_ref, vp_ref, ks_ref, vs_ref, kc_ref, vc_ref, o_ref):
    i = pl.program_id(0)

    @pl.when(i < P_TILES)
    def _():
        _attend_heads(q_ref, [(kp_ref, vp_ref)], o_ref)

    @pl.when(i >= P_TILES)
    def _():
        _attend_heads(q_ref, [(ks_ref, vs_ref), (kc_ref, vc_ref)], o_ref)


def _attn_call(l, q, k, v, k_c, v_c):
    def seq_b(i):
        return jnp.maximum(i - P_TILES, 0) // S_TILES

    small = lambda w: pl.BlockSpec((TM, w), lambda i: (jnp.minimum(i, P_TILES - 1), 0))
    big = lambda w: pl.BlockSpec((DEC_SEQ, w), lambda i: (M_P // DEC_SEQ + seq_b(i), 0))
    cache = lambda w: pl.BlockSpec((None, None, PAST, w), lambda i: (l, seq_b(i), 0, 0))
    return pl.pallas_call(
        _attn_kernel,
        grid=(N_TILES,),
        in_specs=[
            pl.BlockSpec((TM, QK_W), lambda i: (i, 0)),
            small(QK_W), small(ATTN_W),
            big(QK_W), big(ATTN_W),
            cache(QK_W), cache(ATTN_W),
        ],
        out_specs=pl.BlockSpec((TM, ATTN_W), lambda i: (i, 0)),
        out_shape=jax.ShapeDtypeStruct((M_TOK, ATTN_W), BF16),
        compiler_params=_cparams(("parallel",)),
        name="attention",
    )(q, k, v, k, v, k_c, v_c)


def _out_proj_kernel(moe, two_x, attn_ref, zg_ref, zprev_ref, zcur_ref, znext_ref, *rest):
    x_refs, rest = rest[:1 + two_x], rest[1 + two_x:]
    mods_ref, pw_ref, ps_ref, w_o_ref, n2g_ref = rest[:5]
    rest = rest[5:]
    if moe:
        rw_ref, xo_ref, hp_ref, rinfo_ref, ext_ref, mix_ref = rest
    else:
        xo_ref, h2b_ref, ext_ref, mix_ref = rest
    i = pl.program_id(0)
    ctx_steps = M_P // TMO
    is_ctx = i < ctx_steps
    seq_len = jnp.where(is_ctx, SEQ, DEC_SEQ)
    m = mods_ref[pl.ds(jnp.where(is_ctx, 0, 1 + (i - ctx_steps) // (DEC_SEQ // TMO)), 1), :]
    g1, sh2, sc2 = m[:, 2 * D:3 * D], m[:, 3 * D:4 * D], m[:, 4 * D:5 * D]

    pos0 = jnp.where(is_ctx, 0, ((i - ctx_steps) % (DEC_SEQ // TMO)) * TMO)
    zero_halo = jnp.zeros((POOL_HALO, POOL_W), F32)
    ext_ref[0:POOL_HALO, :] = jnp.where(pos0 == 0, zero_halo, zprev_ref[TMO - POOL_HALO:, :])
    ext_ref[POOL_HALO:POOL_HALO + TMO, :] = zcur_ref[...]
    ext_ref[POOL_HALO + TMO:, :] = jnp.where(pos0 + TMO == seq_len, zero_halo, znext_ref[0:POOL_HALO, :])
    mix_ref[:, 0:ATTN_W] = attn_ref[...]
    mix_ref[:, ATTN_W:ATTN_W + GM_W] = zg_ref[...]

    assert TMO == TM and SEQ % TMO == 0 and DEC_SEQ % TMO == 0
    for sub in range(TMO // TM):
        rs = slice(sub * TM, (sub + 1) * TM)
        pos = pos0 + lax.broadcasted_iota(I32, (TM, 1), 0)
        for gi, w in enumerate(POOL_WINDOWS):
            cols = slice(gi * LANE, (gi + 1) * LANE)
            win = None
            for dlt in range(-(w // 2), w - w // 2):
                r0 = POOL_HALO + dlt
                part = ext_ref[r0:r0 + TM, cols]
                win = part if win is None else win + part
            lo = jnp.maximum(pos - w // 2, 0)
            hi = jnp.minimum(pos + (w - 1 - w // 2), seq_len - 1)
            cnt = (hi - lo + 1).astype(F32)
            pooled = win / cnt - zcur_ref[rs, cols]
            mixed = jnp.dot(pooled.astype(BF16), pw_ref[gi], preferred_element_type=F32)
            mix_ref[rs, ATTN_W + GM_W + gi * LANE:ATTN_W + GM_W + (gi + 1) * LANE] = (
                mixed * ps_ref[:, cols]).astype(BF16)

        out = jnp.dot(mix_ref[rs, :], w_o_ref[...], preferred_element_type=F32)
        x = x_refs[0][rs, :] if len(x_refs) == 1 else jnp.where(is_ctx, x_refs[0][rs, :], x_refs[1][rs, :])
        xn = x + g1 * out
        xo_ref[rs, :] = xn
        h2 = _rms(xn, n2g_ref[...], D) * (1.0 + sc2) + sh2

        if not moe:
            h2b_ref[rs, :] = h2.astype(BF16)
            continue

        hp_ref[rs, :] = _pack(h2[:, :HALF], h2[:, HALF:])
        h_hi = h2.astype(BF16)
        h_lo = (h2 - h_hi.astype(F32)).astype(BF16)
        rw = rw_ref[...]
        w_hi = rw.astype(BF16)
        w_lo = (rw - w_hi.astype(F32)).astype(BF16)
        logits = (jnp.dot(h_hi, w_hi, preferred_element_type=F32)
                  + jnp.dot(h_lo, w_hi, preferred_element_type=F32)
                  + jnp.dot(h_hi, w_lo, preferred_element_type=F32))
        lane = lax.broadcasted_iota(I32, (TM, LANE), 1).astype(F32)
        neg = jnp.float32(-jnp.inf)
        logits = jnp.where(lane < N_EXP, logits, neg)
        v1 = logits.max(axis=-1, keepdims=True)
        i1 = jnp.where(logits == v1, lane, float(LANE)).min(axis=-1, keepdims=True)
        l2 = jnp.where(lane == i1, neg, logits)
        v2 = l2.max(axis=-1, keepdims=True)
        i2 = jnp.where(l2 == v2, lane, float(LANE)).min(axis=-1, keepdims=True)
        e2 = jnp.exp(v2 - v1)
        gate1 = 1.0 / (1.0 + e2)
        gate2 = e2 / (1.0 + e2)
        rinfo_ref[rs, :] = jnp.where(lane == 0, i1, jnp.where(lane == 1, i2, jnp.where(
            lane == 2, gate1, jnp.where(lane == 3, gate2, 0.0))))


def _out_proj_call(l, moe, attn, zg, zp, x, mods_all, pw_b, ps, w_o_b, n2g, rw_p=None):
    n_steps = M_TOK // TMO
    row = lambda w: pl.BlockSpec((TMO, w), lambda i: (i, 0))
    two_x = isinstance(x, tuple)
    x_args, x_specs = _x_operands(x, TMO)
    in_specs = [
        row(ATTN_W), row(GM_W),
        pl.BlockSpec((TMO, POOL_W), lambda i: (jnp.maximum(i - 1, 0), 0)),
        row(POOL_W),
        pl.BlockSpec((TMO, POOL_W), lambda i: (jnp.minimum(i + 1, n_steps - 1), 0)),
    ] + x_specs + [
        _layer_spec(l, (MOD_ROWS, 6 * D)),
        _layer_spec(l, (len(POOL_WINDOWS), LANE, LANE)),
        _layer_spec(l, (1, POOL_W)),
        _layer_spec(l, (D, D)),
        _layer_spec(l, (1, D)),
    ]
    args = [attn, zg, zp, zp, zp, *x_args, mods_all, pw_b, ps, w_o_b, n2g]
    if moe:
        in_specs.append(_layer_spec(l // 2, (D, LANE)))
        args.append(rw_p)
        out_specs = [row(D), row(HALF), row(LANE)]
        out_shape = [jax.ShapeDtypeStruct((M_TOK, D), F32), jax.ShapeDtypeStruct((M_TOK, HALF), U32),
                     jax.ShapeDtypeStruct((M_TOK, LANE), F32)]
    else:
        out_specs = [row(D), row(D)]
        out_shape = [jax.ShapeDtypeStruct((M_TOK, D), F32), jax.ShapeDtypeStruct((M_TOK, D), BF16)]
    return pl.pallas_call(
        functools.partial(_out_proj_kernel, moe, two_x),
        grid=(n_steps,),
        in_specs=in_specs,
        out_specs=out_specs,
        out_shape=out_shape,
        scratch_shapes=[pltpu.VMEM((TMO + 2 * POOL_HALO, POOL_W), F32), pltpu.VMEM((TMO, D), BF16)],
        compiler_params=_cparams(("parallel",)),
        name="out_proj_moe" if moe else "out_proj",
    )(*args)


def _ffn_up_kernel(h_ref, w1_ref, w3_ref, g_ref):
    h = h_ref[...]
    a = jnp.dot(h, w1_ref[...].astype(BF16), preferred_element_type=F32)
    b = jnp.dot(h, w3_ref[...].astype(BF16), preferred_element_type=F32)
    g_ref[...] = (a * jax.nn.sigmoid(a) * b).astype(BF16)


def _ffn_up_call(i, h, w1, w3):
    wspec = pl.BlockSpec((None, D, TF_UP), lambda j, t: (i, 0, j))
    return pl.pallas_call(
        _ffn_up_kernel,
        grid=(D_FF // TF_UP, M_TOK // TE_UP),
        in_specs=[pl.BlockSpec((TE_UP, D), lambda j, t: (t, 0)), wspec, wspec],
        out_specs=pl.BlockSpec((TE_UP, TF_UP), lambda j, t: (t, j)),
        out_shape=jax.ShapeDtypeStruct((M_TOK, D_FF), BF16),
        compiler_params=_cparams(("parallel", "parallel")),
        name="ffn_up",
    )(h, w1, w3)


def _ffn_down_kernel(g_ref, w2_ref, x_ref, mods_ref, o_ref):
    t = pl.program_id(1)
    y = jnp.dot(g_ref[...], w2_ref[...].astype(BF16), preferred_element_type=F32)
    ctx_tiles = M_P // TE_DOWN
    row = jnp.where(t < ctx_tiles, 0, 1 + (t - ctx_tiles) // (DEC_SEQ // TE_DOWN))
    o_ref[...] = x_ref[...] + mods_ref[pl.ds(row, 1), :] * y


def _ffn_down_call(l, g, w2, x, mods_all):
    i = l // 2
    return pl.pallas_call(
        _ffn_down_kernel,
        grid=(D // TN_DOWN, M_TOK // TE_DOWN),
        in_specs=[
            pl.BlockSpec((TE_DOWN, D_FF), lambda j, t: (t, 0)),
            pl.BlockSpec((None, D_FF, TN_DOWN), lambda j, t: (i, 0, j)),
            pl.BlockSpec((TE_DOWN, TN_DOWN), lambda j, t: (t, j)),
            pl.BlockSpec((None, MOD_ROWS, TN_DOWN), lambda j, t: (l, 0, 5 * (D // TN_DOWN) + j)),
        ],
        out_specs=pl.BlockSpec((TE_DOWN, TN_DOWN), lambda j, t: (t, j)),
        out_shape=jax.ShapeDtypeStruct((M_TOK, D), F32),
        compiler_params=_cparams(("parallel", "parallel")),
        name="ffn_down",
    )(g, w2, x, mods_all)


def _src_kernel(cpos_ref, src_ref):
    def body(i, c):
        src_ref[cpos_ref[i]] = lax.shift_right_logical(i, TOP_K_SHIFT)
        return c
    lax.fori_loop(0, N_SLOTS, body, 0, unroll=8)


def _src_call(cpos):
    return pl.pallas_call(
        _src_kernel,
        in_specs=[pl.BlockSpec(memory_space=pltpu.SMEM)],
        out_specs=pl.BlockSpec(memory_space=pltpu.SMEM),
        out_shape=jax.ShapeDtypeStruct((N_SLOTS,), I32),
        name="moe_src",
    )(cpos)


def _for_item_rows(cnt, live, fn):
    half, quarter = MC // 2, MC // 4
    n_full = cnt // MC
    rem = cnt - n_full * MC
    whole = rem > half + quarter
    n_mc = n_full + whole.astype(I32)
    use_half = ~whole & (rem > quarter)
    use_quarter = ~whole & (rem > 0) & ((rem <= quarter) | (rem > half))

    @pl.when(live)
    def _():
        def body(c, carry):
            fn(c * MC, MC)
            return carry
        lax.fori_loop(0, n_mc, body, 0)

    @pl.when(live & use_half)
    def _():
        fn(n_mc * MC, half)

    @pl.when(live & use_quarter)
    def _():
        fn(n_mc * MC + jnp.where(use_half, half, 0), quarter)


def _gather_copy(hp_hbm, tok, buf, b, q, u, sem, slot):
    return pltpu.make_async_copy(hp_hbm.at[pl.ds(tok, 1)], buf.at[b, q, pl.ds(u, 1)], sem.at[slot])


def _moe_up_kernel(src_ref, ie_ref, ics_ref, icnt_ref, nit_ref, hp_hbm, w1_ref, w3_ref, g_ref, buf, sem):
    k = pl.program_id(0)
    j = pl.program_id(1)
    nit = nit_ref[0]
    groups = GC // SUBLANES

    def chunk_rows(item, c):
        return jnp.clip(icnt_ref[item] - c * GC, 0, GC)

    def issue_chunk(item, c, b):
        n = chunk_rows(item, c)
        base = ics_ref[item] + c * GC

        @pl.when(n > 0)
        def _():
            def body(q, carry):
                for u in range(SUBLANES):
                    tok = src_ref[base + jnp.minimum(q * SUBLANES + u, n - 1)]
                    _gather_copy(hp_hbm, tok, buf, b, c * groups + q, u, sem, c & 1).start()
                return carry
            lax.fori_loop(0, groups, body, 0)

    def finish_chunk(item, c, b):
        @pl.when(chunk_rows(item, c) > 0)
        def _():
            def body(r, carry):
                _gather_copy(hp_hbm, 0, buf, b, 0, 0, sem, c & 1).wait()
                return carry
            lax.fori_loop(0, GC, body, 0, unroll=8)

    @pl.when((k == 0) & (j == 0))
    def _():
        buf[...] = jnp.zeros_like(buf)
        issue_chunk(0, 0, 0)

        def body(c, carry):
            @pl.when(c + 1 < N_GC)
            def _():
                issue_chunk(0, c + 1, 0)
            finish_chunk(0, c, 0)
            return carry
        lax.fori_loop(0, N_GC, body, 0)

    nxt = k + 1

    @pl.when((nxt < nit) & (j >= 1) & (j <= N_GC))
    def _():
        finish_chunk(nxt, j - 1, nxt & 1)

    @pl.when((nxt < nit) & (j < N_GC))
    def _():
        issue_chunk(nxt, j, nxt & 1)

    g_ref[...] = jnp.zeros_like(g_ref)
    b = k & 1

    def mm_rows(start, size):
        tiles = pl.ds(pl.multiple_of(start // SUBLANES, size // SUBLANES), size // SUBLANES)
        lo, hi = _unpack(buf[b, tiles].reshape(size, HALF))
        lo, hi = lo.astype(BF16), hi.astype(BF16)

        def up(w_ref):
            return (jnp.dot(lo, w_ref[0:HALF, :].astype(BF16), preferred_element_type=F32)
                    + jnp.dot(hi, w_ref[HALF:D, :].astype(BF16), preferred_element_type=F32))
        a = up(w1_ref)
        rows = pl.ds(pl.multiple_of(start, size), size)
        g_ref[rows, :] = (a * jax.nn.sigmoid(a) * up(w3_ref)).astype(BF16)

    _for_item_rows(icnt_ref[k], k < nit, mm_rows)


def _moe_up_call(layer_e0, src, item_e, item_cs, item_cnt, n_items, hp, w1, w3):
    n_f = D_FFE // TF_MOE
    wspec = pl.BlockSpec((None, D, TF_MOE), lambda k, j, s, ie, ics, ic, ni: (
        layer_e0 + ie[k], 0, jnp.where(k < ni[0], j, n_f - 1)))
    return pl.pallas_call(
        _moe_up_kernel,
        grid_spec=pltpu.PrefetchScalarGridSpec(
            num_scalar_prefetch=5,
            grid=(N_ITEMS, D_FFE // TF_MOE),
            in_specs=[pl.BlockSpec(memory_space=pl.ANY), wspec, wspec],
            out_specs=pl.BlockSpec((RT, TF_MOE), lambda k, j, s, ie, ics, ic, ni: (k, j)),
            scratch_shapes=[
                pltpu.VMEM((2, RT // SUBLANES, SUBLANES, HALF), U32),
                pltpu.SemaphoreType.DMA((2,)),
            ],
        ),
        out_shape=jax.ShapeDtypeStruct((N_ITEMS * RT, D_FFE), BF16),
        compiler_params=_cparams(("arbitrary", "arbitrary")),
        name="moe_up",
    )(src, item_e, item_cs, item_cnt, n_items, hp, w1, w3)


def _moe_down_kernel(ie_ref, icnt_ref, nit_ref, g_ref, w2a_ref, w2b_ref, o_ref):
    k = pl.program_id(0)
    o_ref[...] = jnp.zeros_like(o_ref)

    def mm_rows(start, size):
        rows = pl.ds(pl.multiple_of(start, size), size)
        g = g_ref[rows, :]
        lo = jnp.dot(g, w2a_ref[...].astype(BF16), preferred_element_type=F32)
        hi = jnp.dot(g, w2b_ref[...].astype(BF16), preferred_element_type=F32)
        o_ref[rows, :] = _pack(lo, hi)

    _for_item_rows(icnt_ref[k], k < nit_ref[0], mm_rows)


def _moe_down_call(layer_e0, item_e, item_cnt, n_items, g, w2):
    live = lambda k, ni: jnp.minimum(k, ni[0] - 1)
    n_half = HALF // TN_MOE
    col = lambda k, n, ni: jnp.where(k < ni[0], n, n_half - 1)
    return pl.pallas_call(
        _moe_down_kernel,
        grid_spec=pltpu.PrefetchScalarGridSpec(
            num_scalar_prefetch=3,
            grid=(N_ITEMS, n_half),
            in_specs=[
                pl.BlockSpec((RT, D_FFE), lambda k, n, ie, ic, ni: (live(k, ni), 0)),
                pl.BlockSpec((None, D_FFE, TN_MOE),
                             lambda k, n, ie, ic, ni: (layer_e0 + ie[k], 0, col(k, n, ni))),
                pl.BlockSpec((None, D_FFE, TN_MOE),
                             lambda k, n, ie, ic, ni: (layer_e0 + ie[k], 0, n_half + col(k, n, ni))),
            ],
            out_specs=pl.BlockSpec((RT, TN_MOE), lambda k, n, ie, ic, ni: (k, n)),
        ),
        out_shape=jax.ShapeDtypeStruct((N_ITEMS * RT, HALF), U32),
        compiler_params=_cparams(("arbitrary", "arbitrary")),
        name="moe_down",
    )(item_e, item_cnt, n_items, g, w2, w2)


def _combine_copy(yo_hbm, row, buf, slot, q, u, sem):
    return pltpu.make_async_copy(yo_hbm.at[pl.ds(row, 1)], buf.at[slot, q, pl.ds(u, 1)], sem.at[slot])


def _combine_kernel(split, pos_ref, yo_hbm, x_ref, rinfo_ref, mods_ref, *rest):
    if split:
        yp_ref, ys_ref, buf, sem = rest
    else:
        o_ref, buf, sem = rest
    i = pl.program_id(0)
    n = pl.num_programs(0)
    rows = TOP_K * TM

    def issue(tile, slot):
        def body(q, c):
            for u in range(SUBLANES):
                _combine_copy(yo_hbm, pos_ref[tile * rows + q * SUBLANES + u], buf, slot, q, u, sem).start()
            return c
        lax.fori_loop(0, rows // SUBLANES, body, 0)

    @pl.when(i == 0)
    def _():
        issue(0, 0)

    @pl.when(i + 1 < n)
    def _():
        issue(i + 1, (i + 1) & 1)

    slot = i & 1

    def body(r, c):
        _combine_copy(yo_hbm, 0, buf, slot, 0, 0, sem).wait()
        return c
    lax.fori_loop(0, rows, body, 0, unroll=8)
    rinfo = rinfo_ref[...]
    tiles = TM // SUBLANES
    lo0, hi0 = _unpack(buf[slot, 0:tiles].reshape(TM, HALF))
    lo1, hi1 = _unpack(buf[slot, tiles:2 * tiles].reshape(TM, HALF))
    gt0, gt1 = rinfo[:, 2:3], rinfo[:, 3:4]
    g2 = mods_ref[pl.ds(_mod_row(i), 1), :]
    y_lo = x_ref[:, 0:HALF] + g2[:, 0:HALF] * (gt0 * lo0 + gt1 * lo1)
    y_hi = x_ref[:, HALF:D] + g2[:, HALF:D] * (gt0 * hi0 + gt1 * hi1)

    def write(ref):
        ref[:, 0:HALF] = y_lo
        ref[:, HALF:D] = y_hi

    if not split:
        write(o_ref)
        return

    @pl.when(i < P_TILES)
    def _():
        write(yp_ref)

    @pl.when(i >= P_TILES)
    def _():
        write(ys_ref)


def _combine_call(l, split, pos, yo, x, rinfo, mods_all):
    if split:
        out_specs = [pl.BlockSpec((TM, D), lambda i, p: (jnp.minimum(i, P_TILES - 1), 0)),
                     pl.BlockSpec((TM, D), lambda i, p: (jnp.maximum(i - P_TILES, 0), 0))]
        out_shape = [jax.ShapeDtypeStruct((M_P, D), F32), jax.ShapeDtypeStruct((M_S, D), F32)]
    else:
        out_specs = pl.BlockSpec((TM, D), lambda i, p: (i, 0))
        out_shape = jax.ShapeDtypeStruct((M_TOK, D), F32)
    return pl.pallas_call(
        functools.partial(_combine_kernel, split),
        grid_spec=pltpu.PrefetchScalarGridSpec(
            num_scalar_prefetch=1,
            grid=(N_TILES,),
            in_specs=[
                pl.BlockSpec(memory_space=pl.ANY),
                pl.BlockSpec((TM, D), lambda i, p: (i, 0)),
                pl.BlockSpec((TM, LANE), lambda i, p: (i, 0)),
                pl.BlockSpec((None, MOD_ROWS, D), lambda i, p: (l, 0, 5)),
            ],
            out_specs=out_specs,
            scratch_shapes=[pltpu.VMEM((2, TOP_K * TM // SUBLANES, SUBLANES, HALF), U32),
                            pltpu.SemaphoreType.DMA((2,))],
        ),
        out_shape=out_shape,
        compiler_params=_cparams(("arbitrary",)),
        name="moe_combine",
    )(pos, yo, x, rinfo, mods_all)


def _route_plan(rinfo):
    flat_e = rinfo[:, 0:TOP_K].astype(I32).reshape(-1)
    onehot = (flat_e[:, None] == jnp.arange(N_EXP, dtype=I32)[None, :]).astype(I32)
    csum = jnp.cumsum(onehot, axis=0)
    pick = lambda table: jnp.sum(onehot * table[None, :], axis=1)
    rank = jnp.sum(onehot * csum, axis=1) - 1
    counts = csum[-1]
    cstart = jnp.cumsum(counts) - counts
    n_it = (counts + RT - 1) // RT
    it_end = jnp.cumsum(n_it)
    it_base = it_end - n_it
    n_items = it_end[-1]
    cpos = pick(cstart) + rank
    ppos = (pick(it_base) + rank // RT) * RT + rank % RT
    q = jnp.arange(N_ITEMS, dtype=I32)
    qc = jnp.minimum(q, n_items - 1)
    qe = jnp.minimum(jnp.sum((it_end[None, :] <= qc[:, None]).astype(I32), axis=1), N_EXP - 1)
    qhot = (qe[:, None] == jnp.arange(N_EXP, dtype=I32)[None, :]).astype(I32)
    qpick = lambda table: jnp.sum(qhot * table[None, :], axis=1)
    s = qc - qpick(it_base)
    item_cnt = jnp.where(q < n_items, jnp.clip(qpick(counts) - s * RT, 0, RT), 0)
    item_cs = qpick(cstart) + s * RT
    pos_tiles = ppos.reshape(N_TILES, TM, TOP_K).transpose(0, 2, 1).reshape(-1)
    return (cpos.astype(I32), qe.astype(I32), item_cs.astype(I32), item_cnt.astype(I32),
            n_items.reshape(1).astype(I32), pos_tiles.astype(I32))


def _rope_tables():
    rows = DEC_SEQ // GRID_W
    row = jnp.repeat(jnp.arange(rows), GRID_W).astype(F32)
    col = jnp.tile(jnp.arange(GRID_W), rows).astype(F32)
    nf = ROPE // 4
    inv = ROPE_BASE ** (-jnp.arange(nf, dtype=F32) / nf)
    ar = row[:, None] * inv[None, :]
    ac = col[:, None] * inv[None, :]
    ang = jnp.concatenate([ar, ar, ac, ac], axis=-1)
    cos, sin = jnp.cos(ang), jnp.sin(ang)
    first = (jnp.arange(ROPE) % 32) < 16
    pad = lambda a, v: jnp.pad(a, ((0, 0), (0, LANE - ROPE)), constant_values=v)
    cos_p = pad(cos, 1.0)
    sin_a = pad(jnp.where(first[None, :], -sin, 0.0), 0.0)
    sin_b = pad(jnp.where(first[None, :], 0.0, sin), 0.0)
    ident = jnp.stack([jnp.ones((TMI, LANE), F32), jnp.zeros((TMI, LANE), F32), jnp.zeros((TMI, LANE), F32)])
    return jnp.concatenate([ident, jnp.stack([cos_p, sin_a, sin_b])], axis=1)


def kernel(x_prompt, x_sample, cache_ckv, cache_krope, c, c_ctx, w_ada, b_ada, norm1_g, norm2_g, w_in,
           qlat_norm_g, kvlat_norm_g, w_uq, w_ukv, q_head_norm_g, k_head_norm_g, gm_norm_g, gm_w_s,
           gm_b_s, pool_w, pool_scale, w_o, ffn_w1, ffn_w3, ffn_w2, router_w, moe_w1, moe_w3, moe_w2):
    pad_w = IN_PAD - w_in.shape[-1]
    col = jnp.arange(IN_PAD)[None, None, :]
    w_in_p = jnp.where(
        col < C_KR + ROPE, jnp.pad(w_in, ((0, 0), (0, 0), (0, pad_w))),
        jnp.where(col < C_KR + LANE, 0.0, jnp.pad(w_in, ((0, 0), (0, 0), (pad_w, 0))))).astype(BF16)
    w_uq_p = jnp.pad(w_uq.reshape(DEPTH, Q_RANK, N_HEADS, QK_DIM),
                     ((0, 0), (0, 0), (0, 0), (0, HEAD_PAD - QK_DIM))
                     ).reshape(DEPTH, Q_RANK, QK_W).astype(BF16)
    w_ukv_b = w_ukv.astype(BF16)
    w_o_b = w_o.astype(BF16)
    qg_p = jnp.pad(q_head_norm_g * (1.0 / math.sqrt(QK_DIM)),
                   ((0, 0), (0, HEAD_PAD - QK_DIM))).reshape(DEPTH, 1, HEAD_PAD)
    kg_n = k_head_norm_g[:, :NOPE].reshape(DEPTH, 1, LANE)
    kg_r = jnp.pad(k_head_norm_g[:, NOPE:], ((0, 0), (0, LANE - ROPE))).reshape(DEPTH, 1, LANE)
    gmw_b = gm_w_s.astype(BF16)
    gmb = jnp.broadcast_to(gm_b_s[:, :, :, None], (DEPTH, GM_GROUPS, CHUNK, LANE))
    pw_b = pool_w.astype(BF16)
    ps = pool_scale.reshape(DEPTH, 1, POOL_W)
    n1g = norm1_g.reshape(DEPTH, 1, D)
    n2g = norm2_g.reshape(DEPTH, 1, D)
    qlg = qlat_norm_g.reshape(DEPTH, 1, Q_RANK)
    kvlg = kvlat_norm_g.reshape(DEPTH, 1, KV_RANK)
    rw_p = jnp.pad(router_w, ((0, 0), (0, 0), (0, LANE - N_EXP)))
    n_moe = moe_w1.shape[0]
    moe_w1_flat = moe_w1.reshape(n_moe * N_EXP, D, D_FFE)
    moe_w3_flat = moe_w3.reshape(n_moe * N_EXP, D, D_FFE)
    moe_w2_flat = moe_w2.reshape(n_moe * N_EXP, D_FFE, D)
    rope_tab = _rope_tables()
    cache_kr_p = jnp.pad(cache_krope, ((0, 0), (0, 0), (0, 0), (0, LANE - ROPE)))

    cond = jnp.zeros((MOD_ROWS, D), F32).at[0].set(c_ctx).at[1:1 + DEC_BATCH].set(c)
    mods_all = _ada_call(cond, w_ada, b_ada)
    k_c, v_c = _cache_kv_call(cache_ckv, cache_kr_p, w_ukv_b, kg_n, kg_r)

    x = (x_prompt.reshape(M_P, D), x_sample.reshape(M_S, D))
    ckv_out, kr_out = [], []
    for l in range(DEPTH):
        q, k, v, ckv, kr, zg, zp = _in_proj_call(
            l, x, mods_all, n1g, w_in_p, qlg, kvlg, w_uq_p, qg_p, w_ukv_b, kg_n, kg_r, rope_tab,
            gm_norm_g, gmw_b, gmb)
        ckv_out.append(ckv[:M_P].reshape(BATCH, SEQ, KV_RANK))
        kr_out.append(kr[:M_P, :ROPE].reshape(BATCH, SEQ, ROPE))
        attn = _attn_call(l, q, k, v, k_c, v_c)
        i = l // 2
        if l % 2 == 0:
            x, h2b = _out_proj_call(l, False, attn, zg, zp, x, mods_all, pw_b, ps, w_o_b, n2g)
            g = _ffn_up_call(i, h2b, ffn_w1, ffn_w3)
            x = _ffn_down_call(l, g, ffn_w2, x, mods_all)
        else:
            x, hp, rinfo = _out_proj_call(l, True, attn, zg, zp, x, mods_all, pw_b, ps, w_o_b, n2g, rw_p)
            cpos, item_e, item_cs, item_cnt, n_items, pos_tiles = _route_plan(rinfo)
            src = _src_call(cpos)
            g = _moe_up_call(i * N_EXP, src, item_e, item_cs, item_cnt, n_items, hp,
                             moe_w1_flat, moe_w3_flat)
            yo = _moe_down_call(i * N_EXP, item_e, item_cnt, n_items, g, moe_w2_flat)
            x = _combine_call(l, l == DEPTH - 1, pos_tiles, yo, x, rinfo, mods_all)

    if DEPTH % 2 == 0:
        y_p, y_s = x
    else:
        y_p, y_s = x[:M_P], x[M_P:]
    return (y_p.reshape(BATCH, SEQ, D), y_s.reshape(DEC_BATCH, DEC_SEQ, D),
            jnp.stack(ckv_out, axis=1), jnp.stack(kr_out, axis=1))
```

```python
import functools
import math

import jax
import jax.numpy as jnp
from jax import lax
from jax.experimental import pallas as pl
from jax.experimental.pallas import tpu as pltpu

F32 = jnp.float32
BF16 = jnp.bfloat16
U32 = jnp.uint32
I32 = jnp.int32

D = 2048
BATCH, SEQ = 16, 256
DEC_BATCH, DEC_SEQ = 2, 1024
PAST = 256
DEPTH = 4
GRID_W = 64
EPS = 1e-6
N_HEADS = 8
NOPE, ROPE, VDIM = 128, 64, 128
QK_DIM = NOPE + ROPE
HEAD_PAD = 256
QK_W = N_HEADS * HEAD_PAD
Q_RANK, KV_RANK = 512, 256
ROPE_BASE = 10000.0
ATTN_W = N_HEADS * VDIM
CHUNK, GM_GROUPS, GM_W = 128, 4, 512
POOL_WINDOWS = (2, 4, 8, 16)
POOL_W = 512
POOL_HALO = 8
D_FF, N_EXP, D_FFE, TOP_K = 5632, 8, 2816, 2
TOP_K_SHIFT = 1

C_Q, C_KV, C_KR, C_GM, C_PL = 0, 512, 768, 896, 1920
IN_PAD = 2432

LANE = 128
SUBLANES = 8
TM = 256
TMI = 512
TMO = 256
M_P, M_S = BATCH * SEQ, DEC_BATCH * DEC_SEQ
M_TOK = M_P + M_S
N_TILES = M_TOK // TM
P_TILES = M_P // TM
S_TILES = DEC_SEQ // TM
MOD_ROWS = 16
ADA_TN = 1024
TE_UP, TF_UP = 1024, 512
TE_DOWN, TN_DOWN = 512, 512
HALF = D // 2
N_SLOTS = TOP_K * M_TOK
RT = 2048
N_ITEMS = N_EXP + (N_SLOTS - N_EXP) // RT
GC = 256
N_GC = RT // GC
MC = 1024
TF_MOE = 256
TN_MOE = 256
V7X_VMEM_BYTES = 64 * 1024 * 1024
VMEM_LIMIT = V7X_VMEM_BYTES * 7 // 8
assert N_GC + 1 <= D_FFE // TF_MOE


def _cparams(sem):
    return pltpu.CompilerParams(dimension_semantics=sem, vmem_limit_bytes=VMEM_LIMIT)


def _layer_spec(l, shape):
    nd = len(shape)
    return pl.BlockSpec((None,) + tuple(shape), lambda *_: (l,) + (0,) * nd)


def _x_operands(x, tile):
    if not isinstance(x, tuple):
        return [x], [pl.BlockSpec((tile, D), lambda i: (i, 0))]
    ctx = M_P // tile
    return list(x), [pl.BlockSpec((tile, D), lambda i: (jnp.minimum(i, ctx - 1), 0)),
                     pl.BlockSpec((tile, D), lambda i: (jnp.maximum(i - ctx, 0), 0))]


def _mod_row(i):
    return jnp.where(i < P_TILES, 0, 1 + (i - P_TILES) // S_TILES)


def _rms(x, g, n):
    ms = jnp.sum(x * x, axis=-1, keepdims=True) * (1.0 / n)
    return x * lax.rsqrt(ms + EPS) * g


def _rope(r, cos, sin_a, sin_b):
    return r * cos + pltpu.roll(r, LANE - 16, 1) * sin_a + pltpu.roll(r, 16, 1) * sin_b


def _pack(lo, hi):
    lo_b = lax.bitcast_convert_type(lo.astype(BF16).astype(F32), U32)
    hi_b = lax.bitcast_convert_type(hi.astype(BF16).astype(F32), U32)
    return (lo_b >> 16) | hi_b


def _unpack(w):
    lo = lax.bitcast_convert_type(w << 16, F32)
    hi = lax.bitcast_convert_type(w & jnp.uint32(0xFFFF0000), F32)
    return lo, hi


def _ada_kernel(cond_ref, w_ref, b_ref, o_ref):
    c = cond_ref[...]
    s = c * jax.nn.sigmoid(c)
    s_hi = s.astype(BF16)
    s_lo = (s - s_hi.astype(F32)).astype(BF16)
    w = w_ref[...]
    w_hi = w.astype(BF16)
    w_lo = (w - w_hi.astype(F32)).astype(BF16)
    lhs = jnp.concatenate([s_hi, s_lo], axis=0)
    r1 = jnp.dot(lhs, w_hi, preferred_element_type=F32)
    r2 = jnp.dot(s_hi, w_lo, preferred_element_type=F32)
    o_ref[...] = r1[:MOD_ROWS] + r1[MOD_ROWS:] + r2 + b_ref[...]


def _ada_call(cond, w_ada, b_ada):
    n = 6 * D
    return pl.pallas_call(
        _ada_kernel,
        grid=(DEPTH, n // ADA_TN),
        in_specs=[
            pl.BlockSpec((MOD_ROWS, D), lambda l, j: (0, 0)),
            pl.BlockSpec((None, D, ADA_TN), lambda l, j: (l, 0, j)),
            pl.BlockSpec((None, 1, ADA_TN), lambda l, j: (l, 0, j)),
        ],
        out_specs=pl.BlockSpec((None, MOD_ROWS, ADA_TN), lambda l, j: (l, 0, j)),
        out_shape=jax.ShapeDtypeStruct((DEPTH, MOD_ROWS, n), F32),
        compiler_params=_cparams(("parallel", "parallel")),
        name="ada",
    )(cond, w_ada, b_ada.reshape(DEPTH, 1, n))


def _write_kv(ckv_b, kr, w_ukv_ref, kg_n, kg_r, rope, k_ref, v_ref, rows=slice(None)):
    kv = jnp.dot(ckv_b, w_ukv_ref[...], preferred_element_type=F32)
    ssr = jnp.sum(kr * kr, axis=-1, keepdims=True)
    shared = kr * kg_r
    if rope is not None:
        shared = _rope(shared, *rope)
    for h in range(N_HEADS):
        kn = kv[:, h * HEAD_PAD:h * HEAD_PAD + NOPE]
        vv = kv[:, h * HEAD_PAD + NOPE:(h + 1) * HEAD_PAD]
        ms = (jnp.sum(kn * kn, axis=-1, keepdims=True) + ssr) * (1.0 / QK_DIM)
        inv = lax.rsqrt(ms + EPS)
        k_ref[rows, h * HEAD_PAD:h * HEAD_PAD + NOPE] = (kn * inv * kg_n).astype(BF16)
        k_ref[rows, h * HEAD_PAD + NOPE:(h + 1) * HEAD_PAD] = (shared * inv).astype(BF16)
        v_ref[rows, h * VDIM:(h + 1) * VDIM] = vv.astype(BF16)


def _cache_kv_kernel(ckv_ref, kr_ref, w_ukv_ref, kgn_ref, kgr_ref, k_ref, v_ref):
    _write_kv(ckv_ref[...].astype(BF16), kr_ref[...], w_ukv_ref, kgn_ref[...], kgr_ref[...],
              None, k_ref, v_ref)


def _cache_kv_call(cache_ckv, cache_kr_p, w_ukv_b, kg_n, kg_r):
    return pl.pallas_call(
        _cache_kv_kernel,
        grid=(DEPTH, DEC_BATCH),
        in_specs=[
            pl.BlockSpec((None, None, PAST, KV_RANK), lambda l, b: (b, l, 0, 0)),
            pl.BlockSpec((None, None, PAST, LANE), lambda l, b: (b, l, 0, 0)),
            pl.BlockSpec((None, KV_RANK, QK_W), lambda l, b: (l, 0, 0)),
            pl.BlockSpec((None, 1, LANE), lambda l, b: (l, 0, 0)),
            pl.BlockSpec((None, 1, LANE), lambda l, b: (l, 0, 0)),
        ],
        out_specs=[
            pl.BlockSpec((None, None, PAST, QK_W), lambda l, b: (l, b, 0, 0)),
            pl.BlockSpec((None, None, PAST, ATTN_W), lambda l, b: (l, b, 0, 0)),
        ],
        out_shape=[
            jax.ShapeDtypeStruct((DEPTH, DEC_BATCH, PAST, QK_W), BF16),
            jax.ShapeDtypeStruct((DEPTH, DEC_BATCH, PAST, ATTN_W), BF16),
        ],
        compiler_params=_cparams(("parallel", "parallel")),
        name="cache_kv",
    )(cache_ckv, cache_kr_p, w_ukv_b, kg_n, kg_r)


def _in_proj_kernel(two_x, *refs):
    x_refs, refs = refs[:1 + two_x], refs[1 + two_x:]
    _in_proj_body(x_refs, *refs)


def _in_proj_body(x_refs, mods_ref, n1g_ref, w_in_ref, qlg_ref, kvlg_ref, w_uq_ref, qg_ref,
                    w_ukv_ref, kgn_ref, kgr_ref, rope_ref, gmg_ref, gmw_ref, gmb_ref,
                    q_ref, k_ref, v_ref, ckv_ref, kr_ref, zg_ref, zp_ref):
    i = pl.program_id(0)
    row = jnp.where(i < M_P // TMI, 0, 1 + (i - M_P // TMI) // (DEC_SEQ // TMI))
    m = mods_ref[pl.ds(row, 1), :]
    sh1, sc1 = m[:, 0:D], m[:, D:2 * D]
    for sub in range(TMI // TM):
        rs = slice(sub * TM, (sub + 1) * TM)
        x = x_refs[0][rs, :] if len(x_refs) == 1 else jnp.where(i < M_P // TMI, x_refs[0][rs, :], x_refs[1][rs, :])
        h = _rms(x, n1g_ref[...], D) * (1.0 + sc1) + sh1
        z = jnp.dot(h.astype(BF16), w_in_ref[...], preferred_element_type=F32)
        rope = (rope_ref[0, rs, :], rope_ref[1, rs, :], rope_ref[2, rs, :])

        ql = _rms(z[:, C_Q:C_Q + Q_RANK], qlg_ref[...], Q_RANK)
        qf = jnp.dot(ql.astype(BF16), w_uq_ref[...], preferred_element_type=F32)
        qg = qg_ref[...]
        for hd in range(N_HEADS):
            seg = qf[:, hd * HEAD_PAD:(hd + 1) * HEAD_PAD]
            seg = _rms(seg, qg, QK_DIM)
            q_ref[rs, hd * HEAD_PAD:hd * HEAD_PAD + NOPE] = seg[:, :NOPE].astype(BF16)
            q_ref[rs, hd * HEAD_PAD + NOPE:(hd + 1) * HEAD_PAD] = _rope(seg[:, NOPE:], *rope).astype(BF16)

        ckv = _rms(z[:, C_KV:C_KV + KV_RANK], kvlg_ref[...], KV_RANK)
        kr = z[:, C_KR:C_KR + LANE]
        ckv_ref[rs, :] = ckv
        kr_ref[rs, :] = kr
        _write_kv(ckv.astype(BF16), kr, w_ukv_ref, kgn_ref[...], kgr_ref[...], rope, k_ref, v_ref, rs)

        zg = jax.nn.gelu(z[:, C_GM:C_GM + 2 * GM_W], approximate=True)
        for g in range(GM_GROUPS):
            u = zg[:, g * LANE:(g + 1) * LANE]
            vg = _rms(zg[:, GM_W + g * LANE:GM_W + (g + 1) * LANE], gmg_ref[g:g + 1, :], LANE)
            vb = vg.astype(BF16)
            for c in range(TM // CHUNK):
                sv = jnp.dot(gmw_ref[g], vb[c * CHUNK:(c + 1) * CHUNK, :], preferred_element_type=F32)
                sv = sv + gmb_ref[g]
                r0 = sub * TM + c * CHUNK
                zg_ref[r0:r0 + CHUNK, g * LANE:(g + 1) * LANE] = (
                    u[c * CHUNK:(c + 1) * CHUNK, :] * sv).astype(BF16)

        zp_ref[rs, :] = z[:, C_PL:C_PL + POOL_W]


def _in_proj_call(l, x, mods_all, n1g, w_in_b, qlg, kvlg, w_uq_b, qg, w_ukv_b, kg_n, kg_r, rope_tab,
                  gmg, gmw_b, gmb):
    def rope_idx(i):
        return (0, jnp.where(i < M_P // TMI, 0, 1 + (i - M_P // TMI) % (DEC_SEQ // TMI)), 0)

    row = lambda w: pl.BlockSpec((TMI, w), lambda i: (i, 0))
    two_x = isinstance(x, tuple)
    x_args, x_specs = _x_operands(x, TMI)
    return pl.pallas_call(
        functools.partial(_in_proj_kernel, two_x),
        grid=(M_TOK // TMI,),
        in_specs=x_specs + [
            _layer_spec(l, (MOD_ROWS, 6 * D)),
            _layer_spec(l, (1, D)),
            _layer_spec(l, (D, IN_PAD)),
            _layer_spec(l, (1, Q_RANK)),
            _layer_spec(l, (1, KV_RANK)),
            _layer_spec(l, (Q_RANK, QK_W)),
            _layer_spec(l, (1, HEAD_PAD)),
            _layer_spec(l, (KV_RANK, QK_W)),
            _layer_spec(l, (1, LANE)),
            _layer_spec(l, (1, LANE)),
            pl.BlockSpec((3, TMI, LANE), rope_idx),
            _layer_spec(l, (GM_GROUPS, LANE)),
            _layer_spec(l, (GM_GROUPS, CHUNK, CHUNK)),
            _layer_spec(l, (GM_GROUPS, CHUNK, LANE)),
        ],
        out_specs=[row(QK_W), row(QK_W), row(ATTN_W), row(KV_RANK), row(LANE), row(GM_W), row(POOL_W)],
        out_shape=[
            jax.ShapeDtypeStruct((M_TOK, QK_W), BF16),
            jax.ShapeDtypeStruct((M_TOK, QK_W), BF16),
            jax.ShapeDtypeStruct((M_TOK, ATTN_W), BF16),
            jax.ShapeDtypeStruct((M_TOK, KV_RANK), F32),
            jax.ShapeDtypeStruct((M_TOK, LANE), F32),
            jax.ShapeDtypeStruct((M_TOK, GM_W), BF16),
            jax.ShapeDtypeStruct((M_TOK, POOL_W), F32),
        ],
        compiler_params=_cparams(("parallel",)),
        name="in_proj",
    )(*x_args, mods_all, n1g, w_in_b, qlg, kvlg, w_uq_b, qg, w_ukv_b, kg_n, kg_r, rope_tab, gmg, gmw_b, gmb)


def _attend_heads(q_ref, kv_pairs, o_ref):
    nt = (((1,), (1,)), ((), ()))
    for h in range(N_HEADS):
        qh = q_ref[:, h * HEAD_PAD:(h + 1) * HEAD_PAD]
        ss = [lax.dot_general(qh, k_ref[:, h * HEAD_PAD:(h + 1) * HEAD_PAD], nt,
                              preferred_element_type=F32) for k_ref, _ in kv_pairs]
        mx = ss[0].max(axis=-1, keepdims=True)
        for s in ss[1:]:
            mx = jnp.maximum(mx, s.max(axis=-1, keepdims=True))
        den = None
        acc = None
        for s, (_, v_ref) in zip(ss, kv_pairs):
            p = jnp.exp(s - mx)
            d = jnp.sum(p, axis=-1, keepdims=True)
            a = jnp.dot(p.astype(BF16), v_ref[:, h * VDIM:(h + 1) * VDIM], preferred_element_type=F32)
            den = d if den is None else den + d
            acc = a if acc is None else acc + a
        o_ref[:, h * VDIM:(h + 1) * VDIM] = (acc / den).astype(BF16)


def _attn_kernel(q_ref, kp_ref, vp_ref, ks_ref, vs_ref, kc_ref, vc_ref, o_ref):
    i = pl.program_id(0)

    @pl.when(i < P_TILES)
    def _():
        _attend_heads(q_ref, [(kp_ref, vp_ref)], o_ref)

    @pl.when(i >= P_TILES)
    def _():
        _attend_heads(q_ref, [(ks_ref, vs_ref), (kc_ref, vc_ref)], o_ref)


def _attn_call(l, q, k, v, k_c, v_c):
    def seq_b(i):
        return jnp.maximum(i - P_TILES, 0) // S_TILES

    small = lambda w: pl.BlockSpec((TM, w), lambda i: (jnp.minimum(i, P_TILES - 1), 0))
    big = lambda w: pl.BlockSpec((DEC_SEQ, w), lambda i: (M_P // DEC_SEQ + seq_b(i), 0))
    cache = lambda w: pl.BlockSpec((None, None, PAST, w), lambda i: (l, seq_b(i), 0, 0))
    return pl.pallas_call(
        _attn_kernel,
        grid=(N_TILES,),
        in_specs=[
            pl.BlockSpec((TM, QK_W), lambda i: (i, 0)),
            small(QK_W), small(ATTN_W),
            big(QK_W), big(ATTN_W),
            cache(QK_W), cache(ATTN_W),
        ],
        out_specs=pl.BlockSpec((TM, ATTN_W), lambda i: (i, 0)),
        out_shape=jax.ShapeDtypeStruct((M_TOK, ATTN_W), BF16),
        compiler_params=_cparams(("parallel",)),
        name="attention",
    )(q, k, v, k, v, k_c, v_c)


def _out_proj_kernel(moe, two_x, attn_ref, zg_ref, zprev_ref, zcur_ref, znext_ref, *rest):
    x_refs, rest = rest[:1 + two_x], rest[1 + two_x:]
    mods_ref, pw_ref, ps_ref, w_o_ref, n2g_ref = rest[:5]
    rest = rest[5:]
    if moe:
        rw_ref, xo_ref, hp_ref, rinfo_ref, ext_ref, mix_ref = rest
    else:
        xo_ref, h2b_ref, ext_ref, mix_ref = rest
    i = pl.program_id(0)
    ctx_steps = M_P // TMO
    is_ctx = i < ctx_steps
    seq_len = jnp.where(is_ctx, SEQ, DEC_SEQ)
    m = mods_ref[pl.ds(jnp.where(is_ctx, 0, 1 + (i - ctx_steps) // (DEC_SEQ // TMO)), 1), :]
    g1, sh2, sc2 = m[:, 2 * D:3 * D], m[:, 3 * D:4 * D], m[:, 4 * D:5 * D]

    pos0 = jnp.where(is_ctx, 0, ((i - ctx_steps) % (DEC_SEQ // TMO)) * TMO)
    zero_halo = jnp.zeros((POOL_HALO, POOL_W), F32)
    ext_ref[0:POOL_HALO, :] = jnp.where(pos0 == 0, zero_halo, zprev_ref[TMO - POOL_HALO:, :])
    ext_ref[POOL_HALO:POOL_HALO + TMO, :] = zcur_ref[...]
    ext_ref[POOL_HALO + TMO:, :] = jnp.where(pos0 + TMO == seq_len, zero_halo, znext_ref[0:POOL_HALO, :])
    mix_ref[:, 0:ATTN_W] = attn_ref[...]
    mix_ref[:, ATTN_W:ATTN_W + GM_W] = zg_ref[...]

    assert TMO == TM and SEQ % TMO == 0 and DEC_SEQ % TMO == 0
    for sub in range(TMO // TM):
        rs = slice(sub * TM, (sub + 1) * TM)
        pos = pos0 + lax.broadcasted_iota(I32, (TM, 1), 0)
        for gi, w in enumerate(POOL_WINDOWS):
            cols = slice(gi * LANE, (gi + 1) * LANE)
            win = None
            for dlt in range(-(w // 2), w - w // 2):
                r0 = POOL_HALO + dlt
                part = ext_ref[r0:r0 + TM, cols]
                win = part if win is None else win + part
            lo = jnp.maximum(pos - w // 2, 0)
            hi = jnp.minimum(pos + (w - 1 - w // 2), seq_len - 1)
            cnt = (hi - lo + 1).astype(F32)
            pooled = win / cnt - zcur_ref[rs, cols]
            mixed = jnp.dot(pooled.astype(BF16), pw_ref[gi], preferred_element_type=F32)
            mix_ref[rs, ATTN_W + GM_W + gi * LANE:ATTN_W + GM_W + (gi + 1) * LANE] = (
                mixed * ps_ref[:, cols]).astype(BF16)

        out = jnp.dot(mix_ref[rs, :], w_o_ref[...], preferred_element_type=F32)
        x = x_refs[0][rs, :] if len(x_refs) == 1 else jnp.where(is_ctx, x_refs[0][rs, :], x_refs[1][rs, :])
        xn = x + g1 * out
        xo_ref[rs, :] = xn
        h2 = _rms(xn, n2g_ref[...], D) * (1.0 + sc2) + sh2

        if not moe:
            h2b_ref[rs, :] = h2.astype(BF16)
            continue

        hp_ref[rs, :] = _pack(h2[:, :HALF], h2[:, HALF:])
        h_hi = h2.astype(BF16)
        h_lo = (h2 - h_hi.astype(F32)).astype(BF16)
        rw = rw_ref[...]
        w_hi = rw.astype(BF16)
        w_lo = (rw - w_hi.astype(F32)).astype(BF16)
        logits = (jnp.dot(h_hi, w_hi, preferred_element_type=F32)
                  + jnp.dot(h_lo, w_hi, preferred_element_type=F32)
                  + jnp.dot(h_hi, w_lo, preferred_element_type=F32))
        lane = lax.broadcasted_iota(I32, (TM, LANE), 1).astype(F32)
        neg = jnp.float32(-jnp.inf)
        logits = jnp.where(lane < N_EXP, logits, neg)
        v1 = logits.max(axis=-1, keepdims=True)
        i1 = jnp.where(logits == v1, lane, float(LANE)).min(axis=-1, keepdims=True)
        l2 = jnp.where(lane == i1, neg, logits)
        v2 = l2.max(axis=-1, keepdims=True)
        i2 = jnp.where(l2 == v2, lane, float(LANE)).min(axis=-1, keepdims=True)
        e2 = jnp.exp(v2 - v1)
        gate1 = 1.0 / (1.0 + e2)
        gate2 = e2 / (1.0 + e2)
        rinfo_ref[rs, :] = jnp.where(lane == 0, i1, jnp.where(lane == 1, i2, jnp.where(
            lane == 2, gate1, jnp.where(lane == 3, gate2, 0.0))))


def _out_proj_call(l, moe, attn, zg, zp, x, mods_all, pw_b, ps, w_o_b, n2g, rw_p=None):
    n_steps = M_TOK // TMO
    row = lambda w: pl.BlockSpec((TMO, w), lambda i: (i, 0))
    two_x = isinstance(x, tuple)
    x_args, x_specs = _x_operands(x, TMO)
    in_specs = [
        row(ATTN_W), row(GM_W),
        pl.BlockSpec((TMO, POOL_W), lambda i: (jnp.maximum(i - 1, 0), 0)),
        row(POOL_W),
        pl.BlockSpec((TMO, POOL_W), lambda i: (jnp.minimum(i + 1, n_steps - 1), 0)),
    ] + x_specs + [
        _layer_spec(l, (MOD_ROWS, 6 * D)),
        _layer_spec(l, (len(POOL_WINDOWS), LANE, LANE)),
        _layer_spec(l, (1, POOL_W)),
        _layer_spec(l, (D, D)),
        _layer_spec(l, (1, D)),
    ]
    args = [attn, zg, zp, zp, zp, *x_args, mods_all, pw_b, ps, w_o_b, n2g]
    if moe:
        in_specs.append(_layer_spec(l // 2, (D, LANE)))
        args.append(rw_p)
        out_specs = [row(D), row(HALF), row(LANE)]
        out_shape = [jax.ShapeDtypeStruct((M_TOK, D), F32), jax.ShapeDtypeStruct((M_TOK, HALF), U32),
                     jax.ShapeDtypeStruct((M_TOK, LANE), F32)]
    else:
        out_specs = [row(D), row(D)]
        out_shape = [jax.ShapeDtypeStruct((M_TOK, D), F32), jax.ShapeDtypeStruct((M_TOK, D), BF16)]
    return pl.pallas_call(
        functools.partial(_out_proj_kernel, moe, two_x),
        grid=(n_steps,),
        in_specs=in_specs,
        out_specs=out_specs,
        out_shape=out_shape,
        scratch_shapes=[pltpu.VMEM((TMO + 2 * POOL_HALO, POOL_W), F32), pltpu.VMEM((TMO, D), BF16)],
        compiler_params=_cparams(("parallel",)),
        name="out_proj_moe" if moe else "out_proj",
    )(*args)


def _ffn_up_kernel(h_ref, w1_ref, w3_ref, g_ref):
    h = h_ref[...]
    a = jnp.dot(h, w1_ref[...].astype(BF16), preferred_element_type=F32)
    b = jnp.dot(h, w3_ref[...].astype(BF16), preferred_element_type=F32)
    g_ref[...] = (a * jax.nn.sigmoid(a) * b).astype(BF16)


def _ffn_up_call(i, h, w1, w3):
    wspec = pl.BlockSpec((None, D, TF_UP), lambda j, t: (i, 0, j))
    return pl.pallas_call(
        _ffn_up_kernel,
        grid=(D_FF // TF_UP, M_TOK // TE_UP),
        in_specs=[pl.BlockSpec((TE_UP, D), lambda j, t: (t, 0)), wspec, wspec],
        out_specs=pl.BlockSpec((TE_UP, TF_UP), lambda j, t: (t, j)),
        out_shape=jax.ShapeDtypeStruct((M_TOK, D_FF), BF16),
        compiler_params=_cparams(("parallel", "parallel")),
        name="ffn_up",
    )(h, w1, w3)


def _ffn_down_kernel(g_ref, w2_ref, x_ref, mods_ref, o_ref):
    t = pl.program_id(1)
    y = jnp.dot(g_ref[...], w2_ref[...].astype(BF16), preferred_element_type=F32)
    ctx_tiles = M_P // TE_DOWN
    row = jnp.where(t < ctx_tiles, 0, 1 + (t - ctx_tiles) // (DEC_SEQ // TE_DOWN))
    o_ref[...] = x_ref[...] + mods_ref[pl.ds(row, 1), :] * y


def _ffn_down_call(l, g, w2, x, mods_all):
    i = l // 2
    return pl.pallas_call(
        _ffn_down_kernel,
        grid=(D // TN_DOWN, M_TOK // TE_DOWN),
        in_specs=[
            pl.BlockSpec((TE_DOWN, D_FF), lambda j, t: (t, 0)),
            pl.BlockSpec((None, D_FF, TN_DOWN), lambda j, t: (i, 0, j)),
            pl.BlockSpec((TE_DOWN, TN_DOWN), lambda j, t: (t, j)),
            pl.BlockSpec((None, MOD_ROWS, TN_DOWN), lambda j, t: (l, 0, 5 * (D // TN_DOWN) + j)),
        ],
        out_specs=pl.BlockSpec((TE_DOWN, TN_DOWN), lambda j, t: (t, j)),
        out_shape=jax.ShapeDtypeStruct((M_TOK, D), F32),
        compiler_params=_cparams(("parallel", "parallel")),
        name="ffn_down",
    )(g, w2, x, mods_all)


def _src_kernel(cpos_ref, src_ref):
    def body(i, c):
        src_ref[cpos_ref[i]] = lax.shift_right_logical(i, TOP_K_SHIFT)
        return c
    lax.fori_loop(0, N_SLOTS, body, 0, unroll=8)


def _src_call(cpos):
    return pl.pallas_call(
        _src_kernel,
        in_specs=[pl.BlockSpec(memory_space=pltpu.SMEM)],
        out_specs=pl.BlockSpec(memory_space=pltpu.SMEM),
        out_shape=jax.ShapeDtypeStruct((N_SLOTS,), I32),
        name="moe_src",
    )(cpos)


def _for_item_rows(cnt, live, fn):
    half, quarter = MC // 2, MC // 4
    n_full = cnt // MC
    rem = cnt - n_full * MC
    whole = rem > half + quarter
    n_mc = n_full + whole.astype(I32)
    use_half = ~whole & (rem > quarter)
    use_quarter = ~whole & (rem > 0) & ((rem <= quarter) | (rem > half))

    @pl.when(live)
    def _():
        def body(c, carry):
            fn(c * MC, MC)
            return carry
        lax.fori_loop(0, n_mc, body, 0)

    @pl.when(live & use_half)
    def _():
        fn(n_mc * MC, half)

    @pl.when(live & use_quarter)
    def _():
        fn(n_mc * MC + jnp.where(use_half, half, 0), quarter)


def _gather_copy(hp_hbm, tok, buf, b, q, u, sem, slot):
    return pltpu.make_async_copy(hp_hbm.at[pl.ds(tok, 1)], buf.at[b, q, pl.ds(u, 1)], sem.at[slot])


def _moe_up_kernel(src_ref, ie_ref, ics_ref, icnt_ref, nit_ref, hp_hbm, w1_ref, w3_ref, g_ref, buf, sem):
    k = pl.program_id(0)
    j = pl.program_id(1)
    nit = nit_ref[0]
    groups = GC // SUBLANES

    def chunk_rows(item, c):
        return jnp.clip(icnt_ref[item] - c * GC, 0, GC)

    def issue_chunk(item, c, b):
        n = chunk_rows(item, c)
        base = ics_ref[item] + c * GC

        @pl.when(n > 0)
        def _():
            def body(q, carry):
                for u in range(SUBLANES):
                    tok = src_ref[base + jnp.minimum(q * SUBLANES + u, n - 1)]
                    _gather_copy(hp_hbm, tok, buf, b, c * groups + q, u, sem, c & 1).start()
                return carry
            lax.fori_loop(0, groups, body, 0)

    def finish_chunk(item, c, b):
        @pl.when(chunk_rows(item, c) > 0)
        def _():
            def body(r, carry):
                _gather_copy(hp_hbm, 0, buf, b, 0, 0, sem, c & 1).wait()
                return carry
            lax.fori_loop(0, GC, body, 0, unroll=8)

    @pl.when((k == 0) & (j == 0))
    def _():
        buf[...] = jnp.zeros_like(buf)
        issue_chunk(0, 0, 0)

        def body(c, carry):
            @pl.when(c + 1 < N_GC)
            def _():
                issue_chunk(0, c + 1, 0)
            finish_chunk(0, c, 0)
            return carry
        lax.fori_loop(0, N_GC, body, 0)

    nxt = k + 1

    @pl.when((nxt < nit) & (j >= 1) & (j <= N_GC))
    def _():
        finish_chunk(nxt, j - 1, nxt & 1)

    @pl.when((nxt < nit) & (j < N_GC))
    def _():
        issue_chunk(nxt, j, nxt & 1)

    g_ref[...] = jnp.zeros_like(g_ref)
    b = k & 1

    def mm_rows(start, size):
        tiles = pl.ds(pl.multiple_of(start // SUBLANES, size // SUBLANES), size // SUBLANES)
        lo, hi = _unpack(buf[b, tiles].reshape(size, HALF))
        lo, hi = lo.astype(BF16), hi.astype(BF16)

        def up(w_ref):
            return (jnp.dot(lo, w_ref[0:HALF, :].astype(BF16), preferred_element_type=F32)
                    + jnp.dot(hi, w_ref[HALF:D, :].astype(BF16), preferred_element_type=F32))
        a = up(w1_ref)
        rows = pl.ds(pl.multiple_of(start, size), size)
        g_ref[rows, :] = (a * jax.nn.sigmoid(a) * up(w3_ref)).astype(BF16)

    _for_item_rows(icnt_ref[k], k < nit, mm_rows)


def _moe_up_call(layer_e0, src, item_e, item_cs, item_cnt, n_items, hp, w1, w3):
    n_f = D_FFE // TF_MOE
    wspec = pl.BlockSpec((None, D, TF_MOE), lambda k, j, s, ie, ics, ic, ni: (
        layer_e0 + ie[k], 0, jnp.where(k < ni[0], j, n_f - 1)))
    return pl.pallas_call(
        _moe_up_kernel,
        grid_spec=pltpu.PrefetchScalarGridSpec(
            num_scalar_prefetch=5,
            grid=(N_ITEMS, D_FFE // TF_MOE),
            in_specs=[pl.BlockSpec(memory_space=pl.ANY), wspec, wspec],
            out_specs=pl.BlockSpec((RT, TF_MOE), lambda k, j, s, ie, ics, ic, ni: (k, j)),
            scratch_shapes=[
                pltpu.VMEM((2, RT // SUBLANES, SUBLANES, HALF), U32),
                pltpu.SemaphoreType.DMA((2,)),
            ],
        ),
        out_shape=jax.ShapeDtypeStruct((N_ITEMS * RT, D_FFE), BF16),
        compiler_params=_cparams(("arbitrary", "arbitrary")),
        name="moe_up",
    )(src, item_e, item_cs, item_cnt, n_items, hp, w1, w3)


def _moe_down_kernel(ie_ref, icnt_ref, nit_ref, g_ref, w2a_ref, w2b_ref, o_ref):
    k = pl.program_id(0)
    o_ref[...] = jnp.zeros_like(o_ref)

    def mm_rows(start, size):
        rows = pl.ds(pl.multiple_of(start, size), size)
        g = g_ref[rows, :]
        lo = jnp.dot(g, w2a_ref[...].astype(BF16), preferred_element_type=F32)
        hi = jnp.dot(g, w2b_ref[...].astype(BF16), preferred_element_type=F32)
        o_ref[rows, :] = _pack(lo, hi)

    _for_item_rows(icnt_ref[k], k < nit_ref[0], mm_rows)


def _moe_down_call(layer_e0, item_e, item_cnt, n_items, g, w2):
    live = lambda k, ni: jnp.minimum(k, ni[0] - 1)
    n_half = HALF // TN_MOE
    col = lambda k, n, ni: jnp.where(k < ni[0], n, n_half - 1)
    return pl.pallas_call(
        _moe_down_kernel,
        grid_spec=pltpu.PrefetchScalarGridSpec(
            num_scalar_prefetch=3,
            grid=(N_ITEMS, n_half),
            in_specs=[
                pl.BlockSpec((RT, D_FFE), lambda k, n, ie, ic, ni: (live(k, ni), 0)),
                pl.BlockSpec((None, D_FFE, TN_MOE),
                             lambda k, n, ie, ic, ni: (layer_e0 + ie[k], 0, col(k, n, ni))),
                pl.BlockSpec((None, D_FFE, TN_MOE),
                             lambda k, n, ie, ic, ni: (layer_e0 + ie[k], 0, n_half + col(k, n, ni))),
            ],
            out_specs=pl.BlockSpec((RT, TN_MOE), lambda k, n, ie, ic, ni: (k, n)),
        ),
        out_shape=jax.ShapeDtypeStruct((N_ITEMS * RT, HALF), U32),
        compiler_params=_cparams(("arbitrary", "arbitrary")),
        name="moe_down",
    )(item_e, item_cnt, n_items, g, w2, w2)


def _combine_copy(yo_hbm, row, buf, slot, q, u, sem):
    return pltpu.make_async_copy(yo_hbm.at[pl.ds(row, 1)], buf.at[slot, q, pl.ds(u, 1)], sem.at[slot])


def _combine_kernel(split, pos_ref, yo_hbm, x_ref, rinfo_ref, mods_ref, *rest):
    if split:
        yp_ref, ys_ref, buf, sem = rest
    else:
        o_ref, buf, sem = rest
    i = pl.program_id(0)
    n = pl.num_programs(0)
    rows = TOP_K * TM

    def issue(tile, slot):
        def body(q, c):
            for u in range(SUBLANES):
                _combine_copy(yo_hbm, pos_ref[tile * rows + q * SUBLANES + u], buf, slot, q, u, sem).start()
            return c
        lax.fori_loop(0, rows // SUBLANES, body, 0)

    @pl.when(i == 0)
    def _():
        issue(0, 0)

    @pl.when(i + 1 < n)
    def _():
        issue(i + 1, (i + 1) & 1)

    slot = i & 1

    def body(r, c):
        _combine_copy(yo_hbm, 0, buf, slot, 0, 0, sem).wait()
        return c
    lax.fori_loop(0, rows, body, 0, unroll=8)
    rinfo = rinfo_ref[...]
    tiles = TM // SUBLANES
    lo0, hi0 = _unpack(buf[slot, 0:tiles].reshape(TM, HALF))
    lo1, hi1 = _unpack(buf[slot, tiles:2 * tiles].reshape(TM, HALF))
    gt0, gt1 = rinfo[:, 2:3], rinfo[:, 3:4]
    g2 = mods_ref[pl.ds(_mod_row(i), 1), :]
    y_lo = x_ref[:, 0:HALF] + g2[:, 0:HALF] * (gt0 * lo0 + gt1 * lo1)
    y_hi = x_ref[:, HALF:D] + g2[:, HALF:D] * (gt0 * hi0 + gt1 * hi1)

    def write(ref):
        ref[:, 0:HALF] = y_lo
        ref[:, HALF:D] = y_hi

    if not split:
        write(o_ref)
        return

    @pl.when(i < P_TILES)
    def _():
        write(yp_ref)

    @pl.when(i >= P_TILES)
    def _():
        write(ys_ref)


def _combine_call(l, split, pos, yo, x, rinfo, mods_all):
    if split:
        out_specs = [pl.BlockSpec((TM, D), lambda i, p: (jnp.minimum(i, P_TILES - 1), 0)),
                     pl.BlockSpec((TM, D), lambda i, p: (jnp.maximum(i - P_TILES, 0), 0))]
        out_shape = [jax.ShapeDtypeStruct((M_P, D), F32), jax.ShapeDtypeStruct((M_S, D), F32)]
    else:
        out_specs = pl.BlockSpec((TM, D), lambda i, p: (i, 0))
        out_shape = jax.ShapeDtypeStruct((M_TOK, D), F32)
    return pl.pallas_call(
        functools.partial(_combine_kernel, split),
        grid_spec=pltpu.PrefetchScalarGridSpec(
            num_scalar_prefetch=1,
            grid=(N_TILES,),
            in_specs=[
                pl.BlockSpec(memory_space=pl.ANY),
                pl.BlockSpec((TM, D), lambda i, p: (i, 0)),
                pl.BlockSpec((TM, LANE), lambda i, p: (i, 0)),
                pl.BlockSpec((None, MOD_ROWS, D), lambda i, p: (l, 0, 5)),
            ],
            out_specs=out_specs,
            scratch_shapes=[pltpu.VMEM((2, TOP_K * TM // SUBLANES, SUBLANES, HALF), U32),
                            pltpu.SemaphoreType.DMA((2,))],
        ),
        out_shape=out_shape,
        compiler_params=_cparams(("arbitrary",)),
        name="moe_combine",
    )(pos, yo, x, rinfo, mods_all)


def _route_plan(rinfo):
    flat_e = rinfo[:, 0:TOP_K].astype(I32).reshape(-1)
    onehot = (flat_e[:, None] == jnp.arange(N_EXP, dtype=I32)[None, :]).astype(I32)
    csum = jnp.cumsum(onehot, axis=0)
    pick = lambda table: jnp.sum(onehot * table[None, :], axis=1)
    rank = jnp.sum(onehot * csum, axis=1) - 1
    counts = csum[-1]
    cstart = jnp.cumsum(counts) - counts
    n_it = (counts + RT - 1) // RT
    it_end = jnp.cumsum(n_it)
    it_base = it_end - n_it
    n_items = it_end[-1]
    cpos = pick(cstart) + rank
    ppos = (pick(it_base) + rank // RT) * RT + rank % RT
    q = jnp.arange(N_ITEMS, dtype=I32)
    qc = jnp.minimum(q, n_items - 1)
    qe = jnp.minimum(jnp.sum((it_end[None, :] <= qc[:, None]).astype(I32), axis=1), N_EXP - 1)
    qhot = (qe[:, None] == jnp.arange(N_EXP, dtype=I32)[None, :]).astype(I32)
    qpick = lambda table: jnp.sum(qhot * table[None, :], axis=1)
    s = qc - qpick(it_base)
    item_cnt = jnp.where(q < n_items, jnp.clip(qpick(counts) - s * RT, 0, RT), 0)
    item_cs = qpick(cstart) + s * RT
    pos_tiles = ppos.reshape(N_TILES, TM, TOP_K).transpose(0, 2, 1).reshape(-1)
    return (cpos.astype(I32), qe.astype(I32), item_cs.astype(I32), item_cnt.astype(I32),
            n_items.reshape(1).astype(I32), pos_tiles.astype(I32))


def _rope_tables():
    rows = DEC_SEQ // GRID_W
    row = jnp.repeat(jnp.arange(rows), GRID_W).astype(F32)
    col = jnp.tile(jnp.arange(GRID_W), rows).astype(F32)
    nf = ROPE // 4
    inv = ROPE_BASE ** (-jnp.arange(nf, dtype=F32) / nf)
    ar = row[:, None] * inv[None, :]
    ac = col[:, None] * inv[None, :]
    ang = jnp.concatenate([ar, ar, ac, ac], axis=-1)
    cos, sin = jnp.cos(ang), jnp.sin(ang)
    first = (jnp.arange(ROPE) % 32) < 16
    pad = lambda a, v: jnp.pad(a, ((0, 0), (0, LANE - ROPE)), constant_values=v)
    cos_p = pad(cos, 1.0)
    sin_a = pad(jnp.where(first[None, :], -sin, 0.0), 0.0)
    sin_b = pad(jnp.where(first[None, :], 0.0, sin), 0.0)
    ident = jnp.stack([jnp.ones((TMI, LANE), F32), jnp.zeros((TMI, LANE), F32), jnp.zeros((TMI, LANE), F32)])
    return jnp.concatenate([ident, jnp.stack([cos_p, sin_a, sin_b])], axis=1)


def kernel(x_prompt, x_sample, cache_ckv, cache_krope, c, c_ctx, w_ada, b_ada, norm1_g, norm2_g, w_in,
           qlat_norm_g, kvlat_norm_g, w_uq, w_ukv, q_head_norm_g, k_head_norm_g, gm_norm_g, gm_w_s,
           gm_b_s, pool_w, pool_scale, w_o, ffn_w1, ffn_w3, ffn_w2, router_w, moe_w1, moe_w3, moe_w2):
    pad_w = IN_PAD - w_in.shape[-1]
    col = jnp.arange(IN_PAD)[None, None, :]
    w_in_p = jnp.where(
        col < C_KR + ROPE, jnp.pad(w_in, ((0, 0), (0, 0), (0, pad_w))),
        jnp.where(col < C_KR + LANE, 0.0, jnp.pad(w_in, ((0, 0), (0, 0), (pad_w, 0))))).astype(BF16)
    w_uq_p = jnp.pad(w_uq.reshape(DEPTH, Q_RANK, N_HEADS, QK_DIM),
                     ((0, 0), (0, 0), (0, 0), (0, HEAD_PAD - QK_DIM))
                     ).reshape(DEPTH, Q_RANK, QK_W).astype(BF16)
    w_ukv_b = w_ukv.astype(BF16)
    w_o_b = w_o.astype(BF16)
    qg_p = jnp.pad(q_head_norm_g * (1.0 / math.sqrt(QK_DIM)),
                   ((0, 0), (0, HEAD_PAD - QK_DIM))).reshape(DEPTH, 1, HEAD_PAD)
    kg_n = k_head_norm_g[:, :NOPE].reshape(DEPTH, 1, LANE)
    kg_r = jnp.pad(k_head_norm_g[:, NOPE:], ((0, 0), (0, LANE - ROPE))).reshape(DEPTH, 1, LANE)
    gmw_b = gm_w_s.astype(BF16)
    gmb = jnp.broadcast_to(gm_b_s[:, :, :, None], (DEPTH, GM_GROUPS, CHUNK, LANE))
    pw_b = pool_w.astype(BF16)
    ps = pool_scale.reshape(DEPTH, 1, POOL_W)
    n1g = norm1_g.reshape(DEPTH, 1, D)
    n2g = norm2_g.reshape(DEPTH, 1, D)
    qlg = qlat_norm_g.reshape(DEPTH, 1, Q_RANK)
    kvlg = kvlat_norm_g.reshape(DEPTH, 1, KV_RANK)
    rw_p = jnp.pad(router_w, ((0, 0), (0, 0), (0, LANE - N_EXP)))
    n_moe = moe_w1.shape[0]
    moe_w1_flat = moe_w1.reshape(n_moe * N_EXP, D, D_FFE)
    moe_w3_flat = moe_w3.reshape(n_moe * N_EXP, D, D_FFE)
    moe_w2_flat = moe_w2.reshape(n_moe * N_EXP, D_FFE, D)
    rope_tab = _rope_tables()
    cache_kr_p = jnp.pad(cache_krope, ((0, 0), (0, 0), (0, 0), (0, LANE - ROPE)))

    cond = jnp.zeros((MOD_ROWS, D), F32).at[0].set(c_ctx).at[1:1 + DEC_BATCH].set(c)
    mods_all = _ada_call(cond, w_ada, b_ada)
    k_c, v_c = _cache_kv_call(cache_ckv, cache_kr_p, w_ukv_b, kg_n, kg_r)

    x = (x_prompt.reshape(M_P, D), x_sample.reshape(M_S, D))
    ckv_out, kr_out = [], []
    for l in range(DEPTH):
        q, k, v, ckv, kr, zg, zp = _in_proj_call(
            l, x, mods_all, n1g, w_in_p, qlg, kvlg, w_uq_p, qg_p, w_ukv_b, kg_n, kg_r, rope_tab,
            gm_norm_g, gmw_b, gmb)
        ckv_out.append(ckv[:M_P].reshape(BATCH, SEQ, KV_RANK))
        kr_out.append(kr[:M_P, :ROPE].reshape(BATCH, SEQ, ROPE))
        attn = _attn_call(l, q, k, v, k_c, v_c)
        i = l // 2
        if l % 2 == 0:
            x, h2b = _out_proj_call(l, False, attn, zg, zp, x, mods_all, pw_b, ps, w_o_b, n2g)
            g = _ffn_up_call(i, h2b, ffn_w1, ffn_w3)
            x = _ffn_down_call(l, g, ffn_w2, x, mods_all)
        else:
            x, hp, rinfo = _out_proj_call(l, True, attn, zg, zp, x, mods_all, pw_b, ps, w_o_b, n2g, rw_p)
            cpos, item_e, item_cs, item_cnt, n_items, pos_tiles = _route_plan(rinfo)
            src = _src_call(cpos)
            g = _moe_up_call(i * N_EXP, src, item_e, item_cs, item_cnt, n_items, hp,
                             moe_w1_flat, moe_w3_flat)
            yo = _moe_down_call(i * N_EXP, item_e, item_cnt, n_items, g, moe_w2_flat)
            x = _combine_call(l, l == DEPTH - 1, pos_tiles, yo, x, rinfo, mods_all)

    if DEPTH % 2 == 0:
        y_p, y_s = x
    else:
        y_p, y_s = x[:M_P], x[M_P:]
    return (y_p.reshape(BATCH, SEQ, D), y_s.reshape(DEC_BATCH, DEC_SEQ, D),
            jnp.stack(ckv_out, axis=1), jnp.stack(kr_out, axis=1))
```

```python
import functools
import math

import jax
import jax.numpy as jnp
from jax import lax
from jax.experimental import pallas as pl
from jax.experimental.pallas import tpu as pltpu

F32 = jnp.float32
BF16 = jnp.bfloat16
U32 = jnp.uint32
I32 = jnp.int32

D = 2048
BATCH, SEQ = 16, 256
DEC_BATCH, DEC_SEQ = 2, 1024
PAST = 256
DEPTH = 4
GRID_W = 64
EPS = 1e-6
N_HEADS = 8
NOPE, ROPE, VDIM = 128, 64, 128
QK_DIM = NOPE + ROPE
HEAD_PAD = 256
QK_W = N_HEADS * HEAD_PAD
Q_RANK, KV_RANK = 512, 256
ROPE_BASE = 10000.0
ATTN_W = N_HEADS * VDIM
CHUNK, GM_GROUPS, GM_W = 128, 4, 512
POOL_WINDOWS = (2, 4, 8, 16)
POOL_W = 512
POOL_HALO = 8
D_FF, N_EXP, D_FFE, TOP_K = 5632, 8, 2816, 2
TOP_K_SHIFT = 1

C_Q, C_KV, C_KR, C_GM, C_PL = 0, 512, 768, 896, 1920
IN_PAD = 2432

LANE = 128
SUBLANES = 8
TM = 256
TMI = 512
TMO = 256
M_P, M_S = BATCH * SEQ, DEC_BATCH * DEC_SEQ
M_TOK = M_P + M_S
N_TILES = M_TOK // TM
P_TILES = M_P // TM
S_TILES = DEC_SEQ // TM
MOD_ROWS = 16
ADA_TN = 1024
TE_UP, TF_UP = 1024, 512
TE_DOWN, TN_DOWN = 512, 512
HALF = D // 2
N_SLOTS = TOP_K * M_TOK
RT = 2560
N_ITEMS = N_EXP + (N_SLOTS - N_EXP) // RT
GC = 256
N_GC = RT // GC
MC = 1024
TF_MOE = 256
TN_MOE = 256
V7X_VMEM_BYTES = 64 * 1024 * 1024
VMEM_LIMIT = V7X_VMEM_BYTES * 7 // 8
assert N_GC + 1 <= D_FFE // TF_MOE


def _cparams(sem):
    return pltpu.CompilerParams(dimension_semantics=sem, vmem_limit_bytes=VMEM_LIMIT)


def _layer_spec(l, shape):
    nd = len(shape)
    return pl.BlockSpec((None,) + tuple(shape), lambda *_: (l,) + (0,) * nd)


def _x_operands(x, tile):
    if not isinstance(x, tuple):
        return [x], [pl.BlockSpec((tile, D), lambda i: (i, 0))]
    ctx = M_P // tile
    return list(x), [pl.BlockSpec((tile, D), lambda i: (jnp.minimum(i, ctx - 1), 0)),
                     pl.BlockSpec((tile, D), lambda i: (jnp.maximum(i - ctx, 0), 0))]


def _mod_row(i):
    return jnp.where(i < P_TILES, 0, 1 + (i - P_TILES) // S_TILES)


def _rms(x, g, n):
    ms = jnp.sum(x * x, axis=-1, keepdims=True) * (1.0 / n)
    return x * lax.rsqrt(ms + EPS) * g


def _rope(r, cos, sin_a, sin_b):
    return r * cos + pltpu.roll(r, LANE - 16, 1) * sin_a + pltpu.roll(r, 16, 1) * sin_b


def _pack(lo, hi):
    lo_b = lax.bitcast_convert_type(lo.astype(BF16).astype(F32), U32)
    hi_b = lax.bitcast_convert_type(hi.astype(BF16).astype(F32), U32)
    return (lo_b >> 16) | hi_b


def _unpack(w):
    lo = lax.bitcast_convert_type(w << 16, F32)
    hi = lax.bitcast_convert_type(w & jnp.uint32(0xFFFF0000), F32)
    return lo, hi


def _ada_kernel(cond_ref, w_ref, b_ref, o_ref):
    c = cond_ref[...]
    s = c * jax.nn.sigmoid(c)
    s_hi = s.astype(BF16)
    s_lo = (s - s_hi.astype(F32)).astype(BF16)
    w = w_ref[...]
    w_hi = w.astype(BF16)
    w_lo = (w - w_hi.astype(F32)).astype(BF16)
    lhs = jnp.concatenate([s_hi, s_lo], axis=0)
    r1 = jnp.dot(lhs, w_hi, preferred_element_type=F32)
    r2 = jnp.dot(s_hi, w_lo, preferred_element_type=F32)
    o_ref[...] = r1[:MOD_ROWS] + r1[MOD_ROWS:] + r2 + b_ref[...]


def _ada_call(cond, w_ada, b_ada):
    n = 6 * D
    return pl.pallas_call(
        _ada_kernel,
        grid=(DEPTH, n // ADA_TN),
        in_specs=[
            pl.BlockSpec((MOD_ROWS, D), lambda l, j: (0, 0)),
            pl.BlockSpec((None, D, ADA_TN), lambda l, j: (l, 0, j)),
            pl.BlockSpec((None, 1, ADA_TN), lambda l, j: (l, 0, j)),
        ],
        out_specs=pl.BlockSpec((None, MOD_ROWS, ADA_TN), lambda l, j: (l, 0, j)),
        out_shape=jax.ShapeDtypeStruct((DEPTH, MOD_ROWS, n), F32),
        compiler_params=_cparams(("parallel", "parallel")),
        name="ada",
    )(cond, w_ada, b_ada.reshape(DEPTH, 1, n))


def _write_kv(ckv_b, kr, w_ukv_ref, kg_n, kg_r, rope, k_ref, v_ref, rows=slice(None)):
    kv = jnp.dot(ckv_b, w_ukv_ref[...], preferred_element_type=F32)
    ssr = jnp.sum(kr * kr, axis=-1, keepdims=True)
    shared = kr * kg_r
    if rope is not None:
        shared = _rope(shared, *rope)
    for h in range(N_HEADS):
        kn = kv[:, h * HEAD_PAD:h * HEAD_PAD + NOPE]
        vv = kv[:, h * HEAD_PAD + NOPE:(h + 1) * HEAD_PAD]
        ms = (jnp.sum(kn * kn, axis=-1, keepdims=True) + ssr) * (1.0 / QK_DIM)
        inv = lax.rsqrt(ms + EPS)
        k_ref[rows, h * HEAD_PAD:h * HEAD_PAD + NOPE] = (kn * inv * kg_n).astype(BF16)
        k_ref[rows, h * HEAD_PAD + NOPE:(h + 1) * HEAD_PAD] = (shared * inv).astype(BF16)
        v_ref[rows, h * VDIM:(h + 1) * VDIM] = vv.astype(BF16)


def _cache_kv_kernel(ckv_ref, kr_ref, w_ukv_ref, kgn_ref, kgr_ref, k_ref, v_ref):
    _write_kv(ckv_ref[...].astype(BF16), kr_ref[...], w_ukv_ref, kgn_ref[...], kgr_ref[...],
              None, k_ref, v_ref)


def _cache_kv_call(cache_ckv, cache_kr_p, w_ukv_b, kg_n, kg_r):
    return pl.pallas_call(
        _cache_kv_kernel,
        grid=(DEPTH, DEC_BATCH),
        in_specs=[
            pl.BlockSpec((None, None, PAST, KV_RANK), lambda l, b: (b, l, 0, 0)),
            pl.BlockSpec((None, None, PAST, LANE), lambda l, b: (b, l, 0, 0)),
            pl.BlockSpec((None, KV_RANK, QK_W), lambda l, b: (l, 0, 0)),
            pl.BlockSpec((None, 1, LANE), lambda l, b: (l, 0, 0)),
            pl.BlockSpec((None, 1, LANE), lambda l, b: (l, 0, 0)),
        ],
        out_specs=[
            pl.BlockSpec((None, None, PAST, QK_W), lambda l, b: (l, b, 0, 0)),
            pl.BlockSpec((None, None, PAST, ATTN_W), lambda l, b: (l, b, 0, 0)),
        ],
        out_shape=[
            jax.ShapeDtypeStruct((DEPTH, DEC_BATCH, PAST, QK_W), BF16),
            jax.ShapeDtypeStruct((DEPTH, DEC_BATCH, PAST, ATTN_W), BF16),
        ],
        compiler_params=_cparams(("parallel", "parallel")),
        name="cache_kv",
    )(cache_ckv, cache_kr_p, w_ukv_b, kg_n, kg_r)


def _in_proj_kernel(two_x, *refs):
    x_refs, refs = refs[:1 + two_x], refs[1 + two_x:]
    _in_proj_body(x_refs, *refs)


def _in_proj_body(x_refs, mods_ref, n1g_ref, w_in_ref, qlg_ref, kvlg_ref, w_uq_ref, qg_ref,
                    w_ukv_ref, kgn_ref, kgr_ref, rope_ref, gmg_ref, gmw_ref, gmb_ref,
                    q_ref, k_ref, v_ref, ckv_ref, kr_ref, zg_ref, zp_ref):
    i = pl.program_id(0)
    row = jnp.where(i < M_P // TMI, 0, 1 + (i - M_P // TMI) // (DEC_SEQ // TMI))
    m = mods_ref[pl.ds(row, 1), :]
    sh1, sc1 = m[:, 0:D], m[:, D:2 * D]
    for sub in range(TMI // TM):
        rs = slice(sub * TM, (sub + 1) * TM)
        x = x_refs[0][rs, :] if len(x_refs) == 1 else jnp.where(i < M_P // TMI, x_refs[0][rs, :], x_refs[1][rs, :])
        h = _rms(x, n1g_ref[...], D) * (1.0 + sc1) + sh1
        z = jnp.dot(h.astype(BF16), w_in_ref[...], preferred_element_type=F32)
        rope = (rope_ref[0, rs, :], rope_ref[1, rs, :], rope_ref[2, rs, :])

        ql = _rms(z[:, C_Q:C_Q + Q_RANK], qlg_ref[...], Q_RANK)
        qf = jnp.dot(ql.astype(BF16), w_uq_ref[...], preferred_element_type=F32)
        qg = qg_ref[...]
        for hd in range(N_HEADS):
            seg = qf[:, hd * HEAD_PAD:(hd + 1) * HEAD_PAD]
            seg = _rms(seg, qg, QK_DIM)
            q_ref[rs, hd * HEAD_PAD:hd * HEAD_PAD + NOPE] = seg[:, :NOPE].astype(BF16)
            q_ref[rs, hd * HEAD_PAD + NOPE:(hd + 1) * HEAD_PAD] = _rope(seg[:, NOPE:], *rope).astype(BF16)

        ckv = _rms(z[:, C_KV:C_KV + KV_RANK], kvlg_ref[...], KV_RANK)
        kr = z[:, C_KR:C_KR + LANE]
        ckv_ref[rs, :] = ckv
        kr_ref[rs, :] = kr
        _write_kv(ckv.astype(BF16), kr, w_ukv_ref, kgn_ref[...], kgr_ref[...], rope, k_ref, v_ref, rs)

        zg = jax.nn.gelu(z[:, C_GM:C_GM + 2 * GM_W], approximate=True)
        for g in range(GM_GROUPS):
            u = zg[:, g * LANE:(g + 1) * LANE]
            vg = _rms(zg[:, GM_W + g * LANE:GM_W + (g + 1) * LANE], gmg_ref[g:g + 1, :], LANE)
            vb = vg.astype(BF16)
            for c in range(TM // CHUNK):
                sv = jnp.dot(gmw_ref[g], vb[c * CHUNK:(c + 1) * CHUNK, :], preferred_element_type=F32)
                sv = sv + gmb_ref[g]
                r0 = sub * TM + c * CHUNK
                zg_ref[r0:r0 + CHUNK, g * LANE:(g + 1) * LANE] = (
                    u[c * CHUNK:(c + 1) * CHUNK, :] * sv).astype(BF16)

        zp_ref[rs, :] = z[:, C_PL:C_PL + POOL_W]


def _in_proj_call(l, x, mods_all, n1g, w_in_b, qlg, kvlg, w_uq_b, qg, w_ukv_b, kg_n, kg_r, rope_tab,
                  gmg, gmw_b, gmb):
    def rope_idx(i):
        return (0, jnp.where(i < M_P // TMI, 0, 1 + (i - M_P // TMI) % (DEC_SEQ // TMI)), 0)

    row = lambda w: pl.BlockSpec((TMI, w), lambda i: (i, 0))
    two_x = isinstance(x, tuple)
    x_args, x_specs = _x_operands(x, TMI)
    return pl.pallas_call(
        functools.partial(_in_proj_kernel, two_x),
        grid=(M_TOK // TMI,),
        in_specs=x_specs + [
            _layer_spec(l, (MOD_ROWS, 6 * D)),
            _layer_spec(l, (1, D)),
            _layer_spec(l, (D, IN_PAD)),
            _layer_spec(l, (1, Q_RANK)),
            _layer_spec(l, (1, KV_RANK)),
            _layer_spec(l, (Q_RANK, QK_W)),
            _layer_spec(l, (1, HEAD_PAD)),
            _layer_spec(l, (KV_RANK, QK_W)),
            _layer_spec(l, (1, LANE)),
            _layer_spec(l, (1, LANE)),
            pl.BlockSpec((3, TMI, LANE), rope_idx),
            _layer_spec(l, (GM_GROUPS, LANE)),
            _layer_spec(l, (GM_GROUPS, CHUNK, CHUNK)),
            _layer_spec(l, (GM_GROUPS, CHUNK, LANE)),
        ],
        out_specs=[row(QK_W), row(QK_W), row(ATTN_W), row(KV_RANK), row(LANE), row(GM_W), row(POOL_W)],
        out_shape=[
            jax.ShapeDtypeStruct((M_TOK, QK_W), BF16),
            jax.ShapeDtypeStruct((M_TOK, QK_W), BF16),
            jax.ShapeDtypeStruct((M_TOK, ATTN_W), BF16),
            jax.ShapeDtypeStruct((M_TOK, KV_RANK), F32),
            jax.ShapeDtypeStruct((M_TOK, LANE), F32),
            jax.ShapeDtypeStruct((M_TOK, GM_W), BF16),
            jax.ShapeDtypeStruct((M_TOK, POOL_W), F32),
        ],
        compiler_params=_cparams(("parallel",)),
        name="in_proj",
    )(*x_args, mods_all, n1g, w_in_b, qlg, kvlg, w_uq_b, qg, w_ukv_b, kg_n, kg_r, rope_tab, gmg, gmw_b, gmb)


def _attend_heads(q_ref, kv_pairs, o_ref):
    nt = (((1,), (1,)), ((), ()))
    for h in range(N_HEADS):
        qh = q_ref[:, h * HEAD_PAD:(h + 1) * HEAD_PAD]
        ss = [lax.dot_general(qh, k_ref[:, h * HEAD_PAD:(h + 1) * HEAD_PAD], nt,
                              preferred_element_type=F32) for k_ref, _ in kv_pairs]
        mx = ss[0].max(axis=-1, keepdims=True)
        for s in ss[1:]:
            mx = jnp.maximum(mx, s.max(axis=-1, keepdims=True))
        den = None
        acc = None
        for s, (_, v_ref) in zip(ss, kv_pairs):
            p = jnp.exp(s - mx)
            d = jnp.sum(p, axis=-1, keepdims=True)
            a = jnp.dot(p.astype(BF16), v_ref[:, h * VDIM:(h + 1) * VDIM], preferred_element_type=F32)
            den = d if den is None else den + d
            acc = a if acc is None else acc + a
        o_ref[:, h * VDIM:(h + 1) * VDIM] = (acc / den).astype(BF16)


def _attn_kernel(q_ref, kp_ref, vp_ref, ks_ref, vs_ref, kc_ref, vc_ref, o_ref):
    i = pl.program_id(0)

    @pl.when(i < P_TILES)
    def _():
        _attend_heads(q_ref, [(kp_ref, vp_ref)], o_ref)

    @pl.when(i >= P_TILES)
    def _():
        _attend_heads(q_ref, [(ks_ref, vs_ref), (kc_ref, vc_ref)], o_ref)


def _attn_call(l, q, k, v, k_c, v_c):
    def seq_b(i):
        return jnp.maximum(i - P_TILES, 0) // S_TILES

    small = lambda w: pl.BlockSpec((TM, w), lambda i: (jnp.minimum(i, P_TILES - 1), 0))
    big = lambda w: pl.BlockSpec((DEC_SEQ, w), lambda i: (M_P // DEC_SEQ + seq_b(i), 0))
    cache = lambda w: pl.BlockSpec((None, None, PAST, w), lambda i: (l, seq_b(i), 0, 0))
    return pl.pallas_call(
        _attn_kernel,
        grid=(N_TILES,),
        in_specs=[
            pl.BlockSpec((TM, QK_W), lambda i: (i, 0)),
            small(QK_W), small(ATTN_W),
            big(QK_W), big(ATTN_W),
            cache(QK_W), cache(ATTN_W),
        ],
        out_specs=pl.BlockSpec((TM, ATTN_W), lambda i: (i, 0)),
        out_shape=jax.ShapeDtypeStruct((M_TOK, ATTN_W), BF16),
        compiler_params=_cparams(("parallel",)),
        name="attention",
    )(q, k, v, k, v, k_c, v_c)


def _out_proj_kernel(moe, two_x, attn_ref, zg_ref, zprev_ref, zcur_ref, znext_ref, *rest):
    x_refs, rest = rest[:1 + two_x], rest[1 + two_x:]
    mods_ref, pw_ref, ps_ref, w_o_ref, n2g_ref = rest[:5]
    rest = rest[5:]
    if moe:
        rw_ref, xo_ref, hp_ref, rinfo_ref, ext_ref, mix_ref = rest
    else:
        xo_ref, h2b_ref, ext_ref, mix_ref = rest
    i = pl.program_id(0)
    ctx_steps = M_P // TMO
    is_ctx = i < ctx_steps
    seq_len = jnp.where(is_ctx, SEQ, DEC_SEQ)
    m = mods_ref[pl.ds(jnp.where(is_ctx, 0, 1 + (i - ctx_steps) // (DEC_SEQ // TMO)), 1), :]
    g1, sh2, sc2 = m[:, 2 * D:3 * D], m[:, 3 * D:4 * D], m[:, 4 * D:5 * D]

    pos0 = jnp.where(is_ctx, 0, ((i - ctx_steps) % (DEC_SEQ // TMO)) * TMO)
    zero_halo = jnp.zeros((POOL_HALO, POOL_W), F32)
    ext_ref[0:POOL_HALO, :] = jnp.where(pos0 == 0, zero_halo, zprev_ref[TMO - POOL_HALO:, :])
    ext_ref[POOL_HALO:POOL_HALO + TMO, :] = zcur_ref[...]
    ext_ref[POOL_HALO + TMO:, :] = jnp.where(pos0 + TMO == seq_len, zero_halo, znext_ref[0:POOL_HALO, :])
    mix_ref[:, 0:ATTN_W] = attn_ref[...]
    mix_ref[:, ATTN_W:ATTN_W + GM_W] = zg_ref[...]

    assert TMO == TM and SEQ % TMO == 0 and DEC_SEQ % TMO == 0
    for sub in range(TMO // TM):
        rs = slice(sub * TM, (sub + 1) * TM)
        pos = pos0 + lax.broadcasted_iota(I32, (TM, 1), 0)
        for gi, w in enumerate(POOL_WINDOWS):
            cols = slice(gi * LANE, (gi + 1) * LANE)
            win = None
            for dlt in range(-(w // 2), w - w // 2):
                r0 = POOL_HALO + dlt
                part = ext_ref[r0:r0 + TM, cols]
                win = part if win is None else win + part
            lo = jnp.maximum(pos - w // 2, 0)
            hi = jnp.minimum(pos + (w - 1 - w // 2), seq_len - 1)
            cnt = (hi - lo + 1).astype(F32)
            pooled = win / cnt - zcur_ref[rs, cols]
            mixed = jnp.dot(pooled.astype(BF16), pw_ref[gi], preferred_element_type=F32)
            mix_ref[rs, ATTN_W + GM_W + gi * LANE:ATTN_W + GM_W + (gi + 1) * LANE] = (
                mixed * ps_ref[:, cols]).astype(BF16)

        out = jnp.dot(mix_ref[rs, :], w_o_ref[...], preferred_element_type=F32)
        x = x_refs[0][rs, :] if len(x_refs) == 1 else jnp.where(is_ctx, x_refs[0][rs, :], x_refs[1][rs, :])
        xn = x + g1 * out
        xo_ref[rs, :] = xn
        h2 = _rms(xn, n2g_ref[...], D) * (1.0 + sc2) + sh2

        if not moe:
            h2b_ref[rs, :] = h2.astype(BF16)
            continue

        hp_ref[rs, :] = _pack(h2[:, :HALF], h2[:, HALF:])
        h_hi = h2.astype(BF16)
        h_lo = (h2 - h_hi.astype(F32)).astype(BF16)
        rw = rw_ref[...]
        w_hi = rw.astype(BF16)
        w_lo = (rw - w_hi.astype(F32)).astype(BF16)
        logits = (jnp.dot(h_hi, w_hi, preferred_element_type=F32)
                  + jnp.dot(h_lo, w_hi, preferred_element_type=F32)
                  + jnp.dot(h_hi, w_lo, preferred_element_type=F32))
        lane = lax.broadcasted_iota(I32, (TM, LANE), 1).astype(F32)
        neg = jnp.float32(-jnp.inf)
        logits = jnp.where(lane < N_EXP, logits, neg)
        v1 = logits.max(axis=-1, keepdims=True)
        i1 = jnp.where(logits == v1, lane, float(LANE)).min(axis=-1, keepdims=True)
        l2 = jnp.where(lane == i1, neg, logits)
        v2 = l2.max(axis=-1, keepdims=True)
        i2 = jnp.where(l2 == v2, lane, float(LANE)).min(axis=-1, keepdims=True)
        e2 = jnp.exp(v2 - v1)
        gate1 = 1.0 / (1.0 + e2)
        gate2 = e2 / (1.0 + e2)
        rinfo_ref[rs, :] = jnp.where(lane == 0, i1, jnp.where(lane == 1, i2, jnp.where(
            lane == 2, gate1, jnp.where(lane == 3, gate2, 0.0))))


def _out_proj_call(l, moe, attn, zg, zp, x, mods_all, pw_b, ps, w_o_b, n2g, rw_p=None):
    n_steps = M_TOK // TMO
    row = lambda w: pl.BlockSpec((TMO, w), lambda i: (i, 0))
    two_x = isinstance(x, tuple)
    x_args, x_specs = _x_operands(x, TMO)
    in_specs = [
        row(ATTN_W), row(GM_W),
        pl.BlockSpec((TMO, POOL_W), lambda i: (jnp.maximum(i - 1, 0), 0)),
        row(POOL_W),
        pl.BlockSpec((TMO, POOL_W), lambda i: (jnp.minimum(i + 1, n_steps - 1), 0)),
    ] + x_specs + [
        _layer_spec(l, (MOD_ROWS, 6 * D)),
        _layer_spec(l, (len(POOL_WINDOWS), LANE, LANE)),
        _layer_spec(l, (1, POOL_W)),
        _layer_spec(l, (D, D)),
        _layer_spec(l, (1, D)),
    ]
    args = [attn, zg, zp, zp, zp, *x_args, mods_all, pw_b, ps, w_o_b, n2g]
    if moe:
        in_specs.append(_layer_spec(l // 2, (D, LANE)))
        args.append(rw_p)
        out_specs = [row(D), row(HALF), row(LANE)]
        out_shape = [jax.ShapeDtypeStruct((M_TOK, D), F32), jax.ShapeDtypeStruct((M_TOK, HALF), U32),
                     jax.ShapeDtypeStruct((M_TOK, LANE), F32)]
    else:
        out_specs = [row(D), row(D)]
        out_shape = [jax.ShapeDtypeStruct((M_TOK, D), F32), jax.ShapeDtypeStruct((M_TOK, D), BF16)]
    return pl.pallas_call(
        functools.partial(_out_proj_kernel, moe, two_x),
        grid=(n_steps,),
        in_specs=in_specs,
        out_specs=out_specs,
        out_shape=out_shape,
        scratch_shapes=[pltpu.VMEM((TMO + 2 * POOL_HALO, POOL_W), F32), pltpu.VMEM((TMO, D), BF16)],
        compiler_params=_cparams(("parallel",)),
        name="out_proj_moe" if moe else "out_proj",
    )(*args)


def _ffn_up_kernel(h_ref, w1_ref, w3_ref, g_ref):
    h = h_ref[...]
    a = jnp.dot(h, w1_ref[...].astype(BF16), preferred_element_type=F32)
    b = jnp.dot(h, w3_ref[...].astype(BF16), preferred_element_type=F32)
    g_ref[...] = (a * jax.nn.sigmoid(a) * b).astype(BF16)


def _ffn_up_call(i, h, w1, w3):
    wspec = pl.BlockSpec((None, D, TF_UP), lambda j, t: (i, 0, j))
    return pl.pallas_call(
        _ffn_up_kernel,
        grid=(D_FF // TF_UP, M_TOK // TE_UP),
        in_specs=[pl.BlockSpec((TE_UP, D), lambda j, t: (t, 0)), wspec, wspec],
        out_specs=pl.BlockSpec((TE_UP, TF_UP), lambda j, t: (t, j)),
        out_shape=jax.ShapeDtypeStruct((M_TOK, D_FF), BF16),
        compiler_params=_cparams(("parallel", "parallel")),
        name="ffn_up",
    )(h, w1, w3)


def _ffn_down_kernel(g_ref, w2_ref, x_ref, mods_ref, o_ref):
    t = pl.program_id(1)
    y = jnp.dot(g_ref[...], w2_ref[...].astype(BF16), preferred_element_type=F32)
    ctx_tiles = M_P // TE_DOWN
    row = jnp.where(t < ctx_tiles, 0, 1 + (t - ctx_tiles) // (DEC_SEQ // TE_DOWN))
    o_ref[...] = x_ref[...] + mods_ref[pl.ds(row, 1), :] * y


def _ffn_down_call(l, g, w2, x, mods_all):
    i = l // 2
    return pl.pallas_call(
        _ffn_down_kernel,
        grid=(D // TN_DOWN, M_TOK // TE_DOWN),
        in_specs=[
            pl.BlockSpec((TE_DOWN, D_FF), lambda j, t: (t, 0)),
            pl.BlockSpec((None, D_FF, TN_DOWN), lambda j, t: (i, 0, j)),
            pl.BlockSpec((TE_DOWN, TN_DOWN), lambda j, t: (t, j)),
            pl.BlockSpec((None, MOD_ROWS, TN_DOWN), lambda j, t: (l, 0, 5 * (D // TN_DOWN) + j)),
        ],
        out_specs=pl.BlockSpec((TE_DOWN, TN_DOWN), lambda j, t: (t, j)),
        out_shape=jax.ShapeDtypeStruct((M_TOK, D), F32),
        compiler_params=_cparams(("parallel", "parallel")),
        name="ffn_down",
    )(g, w2, x, mods_all)


def _src_kernel(cpos_ref, src_ref):
    def body(i, c):
        src_ref[cpos_ref[i]] = lax.shift_right_logical(i, TOP_K_SHIFT)
        return c
    lax.fori_loop(0, N_SLOTS, body, 0, unroll=8)


def _src_call(cpos):
    return pl.pallas_call(
        _src_kernel,
        in_specs=[pl.BlockSpec(memory_space=pltpu.SMEM)],
        out_specs=pl.BlockSpec(memory_space=pltpu.SMEM),
        out_shape=jax.ShapeDtypeStruct((N_SLOTS,), I32),
        name="moe_src",
    )(cpos)


def _for_item_rows(cnt, live, fn):
    half, quarter = MC // 2, MC // 4
    n_full = cnt // MC
    rem = cnt - n_full * MC
    whole = rem > half + quarter
    n_mc = n_full + whole.astype(I32)
    use_half = ~whole & (rem > quarter)
    use_quarter = ~whole & (rem > 0) & ((rem <= quarter) | (rem > half))

    @pl.when(live)
    def _():
        def body(c, carry):
            fn(c * MC, MC)
            return carry
        lax.fori_loop(0, n_mc, body, 0)

    @pl.when(live & use_half)
    def _():
        fn(n_mc * MC, half)

    @pl.when(live & use_quarter)
    def _():
        fn(n_mc * MC + jnp.where(use_half, half, 0), quarter)


def _gather_copy(hp_hbm, tok, buf, b, q, u, sem, slot):
    return pltpu.make_async_copy(hp_hbm.at[pl.ds(tok, 1)], buf.at[b, q, pl.ds(u, 1)], sem.at[slot])


def _moe_up_kernel(src_ref, ie_ref, ics_ref, icnt_ref, nit_ref, hp_hbm, w1_ref, w3_ref, g_ref, buf, sem):
    k = pl.program_id(0)
    j = pl.program_id(1)
    nit = nit_ref[0]
    groups = GC // SUBLANES

    def chunk_rows(item, c):
        return jnp.clip(icnt_ref[item] - c * GC, 0, GC)

    def issue_chunk(item, c, b):
        n = chunk_rows(item, c)
        base = ics_ref[item] + c * GC

        @pl.when(n > 0)
        def _():
            def body(q, carry):
                for u in range(SUBLANES):
                    tok = src_ref[base + jnp.minimum(q * SUBLANES + u, n - 1)]
                    _gather_copy(hp_hbm, tok, buf, b, c * groups + q, u, sem, c & 1).start()
                return carry
            lax.fori_loop(0, groups, body, 0)

    def finish_chunk(item, c, b):
        @pl.when(chunk_rows(item, c) > 0)
        def _():
            def body(r, carry):
                _gather_copy(hp_hbm, 0, buf, b, 0, 0, sem, c & 1).wait()
                return carry
            lax.fori_loop(0, GC, body, 0, unroll=8)

    @pl.when((k == 0) & (j == 0))
    def _():
        buf[...] = jnp.zeros_like(buf)
        issue_chunk(0, 0, 0)

        def body(c, carry):
            @pl.when(c + 1 < N_GC)
            def _():
                issue_chunk(0, c + 1, 0)
            finish_chunk(0, c, 0)
            return carry
        lax.fori_loop(0, N_GC, body, 0)

    nxt = k + 1

    @pl.when((nxt < nit) & (j >= 1) & (j <= N_GC))
    def _():
        finish_chunk(nxt, j - 1, nxt & 1)

    @pl.when((nxt < nit) & (j < N_GC))
    def _():
        issue_chunk(nxt, j, nxt & 1)

    g_ref[...] = jnp.zeros_like(g_ref)
    b = k & 1

    def mm_rows(start, size):
        tiles = pl.ds(pl.multiple_of(start // SUBLANES, size // SUBLANES), size // SUBLANES)
        lo, hi = _unpack(buf[b, tiles].reshape(size, HALF))
        lo, hi = lo.astype(BF16), hi.astype(BF16)

        def up(w_ref):
            return (jnp.dot(lo, w_ref[0:HALF, :].astype(BF16), preferred_element_type=F32)
                    + jnp.dot(hi, w_ref[HALF:D, :].astype(BF16), preferred_element_type=F32))
        a = up(w1_ref)
        rows = pl.ds(pl.multiple_of(start, size), size)
        g_ref[rows, :] = (a * jax.nn.sigmoid(a) * up(w3_ref)).astype(BF16)

    _for_item_rows(icnt_ref[k], k < nit, mm_rows)


def _moe_up_call(layer_e0, src, item_e, item_cs, item_cnt, n_items, hp, w1, w3):
    n_f = D_FFE // TF_MOE
    wspec = pl.BlockSpec((None, D, TF_MOE), lambda k, j, s, ie, ics, ic, ni: (
        layer_e0 + ie[k], 0, jnp.where(k < ni[0], j, n_f - 1)))
    return pl.pallas_call(
        _moe_up_kernel,
        grid_spec=pltpu.PrefetchScalarGridSpec(
            num_scalar_prefetch=5,
            grid=(N_ITEMS, D_FFE // TF_MOE),
            in_specs=[pl.BlockSpec(memory_space=pl.ANY), wspec, wspec],
            out_specs=pl.BlockSpec((RT, TF_MOE), lambda k, j, s, ie, ics, ic, ni: (k, j)),
            scratch_shapes=[
                pltpu.VMEM((2, RT // SUBLANES, SUBLANES, HALF), U32),
                pltpu.SemaphoreType.DMA((2,)),
            ],
        ),
        out_shape=jax.ShapeDtypeStruct((N_ITEMS * RT, D_FFE), BF16),
        compiler_params=_cparams(("arbitrary", "arbitrary")),
        name="moe_up",
    )(src, item_e, item_cs, item_cnt, n_items, hp, w1, w3)


def _moe_down_kernel(ie_ref, icnt_ref, nit_ref, g_ref, w2a_ref, w2b_ref, o_ref):
    k = pl.program_id(0)
    o_ref[...] = jnp.zeros_like(o_ref)

    def mm_rows(start, size):
        rows = pl.ds(pl.multiple_of(start, size), size)
        g = g_ref[rows, :]
        lo = jnp.dot(g, w2a_ref[...].astype(BF16), preferred_element_type=F32)
        hi = jnp.dot(g, w2b_ref[...].astype(BF16), preferred_element_type=F32)
        o_ref[rows, :] = _pack(lo, hi)

    _for_item_rows(icnt_ref[k], k < nit_ref[0], mm_rows)


def _moe_down_call(layer_e0, item_e, item_cnt, n_items, g, w2):
    live = lambda k, ni: jnp.minimum(k, ni[0] - 1)
    n_half = HALF // TN_MOE
    col = lambda k, n, ni: jnp.where(k < ni[0], n, n_half - 1)
    return pl.pallas_call(
        _moe_down_kernel,
        grid_spec=pltpu.PrefetchScalarGridSpec(
            num_scalar_prefetch=3,
            grid=(N_ITEMS, n_half),
            in_specs=[
                pl.BlockSpec((RT, D_FFE), lambda k, n, ie, ic, ni: (live(k, ni), 0)),
                pl.BlockSpec((None, D_FFE, TN_MOE),
                             lambda k, n, ie, ic, ni: (layer_e0 + ie[k], 0, col(k, n, ni))),
                pl.BlockSpec((None, D_FFE, TN_MOE),
                             lambda k, n, ie, ic, ni: (layer_e0 + ie[k], 0, n_half + col(k, n, ni))),
            ],
            out_specs=pl.BlockSpec((RT, TN_MOE), lambda k, n, ie, ic, ni: (k, n)),
        ),
        out_shape=jax.ShapeDtypeStruct((N_ITEMS * RT, HALF), U32),
        compiler_params=_cparams(("arbitrary", "arbitrary")),
        name="moe_down",
    )(item_e, item_cnt, n_items, g, w2, w2)


def _combine_copy(yo_hbm, row, buf, slot, q, u, sem):
    return pltpu.make_async_copy(yo_hbm.at[pl.ds(row, 1)], buf.at[slot, q, pl.ds(u, 1)], sem.at[slot])


def _combine_kernel(split, pos_ref, yo_hbm, x_ref, rinfo_ref, mods_ref, *rest):
    if split:
        yp_ref, ys_ref, buf, sem = rest
    else:
        o_ref, buf, sem = rest
    i = pl.program_id(0)
    n = pl.num_programs(0)
    rows = TOP_K * TM

    def issue(tile, slot):
        def body(q, c):
            for u in range(SUBLANES):
                _combine_copy(yo_hbm, pos_ref[tile * rows + q * SUBLANES + u], buf, slot, q, u, sem).start()
            return c
        lax.fori_loop(0, rows // SUBLANES, body, 0)

    @pl.when(i == 0)
    def _():
        issue(0, 0)

    @pl.when(i + 1 < n)
    def _():
        issue(i + 1, (i + 1) & 1)

    slot = i & 1

    def body(r, c):
        _combine_copy(yo_hbm, 0, buf, slot, 0, 0, sem).wait()
        return c
    lax.fori_loop(0, rows, body, 0, unroll=8)
    rinfo = rinfo_ref[...]
    tiles = TM // SUBLANES
    lo0, hi0 = _unpack(buf[slot, 0:tiles].reshape(TM, HALF))
    lo1, hi1 = _unpack(buf[slot, tiles:2 * tiles].reshape(TM, HALF))
    gt0, gt1 = rinfo[:, 2:3], rinfo[:, 3:4]
    g2 = mods_ref[pl.ds(_mod_row(i), 1), :]
    y_lo = x_ref[:, 0:HALF] + g2[:, 0:HALF] * (gt0 * lo0 + gt1 * lo1)
    y_hi = x_ref[:, HALF:D] + g2[:, HALF:D] * (gt0 * hi0 + gt1 * hi1)

    def write(ref):
        ref[:, 0:HALF] = y_lo
        ref[:, HALF:D] = y_hi

    if not split:
        write(o_ref)
        return

    @pl.when(i < P_TILES)
    def _():
        write(yp_ref)

    @pl.when(i >= P_TILES)
    def _():
        write(ys_ref)


def _combine_call(l, split, pos, yo, x, rinfo, mods_all):
    if split:
        out_specs = [pl.BlockSpec((TM, D), lambda i, p: (jnp.minimum(i, P_TILES - 1), 0)),
                     pl.BlockSpec((TM, D), lambda i, p: (jnp.maximum(i - P_TILES, 0), 0))]
        out_shape = [jax.ShapeDtypeStruct((M_P, D), F32), jax.ShapeDtypeStruct((M_S, D), F32)]
    else:
        out_specs = pl.BlockSpec((TM, D), lambda i, p: (i, 0))
        out_shape = jax.ShapeDtypeStruct((M_TOK, D), F32)
    return pl.pallas_call(
        functools.partial(_combine_kernel, split),
        grid_spec=pltpu.PrefetchScalarGridSpec(
            num_scalar_prefetch=1,
            grid=(N_TILES,),
            in_specs=[
                pl.BlockSpec(memory_space=pl.ANY),
                pl.BlockSpec((TM, D), lambda i, p: (i, 0)),
                pl.BlockSpec((TM, LANE), lambda i, p: (i, 0)),
                pl.BlockSpec((None, MOD_ROWS, D), lambda i, p: (l, 0, 5)),
            ],
            out_specs=out_specs,
            scratch_shapes=[pltpu.VMEM((2, TOP_K * TM // SUBLANES, SUBLANES, HALF), U32),
                            pltpu.SemaphoreType.DMA((2,))],
        ),
        out_shape=out_shape,
        compiler_params=_cparams(("arbitrary",)),
        name="moe_combine",
    )(pos, yo, x, rinfo, mods_all)


def _route_plan(rinfo):
    flat_e = rinfo[:, 0:TOP_K].astype(I32).reshape(-1)
    onehot = (flat_e[:, None] == jnp.arange(N_EXP, dtype=I32)[None, :]).astype(I32)
    csum = jnp.cumsum(onehot, axis=0)
    pick = lambda table: jnp.sum(onehot * table[None, :], axis=1)
    rank = jnp.sum(onehot * csum, axis=1) - 1
    counts = csum[-1]
    cstart = jnp.cumsum(counts) - counts
    n_it = (counts + RT - 1) // RT
    it_end = jnp.cumsum(n_it)
    it_base = it_end - n_it
    n_items = it_end[-1]
    cpos = pick(cstart) + rank
    ppos = (pick(it_base) + rank // RT) * RT + rank % RT
    q = jnp.arange(N_ITEMS, dtype=I32)
    qc = jnp.minimum(q, n_items - 1)
    qe = jnp.minimum(jnp.sum((it_end[None, :] <= qc[:, None]).astype(I32), axis=1), N_EXP - 1)
    qhot = (qe[:, None] == jnp.arange(N_EXP, dtype=I32)[None, :]).astype(I32)
    qpick = lambda table: jnp.sum(qhot * table[None, :], axis=1)
    s = qc - qpick(it_base)
    item_cnt = jnp.where(q < n_items, jnp.clip(qpick(counts) - s * RT, 0, RT), 0)
    item_cs = qpick(cstart) + s * RT
    pos_tiles = ppos.reshape(N_TILES, TM, TOP_K).transpose(0, 2, 1).reshape(-1)
    return (cpos.astype(I32), qe.astype(I32), item_cs.astype(I32), item_cnt.astype(I32),
            n_items.reshape(1).astype(I32), pos_tiles.astype(I32))


def _rope_tables():
    rows = DEC_SEQ // GRID_W
    row = jnp.repeat(jnp.arange(rows), GRID_W).astype(F32)
    col = jnp.tile(jnp.arange(GRID_W), rows).astype(F32)
    nf = ROPE // 4
    inv = ROPE_BASE ** (-jnp.arange(nf, dtype=F32) / nf)
    ar = row[:, None] * inv[None, :]
    ac = col[:, None] * inv[None, :]
    ang = jnp.concatenate([ar, ar, ac, ac], axis=-1)
    cos, sin = jnp.cos(ang), jnp.sin(ang)
    first = (jnp.arange(ROPE) % 32) < 16
    pad = lambda a, v: jnp.pad(a, ((0, 0), (0, LANE - ROPE)), constant_values=v)
    cos_p = pad(cos, 1.0)
    sin_a = pad(jnp.where(first[None, :], -sin, 0.0), 0.0)
    sin_b = pad(jnp.where(first[None, :], 0.0, sin), 0.0)
    ident = jnp.stack([jnp.ones((TMI, LANE), F32), jnp.zeros((TMI, LANE), F32), jnp.zeros((TMI, LANE), F32)])
    return jnp.concatenate([ident, jnp.stack([cos_p, sin_a, sin_b])], axis=1)


def kernel(x_prompt, x_sample, cache_ckv, cache_krope, c, c_ctx, w_ada, b_ada, norm1_g, norm2_g, w_in,
           qlat_norm_g, kvlat_norm_g, w_uq, w_ukv, q_head_norm_g, k_head_norm_g, gm_norm_g, gm_w_s,
           gm_b_s, pool_w, pool_scale, w_o, ffn_w1, ffn_w3, ffn_w2, router_w, moe_w1, moe_w3, moe_w2):
    pad_w = IN_PAD - w_in.shape[-1]
    col = jnp.arange(IN_PAD)[None, None, :]
    w_in_p = jnp.where(
        col < C_KR + ROPE, jnp.pad(w_in, ((0, 0), (0, 0), (0, pad_w))),
        jnp.where(col < C_KR + LANE, 0.0, jnp.pad(w_in, ((0, 0), (0, 0), (pad_w, 0))))).astype(BF16)
    w_uq_p = jnp.pad(w_uq.reshape(DEPTH, Q_RANK, N_HEADS, QK_DIM),
                     ((0, 0), (0, 0), (0, 0), (0, HEAD_PAD - QK_DIM))
                     ).reshape(DEPTH, Q_RANK, QK_W).astype(BF16)
    w_ukv_b = w_ukv.astype(BF16)
    w_o_b = w_o.astype(BF16)
    qg_p = jnp.pad(q_head_norm_g * (1.0 / math.sqrt(QK_DIM)),
                   ((0, 0), (0, HEAD_PAD - QK_DIM))).reshape(DEPTH, 1, HEAD_PAD)
    kg_n = k_head_norm_g[:, :NOPE].reshape(DEPTH, 1, LANE)
    kg_r = jnp.pad(k_head_norm_g[:, NOPE:], ((0, 0), (0, LANE - ROPE))).reshape(DEPTH, 1, LANE)
    gmw_b = gm_w_s.astype(BF16)
    gmb = jnp.broadcast_to(gm_b_s[:, :, :, None], (DEPTH, GM_GROUPS, CHUNK, LANE))
    pw_b = pool_w.astype(BF16)
    ps = pool_scale.reshape(DEPTH, 1, POOL_W)
    n1g = norm1_g.reshape(DEPTH, 1, D)
    n2g = norm2_g.reshape(DEPTH, 1, D)
    qlg = qlat_norm_g.reshape(DEPTH, 1, Q_RANK)
    kvlg = kvlat_norm_g.reshape(DEPTH, 1, KV_RANK)
    rw_p = jnp.pad(router_w, ((0, 0), (0, 0), (0, LANE - N_EXP)))
    n_moe = moe_w1.shape[0]
    moe_w1_flat = moe_w1.reshape(n_moe * N_EXP, D, D_FFE)
    moe_w3_flat = moe_w3.reshape(n_moe * N_EXP, D, D_FFE)
    moe_w2_flat = moe_w2.reshape(n_moe * N_EXP, D_FFE, D)
    rope_tab = _rope_tables()
    cache_kr_p = jnp.pad(cache_krope, ((0, 0), (0, 0), (0, 0), (0, LANE - ROPE)))

    cond = jnp.zeros((MOD_ROWS, D), F32).at[0].set(c_ctx).at[1:1 + DEC_BATCH].set(c)
    mods_all = _ada_call(cond, w_ada, b_ada)
    k_c, v_c = _cache_kv_call(cache_ckv, cache_kr_p, w_ukv_b, kg_n, kg_r)

    x = (x_prompt.reshape(M_P, D), x_sample.reshape(M_S, D))
    ckv_out, kr_out = [], []
    for l in range(DEPTH):
        q, k, v, ckv, kr, zg, zp = _in_proj_call(
            l, x, mods_all, n1g, w_in_p, qlg, kvlg, w_uq_p, qg_p, w_ukv_b, kg_n, kg_r, rope_tab,
            gm_norm_g, gmw_b, gmb)
        ckv_out.append(ckv[:M_P].reshape(BATCH, SEQ, KV_RANK))
        kr_out.append(kr[:M_P, :ROPE].reshape(BATCH, SEQ, ROPE))
        attn = _attn_call(l, q, k, v, k_c, v_c)
        i = l // 2
        if l % 2 == 0:
            x, h2b = _out_proj_call(l, False, attn, zg, zp, x, mods_all, pw_b, ps, w_o_b, n2g)
            g = _ffn_up_call(i, h2b, ffn_w1, ffn_w3)
            x = _ffn_down_call(l, g, ffn_w2, x, mods_all)
        else:
            x, hp, rinfo = _out_proj_call(l, True, attn, zg, zp, x, mods_all, pw_b, ps, w_o_b, n2g, rw_p)
            cpos, item_e, item_cs, item_cnt, n_items, pos_tiles = _route_plan(rinfo)
            src = _src_call(cpos)
            g = _moe_up_call(i * N_EXP, src, item_e, item_cs, item_cnt, n_items, hp,
                             moe_w1_flat, moe_w3_flat)
            yo = _moe_down_call(i * N_EXP, item_e, item_cnt, n_items, g, moe_w2_flat)
            x = _combine_call(l, l == DEPTH - 1, pos_tiles, yo, x, rinfo, mods_all)

    if DEPTH % 2 == 0:
        y_p, y_s = x
    else:
        y_p, y_s = x[:M_P], x[M_P:]
    return (y_p.reshape(BATCH, SEQ, D), y_s.reshape(DEC_BATCH, DEC_SEQ, D),
            jnp.stack(ckv_out, axis=1), jnp.stack(kr_out, axis=1))
```

```python
import functools
import math

import jax
import jax.numpy as jnp
from jax import lax
from jax.experimental import pallas as pl
from jax.experimental.pallas import tpu as pltpu

F32 = jnp.float32
BF16 = jnp.bfloat16
U32 = jnp.uint32
I32 = jnp.int32

D = 2048
BATCH, SEQ = 16, 256
DEC_BATCH, DEC_SEQ = 2, 1024
PAST = 256
DEPTH = 4
GRID_W = 64
EPS = 1e-6
N_HEADS = 8
NOPE, ROPE, VDIM = 128, 64, 128
QK_DIM = NOPE + ROPE
HEAD_PAD = 256
QK_W = N_HEADS * HEAD_PAD
Q_RANK, KV_RANK = 512, 256
ROPE_BASE = 10000.0
ATTN_W = N_HEADS * VDIM
CHUNK, GM_GROUPS, GM_W = 128, 4, 512
POOL_WINDOWS = (2, 4, 8, 16)
POOL_W = 512
POOL_HALO = 8
D_FF, N_EXP, D_FFE, TOP_K = 5632, 8, 2816, 2
TOP_K_SHIFT = 1

C_Q, C_KV, C_KR, C_GM, C_PL = 0, 512, 768, 896, 1920
IN_PAD = 2432

LANE = 128
SUBLANES = 8
TM = 256
TMI = 512
TMO = 256
M_P, M_S = BATCH * SEQ, DEC_BATCH * DEC_SEQ
M_TOK = M_P + M_S
N_TILES = M_TOK // TM
P_TILES = M_P // TM
S_TILES = DEC_SEQ // TM
MOD_ROWS = 16
ADA_TN = 1024
TE_UP, TF_UP = 1024, 512
TE_DOWN, TN_DOWN = 512, 512
HALF = D // 2
N_SLOTS = TOP_K * M_TOK
RT = 2048
N_ITEMS = N_EXP + (N_SLOTS - N_EXP) // RT
GC = 256
N_GC = RT // GC
MC = 1024
TF_MOE = 256
TN_MOE = 256
V7X_VMEM_BYTES = 64 * 1024 * 1024
VMEM_LIMIT = V7X_VMEM_BYTES * 7 // 8
assert N_GC + 1 <= D_FFE // TF_MOE


def _cparams(sem):
    return pltpu.CompilerParams(dimension_semantics=sem, vmem_limit_bytes=VMEM_LIMIT)


def _layer_spec(l, shape):
    nd = len(shape)
    return pl.BlockSpec((None,) + tuple(shape), lambda *_: (l,) + (0,) * nd)


def _x_operands(x, tile):
    if not isinstance(x, tuple):
        return [x], [pl.BlockSpec((tile, D), lambda i: (i, 0))]
    ctx = M_P // tile
    return list(x), [pl.BlockSpec((tile, D), lambda i: (jnp.minimum(i, ctx - 1), 0)),
                     pl.BlockSpec((tile, D), lambda i: (jnp.maximum(i - ctx, 0), 0))]


def _mod_row(i):
    return jnp.where(i < P_TILES, 0, 1 + (i - P_TILES) // S_TILES)


def _rms(x, g, n):
    ms = jnp.sum(x * x, axis=-1, keepdims=True) * (1.0 / n)
    return x * lax.rsqrt(ms + EPS) * g


def _rope(r, cos, sin_a, sin_b):
    return r * cos + pltpu.roll(r, LANE - 16, 1) * sin_a + pltpu.roll(r, 16, 1) * sin_b


def _pack(lo, hi):
    lo_b = lax.bitcast_convert_type(lo.astype(BF16).astype(F32), U32)
    hi_b = lax.bitcast_convert_type(hi.astype(BF16).astype(F32), U32)
    return (lo_b >> 16) | hi_b


def _unpack(w):
    lo = lax.bitcast_convert_type(w << 16, F32)
    hi = lax.bitcast_convert_type(w & jnp.uint32(0xFFFF0000), F32)
    return lo, hi


def _ada_kernel(cond_ref, w_ref, b_ref, o_ref):
    c = cond_ref[...]
    s = c * jax.nn.sigmoid(c)
    s_hi = s.astype(BF16)
    s_lo = (s - s_hi.astype(F32)).astype(BF16)
    w = w_ref[...]
    w_hi = w.astype(BF16)
    w_lo = (w - w_hi.astype(F32)).astype(BF16)
    lhs = jnp.concatenate([s_hi, s_lo], axis=0)
    r1 = jnp.dot(lhs, w_hi, preferred_element_type=F32)
    r2 = jnp.dot(s_hi, w_lo, preferred_element_type=F32)
    o_ref[...] = r1[:MOD_ROWS] + r1[MOD_ROWS:] + r2 + b_ref[...]


def _ada_call(cond, w_ada, b_ada):
    n = 6 * D
    return pl.pallas_call(
        _ada_kernel,
        grid=(DEPTH, n // ADA_TN),
        in_specs=[
            pl.BlockSpec((MOD_ROWS, D), lambda l, j: (0, 0)),
            pl.BlockSpec((None, D, ADA_TN), lambda l, j: (l, 0, j)),
            pl.BlockSpec((None, 1, ADA_TN), lambda l, j: (l, 0, j)),
        ],
        out_specs=pl.BlockSpec((None, MOD_ROWS, ADA_TN), lambda l, j: (l, 0, j)),
        out_shape=jax.ShapeDtypeStruct((DEPTH, MOD_ROWS, n), F32),
        compiler_params=_cparams(("parallel", "parallel")),
        name="ada",
    )(cond, w_ada, b_ada.reshape(DEPTH, 1, n))


def _write_kv(ckv_b, kr, w_ukv_ref, kg_n, kg_r, rope, k_ref, v_ref, rows=slice(None)):
    kv = jnp.dot(ckv_b, w_ukv_ref[...], preferred_element_type=F32)
    ssr = jnp.sum(kr * kr, axis=-1, keepdims=True)
    shared = kr * kg_r
    if rope is not None:
        shared = _rope(shared, *rope)
    for h in range(N_HEADS):
        kn = kv[:, h * HEAD_PAD:h * HEAD_PAD + NOPE]
        vv = kv[:, h * HEAD_PAD + NOPE:(h + 1) * HEAD_PAD]
        ms = (jnp.sum(kn * kn, axis=-1, keepdims=True) + ssr) * (1.0 / QK_DIM)
        inv = lax.rsqrt(ms + EPS)
        k_ref[rows, h * HEAD_PAD:h * HEAD_PAD + NOPE] = (kn * inv * kg_n).astype(BF16)
        k_ref[rows, h * HEAD_PAD + NOPE:(h + 1) * HEAD_PAD] = (shared * inv).astype(BF16)
        v_ref[rows, h * VDIM:(h + 1) * VDIM] = vv.astype(BF16)


def _cache_kv_kernel(ckv_ref, kr_ref, w_ukv_ref, kgn_ref, kgr_ref, k_ref, v_ref):
    _write_kv(ckv_ref[...].astype(BF16), kr_ref[...], w_ukv_ref, kgn_ref[...], kgr_ref[...],
              None, k_ref, v_ref)


def _cache_kv_call(cache_ckv, cache_kr_p, w_ukv_b, kg_n, kg_r):
    return pl.pallas_call(
        _cache_kv_kernel,
        grid=(DEPTH, DEC_BATCH),
        in_specs=[
            pl.BlockSpec((None, None, PAST, KV_RANK), lambda l, b: (b, l, 0, 0)),
            pl.BlockSpec((None, None, PAST, LANE), lambda l, b: (b, l, 0, 0)),
            pl.BlockSpec((None, KV_RANK, QK_W), lambda l, b: (l, 0, 0)),
            pl.BlockSpec((None, 1, LANE), lambda l, b: (l, 0, 0)),
            pl.BlockSpec((None, 1, LANE), lambda l, b: (l, 0, 0)),
        ],
        out_specs=[
            pl.BlockSpec((None, None, PAST, QK_W), lambda l, b: (l, b, 0, 0)),
            pl.BlockSpec((None, None, PAST, ATTN_W), lambda l, b: (l, b, 0, 0)),
        ],
        out_shape=[
            jax.ShapeDtypeStruct((DEPTH, DEC_BATCH, PAST, QK_W), BF16),
            jax.ShapeDtypeStruct((DEPTH, DEC_BATCH, PAST, ATTN_W), BF16),
        ],
        compiler_params=_cparams(("parallel", "parallel")),
        name="cache_kv",
    )(cache_ckv, cache_kr_p, w_ukv_b, kg_n, kg_r)


def _in_proj_kernel(two_x, *refs):
    x_refs, refs = refs[:1 + two_x], refs[1 + two_x:]
    _in_proj_body(x_refs, *refs)


def _in_proj_body(x_refs, mods_ref, n1g_ref, w_in_ref, qlg_ref, kvlg_ref, w_uq_ref, qg_ref,
                    w_ukv_ref, kgn_ref, kgr_ref, rope_ref, gmg_ref, gmw_ref, gmb_ref,
                    q_ref, k_ref, v_ref, ckv_ref, kr_ref, zg_ref, zp_ref):
    i = pl.program_id(0)
    row = jnp.where(i < M_P // TMI, 0, 1 + (i - M_P // TMI) // (DEC_SEQ // TMI))
    m = mods_ref[pl.ds(row, 1), :]
    sh1, sc1 = m[:, 0:D], m[:, D:2 * D]
    for sub in range(TMI // TM):
        rs = slice(sub * TM, (sub + 1) * TM)
        x = x_refs[0][rs, :] if len(x_refs) == 1 else jnp.where(i < M_P // TMI, x_refs[0][rs, :], x_refs[1][rs, :])
        h = _rms(x, n1g_ref[...], D) * (1.0 + sc1) + sh1
        z = jnp.dot(h.astype(BF16), w_in_ref[...], preferred_element_type=F32)
        rope = (rope_ref[0, rs, :], rope_ref[1, rs, :], rope_ref[2, rs, :])

        ql = _rms(z[:, C_Q:C_Q + Q_RANK], qlg_ref[...], Q_RANK)
        qf = jnp.dot(ql.astype(BF16), w_uq_ref[...], preferred_element_type=F32)
        qg = qg_ref[...]
        for hd in range(N_HEADS):
            seg = qf[:, hd * HEAD_PAD:(hd + 1) * HEAD_PAD]
            seg = _rms(seg, qg, QK_DIM)
            q_ref[rs, hd * HEAD_PAD:hd * HEAD_PAD + NOPE] = seg[:, :NOPE].astype(BF16)
            q_ref[rs, hd * HEAD_PAD + NOPE:(hd + 1) * HEAD_PAD] = _rope(seg[:, NOPE:], *rope).astype(BF16)

        ckv = _rms(z[:, C_KV:C_KV + KV_RANK], kvlg_ref[...], KV_RANK)
        kr = z[:, C_KR:C_KR + LANE]
        ckv_ref[rs, :] = ckv
        kr_ref[rs, :] = kr
        _write_kv(ckv.astype(BF16), kr, w_ukv_ref, kgn_ref[...], kgr_ref[...], rope, k_ref, v_ref, rs)

        zg = jax.nn.gelu(z[:, C_GM:C_GM + 2 * GM_W], approximate=True)
        for g in range(GM_GROUPS):
            u = zg[:, g * LANE:(g + 1) * LANE]
            vg = _rms(zg[:, GM_W + g * LANE:GM_W + (g + 1) * LANE], gmg_ref[g:g + 1, :], LANE)
            vb = vg.astype(BF16)
            for c in range(TM // CHUNK):
                sv = jnp.dot(gmw_ref[g], vb[c * CHUNK:(c + 1) * CHUNK, :], preferred_element_type=F32)
                sv = sv + gmb_ref[g]
                r0 = sub * TM + c * CHUNK
                zg_ref[r0:r0 + CHUNK, g * LANE:(g + 1) * LANE] = (
                    u[c * CHUNK:(c + 1) * CHUNK, :] * sv).astype(BF16)

        zp_ref[rs, :] = z[:, C_PL:C_PL + POOL_W]


def _in_proj_call(l, x, mods_all, n1g, w_in_b, qlg, kvlg, w_uq_b, qg, w_ukv_b, kg_n, kg_r, rope_tab,
                  gmg, gmw_b, gmb):
    def rope_idx(i):
        return (0, jnp.where(i < M_P // TMI, 0, 1 + (i - M_P // TMI) % (DEC_SEQ // TMI)), 0)

    row = lambda w: pl.BlockSpec((TMI, w), lambda i: (i, 0))
    two_x = isinstance(x, tuple)
    x_args, x_specs = _x_operands(x, TMI)
    return pl.pallas_call(
        functools.partial(_in_proj_kernel, two_x),
        grid=(M_TOK // TMI,),
        in_specs=x_specs + [
            _layer_spec(l, (MOD_ROWS, 6 * D)),
            _layer_spec(l, (1, D)),
            _layer_spec(l, (D, IN_PAD)),
            _layer_spec(l, (1, Q_RANK)),
            _layer_spec(l, (1, KV_RANK)),
            _layer_spec(l, (Q_RANK, QK_W)),
            _layer_spec(l, (1, HEAD_PAD)),
            _layer_spec(l, (KV_RANK, QK_W)),
            _layer_spec(l, (1, LANE)),
            _layer_spec(l, (1, LANE)),
            pl.BlockSpec((3, TMI, LANE), rope_idx),
            _layer_spec(l, (GM_GROUPS, LANE)),
            _layer_spec(l, (GM_GROUPS, CHUNK, CHUNK)),
            _layer_spec(l, (GM_GROUPS, CHUNK, LANE)),
        ],
        out_specs=[row(QK_W), row(QK_W), row(ATTN_W), row(KV_RANK), row(LANE), row(GM_W), row(POOL_W)],
        out_shape=[
            jax.ShapeDtypeStruct((M_TOK, QK_W), BF16),
            jax.ShapeDtypeStruct((M_TOK, QK_W), BF16),
            jax.ShapeDtypeStruct((M_TOK, ATTN_W), BF16),
            jax.ShapeDtypeStruct((M_TOK, KV_RANK), F32),
            jax.ShapeDtypeStruct((M_TOK, LANE), F32),
            jax.ShapeDtypeStruct((M_TOK, GM_W), BF16),
            jax.ShapeDtypeStruct((M_TOK, POOL_W), F32),
        ],
        compiler_params=_cparams(("parallel",)),
        name="in_proj",
    )(*x_args, mods_all, n1g, w_in_b, qlg, kvlg, w_uq_b, qg, w_ukv_b, kg_n, kg_r, rope_tab, gmg, gmw_b, gmb)


def _attend_heads(q_ref, kv_pairs, o_ref):
    nt = (((1,), (1,)), ((), ()))
    for h in range(N_HEADS):
        qh = q_ref[:, h * HEAD_PAD:(h + 1) * HEAD_PAD]
        ss = [lax.dot_general(qh, k_ref[:, h * HEAD_PAD:(h + 1) * HEAD_PAD], nt,
                              preferred_element_type=F32) for k_ref, _ in kv_pairs]
        mx = ss[0].max(axis=-1, keepdims=True)
        for s in ss[1:]:
            mx = jnp.maximum(mx, s.max(axis=-1, keepdims=True))
        den = None
        acc = None
        for s, (_, v_ref) in zip(ss, kv_pairs):
            p = jnp.exp(s - mx)
            d = jnp.sum(p, axis=-1, keepdims=True)
            a = jnp.dot(p.astype(BF16), v_ref[:, h * VDIM:(h + 1) * VDIM], preferred_element_type=F32)
            den = d if den is None else den + d
            acc = a if acc is None else acc + a
        o_ref[:, h * VDIM:(h + 1) * VDIM] = (acc / den).astype(BF16)


def _attn_kernel(q_ref, kp_ref, vp_ref, ks_ref, vs_ref, kc_ref, vc_ref, o_ref):
    i = pl.program_id(0)

    @pl.when(i < P_TILES)
    def _():
        _attend_heads(q_ref, [(kp_ref, vp_ref)], o_ref)

    @pl.when(i >= P_TILES)
    def _():
        _attend_heads(q_ref, [(ks_ref, vs_ref), (kc_ref, vc_ref)], o_ref)


def _attn_call(l, q, k, v, k_c, v_c):
    def seq_b(i):
        return jnp.maximum(i - P_TILES, 0) // S_TILES

    small = lambda w: pl.BlockSpec((TM, w), lambda i: (jnp.minimum(i, P_TILES - 1), 0))
    big = lambda w: pl.BlockSpec((DEC_SEQ, w), lambda i: (M_P // DEC_SEQ + seq_b(i), 0))
    cache = lambda w: pl.BlockSpec((None, None, PAST, w), lambda i: (l, seq_b(i), 0, 0))
    return pl.pallas_call(
        _attn_kernel,
        grid=(N_TILES,),
        in_specs=[
            pl.BlockSpec((TM, QK_W), lambda i: (i, 0)),
            small(QK_W), small(ATTN_W),
            big(QK_W), big(ATTN_W),
            cache(QK_W), cache(ATTN_W),
        ],
        out_specs=pl.BlockSpec((TM, ATTN_W), lambda i: (i, 0)),
        out_shape=jax.ShapeDtypeStruct((M_TOK, ATTN_W), BF16),
        compiler_params=_cparams(("parallel",)),
        name="attention",
    )(q, k, v, k, v, k_c, v_c)


def _out_proj_kernel(moe, two_x, attn_ref, zg_ref, zprev_ref, zcur_ref, znext_ref, *rest):
    x_refs, rest = rest[:1 + two_x], rest[1 + two_x:]
    mods_ref, pw_ref, ps_ref, w_o_ref, n2g_ref = rest[:5]
    rest = rest[5:]
    if moe:
        rw_ref, xo_ref, hp_ref, rinfo_ref, ext_ref, mix_ref = rest
    else:
        xo_ref, h2b_ref, ext_ref, mix_ref = rest
    i = pl.program_id(0)
    ctx_steps = M_P // TMO
    is_ctx = i < ctx_steps
    seq_len = jnp.where(is_ctx, SEQ, DEC_SEQ)
    m = mods_ref[pl.ds(jnp.where(is_ctx, 0, 1 + (i - ctx_steps) // (DEC_SEQ // TMO)), 1), :]
    g1, sh2, sc2 = m[:, 2 * D:3 * D], m[:, 3 * D:4 * D], m[:, 4 * D:5 * D]

    pos0 = jnp.where(is_ctx, 0, ((i - ctx_steps) % (DEC_SEQ // TMO)) * TMO)
    zero_halo = jnp.zeros((POOL_HALO, POOL_W), F32)
    ext_ref[0:POOL_HALO, :] = jnp.where(pos0 == 0, zero_halo, zprev_ref[TMO - POOL_HALO:, :])
    ext_ref[POOL_HALO:POOL_HALO + TMO, :] = zcur_ref[...]
    ext_ref[POOL_HALO + TMO:, :] = jnp.where(pos0 + TMO == seq_len, zero_halo, znext_ref[0:POOL_HALO, :])
    mix_ref[:, 0:ATTN_W] = attn_ref[...]
    mix_ref[:, ATTN_W:ATTN_W + GM_W] = zg_ref[...]

    assert TMO == TM and SEQ % TMO == 0 and DEC_SEQ % TMO == 0
    for sub in range(TMO // TM):
        rs = slice(sub * TM, (sub + 1) * TM)
        pos = pos0 + lax.broadcasted_iota(I32, (TM, 1), 0)
        for gi, w in enumerate(POOL_WINDOWS):
            cols = slice(gi * LANE, (gi + 1) * LANE)
            win = None
            for dlt in range(-(w // 2), w - w // 2):
                r0 = POOL_HALO + dlt
                part = ext_ref[r0:r0 + TM, cols]
                win = part if win is None else win + part
            lo = jnp.maximum(pos - w // 2, 0)
            hi = jnp.minimum(pos + (w - 1 - w // 2), seq_len - 1)
            cnt = (hi - lo + 1).astype(F32)
            pooled = win / cnt - zcur_ref[rs, cols]
            mixed = jnp.dot(pooled.astype(BF16), pw_ref[gi], preferred_element_type=F32)
            mix_ref[rs, ATTN_W + GM_W + gi * LANE:ATTN_W + GM_W + (gi + 1) * LANE] = (
                mixed * ps_ref[:, cols]).astype(BF16)

        out = jnp.dot(mix_ref[rs, :], w_o_ref[...], preferred_element_type=F32)
        x = x_refs[0][rs, :] if len(x_refs) == 1 else jnp.where(is_ctx, x_refs[0][rs, :], x_refs[1][rs, :])
        xn = x + g1 * out
        xo_ref[rs, :] = xn
        h2 = _rms(xn, n2g_ref[...], D) * (1.0 + sc2) + sh2

        if not moe:
            h2b_ref[rs, :] = h2.astype(BF16)
            continue

        hp_ref[rs, :] = _pack(h2[:, :HALF], h2[:, HALF:])
        h_hi = h2.astype(BF16)
        h_lo = (h2 - h_hi.astype(F32)).astype(BF16)
        rw = rw_ref[...]
        w_hi = rw.astype(BF16)
        w_lo = (rw - w_hi.astype(F32)).astype(BF16)
        logits = (jnp.dot(h_hi, w_hi, preferred_element_type=F32)
                  + jnp.dot(h_lo, w_hi, preferred_element_type=F32)
                  + jnp.dot(h_hi, w_lo, preferred_element_type=F32))
        lane = lax.broadcasted_iota(I32, (TM, LANE), 1).astype(F32)
        neg = jnp.float32(-jnp.inf)
        logits = jnp.where(lane < N_EXP, logits, neg)
        v1 = logits.max(axis=-1, keepdims=True)
        i1 = jnp.where(logits == v1, lane, float(LANE)).min(axis=-1, keepdims=True)
        l2 = jnp.where(lane == i1, neg, logits)
        v2 = l2.max(axis=-1, keepdims=True)
        i2 = jnp.where(l2 == v2, lane, float(LANE)).min(axis=-1, keepdims=True)
        e2 = jnp.exp(v2 - v1)
        gate1 = 1.0 / (1.0 + e2)
        gate2 = e2 / (1.0 + e2)
        rinfo_ref[rs, :] = jnp.where(lane == 0, i1, jnp.where(lane == 1, i2, jnp.where(
            lane == 2, gate1, jnp.where(lane == 3, gate2, 0.0))))


def _out_proj_call(l, moe, attn, zg, zp, x, mods_all, pw_b, ps, w_o_b, n2g, rw_p=None):
    n_steps = M_TOK // TMO
    row = lambda w: pl.BlockSpec((TMO, w), lambda i: (i, 0))
    two_x = isinstance(x, tuple)
    x_args, x_specs = _x_operands(x, TMO)
    in_specs = [
        row(ATTN_W), row(GM_W),
        pl.BlockSpec((TMO, POOL_W), lambda i: (jnp.maximum(i - 1, 0), 0)),
        row(POOL_W),
        pl.BlockSpec((TMO, POOL_W), lambda i: (jnp.minimum(i + 1, n_steps - 1), 0)),
    ] + x_specs + [
        _layer_spec(l, (MOD_ROWS, 6 * D)),
        _layer_spec(l, (len(POOL_WINDOWS), LANE, LANE)),
        _layer_spec(l, (1, POOL_W)),
        _layer_spec(l, (D, D)),
        _layer_spec(l, (1, D)),
    ]
    args = [attn, zg, zp, zp, zp, *x_args, mods_all, pw_b, ps, w_o_b, n2g]
    if moe:
        in_specs.append(_layer_spec(l // 2, (D, LANE)))
        args.append(rw_p)
        out_specs = [row(D), row(HALF), row(LANE)]
        out_shape = [jax.ShapeDtypeStruct((M_TOK, D), F32), jax.ShapeDtypeStruct((M_TOK, HALF), U32),
                     jax.ShapeDtypeStruct((M_TOK, LANE), F32)]
    else:
        out_specs = [row(D), row(D)]
        out_shape = [jax.ShapeDtypeStruct((M_TOK, D), F32), jax.ShapeDtypeStruct((M_TOK, D), BF16)]
    return pl.pallas_call(
        functools.partial(_out_proj_kernel, moe, two_x),
        grid=(n_steps,),
        in_specs=in_specs,
        out_specs=out_specs,
        out_shape=out_shape,
        scratch_shapes=[pltpu.VMEM((TMO + 2 * POOL_HALO, POOL_W), F32), pltpu.VMEM((TMO, D), BF16)],
        compiler_params=_cparams(("parallel",)),
        name="out_proj_moe" if moe else "out_proj",
    )(*args)


def _ffn_up_kernel(h_ref, w1_ref, w3_ref, g_ref):
    h = h_ref[...]
    a = jnp.dot(h, w1_ref[...].astype(BF16), preferred_element_type=F32)
    b = jnp.dot(h, w3_ref[...].astype(BF16), preferred_element_type=F32)
    g_ref[...] = (a * jax.nn.sigmoid(a) * b).astype(BF16)


def _ffn_up_call(i, h, w1, w3):
    wspec = pl.BlockSpec((None, D, TF_UP), lambda j, t: (i, 0, j))
    return pl.pallas_call(
        _ffn_up_kernel,
        grid=(D_FF // TF_UP, M_TOK // TE_UP),
        in_specs=[pl.BlockSpec((TE_UP, D), lambda j, t: (t, 0)), wspec, wspec],
        out_specs=pl.BlockSpec((TE_UP, TF_UP), lambda j, t: (t, j)),
        out_shape=jax.ShapeDtypeStruct((M_TOK, D_FF), BF16),
        compiler_params=_cparams(("parallel", "parallel")),
        name="ffn_up",
    )(h, w1, w3)


def _ffn_down_kernel(g_ref, w2_ref, x_ref, mods_ref, o_ref):
    t = pl.program_id(1)
    y = jnp.dot(g_ref[...], w2_ref[...].astype(BF16), preferred_element_type=F32)
    ctx_tiles = M_P // TE_DOWN
    row = jnp.where(t < ctx_tiles, 0, 1 + (t - ctx_tiles) // (DEC_SEQ // TE_DOWN))
    o_ref[...] = x_ref[...] + mods_ref[pl.ds(row, 1), :] * y


def _ffn_down_call(l, g, w2, x, mods_all):
    i = l // 2
    return pl.pallas_call(
        _ffn_down_kernel,
        grid=(D // TN_DOWN, M_TOK // TE_DOWN),
        in_specs=[
            pl.BlockSpec((TE_DOWN, D_FF), lambda j, t: (t, 0)),
            pl.BlockSpec((None, D_FF, TN_DOWN), lambda j, t: (i, 0, j)),
            pl.BlockSpec((TE_DOWN, TN_DOWN), lambda j, t: (t, j)),
            pl.BlockSpec((None, MOD_ROWS, TN_DOWN), lambda j, t: (l, 0, 5 * (D // TN_DOWN) + j)),
        ],
        out_specs=pl.BlockSpec((TE_DOWN, TN_DOWN), lambda j, t: (t, j)),
        out_shape=jax.ShapeDtypeStruct((M_TOK, D), F32),
        compiler_params=_cparams(("parallel", "parallel")),
        name="ffn_down",
    )(g, w2, x, mods_all)


def _src_kernel(cpos_ref, src_ref):
    def body(i, c):
        src_ref[cpos_ref[i]] = lax.shift_right_logical(i, TOP_K_SHIFT)
        return c
    lax.fori_loop(0, N_SLOTS, body, 0, unroll=8)


def _src_call(cpos):
    return pl.pallas_call(
        _src_kernel,
        in_specs=[pl.BlockSpec(memory_space=pltpu.SMEM)],
        out_specs=pl.BlockSpec(memory_space=pltpu.SMEM),
        out_shape=jax.ShapeDtypeStruct((N_SLOTS,), I32),
        name="moe_src",
    )(cpos)


def _for_item_rows(cnt, live, fn):
    half, quarter = MC // 2, MC // 4
    n_full = cnt // MC
    rem = cnt - n_full * MC
    whole = rem > half + quarter
    n_mc = n_full + whole.astype(I32)
    use_half = ~whole & (rem > quarter)
    use_quarter = ~whole & (rem > 0) & ((rem <= quarter) | (rem > half))

    @pl.when(live)
    def _():
        def body(c, carry):
            fn(c * MC, MC)
            return carry
        lax.fori_loop(0, n_mc, body, 0)

    @pl.when(live & use_half)
    def _():
        fn(n_mc * MC, half)

    @pl.when(live & use_quarter)
    def _():
        fn(n_mc * MC + jnp.where(use_half, half, 0), quarter)


def _gather_copy(hp_hbm, tok, buf, b, q, u, sem, slot):
    return pltpu.make_async_copy(hp_hbm.at[pl.ds(tok, 1)], buf.at[b, q, pl.ds(u, 1)], sem.at[slot])


def _moe_up_kernel(src_ref, ie_ref, ics_ref, icnt_ref, nit_ref, hp_hbm, w1_ref, w3_ref, g_ref, buf, sem):
    k = pl.program_id(0)
    j = pl.program_id(1)
    nit = nit_ref[0]
    groups = GC // SUBLANES

    def chunk_rows(item, c):
        return jnp.clip(icnt_ref[item] - c * GC, 0, GC)

    def issue_chunk(item, c, b):
        n = chunk_rows(item, c)
        base = ics_ref[item] + c * GC

        @pl.when(n > 0)
        def _():
            def body(q, carry):
                for u in range(SUBLANES):
                    tok = src_ref[base + jnp.minimum(q * SUBLANES + u, n - 1)]
                    pltpu.async_copy(hp_hbm.at[pl.ds(tok, 1)], buf.at[b, c * groups + q, pl.ds(u, 1)],
                                     sem.at[c & 1], priority=u % 2)
                return carry
            lax.fori_loop(0, groups, body, 0)

    def finish_chunk(item, c, b):
        @pl.when(chunk_rows(item, c) > 0)
        def _():
            def body(r, carry):
                _gather_copy(hp_hbm, 0, buf, b, 0, 0, sem, c & 1).wait()
                return carry
            lax.fori_loop(0, GC, body, 0, unroll=8)

    @pl.when((k == 0) & (j == 0))
    def _():
        buf[...] = jnp.zeros_like(buf)
        issue_chunk(0, 0, 0)

        def body(c, carry):
            @pl.when(c + 1 < N_GC)
            def _():
                issue_chunk(0, c + 1, 0)
            finish_chunk(0, c, 0)
            return carry
        lax.fori_loop(0, N_GC, body, 0)

    nxt = k + 1

    @pl.when((nxt < nit) & (j >= 1) & (j <= N_GC))
    def _():
        finish_chunk(nxt, j - 1, nxt & 1)

    @pl.when((nxt < nit) & (j < N_GC))
    def _():
        issue_chunk(nxt, j, nxt & 1)

    g_ref[...] = jnp.zeros_like(g_ref)
    b = k & 1

    def mm_rows(start, size):
        tiles = pl.ds(pl.multiple_of(start // SUBLANES, size // SUBLANES), size // SUBLANES)
        lo, hi = _unpack(buf[b, tiles].reshape(size, HALF))
        lo, hi = lo.astype(BF16), hi.astype(BF16)

        def up(w_ref):
            return (jnp.dot(lo, w_ref[0:HALF, :].astype(BF16), preferred_element_type=F32)
                    + jnp.dot(hi, w_ref[HALF:D, :].astype(BF16), preferred_element_type=F32))
        a = up(w1_ref)
        rows = pl.ds(pl.multiple_of(start, size), size)
        g_ref[rows, :] = (a * jax.nn.sigmoid(a) * up(w3_ref)).astype(BF16)

    _for_item_rows(icnt_ref[k], k < nit, mm_rows)


def _moe_up_call(layer_e0, src, item_e, item_cs, item_cnt, n_items, hp, w1, w3):
    n_f = D_FFE // TF_MOE
    wspec = pl.BlockSpec((None, D, TF_MOE), lambda k, j, s, ie, ics, ic, ni: (
        layer_e0 + ie[k], 0, jnp.where(k < ni[0], j, n_f - 1)))
    return pl.pallas_call(
        _moe_up_kernel,
        grid_spec=pltpu.PrefetchScalarGridSpec(
            num_scalar_prefetch=5,
            grid=(N_ITEMS, D_FFE // TF_MOE),
            in_specs=[pl.BlockSpec(memory_space=pl.ANY), wspec, wspec],
            out_specs=pl.BlockSpec((RT, TF_MOE), lambda k, j, s, ie, ics, ic, ni: (k, j)),
            scratch_shapes=[
                pltpu.VMEM((2, RT // SUBLANES, SUBLANES, HALF), U32),
                pltpu.SemaphoreType.DMA((2,)),
            ],
        ),
        out_shape=jax.ShapeDtypeStruct((N_ITEMS * RT, D_FFE), BF16),
        compiler_params=_cparams(("arbitrary", "arbitrary")),
        name="moe_up",
    )(src, item_e, item_cs, item_cnt, n_items, hp, w1, w3)


def _moe_down_kernel(ie_ref, icnt_ref, nit_ref, g_ref, w2a_ref, w2b_ref, o_ref):
    k = pl.program_id(0)
    o_ref[...] = jnp.zeros_like(o_ref)

    def mm_rows(start, size):
        rows = pl.ds(pl.multiple_of(start, size), size)
        g = g_ref[rows, :]
        lo = jnp.dot(g, w2a_ref[...].astype(BF16), preferred_element_type=F32)
        hi = jnp.dot(g, w2b_ref[...].astype(BF16), preferred_element_type=F32)
        o_ref[rows, :] = _pack(lo, hi)

    _for_item_rows(icnt_ref[k], k < nit_ref[0], mm_rows)


def _moe_down_call(layer_e0, item_e, item_cnt, n_items, g, w2):
    live = lambda k, ni: jnp.minimum(k, ni[0] - 1)
    n_half = HALF // TN_MOE
    col = lambda k, n, ni: jnp.where(k < ni[0], n, n_half - 1)
    return pl.pallas_call(
        _moe_down_kernel,
        grid_spec=pltpu.PrefetchScalarGridSpec(
            num_scalar_prefetch=3,
            grid=(N_ITEMS, n_half),
            in_specs=[
                pl.BlockSpec((RT, D_FFE), lambda k, n, ie, ic, ni: (live(k, ni), 0)),
                pl.BlockSpec((None, D_FFE, TN_MOE),
                             lambda k, n, ie, ic, ni: (layer_e0 + ie[k], 0, col(k, n, ni))),
                pl.BlockSpec((None, D_FFE, TN_MOE),
                             lambda k, n, ie, ic, ni: (layer_e0 + ie[k], 0, n_half + col(k, n, ni))),
            ],
            out_specs=pl.BlockSpec((RT, TN_MOE), lambda k, n, ie, ic, ni: (k, n)),
        ),
        out_shape=jax.ShapeDtypeStruct((N_ITEMS * RT, HALF), U32),
        compiler_params=_cparams(("arbitrary", "arbitrary")),
        name="moe_down",
    )(item_e, item_cnt, n_items, g, w2, w2)


def _combine_copy(yo_hbm, row, buf, slot, q, u, sem):
    return pltpu.make_async_copy(yo_hbm.at[pl.ds(row, 1)], buf.at[slot, q, pl.ds(u, 1)], sem.at[slot])


def _combine_kernel(split, pos_ref, yo_hbm, x_ref, rinfo_ref, mods_ref, *rest):
    if split:
        yp_ref, ys_ref, buf, sem = rest
    else:
        o_ref, buf, sem = rest
    i = pl.program_id(0)
    n = pl.num_programs(0)
    rows = TOP_K * TM

    def issue(tile, slot):
        def body(q, c):
            for u in range(SUBLANES):
                row = pos_ref[tile * rows + q * SUBLANES + u]
                pltpu.async_copy(yo_hbm.at[pl.ds(row, 1)], buf.at[slot, q, pl.ds(u, 1)], sem.at[slot],
                                 priority=u % 2)
            return c
        lax.fori_loop(0, rows // SUBLANES, body, 0)

    @pl.when(i == 0)
    def _():
        issue(0, 0)

    @pl.when(i + 1 < n)
    def _():
        issue(i + 1, (i + 1) & 1)

    slot = i & 1

    def body(r, c):
        _combine_copy(yo_hbm, 0, buf, slot, 0, 0, sem).wait()
        return c
    lax.fori_loop(0, rows, body, 0, unroll=8)
    rinfo = rinfo_ref[...]
    tiles = TM // SUBLANES
    lo0, hi0 = _unpack(buf[slot, 0:tiles].reshape(TM, HALF))
    lo1, hi1 = _unpack(buf[slot, tiles:2 * tiles].reshape(TM, HALF))
    gt0, gt1 = rinfo[:, 2:3], rinfo[:, 3:4]
    g2 = mods_ref[pl.ds(_mod_row(i), 1), :]
    y_lo = x_ref[:, 0:HALF] + g2[:, 0:HALF] * (gt0 * lo0 + gt1 * lo1)
    y_hi = x_ref[:, HALF:D] + g2[:, HALF:D] * (gt0 * hi0 + gt1 * hi1)

    def write(ref):
        ref[:, 0:HALF] = y_lo
        ref[:, HALF:D] = y_hi

    if not split:
        write(o_ref)
        return

    @pl.when(i < P_TILES)
    def _():
        write(yp_ref)

    @pl.when(i >= P_TILES)
    def _():
        write(ys_ref)


def _combine_call(l, split, pos, yo, x, rinfo, mods_all):
    if split:
        out_specs = [pl.BlockSpec((TM, D), lambda i, p: (jnp.minimum(i, P_TILES - 1), 0)),
                     pl.BlockSpec((TM, D), lambda i, p: (jnp.maximum(i - P_TILES, 0), 0))]
        out_shape = [jax.ShapeDtypeStruct((M_P, D), F32), jax.ShapeDtypeStruct((M_S, D), F32)]
    else:
        out_specs = pl.BlockSpec((TM, D), lambda i, p: (i, 0))
        out_shape = jax.ShapeDtypeStruct((M_TOK, D), F32)
    return pl.pallas_call(
        functools.partial(_combine_kernel, split),
        grid_spec=pltpu.PrefetchScalarGridSpec(
            num_scalar_prefetch=1,
            grid=(N_TILES,),
            in_specs=[
                pl.BlockSpec(memory_space=pl.ANY),
                pl.BlockSpec((TM, D), lambda i, p: (i, 0)),
                pl.BlockSpec((TM, LANE), lambda i, p: (i, 0)),
                pl.BlockSpec((None, MOD_ROWS, D), lambda i, p: (l, 0, 5)),
            ],
            out_specs=out_specs,
            scratch_shapes=[pltpu.VMEM((2, TOP_K * TM // SUBLANES, SUBLANES, HALF), U32),
                            pltpu.SemaphoreType.DMA((2,))],
        ),
        out_shape=out_shape,
        compiler_params=_cparams(("arbitrary",)),
        name="moe_combine",
    )(pos, yo, x, rinfo, mods_all)


def _route_plan(rinfo):
    flat_e = rinfo[:, 0:TOP_K].astype(I32).reshape(-1)
    onehot = (flat_e[:, None] == jnp.arange(N_EXP, dtype=I32)[None, :]).astype(I32)
    csum = jnp.cumsum(onehot, axis=0)
    pick = lambda table: jnp.sum(onehot * table[None, :], axis=1)
    rank = jnp.sum(onehot * csum, axis=1) - 1
    counts = csum[-1]
    cstart = jnp.cumsum(counts) - counts
    n_it = (counts + RT - 1) // RT
    it_end = jnp.cumsum(n_it)
    it_base = it_end - n_it
    n_items = it_end[-1]
    cpos = pick(cstart) + rank
    ppos = (pick(it_base) + rank // RT) * RT + rank % RT
    q = jnp.arange(N_ITEMS, dtype=I32)
    qc = jnp.minimum(q, n_items - 1)
    qe = jnp.minimum(jnp.sum((it_end[None, :] <= qc[:, None]).astype(I32), axis=1), N_EXP - 1)
    qhot = (qe[:, None] == jnp.arange(N_EXP, dtype=I32)[None, :]).astype(I32)
    qpick = lambda table: jnp.sum(qhot * table[None, :], axis=1)
    s = qc - qpick(it_base)
    item_cnt = jnp.where(q < n_items, jnp.clip(qpick(counts) - s * RT, 0, RT), 0)
    item_cs = qpick(cstart) + s * RT
    pos_tiles = ppos.reshape(N_TILES, TM, TOP_K).transpose(0, 2, 1).reshape(-1)
    return (cpos.astype(I32), qe.astype(I32), item_cs.astype(I32), item_cnt.astype(I32),
            n_items.reshape(1).astype(I32), pos_tiles.astype(I32))


def _rope_tables():
    rows = DEC_SEQ // GRID_W
    row = jnp.repeat(jnp.arange(rows), GRID_W).astype(F32)
    col = jnp.tile(jnp.arange(GRID_W), rows).astype(F32)
    nf = ROPE // 4
    inv = ROPE_BASE ** (-jnp.arange(nf, dtype=F32) / nf)
    ar = row[:, None] * inv[None, :]
    ac = col[:, None] * inv[None, :]
    ang = jnp.concatenate([ar, ar, ac, ac], axis=-1)
    cos, sin = jnp.cos(ang), jnp.sin(ang)
    first = (jnp.arange(ROPE) % 32) < 16
    pad = lambda a, v: jnp.pad(a, ((0, 0), (0, LANE - ROPE)), constant_values=v)
    cos_p = pad(cos, 1.0)
    sin_a = pad(jnp.where(first[None, :], -sin, 0.0), 0.0)
    sin_b = pad(jnp.where(first[None, :], 0.0, sin), 0.0)
    ident = jnp.stack([jnp.ones((TMI, LANE), F32), jnp.zeros((TMI, LANE), F32), jnp.zeros((TMI, LANE), F32)])
    return jnp.concatenate([ident, jnp.stack([cos_p, sin_a, sin_b])], axis=1)


def kernel(x_prompt, x_sample, cache_ckv, cache_krope, c, c_ctx, w_ada, b_ada, norm1_g, norm2_g, w_in,
           qlat_norm_g, kvlat_norm_g, w_uq, w_ukv, q_head_norm_g, k_head_norm_g, gm_norm_g, gm_w_s,
           gm_b_s, pool_w, pool_scale, w_o, ffn_w1, ffn_w3, ffn_w2, router_w, moe_w1, moe_w3, moe_w2):
    pad_w = IN_PAD - w_in.shape[-1]
    col = jnp.arange(IN_PAD)[None, None, :]
    w_in_p = jnp.where(
        col < C_KR + ROPE, jnp.pad(w_in, ((0, 0), (0, 0), (0, pad_w))),
        jnp.where(col < C_KR + LANE, 0.0, jnp.pad(w_in, ((0, 0), (0, 0), (pad_w, 0))))).astype(BF16)
    w_uq_p = jnp.pad(w_uq.reshape(DEPTH, Q_RANK, N_HEADS, QK_DIM),
                     ((0, 0), (0, 0), (0, 0), (0, HEAD_PAD - QK_DIM))
                     ).reshape(DEPTH, Q_RANK, QK_W).astype(BF16)
    w_ukv_b = w_ukv.astype(BF16)
    w_o_b = w_o.astype(BF16)
    qg_p = jnp.pad(q_head_norm_g * (1.0 / math.sqrt(QK_DIM)),
                   ((0, 0), (0, HEAD_PAD - QK_DIM))).reshape(DEPTH, 1, HEAD_PAD)
    kg_n = k_head_norm_g[:, :NOPE].reshape(DEPTH, 1, LANE)
    kg_r = jnp.pad(k_head_norm_g[:, NOPE:], ((0, 0), (0, LANE - ROPE))).reshape(DEPTH, 1, LANE)
    gmw_b = gm_w_s.astype(BF16)
    gmb = jnp.broadcast_to(gm_b_s[:, :, :, None], (DEPTH, GM_GROUPS, CHUNK, LANE))
    pw_b = pool_w.astype(BF16)
    ps = pool_scale.reshape(DEPTH, 1, POOL_W)
    n1g = norm1_g.reshape(DEPTH, 1, D)
    n2g = norm2_g.reshape(DEPTH, 1, D)
    qlg = qlat_norm_g.reshape(DEPTH, 1, Q_RANK)
    kvlg = kvlat_norm_g.reshape(DEPTH, 1, KV_RANK)
    rw_p = jnp.pad(router_w, ((0, 0), (0, 0), (0, LANE - N_EXP)))
    n_moe = moe_w1.shape[0]
    moe_w1_flat = moe_w1.reshape(n_moe * N_EXP, D, D_FFE)
    moe_w3_flat = moe_w3.reshape(n_moe * N_EXP, D, D_FFE)
    moe_w2_flat = moe_w2.reshape(n_moe * N_EXP, D_FFE, D)
    rope_tab = _rope_tables()
    cache_kr_p = jnp.pad(cache_krope, ((0, 0), (0, 0), (0, 0), (0, LANE - ROPE)))

    cond = jnp.zeros((MOD_ROWS, D), F32).at[0].set(c_ctx).at[1:1 + DEC_BATCH].set(c)
    mods_all = _ada_call(cond, w_ada, b_ada)
    k_c, v_c = _cache_kv_call(cache_ckv, cache_kr_p, w_ukv_b, kg_n, kg_r)

    x = (x_prompt.reshape(M_P, D), x_sample.reshape(M_S, D))
    ckv_out, kr_out = [], []
    for l in range(DEPTH):
        q, k, v, ckv, kr, zg, zp = _in_proj_call(
            l, x, mods_all, n1g, w_in_p, qlg, kvlg, w_uq_p, qg_p, w_ukv_b, kg_n, kg_r, rope_tab,
            gm_norm_g, gmw_b, gmb)
        ckv_out.append(ckv[:M_P].reshape(BATCH, SEQ, KV_RANK))
        kr_out.append(kr[:M_P, :ROPE].reshape(BATCH, SEQ, ROPE))
        attn = _attn_call(l, q, k, v, k_c, v_c)
        i = l // 2
        if l % 2 == 0:
            x, h2b = _out_proj_call(l, False, attn, zg, zp, x, mods_all, pw_b, ps, w_o_b, n2g)
            g = _ffn_up_call(i, h2b, ffn_w1, ffn_w3)
            x = _ffn_down_call(l, g, ffn_w2, x, mods_all)
        else:
            x, hp, rinfo = _out_proj_call(l, True, attn, zg, zp, x, mods_all, pw_b, ps, w_o_b, n2g, rw_p)
            cpos, item_e, item_cs, item_cnt, n_items, pos_tiles = _route_plan(rinfo)
            src = _src_call(cpos)
            g = _moe_up_call(i * N_EXP, src, item_e, item_cs, item_cnt, n_items, hp,
                             moe_w1_flat, moe_w3_flat)
            yo = _moe_down_call(i * N_EXP, item_e, item_cnt, n_items, g, moe_w2_flat)
            x = _combine_call(l, l == DEPTH - 1, pos_tiles, yo, x, rinfo, mods_all)

    if DEPTH % 2 == 0:
        y_p, y_s = x
    else:
        y_p, y_s = x[:M_P], x[M_P:]
    return (y_p.reshape(BATCH, SEQ, D), y_s.reshape(DEC_BATCH, DEC_SEQ, D),
            jnp.stack(ckv_out, axis=1), jnp.stack(kr_out, axis=1))
```
